```python
import jax, jax.numpy as jnp
from jax import lax
import numpy as np

D_MODEL = 2048
BATCH = 2
SEQ = 4096
DEPTH = 1
DEC_BATCH = 128
DEC_SEQ = 4
PAST_LEN = 16384
PAGE_SIZE = 128

POOL_WIDTH = D_MODEL // 2
POOL_WINDOWS = (2, 4, 8, 16)
N_POOL_GROUPS = len(POOL_WINDOWS)
POOL_GROUP = POOL_WIDTH // N_POOL_GROUPS
POOL_HIST = max(POOL_WINDOWS) - 1
HEAD_DIM = 64
N_HEADS = (D_MODEL - POOL_WIDTH) // HEAD_DIM
N_KV_HEADS = 4
GQA_GROUP = N_HEADS // N_KV_HEADS
ATTN_WIDTH = N_HEADS * HEAD_DIM
KV_WIDTH = N_KV_HEADS * HEAD_DIM
IN_WIDTH = POOL_WIDTH + ATTN_WIDTH + 2 * KV_WIDTH
MIX_WIDTH = POOL_WIDTH + ATTN_WIDTH
WINDOW = 128
BLOCK = 128
ROPE_DIM = HEAD_DIM // 4
ROPE_THETA = 500000.0
D_FF = ((8 * D_MODEL // 3 + 255) // 256) * 256
EPS = 1e-5

kernel_name = "hybrid_pool_swa_sink_decoder_step"


def rms_norm(x, g):
    xf = x.astype(jnp.float32)
    y = xf * lax.rsqrt(jnp.mean(xf * xf, axis=-1, keepdims=True) + EPS)
    return (y * g.astype(jnp.float32)).astype(x.dtype)


def rope(x, pos):
    half = ROPE_DIM // 2
    inv = ROPE_THETA ** (-jnp.arange(0, ROPE_DIM, 2, dtype=jnp.float32) / ROPE_DIM)
    ang = pos.astype(jnp.float32)[:, None] * inv[None, :]
    cos = jnp.cos(ang)[:, None, :]
    sin = jnp.sin(ang)[:, None, :]
    xr = x[..., :ROPE_DIM].astype(jnp.float32)
    x1, x2 = xr[..., :half], xr[..., half:]
    rot = jnp.concatenate([x1 * cos - x2 * sin, x2 * cos + x1 * sin], axis=-1).astype(x.dtype)
    return jnp.concatenate([rot, x[..., ROPE_DIM:]], axis=-1)


def pool_mix(u, prev, start_pos, w_pool, pool_scale):
    b, t, c = u.shape
    ext = jnp.concatenate([prev.astype(u.dtype), u], axis=1)
    extf = ext.astype(jnp.float32)
    cs = jnp.concatenate([jnp.zeros((b, 1, c), jnp.float32), jnp.cumsum(extf, axis=1)], axis=1)
    end = cs[:, POOL_HIST + 1:POOL_HIST + 1 + t]
    pos = start_pos + jnp.arange(t)
    means = []
    for gi, w in enumerate(POOL_WINDOWS):
        sl = slice(gi * POOL_GROUP, (gi + 1) * POOL_GROUP)
        begin = cs[:, POOL_HIST + 1 - w:POOL_HIST + 1 - w + t, sl]
        cnt = jnp.minimum(pos + 1, w).astype(jnp.float32)[:, None]
        means.append((end[..., sl] - begin) / cnt)
    pooled = jnp.concatenate(means, axis=-1)
    d = (pooled - u.astype(jnp.float32)).astype(u.dtype).reshape(b, t, N_POOL_GROUPS, POOL_GROUP)
    out = jnp.einsum('btgc,gcd->btgd', d, w_pool).reshape(b, t, c) * pool_scale
    return out.astype(u.dtype), ext[:, -POOL_HIST:]


def sink_attention(q, k, v, mask, sinks):
    s = jnp.einsum('...qkgd,...skd->...kgqs', q, k).astype(jnp.float32) * (HEAD_DIM ** -0.5)
    s = jnp.where(mask, s, -jnp.inf)
    sink = sinks.astype(jnp.float32).reshape(N_KV_HEADS, GQA_GROUP, 1, 1)
    m = jnp.maximum(jnp.max(s, axis=-1, keepdims=True), sink)
    p = jnp.exp(s - m)
    p = p / (jnp.sum(p, axis=-1, keepdims=True) + jnp.exp(sink - m))
    return jnp.einsum('...kgqs,...skd->...qkgd', p.astype(v.dtype), v)


def swa_prompt(q, k, v, sinks):
    b, s = q.shape[0], q.shape[1]
    nb = s // BLOCK
    qb = q.reshape(b, nb, BLOCK, N_KV_HEADS, GQA_GROUP, HEAD_DIM)
    zpad = jnp.zeros((b, BLOCK, N_KV_HEADS, HEAD_DIM), k.dtype)
    kp = jnp.concatenate([zpad, k], axis=1).reshape(b, nb + 1, BLOCK, N_KV_HEADS, HEAD_DIM)
    vp = jnp.concatenate([zpad, v], axis=1).reshape(b, nb + 1, BLOCK, N_KV_HEADS, HEAD_DIM)
    kb = jnp.concatenate([kp[:, :-1], kp[:, 1:]], axis=2)
    vb = jnp.concatenate([vp[:, :-1], vp[:, 1:]], axis=2)
    qi = jnp.arange(BLOCK)[:, None] + BLOCK
    kj = jnp.arange(2 * BLOCK)[None, :]
    diff = qi - kj
    n = jnp.arange(nb)[:, None, None]
    valid = (diff >= 0) & (diff <= WINDOW) & ((n * BLOCK - BLOCK + kj) >= 0)
    mask = valid[:, None, None, :, :]
    o = sink_attention(qb, kb, vb, mask, sinks)
    return o.reshape(b, s, ATTN_WIDTH)


def swa_sample(q, k_new, v_new, k_cache, v_cache, sinks):
    bd, t = q.shape[0], q.shape[1]
    w = k_cache.shape[1]
    k_all = jnp.concatenate([k_cache.astype(k_new.dtype), k_new], axis=1)
    v_all = jnp.concatenate([v_cache.astype(v_new.dtype), v_new], axis=1)
    qpos = PAST_LEN + jnp.arange(t)
    kpos = PAST_LEN - w + jnp.arange(w + t)
    diff = qpos[:, None] - kpos[None, :]
    mask = (diff >= 0) & (diff <= WINDOW) & (kpos[None, :] >= 0)
    qg = q.reshape(bd, t, N_KV_HEADS, GQA_GROUP, HEAD_DIM)
    o = sink_attention(qg, k_all, v_all, mask, sinks)
    return o.reshape(bd, t, ATTN_WIDTH), k_all[:, -w:], v_all[:, -w:]


def project(x, g_mix, w_in, pos):
    b, t, _ = x.shape
    h = rms_norm(x, g_mix)
    proj = h @ w_in
    u = proj[..., :POOL_WIDTH]
    q = proj[..., POOL_WIDTH:POOL_WIDTH + ATTN_WIDTH].reshape(b, t, N_HEADS, HEAD_DIM)
    k = proj[..., POOL_WIDTH + ATTN_WIDTH:POOL_WIDTH + ATTN_WIDTH + KV_WIDTH].reshape(b, t, N_KV_HEADS, HEAD_DIM)
    v = proj[..., POOL_WIDTH + ATTN_WIDTH + KV_WIDTH:].reshape(b, t, N_KV_HEADS, HEAD_DIM)
    return u, rope(q, pos), rope(k, pos), v


def finish(x, pool_out, attn_out, w_out, g_ffn, w_gate, w_up, w_down):
    x = x + jnp.concatenate([pool_out, attn_out], axis=-1) @ w_out
    h = rms_norm(x, g_ffn)
    return x + (jax.nn.silu(h @ w_gate) * (h @ w_up)) @ w_down


def setup_inputs(seed: int = 0) -> dict:
    key = jax.random.key(seed)
    ks = jax.random.split(key, 18)
    f32 = jnp.float32
    win_buf = min(WINDOW, PAST_LEN)
    nrm = lambda k, shape, scale: jax.random.normal(k, shape, f32) * scale
    return {
        "x_prompt": nrm(ks[0], (BATCH, SEQ, D_MODEL), 1.0),
        "x_sample": nrm(ks[1], (DEC_BATCH, DEC_SEQ, D_MODEL), 1.0),
        "state_pool": nrm(ks[2], (DEPTH, DEC_BATCH, POOL_HIST, POOL_WIDTH), 1.0),
        "cache_k_win": nrm(ks[3], (DEPTH, DEC_BATCH, win_buf, N_KV_HEADS, HEAD_DIM), 1.0),
        "cache_v_win": nrm(ks[4], (DEPTH, DEC_BATCH, win_buf, N_KV_HEADS, HEAD_DIM), 1.0),
        "g_mix": 1.0 + nrm(ks[5], (DEPTH, D_MODEL), 0.05),
        "w_in": nrm(ks[6], (DEPTH, D_MODEL, IN_WIDTH), D_MODEL ** -0.5),
        "w_pool": nrm(ks[7], (DEPTH, N_POOL_GROUPS, POOL_GROUP, POOL_GROUP), POOL_GROUP ** -0.5),
        "pool_scale": 1.0 + nrm(ks[8], (DEPTH, POOL_WIDTH), 0.1),
        "attn_sinks": nrm(ks[9], (DEPTH, N_HEADS), 1.0),
        "w_out": nrm(ks[10], (DEPTH, MIX_WIDTH, D_MODEL), MIX_WIDTH ** -0.5),
        "g_ffn": 1.0 + nrm(ks[11], (DEPTH, D_MODEL), 0.05),
        "w_gate": nrm(ks[12], (DEPTH, D_MODEL, D_FF), D_MODEL ** -0.5),
        "w_up": nrm(ks[13], (DEPTH, D_MODEL, D_FF), D_MODEL ** -0.5),
        "w_down": nrm(ks[14], (DEPTH, D_FF, D_MODEL), D_FF ** -0.5),
        "g_final": 1.0 + nrm(ks[15], (D_MODEL,), 0.05),
    }


def reference(x_prompt, x_sample, state_pool, cache_k_win, cache_v_win, g_mix, w_in, w_pool,
              pool_scale, attn_sinks, w_out, g_ffn, w_gate, w_up, w_down, g_final):
    xp, xs = x_prompt, x_sample
    pos_p = jnp.arange(xp.shape[1])
    pos_s = PAST_LEN + jnp.arange(xs.shape[1])
    win_p = min(WINDOW, xp.shape[1])
    pool_p, kw_p, vw_p, pool_s, kw_s, vw_s = [], [], [], [], [], []
    for l in range(DEPTH):
        u, q, k, v = project(xp, g_mix[l], w_in[l], pos_p)
        zeros_hist = jnp.zeros((xp.shape[0], POOL_HIST, POOL_WIDTH), u.dtype)
        po, ps = pool_mix(u, zeros_hist, 0, w_pool[l], pool_scale[l])
        ao = swa_prompt(q, k, v, attn_sinks[l])
        xp = finish(xp, po, ao, w_out[l], g_ffn[l], w_gate[l], w_up[l], w_down[l])
        pool_p.append(ps)
        kw_p.append(k[:, -win_p:])
        vw_p.append(v[:, -win_p:])
        u, q, k, v = project(xs, g_mix[l], w_in[l], pos_s)
        po, ps = pool_mix(u, state_pool[l], PAST_LEN, w_pool[l], pool_scale[l])
        ao, kn, vn = swa_sample(q, k, v, cache_k_win[l], cache_v_win[l], attn_sinks[l])
        xs = finish(xs, po, ao, w_out[l], g_ffn[l], w_gate[l], w_up[l], w_down[l])
        pool_s.append(ps)
        kw_s.append(kn)
        vw_s.append(vn)
    y_prompt = rms_norm(xp, g_final)
    y_sample = rms_norm(xs, g_final)
    return (y_prompt, y_sample, jnp.stack(pool_p), jnp.stack(kw_p), jnp.stack(vw_p),
            jnp.stack(pool_s), jnp.stack(kw_s), jnp.stack(vw_s))
```

```python
import functools

import jax
import jax.numpy as jnp
from jax import lax
from jax.experimental import pallas as pl
from jax.experimental.pallas import tpu as pltpu

F32 = jnp.float32
BF16 = jnp.bfloat16

D_MODEL = 2048
POOL_WIDTH = D_MODEL // 2
POOL_WINDOWS = (2, 4, 8, 16)
N_POOL_GROUPS = len(POOL_WINDOWS)
POOL_GROUP = POOL_WIDTH // N_POOL_GROUPS
POOL_HIST = max(POOL_WINDOWS) - 1
HEAD_DIM = 64
N_HEADS = (D_MODEL - POOL_WIDTH) // HEAD_DIM
N_KV_HEADS = 4
GQA_GROUP = N_HEADS // N_KV_HEADS
ATTN_WIDTH = N_HEADS * HEAD_DIM
KV_WIDTH = N_KV_HEADS * HEAD_DIM
IN_WIDTH = POOL_WIDTH + ATTN_WIDTH + 2 * KV_WIDTH
MIX_WIDTH = POOL_WIDTH + ATTN_WIDTH
WINDOW = 128
BLOCK = 128
ROPE_DIM = HEAD_DIM // 4
ROPE_HALF = ROPE_DIM // 2
ROPE_THETA = 500000.0
D_FF = ((8 * D_MODEL // 3 + 255) // 256) * 256
EPS = 1e-5
PAST_LEN = 16384
Q_SCALE = HEAD_DIM ** -0.5

LANES = 128
HALO = 16
TOKEN_TILE = 512
FF_TILE = 512
SEQ_BLOCK = 8
SAMPLE_KEYS = 256
VMEM_LIMIT = 56 * 1024 * 1024


def _params(*semantics):
    return pltpu.CompilerParams(dimension_semantics=semantics, vmem_limit_bytes=VMEM_LIMIT)


def _rms(x, g):
    ms = jnp.mean(x * x, axis=-1, keepdims=True)
    return x * lax.rsqrt(ms + EPS) * g


def _proj_kernel(x_ref, g_ref, w_ref, c_ref, s1_ref, s2_ref,
                 u_ref, q_ref, k_ref, v_ref, kk_ref, vb_ref):
    h = _rms(x_ref[...], g_ref[...]).astype(BF16)
    proj = jnp.dot(h, w_ref[...], preferred_element_type=F32)
    u_ref[...] = proj[:, :POOL_WIDTH]
    c, s1, s2 = c_ref[...], s1_ref[...], s2_ref[...]

    def rope(z):
        return (z * c + pltpu.roll(z, LANES - ROPE_HALF, 1) * s1
                + pltpu.roll(z, ROPE_HALF, 1) * s2)

    for col in range(ATTN_WIDTH // LANES):
        z = proj[:, POOL_WIDTH + col * LANES:POOL_WIDTH + (col + 1) * LANES]
        q_ref[:, col * LANES:(col + 1) * LANES] = (rope(z) * Q_SCALE).astype(BF16)
    low_half = lax.broadcasted_iota(jnp.int32, (1, LANES), 1) < HEAD_DIM
    k0 = POOL_WIDTH + ATTN_WIDTH
    for a in range(KV_WIDTH // LANES):
        z = rope(proj[:, k0 + a * LANES:k0 + (a + 1) * LANES])
        k_ref[:, a * LANES:(a + 1) * LANES] = z
        kk_ref[:, (2 * a) * LANES:(2 * a + 1) * LANES] = jnp.where(low_half, z, 0.0).astype(BF16)
        kk_ref[:, (2 * a + 1) * LANES:(2 * a + 2) * LANES] = jnp.where(low_half, 0.0, z).astype(BF16)
    v = proj[:, k0 + KV_WIDTH:]
    v_ref[...] = v
    vb_ref[...] = v.astype(BF16)


def _proj(x, g, w_in, tabs, n_tab_blocks):
    t = x.shape[0]
    tm = TOKEN_TILE
    row = lambda w: pl.BlockSpec((tm, w), lambda i: (i, 0))
    tab = pl.BlockSpec((tm, LANES), lambda i: (i % n_tab_blocks, 0))
    full = lambda a: pl.BlockSpec(a.shape, lambda i: (0,) * a.ndim)
    return pl.pallas_call(
        _proj_kernel,
        grid=(t // tm,),
        in_specs=[row(D_MODEL), full(g), full(w_in), tab, tab, tab],
        out_specs=[row(POOL_WIDTH), row(ATTN_WIDTH), row(KV_WIDTH), row(KV_WIDTH),
                   row(2 * KV_WIDTH), row(KV_WIDTH)],
        out_shape=[jax.ShapeDtypeStruct((t, POOL_WIDTH), F32),
                   jax.ShapeDtypeStruct((t, ATTN_WIDTH), BF16),
                   jax.ShapeDtypeStruct((t, KV_WIDTH), F32),
                   jax.ShapeDtypeStruct((t, KV_WIDTH), F32),
                   jax.ShapeDtypeStruct((t, 2 * KV_WIDTH), BF16),
                   jax.ShapeDtypeStruct((t, KV_WIDTH), BF16)],
        compiler_params=_params("parallel"),
        name="proj",
    )(x, g, w_in, *tabs)


def _sink_softmax(s, mask, sink):
    s = jnp.where(mask, s, -jnp.inf)
    m = jnp.maximum(jnp.max(s, axis=1, keepdims=True), sink)
    p = jnp.exp(s - m)
    den = jnp.sum(p, axis=1, keepdims=True) + jnp.exp(sink - m)
    return p / den


def _mix_prompt_kernel(sink_ref, u_ref, halo_ref, q_ref, kkc_ref, kkp_ref, vc_ref, vp_ref,
                       wpool_ref, pscale_ref, mix_ref, ext_ref, kwin_ref, vwin_ref):
    tq = u_ref.shape[0]
    i = pl.program_id(1)
    first = i == 0

    ext_ref[0:HALO, :] = halo_ref[...] * jnp.where(first, 0.0, 1.0)
    ext_ref[HALO:, :] = u_ref[...]
    pos1 = i * tq + lax.broadcasted_iota(jnp.int32, (tq, 1), 0) + 1
    for gi, w in enumerate(POOL_WINDOWS):
        cols = slice(gi * POOL_GROUP, (gi + 1) * POOL_GROUP)
        tok = ext_ref[HALO:HALO + tq, cols]
        acc = tok
        for j in range(1, w):
            acc = acc + ext_ref[HALO - j:HALO - j + tq, cols]
        cnt = jnp.minimum(pos1, w).astype(F32)
        d = (acc / cnt - tok).astype(BF16)
        po = jnp.dot(d, wpool_ref[gi], preferred_element_type=F32) * pscale_ref[:, cols]
        mix_ref[:, cols] = po.astype(BF16)

    kwin_ref[0:BLOCK, :] = kkp_ref[...]
    kwin_ref[BLOCK:, :] = kkc_ref[...]
    vwin_ref[0:BLOCK, :] = vp_ref[...]
    vwin_ref[BLOCK:, :] = vc_ref[...]
    qi = lax.broadcasted_iota(jnp.int32, (BLOCK, 2 * BLOCK), 0)
    kj = lax.broadcasted_iota(jnp.int32, (BLOCK, 2 * BLOCK), 1)
    band = (kj >= qi) & (kj <= qi + WINDOW)
    first_lo = jnp.where(first, BLOCK, 0)
    low_half = lax.broadcasted_iota(jnp.int32, (1, LANES), 1) < HEAD_DIM
    for n in range(tq // BLOCK):
        mask = (band & (kj >= first_lo)) if n == 0 else band
        mask4 = jnp.concatenate([mask] * GQA_GROUP, axis=0)
        rows = slice(n * BLOCK, (n + 1) * BLOCK)
        keys = slice(n * BLOCK, (n + 2) * BLOCK)
        for a in range(N_KV_HEADS // 2):
            qs = jnp.concatenate(
                [q_ref[rows, (GQA_GROUP * a + j) * LANES:(GQA_GROUP * a + j + 1) * LANES]
                 for j in range(GQA_GROUP)], axis=0)
            vv = vwin_ref[keys, a * LANES:(a + 1) * LANES]
            outs = []
            for par in range(2):
                g = 2 * a + par
                kk = kwin_ref[keys, g * LANES:(g + 1) * LANES]
                s = lax.dot_general(qs, kk, (((1,), (1,)), ((), ())), preferred_element_type=F32)
                p = _sink_softmax(s, mask4, sink_ref[g])
                outs.append(jnp.dot(p.astype(BF16), vv, preferred_element_type=F32))
            o = jnp.where(low_half, outs[0], outs[1]).astype(BF16)
            for j in range(GQA_GROUP):
                col = POOL_WIDTH + (GQA_GROUP * a + j) * LANES
                mix_ref[rows, col:col + LANES] = o[j * BLOCK:(j + 1) * BLOCK]


def _mix_prompt(sink_cols, u, q, kk, vb, wpool, pscale, batch, seq):
    tq = TOKEN_TILE
    nt = seq // tq
    bpt = tq // BLOCK
    hpt = tq // HALO
    row = lambda w: pl.BlockSpec((tq, w), lambda b, i: (b * nt + i, 0))
    prev = lambda w: pl.BlockSpec(
        (BLOCK, w), lambda b, i: (jnp.maximum((b * nt + i) * bpt - 1, 0), 0))
    halo = pl.BlockSpec((HALO, POOL_WIDTH), lambda b, i: (jnp.maximum((b * nt + i) * hpt - 1, 0), 0))
    full = lambda a: pl.BlockSpec(a.shape, lambda b, i: (0,) * a.ndim)
    return pl.pallas_call(
        _mix_prompt_kernel,
        grid=(batch, nt),
        in_specs=[full(sink_cols), row(POOL_WIDTH), halo, row(ATTN_WIDTH),
                  row(2 * KV_WIDTH), prev(2 * KV_WIDTH), row(KV_WIDTH), prev(KV_WIDTH),
                  full(wpool), full(pscale)],
        out_specs=row(MIX_WIDTH),
        out_shape=jax.ShapeDtypeStruct((batch * seq, MIX_WIDTH), BF16),
        scratch_shapes=[pltpu.VMEM((HALO + tq, POOL_WIDTH), F32),
                        pltpu.VMEM((BLOCK + tq, 2 * KV_WIDTH), BF16),
                        pltpu.VMEM((BLOCK + tq, KV_WIDTH), BF16)],
        compiler_params=_params("parallel", "parallel"),
        name="mix_prompt",
    )(sink_cols, u, u, q, kk, kk, vb, vb, wpool, pscale)


def _mix_sample_kernel(sink_ref, u_ref, q_ref, kn_ref, vn_ref, st_ref, ck_ref, cv_ref,
                       wpool_ref, pscale_ref, mix_ref, npool_ref, nk_ref, nv_ref,
                       kall_ref, vall_ref, *, dec_seq):
    sb = SEQ_BLOCK
    rows_per_seq = N_HEADS * dec_seq
    nrow = sb * rows_per_seq
    ntok = sb * dec_seq
    win = ck_ref.shape[1]

    @pl.when(pl.program_id(0) == 0)
    def _():
        kall_ref[...] = jnp.zeros_like(kall_ref)
        vall_ref[...] = jnp.zeros_like(vall_ref)

    ext = [st_ref[h] for h in range(POOL_HIST)]
    ext += [u_ref[t] for t in range(dec_seq)]
    for h in range(POOL_HIST):
        npool_ref[h] = ext[h + dec_seq]
    r_out = lax.broadcasted_iota(jnp.int32, (ntok, ntok), 0)
    r_in = lax.broadcasted_iota(jnp.int32, (ntok, ntok), 1)
    to_seq_major = ((r_in % sb) * dec_seq + r_in // sb == r_out).astype(BF16)
    for gi, w in enumerate(POOL_WINDOWS):
        cols = slice(gi * POOL_GROUP, (gi + 1) * POOL_GROUP)
        ds = []
        for t in range(dec_seq):
            tok = ext[POOL_HIST + t][:, cols]
            acc = tok
            for j in range(1, w):
                acc = acc + ext[POOL_HIST + t - j][:, cols]
            ds.append(acc * (1.0 / w) - tok)
        d = jnp.concatenate(ds, axis=0).astype(BF16)
        po = jnp.dot(d, wpool_ref[gi], preferred_element_type=F32) * pscale_ref[:, cols]
        po = jnp.dot(to_seq_major, po.astype(BF16), preferred_element_type=F32)
        mix_ref[:, cols] = po.astype(BF16)

    for b in range(sb):
        tok_rows = slice(b * dec_seq, (b + 1) * dec_seq)
        kall_ref[b, 0:win, :] = ck_ref[b]
        kall_ref[b, win:win + dec_seq, :] = kn_ref[tok_rows, :]
        vall_ref[b, 0:win, :] = cv_ref[b]
        vall_ref[b, win:win + dec_seq, :] = vn_ref[tok_rows, :]
        nk_ref[b] = kall_ref[b, dec_seq:dec_seq + win, :]
        nv_ref[b] = vall_ref[b, dec_seq:dec_seq + win, :]

    r = lax.broadcasted_iota(jnp.int32, (nrow, ntok), 0)
    c = lax.broadcasted_iota(jnp.int32, (nrow, ntok), 1)
    pick = (c == (r // rows_per_seq) * dec_seq + r % dec_seq)
    qrep = jnp.dot(pick.astype(BF16), q_ref[...], preferred_element_type=F32)
    rr = lax.broadcasted_iota(jnp.int32, (nrow, LANES), 0) % rows_per_seq
    row_g = rr // (GQA_GROUP * dec_seq)
    row_j = (rr // dec_seq) % GQA_GROUP
    lane_par = lax.broadcasted_iota(jnp.int32, (nrow, LANES), 1) // HEAD_DIM
    sel = [[(row_g // 2 == a) & (row_j == j) & (row_g % 2 == lane_par)
            for j in range(GQA_GROUP)] for a in range(2)]
    lhs_cols = []
    for a in range(2):
        acc = jnp.zeros((nrow, LANES), F32)
        for j in range(GQA_GROUP):
            col = GQA_GROUP * a + j
            acc = acc + jnp.where(sel[a][j], qrep[:, col * LANES:(col + 1) * LANES], 0.0)
        lhs_cols.append(acc)
    lhs = jnp.concatenate(lhs_cols, axis=1).astype(BF16)
    scores = []
    for b in range(sb):
        rows = slice(b * rows_per_seq, (b + 1) * rows_per_seq)
        scores.append(lax.dot_general(lhs[rows], kall_ref[b].astype(BF16),
                                      (((1,), (1,)), ((), ())), preferred_element_type=F32))
    s = jnp.concatenate(scores, axis=0)
    t_row = lax.broadcasted_iota(jnp.int32, (nrow, SAMPLE_KEYS), 0) % dec_seq
    kj = lax.broadcasted_iota(jnp.int32, (nrow, SAMPLE_KEYS), 1)
    mask = (kj >= t_row + (win - WINDOW)) & (kj <= t_row + win)
    p = _sink_softmax(s, mask, sink_ref[...]).astype(BF16)
    outs = []
    for b in range(sb):
        rows = slice(b * rows_per_seq, (b + 1) * rows_per_seq)
        outs.append(jnp.dot(p[rows], vall_ref[b].astype(BF16), preferred_element_type=F32))
    o = jnp.concatenate(outs, axis=0)
    z = jnp.concatenate(
        [jnp.where(sel[a][j], o[:, a * LANES:(a + 1) * LANES], 0.0)
         for a in range(2) for j in range(GQA_GROUP)], axis=1).astype(BF16)
    rt = lax.broadcasted_iota(jnp.int32, (ntok, nrow), 0)
    ct = lax.broadcasted_iota(jnp.int32, (ntok, nrow), 1)
    unpick = (rt == (ct // rows_per_seq) * dec_seq + ct % dec_seq)
    attn = jnp.dot(unpick.astype(BF16), z, preferred_element_type=F32)
    mix_ref[:, POOL_WIDTH:] = attn.astype(BF16)


def _mix_sample(sink_rows, u, q, k_new, v_new, state_t, cache_k, cache_v, wpool, pscale, dec_seq):
    nseq, win, _ = cache_k.shape
    sb = SEQ_BLOCK
    ntok = sb * dec_seq
    row = lambda w: pl.BlockSpec((ntok, w), lambda i: (i, 0))
    slab = pl.BlockSpec((POOL_HIST, sb, POOL_WIDTH), lambda i: (0, i, 0))
    u_slab = pl.BlockSpec((dec_seq, sb, POOL_WIDTH), lambda i: (0, i, 0))
    cache = pl.BlockSpec((sb, win, KV_WIDTH), lambda i: (i, 0, 0))
    full = lambda a: pl.BlockSpec(a.shape, lambda i: (0,) * a.ndim)
    return pl.pallas_call(
        functools.partial(_mix_sample_kernel, dec_seq=dec_seq),
        grid=(nseq // sb,),
        in_specs=[full(sink_rows), u_slab, row(ATTN_WIDTH), row(KV_WIDTH), row(KV_WIDTH),
                  slab, cache, cache, full(wpool), full(pscale)],
        out_specs=[row(MIX_WIDTH), slab, cache, cache],
        out_shape=[jax.ShapeDtypeStruct((nseq * dec_seq, MIX_WIDTH), BF16),
                   jax.ShapeDtypeStruct((POOL_HIST, nseq, POOL_WIDTH), F32),
                   jax.ShapeDtypeStruct(cache_k.shape, F32),
                   jax.ShapeDtypeStruct(cache_v.shape, F32)],
        scratch_shapes=[pltpu.VMEM((sb, SAMPLE_KEYS, KV_WIDTH), F32),
                        pltpu.VMEM((sb, SAMPLE_KEYS, KV_WIDTH), F32)],
        compiler_params=_params("arbitrary"),
        name="mix_sample",
    )(sink_rows, u, q, k_new, v_new, state_t, cache_k, cache_v, wpool, pscale)


def _outproj_kernel(x_ref, mix_ref, w_ref, g_ref, x1_ref, h2_ref):
    x1 = x_ref[...] + jnp.dot(mix_ref[...], w_ref[...], preferred_element_type=F32)
    x1_ref[...] = x1
    h2_ref[...] = _rms(x1, g_ref[...]).astype(BF16)


def _outproj(x, mix, w_out, g):
    t = x.shape[0]
    tm = TOKEN_TILE
    row = lambda w: pl.BlockSpec((tm, w), lambda i: (i, 0))
    full = lambda a: pl.BlockSpec(a.shape, lambda i: (0,) * a.ndim)
    return pl.pallas_call(
        _outproj_kernel,
        grid=(t // tm,),
        in_specs=[row(D_MODEL), row(MIX_WIDTH), full(w_out), full(g)],
        out_specs=[row(D_MODEL), row(D_MODEL)],
        out_shape=[jax.ShapeDtypeStruct((t, D_MODEL), F32),
                   jax.ShapeDtypeStruct((t, D_MODEL), BF16)],
        compiler_params=_params("parallel"),
        name="outproj",
    )(x, mix, w_out, g)


def _ffn_kernel(x1_ref, h2_ref, wg_ref, wu_ref, wd_ref, gf_ref, y_ref, acc_ref, *, final_norm):
    j = pl.program_id(1)

    @pl.when(j == 0)
    def _():
        acc_ref[...] = x1_ref[...]

    h = h2_ref[...]
    gate = jnp.dot(h, wg_ref[...], preferred_element_type=F32)
    up = jnp.dot(h, wu_ref[...], preferred_element_type=F32)
    act = (jax.nn.silu(gate) * up).astype(BF16)
    acc_ref[...] += jnp.dot(act, wd_ref[...], preferred_element_type=F32)

    @pl.when(j == pl.num_programs(1) - 1)
    def _():
        y = acc_ref[...]
        y_ref[...] = _rms(y, gf_ref[...]) if final_norm else y


def _ffn(x1, h2, wg, wu, wd, gf, final_norm):
    t = x1.shape[0]
    tm, tf = TOKEN_TILE, FF_TILE
    row = pl.BlockSpec((tm, D_MODEL), lambda i, j: (i, 0))
    return pl.pallas_call(
        functools.partial(_ffn_kernel, final_norm=final_norm),
        grid=(t // tm, D_FF // tf),
        in_specs=[row, row,
                  pl.BlockSpec((D_MODEL, tf), lambda i, j: (0, j)),
                  pl.BlockSpec((D_MODEL, tf), lambda i, j: (0, j)),
                  pl.BlockSpec((tf, D_MODEL), lambda i, j: (j, 0)),
                  pl.BlockSpec(gf.shape, lambda i, j: (0, 0))],
        out_specs=row,
        out_shape=jax.ShapeDtypeStruct((t, D_MODEL), F32),
        scratch_shapes=[pltpu.VMEM((tm, D_MODEL), F32)],
        compiler_params=_params("parallel", "arbitrary"),
        name="ffn",
    )(x1, h2, wg, wu, wd, gf)


def _rope_tables(pos):
    inv = ROPE_THETA ** (-jnp.arange(0, ROPE_DIM, 2, dtype=F32) / ROPE_DIM)
    ang = pos.astype(F32)[:, None] * inv[None, :]
    cos, sin = jnp.cos(ang), jnp.sin(ang)
    n = pos.shape[0]
    rest = HEAD_DIM - ROPE_DIM
    one, zero = jnp.ones((n, rest), F32), jnp.zeros((n, rest), F32)
    zh = jnp.zeros((n, ROPE_HALF), F32)
    reps = LANES // HEAD_DIM
    c = jnp.tile(jnp.concatenate([cos, cos, one], axis=1), (1, reps))
    s1 = jnp.tile(jnp.concatenate([-sin, zh, zero], axis=1), (1, reps))
    s2 = jnp.tile(jnp.concatenate([zh, sin, zero], axis=1), (1, reps))
    return c, s1, s2


def _to_column_order(w, axis):
    shape = w.shape
    split = shape[:axis] + (2, 2, GQA_GROUP, HEAD_DIM) + shape[axis + 1:]
    perm = list(range(len(split)))
    perm[axis + 1], perm[axis + 2] = perm[axis + 2], perm[axis + 1]
    return w.reshape(split).transpose(perm).reshape(shape)


def kernel(x_prompt, x_sample, state_pool, cache_k_win, cache_v_win, g_mix, w_in, w_pool,
           pool_scale, attn_sinks, w_out, g_ffn, w_gate, w_up, w_down, g_final):
    batch, seq, _ = x_prompt.shape
    nseq, dec_seq, _ = x_sample.shape
    depth = w_in.shape[0]
    win_s = cache_k_win.shape[2]
    assert seq % TOKEN_TILE == 0 and (nseq * dec_seq) % TOKEN_TILE == 0 and nseq % SEQ_BLOCK == 0
    assert win_s == WINDOW and win_s + dec_seq <= SAMPLE_KEYS and seq >= WINDOW
    assert PAST_LEN >= max(POOL_HIST, WINDOW)

    xp = x_prompt.reshape(batch * seq, D_MODEL)
    xs = x_sample.reshape(nseq * dec_seq, D_MODEL)
    tabs_p = _rope_tables(jnp.arange(seq))
    tabs_s = _rope_tables(PAST_LEN + jnp.arange(nseq * dec_seq) % dec_seq)
    gf = g_final.reshape(1, D_MODEL)

    outs = [[] for _ in range(6)]
    for l in range(depth):
        w_in_l = jnp.concatenate(
            [w_in[l][:, :POOL_WIDTH],
             _to_column_order(w_in[l][:, POOL_WIDTH:POOL_WIDTH + ATTN_WIDTH], 1),
             w_in[l][:, POOL_WIDTH + ATTN_WIDTH:]], axis=1).astype(BF16)
        w_out_l = jnp.concatenate(
            [w_out[l][:POOL_WIDTH], _to_column_order(w_out[l][POOL_WIDTH:], 0)], axis=0).astype(BF16)
        wpool_l = w_pool[l].astype(BF16)
        wg_l, wu_l, wd_l = w_gate[l].astype(BF16), w_up[l].astype(BF16), w_down[l].astype(BF16)
        gm = g_mix[l].reshape(1, D_MODEL)
        gn = g_ffn[l].reshape(1, D_MODEL)
        pscale = pool_scale[l].reshape(1, POOL_WIDTH)
        sinks = attn_sinks[l].reshape(N_KV_HEADS, GQA_GROUP)
        sink_cols = jnp.repeat(sinks, BLOCK, axis=1)[..., None]
        sink_rows = jnp.tile(jnp.repeat(sinks.reshape(-1), dec_seq), SEQ_BLOCK)[:, None]

        u, q, k, v, kk, vb = _proj(xp, gm, w_in_l, tabs_p, seq // TOKEN_TILE)
        mix = _mix_prompt(sink_cols, u, q, kk, vb, wpool_l, pscale, batch, seq)
        x1, h2 = _outproj(xp, mix, w_out_l, gn)
        xp = _ffn(x1, h2, wg_l, wu_l, wd_l, gf, l == depth - 1)
        outs[0].append(u.reshape(batch, seq, POOL_WIDTH)[:, -POOL_HIST:])
        outs[1].append(k.reshape(batch, seq, N_KV_HEADS, HEAD_DIM)[:, -WINDOW:])
        outs[2].append(v.reshape(batch, seq, N_KV_HEADS, HEAD_DIM)[:, -WINDOW:])

        u, q, k, v, _, _ = _proj(xs, gm, w_in_l, tabs_s, 1)
        state_t = jnp.transpose(state_pool[l], (1, 0, 2))
        ck = cache_k_win[l].reshape(nseq, win_s, KV_WIDTH)
        cv = cache_v_win[l].reshape(nseq, win_s, KV_WIDTH)
        u_t = jnp.transpose(u.reshape(nseq, dec_seq, POOL_WIDTH), (1, 0, 2))
        mix, npool, nk, nv = _mix_sample(sink_rows, u_t, q, k, v, state_t, ck, cv, wpool_l, pscale,
                                         dec_seq)
        x1, h2 = _outproj(xs, mix, w_out_l, gn)
        xs = _ffn(x1, h2, wg_l, wu_l, wd_l, gf, l == depth - 1)
        outs[3].append(jnp.transpose(npool, (1, 0, 2)))
        outs[4].append(nk.reshape(nseq, win_s, N_KV_HEADS, HEAD_DIM))
        outs[5].append(nv.reshape(nseq, win_s, N_KV_HEADS, HEAD_DIM))

    y_prompt = xp.reshape(batch, seq, D_MODEL)
    y_sample = xs.reshape(nseq, dec_seq, D_MODEL)
    return (y_prompt, y_sample) + tuple(jnp.stack(o) for o in outs)
```

```python
import functools

import jax
import jax.numpy as jnp
from jax import lax
from jax.experimental import pallas as pl
from jax.experimental.pallas import tpu as pltpu

F32 = jnp.float32
BF16 = jnp.bfloat16

D_MODEL = 2048
POOL_WIDTH = D_MODEL // 2
POOL_WINDOWS = (2, 4, 8, 16)
N_POOL_GROUPS = len(POOL_WINDOWS)
POOL_GROUP = POOL_WIDTH // N_POOL_GROUPS
POOL_HIST = max(POOL_WINDOWS) - 1
HEAD_DIM = 64
N_HEADS = (D_MODEL - POOL_WIDTH) // HEAD_DIM
N_KV_HEADS = 4
GQA_GROUP = N_HEADS // N_KV_HEADS
ATTN_WIDTH = N_HEADS * HEAD_DIM
KV_WIDTH = N_KV_HEADS * HEAD_DIM
IN_WIDTH = POOL_WIDTH + ATTN_WIDTH + 2 * KV_WIDTH
MIX_WIDTH = POOL_WIDTH + ATTN_WIDTH
WINDOW = 128
BLOCK = 128
ROPE_DIM = HEAD_DIM // 4
ROPE_HALF = ROPE_DIM // 2
ROPE_THETA = 500000.0
D_FF = ((8 * D_MODEL // 3 + 255) // 256) * 256
EPS = 1e-5
PAST_LEN = 16384
Q_SCALE = HEAD_DIM ** -0.5

LANES = 128
SUBLANES = 8
HALO = 16
POOL_TOP = 2 * HALO
SOFTMAX_ROWS = 64
TOKEN_TILE = 512
FF_TILE = 512
SEQ_BLOCK = 8
SAMPLE_KEYS = 256
VMEM_LIMIT = 56 * 1024 * 1024


def _params(*semantics):
    return pltpu.CompilerParams(dimension_semantics=semantics, vmem_limit_bytes=VMEM_LIMIT)


def _rms(x, g):
    ms = jnp.mean(x * x, axis=-1, keepdims=True)
    return x * lax.rsqrt(ms + EPS) * g


def _proj_kernel(x_ref, g_ref, w_ref, c_ref, s1_ref, s2_ref,
                 u_ref, q_ref, k_ref, v_ref, kk_ref, vb_ref):
    h = _rms(x_ref[...], g_ref[...]).astype(BF16)
    proj = jnp.dot(h, w_ref[...], preferred_element_type=F32)
    u_ref[...] = proj[:, :POOL_WIDTH]
    c, s1, s2 = c_ref[...], s1_ref[...], s2_ref[...]

    def rope(z):
        return (z * c + pltpu.roll(z, LANES - ROPE_HALF, 1) * s1
                + pltpu.roll(z, ROPE_HALF, 1) * s2)

    for col in range(ATTN_WIDTH // LANES):
        z = proj[:, POOL_WIDTH + col * LANES:POOL_WIDTH + (col + 1) * LANES]
        q_ref[:, col * LANES:(col + 1) * LANES] = (rope(z) * Q_SCALE).astype(BF16)
    low_half = lax.broadcasted_iota(jnp.int32, (1, LANES), 1) < HEAD_DIM
    k0 = POOL_WIDTH + ATTN_WIDTH
    for a in range(KV_WIDTH // LANES):
        z = rope(proj[:, k0 + a * LANES:k0 + (a + 1) * LANES])
        k_ref[:, a * LANES:(a + 1) * LANES] = z
        kk_ref[:, (2 * a) * LANES:(2 * a + 1) * LANES] = jnp.where(low_half, z, 0.0).astype(BF16)
        kk_ref[:, (2 * a + 1) * LANES:(2 * a + 2) * LANES] = jnp.where(low_half, 0.0, z).astype(BF16)
    v = proj[:, k0 + KV_WIDTH:]
    v_ref[...] = v
    vb_ref[...] = v.astype(BF16)


def _proj(x, g, w_in, tabs, n_tab_blocks):
    t = x.shape[0]
    tm = TOKEN_TILE
    row = lambda w: pl.BlockSpec((tm, w), lambda i: (i, 0))
    tab = pl.BlockSpec((tm, LANES), lambda i: (i % n_tab_blocks, 0))
    full = lambda a: pl.BlockSpec(a.shape, lambda i: (0,) * a.ndim)
    return pl.pallas_call(
        _proj_kernel,
        grid=(t // tm,),
        in_specs=[row(D_MODEL), full(g), full(w_in), tab, tab, tab],
        out_specs=[row(POOL_WIDTH), row(ATTN_WIDTH), row(KV_WIDTH), row(KV_WIDTH),
                   row(2 * KV_WIDTH), row(KV_WIDTH)],
        out_shape=[jax.ShapeDtypeStruct((t, POOL_WIDTH), F32),
                   jax.ShapeDtypeStruct((t, ATTN_WIDTH), BF16),
                   jax.ShapeDtypeStruct((t, KV_WIDTH), F32),
                   jax.ShapeDtypeStruct((t, KV_WIDTH), F32),
                   jax.ShapeDtypeStruct((t, 2 * KV_WIDTH), BF16),
                   jax.ShapeDtypeStruct((t, KV_WIDTH), BF16)],
        compiler_params=_params("parallel"),
        name="proj",
    )(x, g, w_in, *tabs)


def _sink_softmax(s, mask, sink):
    s = jnp.where(mask, s, -jnp.inf)
    m = jnp.maximum(jnp.max(s, axis=1, keepdims=True), sink)
    p = jnp.exp(s - m)
    den = jnp.sum(p, axis=1, keepdims=True) + jnp.exp(sink - m)
    return p / den


def _mix_prompt_kernel(sink_ref, u_ref, halo_ref, q_ref, kkc_ref, kkp_ref, vc_ref, vp_ref,
                       wpool_ref, pscale_ref, mix_ref, ext_ref, lvl_ref, kwin_ref, vwin_ref):
    tq = u_ref.shape[0]
    i = pl.program_id(1)
    first = i == 0

    top = POOL_TOP
    ext_ref[0:top - HALO, :] = jnp.zeros((top - HALO, POOL_WIDTH), F32)
    ext_ref[top - HALO:top, :] = halo_ref[...] * jnp.where(first, 0.0, 1.0)
    ext_ref[top:, :] = u_ref[...]
    lvl_ref[:, 0:SUBLANES, :] = jnp.zeros((2, SUBLANES, POOL_GROUP), F32)
    nlive = top + tq - SUBLANES
    pos1 = i * tq + lax.broadcasted_iota(jnp.int32, (tq, 1), 0) + 1
    for gi, w in enumerate(POOL_WINDOWS):
        cols = slice(gi * POOL_GROUP, (gi + 1) * POOL_GROUP)
        src = ext_ref.at[:, cols]
        shift, slot = 1, 0
        while shift < w:
            dst = lvl_ref.at[slot]
            dst[SUBLANES:, :] = (src[SUBLANES:SUBLANES + nlive, :]
                                 + src[SUBLANES - shift:SUBLANES - shift + nlive, :])
            src, shift, slot = dst, 2 * shift, 1 - slot
        tok = ext_ref[top:, cols]
        inv_cnt = 1.0 / jnp.minimum(pos1, w).astype(F32)
        d = (src[top:, :] * inv_cnt - tok).astype(BF16)
        po = jnp.dot(d, wpool_ref[gi], preferred_element_type=F32) * pscale_ref[:, cols]
        mix_ref[:, cols] = po.astype(BF16)

    kwin_ref[0:BLOCK, :] = kkp_ref[...]
    kwin_ref[BLOCK:, :] = kkc_ref[...]
    vwin_ref[0:BLOCK, :] = vp_ref[...]
    vwin_ref[BLOCK:, :] = vc_ref[...]
    qi = lax.broadcasted_iota(jnp.int32, (BLOCK, 2 * BLOCK), 0)
    kj = lax.broadcasted_iota(jnp.int32, (BLOCK, 2 * BLOCK), 1)
    band = (kj >= qi) & (kj <= qi + WINDOW)
    first_lo = jnp.where(first, BLOCK, 0)
    low_half = lax.broadcasted_iota(jnp.int32, (1, LANES), 1) < HEAD_DIM
    for n in range(tq // BLOCK):
        mask = (band & (kj >= first_lo)) if n == 0 else band
        rows = slice(n * BLOCK, (n + 1) * BLOCK)
        keys = slice(n * BLOCK, (n + 2) * BLOCK)
        for a in range(N_KV_HEADS // 2):
            qs = jnp.concatenate(
                [q_ref[rows, (GQA_GROUP * a + j) * LANES:(GQA_GROUP * a + j + 1) * LANES]
                 for j in range(GQA_GROUP)], axis=0)
            vv = vwin_ref[keys, a * LANES:(a + 1) * LANES]
            outs = []
            for par in range(2):
                g = 2 * a + par
                kk = kwin_ref[keys, g * LANES:(g + 1) * LANES]
                s_all = lax.dot_general(qs, kk, (((1,), (1,)), ((), ())),
                                        preferred_element_type=F32)
                ps, rs = [], []
                for j in range(GQA_GROUP):
                    sink = sink_ref[GQA_GROUP * g + j]
                    for c in range(BLOCK // SOFTMAX_ROWS):
                        r0 = c * SOFTMAX_ROWS
                        s = jnp.where(mask[r0:r0 + SOFTMAX_ROWS],
                                      s_all[j * BLOCK + r0:j * BLOCK + r0 + SOFTMAX_ROWS], -jnp.inf)
                        m = jnp.max(s, axis=1, keepdims=True)
                        p = jnp.exp(s - m)
                        den = jnp.sum(p, axis=1, keepdims=True) + jnp.exp(sink - m)
                        ps.append(p.astype(BF16))
                        rs.append(1.0 / den)
                pv = jnp.dot(jnp.concatenate(ps, axis=0), vv, preferred_element_type=F32)
                outs.append(pv * jnp.concatenate(rs, axis=0))
            o = jnp.where(low_half, outs[0], outs[1]).astype(BF16)
            for j in range(GQA_GROUP):
                col = POOL_WIDTH + (GQA_GROUP * a + j) * LANES
                mix_ref[rows, col:col + LANES] = o[j * BLOCK:(j + 1) * BLOCK]


def _mix_prompt(sinks, u, q, kk, vb, wpool, pscale, batch, seq):
    tq = TOKEN_TILE
    nt = seq // tq
    bpt = tq // BLOCK
    hpt = tq // HALO
    row = lambda w: pl.BlockSpec((tq, w), lambda b, i: (b * nt + i, 0))
    prev = lambda w: pl.BlockSpec(
        (BLOCK, w), lambda b, i: (jnp.maximum((b * nt + i) * bpt - 1, 0), 0))
    halo = pl.BlockSpec((HALO, POOL_WIDTH), lambda b, i: (jnp.maximum((b * nt + i) * hpt - 1, 0), 0))
    full = lambda a: pl.BlockSpec(a.shape, lambda b, i: (0,) * a.ndim)
    return pl.pallas_call(
        _mix_prompt_kernel,
        grid=(batch, nt),
        in_specs=[pl.BlockSpec(memory_space=pltpu.SMEM), row(POOL_WIDTH), halo, row(ATTN_WIDTH),
                  row(2 * KV_WIDTH), prev(2 * KV_WIDTH), row(KV_WIDTH), prev(KV_WIDTH),
                  full(wpool), full(pscale)],
        out_specs=row(MIX_WIDTH),
        out_shape=jax.ShapeDtypeStruct((batch * seq, MIX_WIDTH), BF16),
        scratch_shapes=[pltpu.VMEM((POOL_TOP + tq, POOL_WIDTH), F32),
                        pltpu.VMEM((2, POOL_TOP + tq, POOL_GROUP), F32),
                        pltpu.VMEM((BLOCK + tq, 2 * KV_WIDTH), BF16),
                        pltpu.VMEM((BLOCK + tq, KV_WIDTH), BF16)],
        compiler_params=_params("parallel", "parallel"),
        name="mix_prompt",
    )(sinks, u, u, q, kk, kk, vb, vb, wpool, pscale)


def _mix_sample_kernel(sink_ref, u_ref, q_ref, kn_ref, vn_ref, st_ref, ck_ref, cv_ref,
                       wpool_ref, pscale_ref, mix_ref, npool_ref, nk_ref, nv_ref,
                       kall_ref, vall_ref, *, dec_seq):
    sb = SEQ_BLOCK
    rows_per_seq = N_HEADS * dec_seq
    nrow = sb * rows_per_seq
    ntok = sb * dec_seq
    win = ck_ref.shape[1]

    @pl.when(pl.program_id(0) == 0)
    def _():
        kall_ref[...] = jnp.zeros_like(kall_ref)
        vall_ref[...] = jnp.zeros_like(vall_ref)

    ext = [st_ref[h] for h in range(POOL_HIST)]
    ext += [u_ref[t] for t in range(dec_seq)]
    for h in range(POOL_HIST):
        npool_ref[h] = ext[h + dec_seq]
    r_out = lax.broadcasted_iota(jnp.int32, (ntok, ntok), 0)
    r_in = lax.broadcasted_iota(jnp.int32, (ntok, ntok), 1)
    to_seq_major = ((r_in % sb) * dec_seq + r_in // sb == r_out).astype(BF16)
    for gi, w in enumerate(POOL_WINDOWS):
        cols = slice(gi * POOL_GROUP, (gi + 1) * POOL_GROUP)
        ds = []
        for t in range(dec_seq):
            tok = ext[POOL_HIST + t][:, cols]
            acc = tok
            for j in range(1, w):
                acc = acc + ext[POOL_HIST + t - j][:, cols]
            ds.append(acc * (1.0 / w) - tok)
        d = jnp.concatenate(ds, axis=0).astype(BF16)
        po = jnp.dot(d, wpool_ref[gi], preferred_element_type=F32) * pscale_ref[:, cols]
        po = jnp.dot(to_seq_major, po.astype(BF16), preferred_element_type=F32)
        mix_ref[:, cols] = po.astype(BF16)

    for b in range(sb):
        tok_rows = slice(b * dec_seq, (b + 1) * dec_seq)
        kall_ref[b, 0:win, :] = ck_ref[b]
        kall_ref[b, win:win + dec_seq, :] = kn_ref[tok_rows, :]
        vall_ref[b, 0:win, :] = cv_ref[b]
        vall_ref[b, win:win + dec_seq, :] = vn_ref[tok_rows, :]
        nk_ref[b] = kall_ref[b, dec_seq:dec_seq + win, :]
        nv_ref[b] = vall_ref[b, dec_seq:dec_seq + win, :]

    r = lax.broadcasted_iota(jnp.int32, (nrow, ntok), 0)
    c = lax.broadcasted_iota(jnp.int32, (nrow, ntok), 1)
    pick = (c == (r // rows_per_seq) * dec_seq + r % dec_seq)
    qrep = jnp.dot(pick.astype(BF16), q_ref[...], preferred_element_type=F32)
    rr = lax.broadcasted_iota(jnp.int32, (nrow, LANES), 0) % rows_per_seq
    row_g = rr // (GQA_GROUP * dec_seq)
    row_j = (rr // dec_seq) % GQA_GROUP
    lane_par = lax.broadcasted_iota(jnp.int32, (nrow, LANES), 1) // HEAD_DIM
    sel = [[(row_g // 2 == a) & (row_j == j) & (row_g % 2 == lane_par)
            for j in range(GQA_GROUP)] for a in range(2)]
    lhs_cols = []
    for a in range(2):
        acc = jnp.zeros((nrow, LANES), F32)
        for j in range(GQA_GROUP):
            col = GQA_GROUP * a + j
            acc = acc + jnp.where(sel[a][j], qrep[:, col * LANES:(col + 1) * LANES], 0.0)
        lhs_cols.append(acc)
    lhs = jnp.concatenate(lhs_cols, axis=1).astype(BF16)
    scores = []
    for b in range(sb):
        rows = slice(b * rows_per_seq, (b + 1) * rows_per_seq)
        scores.append(lax.dot_general(lhs[rows], kall_ref[b].astype(BF16),
                                      (((1,), (1,)), ((), ())), preferred_element_type=F32))
    s = jnp.concatenate(scores, axis=0)
    t_row = lax.broadcasted_iota(jnp.int32, (nrow, SAMPLE_KEYS), 0) % dec_seq
    kj = lax.broadcasted_iota(jnp.int32, (nrow, SAMPLE_KEYS), 1)
    mask = (kj >= t_row + (win - WINDOW)) & (kj <= t_row + win)
    p = _sink_softmax(s, mask, sink_ref[...]).astype(BF16)
    outs = []
    for b in range(sb):
        rows = slice(b * rows_per_seq, (b + 1) * rows_per_seq)
        outs.append(jnp.dot(p[rows], vall_ref[b].astype(BF16), preferred_element_type=F32))
    o = jnp.concatenate(outs, axis=0)
    z = jnp.concatenate(
        [jnp.where(sel[a][j], o[:, a * LANES:(a + 1) * LANES], 0.0)
         for a in range(2) for j in range(GQA_GROUP)], axis=1).astype(BF16)
    rt = lax.broadcasted_iota(jnp.int32, (ntok, nrow), 0)
    ct = lax.broadcasted_iota(jnp.int32, (ntok, nrow), 1)
    unpick = (rt == (ct // rows_per_seq) * dec_seq + ct % dec_seq)
    attn = jnp.dot(unpick.astype(BF16), z, preferred_element_type=F32)
    mix_ref[:, POOL_WIDTH:] = attn.astype(BF16)


def _mix_sample(sink_rows, u, q, k_new, v_new, state_t, cache_k, cache_v, wpool, pscale, dec_seq):
    nseq, win, _ = cache_k.shape
    sb = SEQ_BLOCK
    ntok = sb * dec_seq
    row = lambda w: pl.BlockSpec((ntok, w), lambda i: (i, 0))
    slab = pl.BlockSpec((POOL_HIST, sb, POOL_WIDTH), lambda i: (0, i, 0))
    u_slab = pl.BlockSpec((dec_seq, sb, POOL_WIDTH), lambda i: (0, i, 0))
    cache = pl.BlockSpec((sb, win, KV_WIDTH), lambda i: (i, 0, 0))
    full = lambda a: pl.BlockSpec(a.shape, lambda i: (0,) * a.ndim)
    return pl.pallas_call(
        functools.partial(_mix_sample_kernel, dec_seq=dec_seq),
        grid=(nseq // sb,),
        in_specs=[full(sink_rows), u_slab, row(ATTN_WIDTH), row(KV_WIDTH), row(KV_WIDTH),
                  slab, cache, cache, full(wpool), full(pscale)],
        out_specs=[row(MIX_WIDTH), slab, cache, cache],
        out_shape=[jax.ShapeDtypeStruct((nseq * dec_seq, MIX_WIDTH), BF16),
                   jax.ShapeDtypeStruct((POOL_HIST, nseq, POOL_WIDTH), F32),
                   jax.ShapeDtypeStruct(cache_k.shape, F32),
                   jax.ShapeDtypeStruct(cache_v.shape, F32)],
        scratch_shapes=[pltpu.VMEM((sb, SAMPLE_KEYS, KV_WIDTH), F32),
                        pltpu.VMEM((sb, SAMPLE_KEYS, KV_WIDTH), F32)],
        compiler_params=_params("arbitrary"),
        name="mix_sample",
    )(sink_rows, u, q, k_new, v_new, state_t, cache_k, cache_v, wpool, pscale)


def _outproj_kernel(x_ref, mix_ref, w_ref, g_ref, x1_ref, h2_ref):
    x1 = x_ref[...] + jnp.dot(mix_ref[...], w_ref[...], preferred_element_type=F32)
    x1_ref[...] = x1
    h2_ref[...] = _rms(x1, g_ref[...]).astype(BF16)


def _outproj(x, mix, w_out, g):
    t = x.shape[0]
    tm = TOKEN_TILE
    row = lambda w: pl.BlockSpec((tm, w), lambda i: (i, 0))
    full = lambda a: pl.BlockSpec(a.shape, lambda i: (0,) * a.ndim)
    return pl.pallas_call(
        _outproj_kernel,
        grid=(t // tm,),
        in_specs=[row(D_MODEL), row(MIX_WIDTH), full(w_out), full(g)],
        out_specs=[row(D_MODEL), row(D_MODEL)],
        out_shape=[jax.ShapeDtypeStruct((t, D_MODEL), F32),
                   jax.ShapeDtypeStruct((t, D_MODEL), BF16)],
        compiler_params=_params("parallel"),
        name="outproj",
    )(x, mix, w_out, g)


def _ffn_kernel(x1_ref, h2_ref, wg_ref, wu_ref, wd_ref, gf_ref, y_ref, acc_ref, *, final_norm):
    j = pl.program_id(1)

    @pl.when(j == 0)
    def _():
        acc_ref[...] = x1_ref[...]

    h = h2_ref[...]
    gate = jnp.dot(h, wg_ref[...], preferred_element_type=F32)
    up = jnp.dot(h, wu_ref[...], preferred_element_type=F32)
    act = (jax.nn.silu(gate) * up).astype(BF16)
    acc_ref[...] += jnp.dot(act, wd_ref[...], preferred_element_type=F32)

    @pl.when(j == pl.num_programs(1) - 1)
    def _():
        y = acc_ref[...]
        y_ref[...] = _rms(y, gf_ref[...]) if final_norm else y


def _ffn(x1, h2, wg, wu, wd, gf, final_norm):
    t = x1.shape[0]
    tm, tf = TOKEN_TILE, FF_TILE
    row = pl.BlockSpec((tm, D_MODEL), lambda i, j: (i, 0))
    return pl.pallas_call(
        functools.partial(_ffn_kernel, final_norm=final_norm),
        grid=(t // tm, D_FF // tf),
        in_specs=[row, row,
                  pl.BlockSpec((D_MODEL, tf), lambda i, j: (0, j)),
                  pl.BlockSpec((D_MODEL, tf), lambda i, j: (0, j)),
                  pl.BlockSpec((tf, D_MODEL), lambda i, j: (j, 0)),
                  pl.BlockSpec(gf.shape, lambda i, j: (0, 0))],
        out_specs=row,
        out_shape=jax.ShapeDtypeStruct((t, D_MODEL), F32),
        scratch_shapes=[pltpu.VMEM((tm, D_MODEL), F32)],
        compiler_params=_params("parallel", "arbitrary"),
        name="ffn",
    )(x1, h2, wg, wu, wd, gf)


def _rope_tables(pos):
    inv = ROPE_THETA ** (-jnp.arange(0, ROPE_DIM, 2, dtype=F32) / ROPE_DIM)
    ang = pos.astype(F32)[:, None] * inv[None, :]
    cos, sin = jnp.cos(ang), jnp.sin(ang)
    n = pos.shape[0]
    rest = HEAD_DIM - ROPE_DIM
    one, zero = jnp.ones((n, rest), F32), jnp.zeros((n, rest), F32)
    zh = jnp.zeros((n, ROPE_HALF), F32)
    reps = LANES // HEAD_DIM
    c = jnp.tile(jnp.concatenate([cos, cos, one], axis=1), (1, reps))
    s1 = jnp.tile(jnp.concatenate([-sin, zh, zero], axis=1), (1, reps))
    s2 = jnp.tile(jnp.concatenate([zh, sin, zero], axis=1), (1, reps))
    return c, s1, s2


def _to_column_order(w, axis):
    shape = w.shape
    split = shape[:axis] + (2, 2, GQA_GROUP, HEAD_DIM) + shape[axis + 1:]
    perm = list(range(len(split)))
    perm[axis + 1], perm[axis + 2] = perm[axis + 2], perm[axis + 1]
    return w.reshape(split).transpose(perm).reshape(shape)


def kernel(x_prompt, x_sample, state_pool, cache_k_win, cache_v_win, g_mix, w_in, w_pool,
           pool_scale, attn_sinks, w_out, g_ffn, w_gate, w_up, w_down, g_final):
    batch, seq, _ = x_prompt.shape
    nseq, dec_seq, _ = x_sample.shape
    depth = w_in.shape[0]
    win_s = cache_k_win.shape[2]
    assert seq % TOKEN_TILE == 0 and (nseq * dec_seq) % TOKEN_TILE == 0 and nseq % SEQ_BLOCK == 0
    assert win_s == WINDOW and win_s + dec_seq <= SAMPLE_KEYS and seq >= WINDOW
    assert PAST_LEN >= max(POOL_HIST, WINDOW)

    xp = x_prompt.reshape(batch * seq, D_MODEL)
    xs = x_sample.reshape(nseq * dec_seq, D_MODEL)
    tabs_p = _rope_tables(jnp.arange(seq))
    tabs_s = _rope_tables(PAST_LEN + jnp.arange(nseq * dec_seq) % dec_seq)
    gf = g_final.reshape(1, D_MODEL)

    outs = [[] for _ in range(6)]
    for l in range(depth):
        w_in_l = jnp.concatenate(
            [w_in[l][:, :POOL_WIDTH],
             _to_column_order(w_in[l][:, POOL_WIDTH:POOL_WIDTH + ATTN_WIDTH], 1),
             w_in[l][:, POOL_WIDTH + ATTN_WIDTH:]], axis=1).astype(BF16)
        w_out_l = jnp.concatenate(
            [w_out[l][:POOL_WIDTH], _to_column_order(w_out[l][POOL_WIDTH:], 0)], axis=0).astype(BF16)
        wpool_l = w_pool[l].astype(BF16)
        wg_l, wu_l, wd_l = w_gate[l].astype(BF16), w_up[l].astype(BF16), w_down[l].astype(BF16)
        gm = g_mix[l].reshape(1, D_MODEL)
        gn = g_ffn[l].reshape(1, D_MODEL)
        pscale = pool_scale[l].reshape(1, POOL_WIDTH)
        sinks = attn_sinks[l].reshape(N_KV_HEADS, GQA_GROUP)
        sink_rows = jnp.tile(jnp.repeat(sinks.reshape(-1), dec_seq), SEQ_BLOCK)[:, None]

        u, q, k, v, kk, vb = _proj(xp, gm, w_in_l, tabs_p, seq // TOKEN_TILE)
        mix = _mix_prompt(attn_sinks[l], u, q, kk, vb, wpool_l, pscale, batch, seq)
        x1, h2 = _outproj(xp, mix, w_out_l, gn)
        xp = _ffn(x1, h2, wg_l, wu_l, wd_l, gf, l == depth - 1)
        outs[0].append(u.reshape(batch, seq, POOL_WIDTH)[:, -POOL_HIST:])
        outs[1].append(k.reshape(batch, seq, N_KV_HEADS, HEAD_DIM)[:, -WINDOW:])
        outs[2].append(v.reshape(batch, seq, N_KV_HEADS, HEAD_DIM)[:, -WINDOW:])

        u, q, k, v, _, _ = _proj(xs, gm, w_in_l, tabs_s, 1)
        state_t = jnp.transpose(state_pool[l], (1, 0, 2))
        ck = cache_k_win[l].reshape(nseq, win_s, KV_WIDTH)
        cv = cache_v_win[l].reshape(nseq, win_s, KV_WIDTH)
        u_t = jnp.transpose(u.reshape(nseq, dec_seq, POOL_WIDTH), (1, 0, 2))
        mix, npool, nk, nv = _mix_sample(sink_rows, u_t, q, k, v, state_t, ck, cv, wpool_l, pscale,
                                         dec_seq)
        x1, h2 = _outproj(xs, mix, w_out_l, gn)
        xs = _ffn(x1, h2, wg_l, wu_l, wd_l, gf, l == depth - 1)
        outs[3].append(jnp.transpose(npool, (1, 0, 2)))
        outs[4].append(nk.reshape(nseq, win_s, N_KV_HEADS, HEAD_DIM))
        outs[5].append(nv.reshape(nseq, win_s, N_KV_HEADS, HEAD_DIM))

    y_prompt = xp.reshape(batch, seq, D_MODEL)
    y_sample = xs.reshape(nseq, dec_seq, D_MODEL)
    return (y_prompt, y_sample) + tuple(jnp.stack(o) for o in outs)
```

```python
import functools

import jax
import jax.numpy as jnp
from jax import lax
from jax.experimental import pallas as pl
from jax.experimental.pallas import tpu as pltpu

F32 = jnp.float32
BF16 = jnp.bfloat16

D_MODEL = 2048
POOL_WIDTH = D_MODEL // 2
POOL_WINDOWS = (2, 4, 8, 16)
N_POOL_GROUPS = len(POOL_WINDOWS)
POOL_GROUP = POOL_WIDTH // N_POOL_GROUPS
POOL_HIST = max(POOL_WINDOWS) - 1
HEAD_DIM = 64
N_HEADS = (D_MODEL - POOL_WIDTH) // HEAD_DIM
N_KV_HEADS = 4
GQA_GROUP = N_HEADS // N_KV_HEADS
ATTN_WIDTH = N_HEADS * HEAD_DIM
KV_WIDTH = N_KV_HEADS * HEAD_DIM
IN_WIDTH = POOL_WIDTH + ATTN_WIDTH + 2 * KV_WIDTH
MIX_WIDTH = POOL_WIDTH + ATTN_WIDTH
WINDOW = 128
BLOCK = 128
ROPE_DIM = HEAD_DIM // 4
ROPE_HALF = ROPE_DIM // 2
ROPE_THETA = 500000.0
D_FF = ((8 * D_MODEL // 3 + 255) // 256) * 256
EPS = 1e-5
PAST_LEN = 16384
Q_SCALE = HEAD_DIM ** -0.5

LANES = 128
SUBLANES = 8
HEADS_PER_COL = LANES // HEAD_DIM
HALO = 16
POOL_TOP = 2 * HALO
TOKEN_TILE = 512
FF_TILE = 512
SEQ_BLOCK = 8
SAMPLE_KEYS = 256
VMEM_LIMIT = 56 * 1024 * 1024


def _params(*semantics):
    return pltpu.CompilerParams(dimension_semantics=semantics, vmem_limit_bytes=VMEM_LIMIT)


def _rms(x, g):
    ms = jnp.mean(x * x, axis=-1, keepdims=True)
    return x * lax.rsqrt(ms + EPS) * g


def _swap_halves(z):
    return pltpu.roll(z, HEAD_DIM, 1)


def _proj_kernel(x_ref, g_ref, w_ref, c_ref, s1_ref, s2_ref,
                 u_ref, q_ref, kd_ref, vt_ref, utail_ref, ktail_ref, vtail_ref):
    tm = x_ref.shape[0]
    tail = ktail_ref.shape[0]
    h = _rms(x_ref[...], g_ref[...]).astype(BF16)
    proj = jnp.dot(h, w_ref[...], preferred_element_type=F32)
    u = proj[:, :POOL_WIDTH]
    u_ref[...] = u
    utail_ref[...] = u[tm - HALO:, :]
    c, s1, s2 = c_ref[...], s1_ref[...], s2_ref[...]

    def rope(z):
        return (z * c + pltpu.roll(z, LANES - ROPE_HALF, 1) * s1
                + pltpu.roll(z, ROPE_HALF, 1) * s2)

    for col in range(ATTN_WIDTH // LANES):
        z = proj[:, POOL_WIDTH + col * LANES:POOL_WIDTH + (col + 1) * LANES]
        q_ref[:, col * LANES:(col + 1) * LANES] = (rope(z) * Q_SCALE).astype(BF16)
    low_half = lax.broadcasted_iota(jnp.int32, (1, LANES), 1) < HEAD_DIM
    k0 = POOL_WIDTH + ATTN_WIDTH
    for a in range(KV_WIDTH // LANES):
        z = rope(proj[:, k0 + a * LANES:k0 + (a + 1) * LANES])
        ktail_ref[:, a * LANES:(a + 1) * LANES] = z[tm - tail:, :]
        zr = _swap_halves(z)
        kd_ref[:, (2 * a) * LANES:(2 * a + 1) * LANES] = jnp.where(low_half, z, zr).astype(BF16)
        kd_ref[:, (2 * a + 1) * LANES:(2 * a + 2) * LANES] = jnp.where(low_half, zr, z).astype(BF16)
    v = proj[:, k0 + KV_WIDTH:]
    vtail_ref[...] = v[tm - tail:, :]
    vt_ref[...] = v.T.astype(BF16)


def _proj(x, g, w_in, tabs, tiles_per_seq, tail):
    t = x.shape[0]
    tm = TOKEN_TILE
    nseq = t // (tm * tiles_per_seq)
    row = lambda w: pl.BlockSpec((tm, w), lambda i: (i, 0))
    tab = pl.BlockSpec((tm, LANES), lambda i: (i % tiles_per_seq, 0))
    full = lambda a: pl.BlockSpec(a.shape, lambda i: (0,) * a.ndim)
    per_seq = lambda r, w: pl.BlockSpec((r, w), lambda i: (i // tiles_per_seq, 0))
    return pl.pallas_call(
        _proj_kernel,
        grid=(t // tm,),
        in_specs=[row(D_MODEL), full(g), full(w_in), tab, tab, tab],
        out_specs=[row(POOL_WIDTH), row(ATTN_WIDTH), row(2 * KV_WIDTH),
                   pl.BlockSpec((KV_WIDTH, tm), lambda i: (0, i)),
                   per_seq(HALO, POOL_WIDTH), per_seq(tail, KV_WIDTH), per_seq(tail, KV_WIDTH)],
        out_shape=[jax.ShapeDtypeStruct((t, POOL_WIDTH), F32),
                   jax.ShapeDtypeStruct((t, ATTN_WIDTH), BF16),
                   jax.ShapeDtypeStruct((t, 2 * KV_WIDTH), BF16),
                   jax.ShapeDtypeStruct((KV_WIDTH, t), BF16),
                   jax.ShapeDtypeStruct((nseq * HALO, POOL_WIDTH), F32),
                   jax.ShapeDtypeStruct((nseq * tail, KV_WIDTH), F32),
                   jax.ShapeDtypeStruct((nseq * tail, KV_WIDTH), F32)],
        compiler_params=_params("arbitrary"),
        name="proj",
    )(x, g, w_in, *tabs)


def _mix_prompt_kernel(sink_ref, u_ref, halo_ref, q_ref, kdc_ref, kdp_ref, vtc_ref, vtp_ref,
                       wpool_ref, pscale_ref, mix_ref, ext_ref, lvl_ref, kwin_ref, vtwin_ref, bias_ref):
    tq = u_ref.shape[0]
    i = pl.program_id(1)
    first = i == 0

    top = POOL_TOP
    ext_ref[0:top - HALO, :] = jnp.zeros((top - HALO, POOL_WIDTH), F32)
    ext_ref[top - HALO:top, :] = halo_ref[...] * jnp.where(first, 0.0, 1.0)
    ext_ref[top:, :] = u_ref[...]
    lvl_ref[:, 0:SUBLANES, :] = jnp.zeros((2, SUBLANES, POOL_GROUP), F32)
    nlive = top + tq - SUBLANES
    pos1 = i * tq + lax.broadcasted_iota(jnp.int32, (tq, 1), 0) + 1
    for gi, w in enumerate(POOL_WINDOWS):
        cols = slice(gi * POOL_GROUP, (gi + 1) * POOL_GROUP)
        src = ext_ref.at[:, cols]
        shift, slot = 1, 0
        while shift < w:
            dst = lvl_ref.at[slot]
            dst[SUBLANES:, :] = (src[SUBLANES:SUBLANES + nlive, :]
                                 + src[SUBLANES - shift:SUBLANES - shift + nlive, :])
            src, shift, slot = dst, 2 * shift, 1 - slot
        tok = ext_ref[top:, cols]
        inv_cnt = 1.0 / jnp.minimum(pos1, w).astype(F32)
        d = (src[top:, :] * inv_cnt - tok).astype(BF16)
        po = jnp.dot(d, wpool_ref[gi], preferred_element_type=F32) * pscale_ref[:, cols]
        mix_ref[:, cols] = po.astype(BF16)

    kwin_ref[0:BLOCK, :] = kdp_ref[...]
    kwin_ref[BLOCK:, :] = kdc_ref[...]
    vtwin_ref[:, 0:BLOCK] = vtp_ref[...]
    vtwin_ref[:, BLOCK:] = vtc_ref[...]
    kj = lax.broadcasted_iota(jnp.int32, (2 * BLOCK, BLOCK), 0)
    qi = lax.broadcasted_iota(jnp.int32, (2 * BLOCK, BLOCK), 1)
    band = (kj >= qi) & (kj <= qi + WINDOW)
    first_lo = jnp.where(first, BLOCK, 0)
    bias_ref[0] = jnp.where(band, 0.0, -jnp.inf)
    bias_ref[1] = jnp.where(band & (kj >= first_lo), 0.0, -jnp.inf)
    lane = lax.broadcasted_iota(jnp.int32, (1, LANES), 1)
    keep_half = [(lane // HEAD_DIM == hh).astype(BF16) for hh in range(HEADS_PER_COL)]

    def scores(n, g):
        rows = slice(n * BLOCK, (n + 1) * BLOCK)
        qs = jnp.concatenate(
            [q_ref[rows, (h // HEADS_PER_COL) * LANES:(h // HEADS_PER_COL + 1) * LANES]
             * keep_half[h % HEADS_PER_COL]
             for h in range(GQA_GROUP * g, GQA_GROUP * (g + 1))], axis=0)
        kd = kwin_ref[n * BLOCK:(n + 2) * BLOCK, g * LANES:(g + 1) * LANES]
        return lax.dot_general(kd, qs, (((1,), (1,)), ((), ())), preferred_element_type=F32)

    def attend(n, g, s_t):
        p_t = []
        for j in range(GQA_GROUP):
            s = s_t[:, j * BLOCK:(j + 1) * BLOCK] + bias_ref[1 if n == 0 else 0]
            m = jnp.max(s, axis=0, keepdims=True)
            p = jnp.exp(s - m)
            den = jnp.sum(p, axis=0, keepdims=True) + jnp.exp(sink_ref[GQA_GROUP * g + j] - m)
            p_t.append((p * (1.0 / den)).astype(BF16))
        vt = vtwin_ref[g * HEAD_DIM:(g + 1) * HEAD_DIM, n * BLOCK:(n + 2) * BLOCK]
        o_t = jnp.dot(vt, jnp.concatenate(p_t, axis=1), preferred_element_type=F32)
        for c in range(GQA_GROUP // HEADS_PER_COL):
            col_t = jnp.concatenate(
                [o_t[:, (HEADS_PER_COL * c + hh) * BLOCK:(HEADS_PER_COL * c + hh + 1) * BLOCK]
                 for hh in range(HEADS_PER_COL)], axis=0)
            col = POOL_WIDTH + (GQA_GROUP // HEADS_PER_COL * g + c) * LANES
            mix_ref[n * BLOCK:(n + 1) * BLOCK, col:col + LANES] = col_t.T.astype(BF16)

    work = [(n, g) for n in range(tq // BLOCK) for g in range(N_KV_HEADS)]
    s_next = scores(*work[0])
    for idx, (n, g) in enumerate(work):
        s_t = s_next
        if idx + 1 < len(work):
            s_next = scores(*work[idx + 1])
        attend(n, g, s_t)


def _mix_prompt(sinks, u, q, kd, vt, wpool, pscale, batch, seq):
    tq = TOKEN_TILE
    nt = seq // tq
    bpt = tq // BLOCK
    hpt = tq // HALO
    prev_block = lambda b, i: jnp.maximum((b * nt + i) * bpt - 1, 0)
    row = lambda w: pl.BlockSpec((tq, w), lambda b, i: (b * nt + i, 0))
    halo = pl.BlockSpec((HALO, POOL_WIDTH), lambda b, i: (jnp.maximum((b * nt + i) * hpt - 1, 0), 0))
    full = lambda a: pl.BlockSpec(a.shape, lambda b, i: (0,) * a.ndim)
    return pl.pallas_call(
        _mix_prompt_kernel,
        grid=(batch, nt),
        in_specs=[pl.BlockSpec(memory_space=pltpu.SMEM), row(POOL_WIDTH), halo, row(ATTN_WIDTH),
                  row(2 * KV_WIDTH),
                  pl.BlockSpec((BLOCK, 2 * KV_WIDTH), lambda b, i: (prev_block(b, i), 0)),
                  pl.BlockSpec((KV_WIDTH, tq), lambda b, i: (0, b * nt + i)),
                  pl.BlockSpec((KV_WIDTH, BLOCK), lambda b, i: (0, prev_block(b, i))),
                  full(wpool), full(pscale)],
        out_specs=row(MIX_WIDTH),
        out_shape=jax.ShapeDtypeStruct((batch * seq, MIX_WIDTH), BF16),
        scratch_shapes=[pltpu.VMEM((POOL_TOP + tq, POOL_WIDTH), F32),
                        pltpu.VMEM((2, POOL_TOP + tq, POOL_GROUP), F32),
                        pltpu.VMEM((BLOCK + tq, 2 * KV_WIDTH), BF16),
                        pltpu.VMEM((KV_WIDTH, BLOCK + tq), BF16),
                        pltpu.VMEM((2, 2 * BLOCK, BLOCK), F32)],
        compiler_params=_params("parallel", "parallel"),
        name="mix_prompt",
    )(sinks, u, u, q, kd, kd, vt, vt, wpool, pscale)


def _sink_softmax(s, mask, sink):
    s = jnp.where(mask, s, -jnp.inf)
    m = jnp.maximum(jnp.max(s, axis=1, keepdims=True), sink)
    p = jnp.exp(s - m)
    den = jnp.sum(p, axis=1, keepdims=True) + jnp.exp(sink - m)
    return p / den


def _mix_sample_kernel(sink_ref, u_ref, q_ref, kn_ref, vn_ref, st_ref, ck_ref, cv_ref,
                       wpool_ref, pscale_ref, mix_ref, npool_ref, nk_ref, nv_ref,
                       kall_ref, vall_ref, *, dec_seq):
    sb = SEQ_BLOCK
    rows_per_seq = N_HEADS * dec_seq
    nrow = sb * rows_per_seq
    ntok = sb * dec_seq
    win = ck_ref.shape[1]

    @pl.when(pl.program_id(0) == 0)
    def _():
        kall_ref[...] = jnp.zeros_like(kall_ref)
        vall_ref[...] = jnp.zeros_like(vall_ref)

    ext = [st_ref[h] for h in range(POOL_HIST)]
    ext += [u_ref[t] for t in range(dec_seq)]
    for h in range(POOL_HIST):
        npool_ref[h] = ext[h + dec_seq]
    r_out = lax.broadcasted_iota(jnp.int32, (ntok, ntok), 0)
    r_in = lax.broadcasted_iota(jnp.int32, (ntok, ntok), 1)
    to_seq_major = ((r_in % sb) * dec_seq + r_in // sb == r_out).astype(BF16)
    for gi, w in enumerate(POOL_WINDOWS):
        cols = slice(gi * POOL_GROUP, (gi + 1) * POOL_GROUP)
        ds = []
        for t in range(dec_seq):
            tok = ext[POOL_HIST + t][:, cols]
            acc = tok
            for j in range(1, w):
                acc = acc + ext[POOL_HIST + t - j][:, cols]
            ds.append(acc * (1.0 / w) - tok)
        d = jnp.concatenate(ds, axis=0).astype(BF16)
        po = jnp.dot(d, wpool_ref[gi], preferred_element_type=F32) * pscale_ref[:, cols]
        po = jnp.dot(to_seq_major, po.astype(BF16), preferred_element_type=F32)
        mix_ref[:, cols] = po.astype(BF16)

    for b in range(sb):
        tok_rows = slice(b * dec_seq, (b + 1) * dec_seq)
        kall_ref[b, 0:win, :] = ck_ref[b]
        kall_ref[b, win:win + dec_seq, :] = kn_ref[tok_rows, :]
        vall_ref[b, 0:win, :] = cv_ref[b]
        vall_ref[b, win:win + dec_seq, :] = vn_ref[tok_rows, :]
        nk_ref[b] = kall_ref[b, dec_seq:dec_seq + win, :]
        nv_ref[b] = vall_ref[b, dec_seq:dec_seq + win, :]

    r = lax.broadcasted_iota(jnp.int32, (nrow, ntok), 0)
    c = lax.broadcasted_iota(jnp.int32, (nrow, ntok), 1)
    pick = (c == (r // rows_per_seq) * dec_seq + r % dec_seq)
    qrep = jnp.dot(pick.astype(BF16), q_ref[...], preferred_element_type=F32)
    rr = lax.broadcasted_iota(jnp.int32, (nrow, LANES), 0) % rows_per_seq
    row_g = rr // (GQA_GROUP * dec_seq)
    row_j = (rr // dec_seq) % GQA_GROUP
    lane_half = lax.broadcasted_iota(jnp.int32, (nrow, LANES), 1) // HEAD_DIM
    lhs_cols = [jnp.zeros((nrow, LANES), F32) for _ in range(KV_WIDTH // LANES)]
    for qc in range(ATTN_WIDTH // LANES):
        g = qc * HEADS_PER_COL // GQA_GROUP
        src = qrep[:, qc * LANES:(qc + 1) * LANES]
        src_swapped = _swap_halves(src)
        kv_col, kv_half = g // HEADS_PER_COL, g % HEADS_PER_COL
        for hh in range(HEADS_PER_COL):
            j = (qc * HEADS_PER_COL + hh) % GQA_GROUP
            here = (row_g == g) & (row_j == j) & (lane_half == kv_half)
            lhs_cols[kv_col] = lhs_cols[kv_col] + jnp.where(
                here, src if hh == kv_half else src_swapped, 0.0)
    lhs = jnp.concatenate(lhs_cols, axis=1).astype(BF16)
    scores = []
    for b in range(sb):
        rows = slice(b * rows_per_seq, (b + 1) * rows_per_seq)
        scores.append(lax.dot_general(lhs[rows], kall_ref[b].astype(BF16),
                                      (((1,), (1,)), ((), ())), preferred_element_type=F32))
    s = jnp.concatenate(scores, axis=0)
    t_row = lax.broadcasted_iota(jnp.int32, (nrow, SAMPLE_KEYS), 0) % dec_seq
    kj = lax.broadcasted_iota(jnp.int32, (nrow, SAMPLE_KEYS), 1)
    mask = (kj >= t_row + (win - WINDOW)) & (kj <= t_row + win)
    p = _sink_softmax(s, mask, sink_ref[...]).astype(BF16)
    outs = []
    for b in range(sb):
        rows = slice(b * rows_per_seq, (b + 1) * rows_per_seq)
        outs.append(jnp.dot(p[rows], vall_ref[b].astype(BF16), preferred_element_type=F32))
    o = jnp.concatenate(outs, axis=0)
    z_cols = []
    for qc in range(ATTN_WIDTH // LANES):
        g = qc * HEADS_PER_COL // GQA_GROUP
        kv_col, kv_half = g // HEADS_PER_COL, g % HEADS_PER_COL
        src = o[:, kv_col * LANES:(kv_col + 1) * LANES]
        src_swapped = _swap_halves(src)
        zc = jnp.zeros((nrow, LANES), F32)
        for hh in range(HEADS_PER_COL):
            j = (qc * HEADS_PER_COL + hh) % GQA_GROUP
            here = (row_g == g) & (row_j == j) & (lane_half == hh)
            zc = zc + jnp.where(here, src if hh == kv_half else src_swapped, 0.0)
        z_cols.append(zc)
    z = jnp.concatenate(z_cols, axis=1).astype(BF16)
    rt = lax.broadcasted_iota(jnp.int32, (ntok, nrow), 0)
    ct = lax.broadcasted_iota(jnp.int32, (ntok, nrow), 1)
    unpick = (rt == (ct // rows_per_seq) * dec_seq + ct % dec_seq)
    attn = jnp.dot(unpick.astype(BF16), z, preferred_element_type=F32)
    mix_ref[:, POOL_WIDTH:] = attn.astype(BF16)


def _mix_sample(sink_rows, u, q, k_new, v_new, state_t, cache_k, cache_v, wpool, pscale, dec_seq):
    nseq, win, _ = cache_k.shape
    sb = SEQ_BLOCK
    ntok = sb * dec_seq
    row = lambda w: pl.BlockSpec((ntok, w), lambda i: (i, 0))
    slab = pl.BlockSpec((POOL_HIST, sb, POOL_WIDTH), lambda i: (0, i, 0))
    u_slab = pl.BlockSpec((dec_seq, sb, POOL_WIDTH), lambda i: (0, i, 0))
    cache = pl.BlockSpec((sb, win, KV_WIDTH), lambda i: (i, 0, 0))
    full = lambda a: pl.BlockSpec(a.shape, lambda i: (0,) * a.ndim)
    return pl.pallas_call(
        functools.partial(_mix_sample_kernel, dec_seq=dec_seq),
        grid=(nseq // sb,),
        in_specs=[full(sink_rows), u_slab, row(ATTN_WIDTH), row(KV_WIDTH), row(KV_WIDTH),
                  slab, cache, cache, full(wpool), full(pscale)],
        out_specs=[row(MIX_WIDTH), slab, cache, cache],
        out_shape=[jax.ShapeDtypeStruct((nseq * dec_seq, MIX_WIDTH), BF16),
                   jax.ShapeDtypeStruct((POOL_HIST, nseq, POOL_WIDTH), F32),
                   jax.ShapeDtypeStruct(cache_k.shape, F32),
                   jax.ShapeDtypeStruct(cache_v.shape, F32)],
        scratch_shapes=[pltpu.VMEM((sb, SAMPLE_KEYS, KV_WIDTH), F32),
                        pltpu.VMEM((sb, SAMPLE_KEYS, KV_WIDTH), F32)],
        compiler_params=_params("arbitrary"),
        name="mix_sample",
    )(sink_rows, u, q, k_new, v_new, state_t, cache_k, cache_v, wpool, pscale)


def _outproj_kernel(x_ref, mix_ref, w_ref, g_ref, x1_ref, h2_ref):
    x1 = x_ref[...] + jnp.dot(mix_ref[...], w_ref[...], preferred_element_type=F32)
    x1_ref[...] = x1
    h2_ref[...] = _rms(x1, g_ref[...]).astype(BF16)


def _outproj(x, mix, w_out, g):
    t = x.shape[0]
    tm = TOKEN_TILE
    row = lambda w: pl.BlockSpec((tm, w), lambda i: (i, 0))
    full = lambda a: pl.BlockSpec(a.shape, lambda i: (0,) * a.ndim)
    return pl.pallas_call(
        _outproj_kernel,
        grid=(t // tm,),
        in_specs=[row(D_MODEL), row(MIX_WIDTH), full(w_out), full(g)],
        out_specs=[row(D_MODEL), row(D_MODEL)],
        out_shape=[jax.ShapeDtypeStruct((t, D_MODEL), F32),
                   jax.ShapeDtypeStruct((t, D_MODEL), BF16)],
        compiler_params=_params("parallel"),
        name="outproj",
    )(x, mix, w_out, g)


def _ffn_kernel(x1_ref, h2_ref, wg_ref, wu_ref, wd_ref, gf_ref, y_ref, acc_ref, *, final_norm):
    j = pl.program_id(1)

    @pl.when(j == 0)
    def _():
        acc_ref[...] = x1_ref[...]

    h = h2_ref[...]
    gate = jnp.dot(h, wg_ref[...], preferred_element_type=F32)
    up = jnp.dot(h, wu_ref[...], preferred_element_type=F32)
    act = (jax.nn.silu(gate) * up).astype(BF16)
    acc_ref[...] += jnp.dot(act, wd_ref[...], preferred_element_type=F32)

    @pl.when(j == pl.num_programs(1) - 1)
    def _():
        y = acc_ref[...]
        y_ref[...] = _rms(y, gf_ref[...]) if final_norm else y


def _ffn(x1, h2, wg, wu, wd, gf, final_norm):
    t = x1.shape[0]
    tm, tf = TOKEN_TILE, FF_TILE
    row = pl.BlockSpec((tm, D_MODEL), lambda i, j: (i, 0))
    return pl.pallas_call(
        functools.partial(_ffn_kernel, final_norm=final_norm),
        grid=(t // tm, D_FF // tf),
        in_specs=[row, row,
                  pl.BlockSpec((D_MODEL, tf), lambda i, j: (0, j)),
                  pl.BlockSpec((D_MODEL, tf), lambda i, j: (0, j)),
                  pl.BlockSpec((tf, D_MODEL), lambda i, j: (j, 0)),
                  pl.BlockSpec(gf.shape, lambda i, j: (0, 0))],
        out_specs=row,
        out_shape=jax.ShapeDtypeStruct((t, D_MODEL), F32),
        scratch_shapes=[pltpu.VMEM((tm, D_MODEL), F32)],
        compiler_params=_params("parallel", "arbitrary"),
        name="ffn",
    )(x1, h2, wg, wu, wd, gf)


def _rope_tables(pos):
    inv = ROPE_THETA ** (-jnp.arange(0, ROPE_DIM, 2, dtype=F32) / ROPE_DIM)
    ang = pos.astype(F32)[:, None] * inv[None, :]
    cos, sin = jnp.cos(ang), jnp.sin(ang)
    n = pos.shape[0]
    rest = HEAD_DIM - ROPE_DIM
    one, zero = jnp.ones((n, rest), F32), jnp.zeros((n, rest), F32)
    zh = jnp.zeros((n, ROPE_HALF), F32)
    c = jnp.tile(jnp.concatenate([cos, cos, one], axis=1), (1, HEADS_PER_COL))
    s1 = jnp.tile(jnp.concatenate([-sin, zh, zero], axis=1), (1, HEADS_PER_COL))
    s2 = jnp.tile(jnp.concatenate([zh, sin, zero], axis=1), (1, HEADS_PER_COL))
    return c, s1, s2


def kernel(x_prompt, x_sample, state_pool, cache_k_win, cache_v_win, g_mix, w_in, w_pool,
           pool_scale, attn_sinks, w_out, g_ffn, w_gate, w_up, w_down, g_final):
    batch, seq, _ = x_prompt.shape
    nseq, dec_seq, _ = x_sample.shape
    depth = w_in.shape[0]
    win_s = cache_k_win.shape[2]
    ntok_s = nseq * dec_seq
    assert seq % TOKEN_TILE == 0 and ntok_s == TOKEN_TILE and nseq % SEQ_BLOCK == 0
    assert win_s == WINDOW and win_s + dec_seq <= SAMPLE_KEYS and seq >= WINDOW
    assert PAST_LEN >= max(POOL_HIST, WINDOW)

    xp = x_prompt.reshape(batch * seq, D_MODEL)
    xs = x_sample.reshape(ntok_s, D_MODEL)
    tabs_p = _rope_tables(jnp.arange(seq))
    tabs_s = _rope_tables(PAST_LEN + jnp.arange(ntok_s) % dec_seq)
    gf = g_final.reshape(1, D_MODEL)

    outs = [[] for _ in range(6)]
    for l in range(depth):
        w_in_l, w_out_l, wpool_l = w_in[l].astype(BF16), w_out[l].astype(BF16), w_pool[l].astype(BF16)
        wg_l, wu_l, wd_l = w_gate[l].astype(BF16), w_up[l].astype(BF16), w_down[l].astype(BF16)
        gm = g_mix[l].reshape(1, D_MODEL)
        gn = g_ffn[l].reshape(1, D_MODEL)
        pscale = pool_scale[l].reshape(1, POOL_WIDTH)
        sink_rows = jnp.tile(jnp.repeat(attn_sinks[l], dec_seq), SEQ_BLOCK)[:, None]

        u, q, kd, vt, u_tail, k_tail, v_tail = _proj(xp, gm, w_in_l, tabs_p, seq // TOKEN_TILE, WINDOW)
        mix = _mix_prompt(attn_sinks[l], u, q, kd, vt, wpool_l, pscale, batch, seq)
        x1, h2 = _outproj(xp, mix, w_out_l, gn)
        xp = _ffn(x1, h2, wg_l, wu_l, wd_l, gf, l == depth - 1)
        outs[0].append(u_tail.reshape(batch, HALO, POOL_WIDTH)[:, HALO - POOL_HIST:])
        outs[1].append(k_tail.reshape(batch, WINDOW, N_KV_HEADS, HEAD_DIM))
        outs[2].append(v_tail.reshape(batch, WINDOW, N_KV_HEADS, HEAD_DIM))

        u, q, _, _, _, k_new, v_new = _proj(xs, gm, w_in_l, tabs_s, 1, TOKEN_TILE)
        state_t = jnp.transpose(state_pool[l], (1, 0, 2))
        ck = cache_k_win[l].reshape(nseq, win_s, KV_WIDTH)
        cv = cache_v_win[l].reshape(nseq, win_s, KV_WIDTH)
        u_t = jnp.transpose(u.reshape(nseq, dec_seq, POOL_WIDTH), (1, 0, 2))
        mix, npool, nk, nv = _mix_sample(sink_rows, u_t, q, k_new, v_new, state_t, ck, cv, wpool_l,
                                         pscale, dec_seq)
        x1, h2 = _outproj(xs, mix, w_out_l, gn)
        xs = _ffn(x1, h2, wg_l, wu_l, wd_l, gf, l == depth - 1)
        outs[3].append(jnp.transpose(npool, (1, 0, 2)))
        outs[4].append(nk.reshape(nseq, win_s, N_KV_HEADS, HEAD_DIM))
        outs[5].append(nv.reshape(nseq, win_s, N_KV_HEADS, HEAD_DIM))

    y_prompt = xp.reshape(batch, seq, D_MODEL)
    y_sample = xs.reshape(nseq, dec_seq, D_MODEL)
    return (y_prompt, y_sample) + tuple(jnp.stack(o) for o in outs)
```

```python
import functools

import jax
import jax.numpy as jnp
from jax import lax
from jax.experimental import pallas as pl
from jax.experimental.pallas import tpu as pltpu

F32 = jnp.float32
BF16 = jnp.bfloat16

D_MODEL = 2048
POOL_WIDTH = D_MODEL // 2
POOL_WINDOWS = (2, 4, 8, 16)
N_POOL_GROUPS = len(POOL_WINDOWS)
POOL_GROUP = POOL_WIDTH // N_POOL_GROUPS
POOL_HIST = max(POOL_WINDOWS) - 1
HEAD_DIM = 64
N_HEADS = (D_MODEL - POOL_WIDTH) // HEAD_DIM
N_KV_HEADS = 4
GQA_GROUP = N_HEADS // N_KV_HEADS
ATTN_WIDTH = N_HEADS * HEAD_DIM
KV_WIDTH = N_KV_HEADS * HEAD_DIM
IN_WIDTH = POOL_WIDTH + ATTN_WIDTH + 2 * KV_WIDTH
MIX_WIDTH = POOL_WIDTH + ATTN_WIDTH
WINDOW = 128
BLOCK = 128
ROPE_DIM = HEAD_DIM // 4
ROPE_HALF = ROPE_DIM // 2
ROPE_THETA = 500000.0
D_FF = ((8 * D_MODEL // 3 + 255) // 256) * 256
EPS = 1e-5
PAST_LEN = 16384
Q_SCALE = HEAD_DIM ** -0.5

LANES = 128
SUBLANES = 8
HEADS_PER_COL = LANES // HEAD_DIM
HALO = 16
POOL_TOP = 2 * HALO
TOKEN_TILE = 512
FF_TILE = 512
SEQ_BLOCK = 8
SAMPLE_KEYS = 256
VMEM_LIMIT = 56 * 1024 * 1024


def _params(*semantics):
    return pltpu.CompilerParams(dimension_semantics=semantics, vmem_limit_bytes=VMEM_LIMIT)


def _rms(x, g):
    ms = jnp.mean(x * x, axis=-1, keepdims=True)
    return x * lax.rsqrt(ms + EPS) * g


def _swap_halves(z):
    return pltpu.roll(z, HEAD_DIM, 1)


def _proj_kernel(x_ref, g_ref, w_ref, c_ref, s1_ref, s2_ref,
                 u_ref, q_ref, kd_ref, vt_ref, utail_ref, ktail_ref, vtail_ref):
    tm = x_ref.shape[0]
    tail = ktail_ref.shape[0]
    h = _rms(x_ref[...], g_ref[...]).astype(BF16)
    proj = jnp.dot(h, w_ref[...], preferred_element_type=F32)
    u = proj[:, :POOL_WIDTH]
    u_ref[...] = u
    utail_ref[...] = u[tm - HALO:, :]
    c, s1, s2 = c_ref[...], s1_ref[...], s2_ref[...]

    def rope(z):
        return (z * c + pltpu.roll(z, LANES - ROPE_HALF, 1) * s1
                + pltpu.roll(z, ROPE_HALF, 1) * s2)

    for col in range(ATTN_WIDTH // LANES):
        z = proj[:, POOL_WIDTH + col * LANES:POOL_WIDTH + (col + 1) * LANES]
        q_ref[:, col * LANES:(col + 1) * LANES] = (rope(z) * Q_SCALE).astype(BF16)
    low_half = lax.broadcasted_iota(jnp.int32, (1, LANES), 1) < HEAD_DIM
    k0 = POOL_WIDTH + ATTN_WIDTH
    for a in range(KV_WIDTH // LANES):
        z = rope(proj[:, k0 + a * LANES:k0 + (a + 1) * LANES])
        ktail_ref[:, a * LANES:(a + 1) * LANES] = z[tm - tail:, :]
        zr = _swap_halves(z)
        kd_ref[:, (2 * a) * LANES:(2 * a + 1) * LANES] = jnp.where(low_half, z, zr).astype(BF16)
        kd_ref[:, (2 * a + 1) * LANES:(2 * a + 2) * LANES] = jnp.where(low_half, zr, z).astype(BF16)
    v = proj[:, k0 + KV_WIDTH:]
    vtail_ref[...] = v[tm - tail:, :]
    vt_ref[...] = v.T.astype(BF16)


def _proj(x, g, w_in, tabs, tiles_per_seq, tail):
    t = x.shape[0]
    tm = TOKEN_TILE
    nseq = t // (tm * tiles_per_seq)
    row = lambda w: pl.BlockSpec((tm, w), lambda i: (i, 0))
    tab = pl.BlockSpec((tm, LANES), lambda i: (i % tiles_per_seq, 0))
    full = lambda a: pl.BlockSpec(a.shape, lambda i: (0,) * a.ndim)
    per_seq = lambda r, w: pl.BlockSpec((r, w), lambda i: (i // tiles_per_seq, 0))
    return pl.pallas_call(
        _proj_kernel,
        grid=(t // tm,),
        in_specs=[row(D_MODEL), full(g), full(w_in), tab, tab, tab],
        out_specs=[row(POOL_WIDTH), row(ATTN_WIDTH), row(2 * KV_WIDTH),
                   pl.BlockSpec((KV_WIDTH, tm), lambda i: (0, i)),
                   per_seq(HALO, POOL_WIDTH), per_seq(tail, KV_WIDTH), per_seq(tail, KV_WIDTH)],
        out_shape=[jax.ShapeDtypeStruct((t, POOL_WIDTH), F32),
                   jax.ShapeDtypeStruct((t, ATTN_WIDTH), BF16),
                   jax.ShapeDtypeStruct((t, 2 * KV_WIDTH), BF16),
                   jax.ShapeDtypeStruct((KV_WIDTH, t), BF16),
                   jax.ShapeDtypeStruct((nseq * HALO, POOL_WIDTH), F32),
                   jax.ShapeDtypeStruct((nseq * tail, KV_WIDTH), F32),
                   jax.ShapeDtypeStruct((nseq * tail, KV_WIDTH), F32)],
        compiler_params=_params("arbitrary"),
        name="proj",
    )(x, g, w_in, *tabs)


def _mix_prompt_kernel(sink_ref, u_ref, halo_ref, q_ref, kdc_ref, kdp_ref, vtc_ref, vtp_ref,
                       wpool_ref, pscale_ref, mix_ref, ext_ref, lvl_ref, kwin_ref, vtwin_ref, bias_ref):
    tq = u_ref.shape[0]
    i = pl.program_id(1)
    first = i == 0

    top = POOL_TOP
    ext_ref[0:top - HALO, :] = jnp.zeros((top - HALO, POOL_WIDTH), F32)
    ext_ref[top - HALO:top, :] = halo_ref[...] * jnp.where(first, 0.0, 1.0)
    ext_ref[top:, :] = u_ref[...]
    lvl_ref[:, 0:SUBLANES, :] = jnp.zeros((2, SUBLANES, POOL_GROUP), F32)
    nlive = top + tq - SUBLANES
    pos1 = i * tq + lax.broadcasted_iota(jnp.int32, (tq, 1), 0) + 1
    for gi, w in enumerate(POOL_WINDOWS):
        cols = slice(gi * POOL_GROUP, (gi + 1) * POOL_GROUP)
        src = ext_ref.at[:, cols]
        shift, slot = 1, 0
        while shift < w:
            dst = lvl_ref.at[slot]
            dst[SUBLANES:, :] = (src[SUBLANES:SUBLANES + nlive, :]
                                 + src[SUBLANES - shift:SUBLANES - shift + nlive, :])
            src, shift, slot = dst, 2 * shift, 1 - slot
        tok = ext_ref[top:, cols]
        inv_cnt = 1.0 / jnp.minimum(pos1, w).astype(F32)
        d = (src[top:, :] * inv_cnt - tok).astype(BF16)
        po = jnp.dot(d, wpool_ref[gi], preferred_element_type=F32) * pscale_ref[:, cols]
        mix_ref[:, cols] = po.astype(BF16)

    kwin_ref[0:BLOCK, :] = kdp_ref[...]
    kwin_ref[BLOCK:, :] = kdc_ref[...]
    vtwin_ref[:, 0:BLOCK] = vtp_ref[...]
    vtwin_ref[:, BLOCK:] = vtc_ref[...]
    kj = lax.broadcasted_iota(jnp.int32, (2 * BLOCK, BLOCK), 0)
    qi = lax.broadcasted_iota(jnp.int32, (2 * BLOCK, BLOCK), 1)
    band = (kj >= qi) & (kj <= qi + WINDOW)
    first_lo = jnp.where(first, BLOCK, 0)
    bias_ref[0] = jnp.where(band, 0.0, -jnp.inf)
    bias_ref[1] = jnp.where(band & (kj >= first_lo), 0.0, -jnp.inf)
    lane = lax.broadcasted_iota(jnp.int32, (1, LANES), 1)
    keep_half = [(lane // HEAD_DIM == hh).astype(BF16) for hh in range(HEADS_PER_COL)]

    def scores(n, g):
        rows = slice(n * BLOCK, (n + 1) * BLOCK)
        qs = jnp.concatenate(
            [q_ref[rows, (h // HEADS_PER_COL) * LANES:(h // HEADS_PER_COL + 1) * LANES]
             * keep_half[h % HEADS_PER_COL]
             for h in range(GQA_GROUP * g, GQA_GROUP * (g + 1))], axis=0)
        kd = kwin_ref[n * BLOCK:(n + 2) * BLOCK, g * LANES:(g + 1) * LANES]
        return lax.dot_general(kd, qs, (((1,), (1,)), ((), ())), preferred_element_type=F32)

    def attend(n, g, s_t):
        p_t = []
        for j in range(GQA_GROUP):
            s = s_t[:, j * BLOCK:(j + 1) * BLOCK] + bias_ref[1 if n == 0 else 0]
            m = jnp.max(s, axis=0, keepdims=True)
            p = jnp.exp(s - m)
            den = jnp.sum(p, axis=0, keepdims=True) + jnp.exp(sink_ref[GQA_GROUP * g + j] - m)
            p_t.append((p * (1.0 / den)).astype(BF16))
        vt = vtwin_ref[g * HEAD_DIM:(g + 1) * HEAD_DIM, n * BLOCK:(n + 2) * BLOCK]
        o_t = jnp.dot(vt, jnp.concatenate(p_t, axis=1), preferred_element_type=F32)
        for c in range(GQA_GROUP // HEADS_PER_COL):
            col_t = jnp.concatenate(
                [o_t[:, (HEADS_PER_COL * c + hh) * BLOCK:(HEADS_PER_COL * c + hh + 1) * BLOCK]
                 for hh in range(HEADS_PER_COL)], axis=0)
            col = POOL_WIDTH + (GQA_GROUP // HEADS_PER_COL * g + c) * LANES
            mix_ref[n * BLOCK:(n + 1) * BLOCK, col:col + LANES] = col_t.T.astype(BF16)

    work = [(n, g) for n in range(tq // BLOCK) for g in range(N_KV_HEADS)]
    s_next = scores(*work[0])
    for idx, (n, g) in enumerate(work):
        s_t = s_next
        if idx + 1 < len(work):
            s_next = scores(*work[idx + 1])
        attend(n, g, s_t)


def _mix_prompt(sinks, u, q, kd, vt, wpool, pscale, batch, seq):
    tq = TOKEN_TILE
    nt = seq // tq
    bpt = tq // BLOCK
    hpt = tq // HALO
    prev_block = lambda b, i: jnp.maximum((b * nt + i) * bpt - 1, 0)
    row = lambda w: pl.BlockSpec((tq, w), lambda b, i: (b * nt + i, 0))
    halo = pl.BlockSpec((HALO, POOL_WIDTH), lambda b, i: (jnp.maximum((b * nt + i) * hpt - 1, 0), 0))
    full = lambda a: pl.BlockSpec(a.shape, lambda b, i: (0,) * a.ndim)
    return pl.pallas_call(
        _mix_prompt_kernel,
        grid=(batch, nt),
        in_specs=[pl.BlockSpec(memory_space=pltpu.SMEM), row(POOL_WIDTH), halo, row(ATTN_WIDTH),
                  row(2 * KV_WIDTH),
                  pl.BlockSpec((BLOCK, 2 * KV_WIDTH), lambda b, i: (prev_block(b, i), 0)),
                  pl.BlockSpec((KV_WIDTH, tq), lambda b, i: (0, b * nt + i)),
                  pl.BlockSpec((KV_WIDTH, BLOCK), lambda b, i: (0, prev_block(b, i))),
                  full(wpool), full(pscale)],
        out_specs=row(MIX_WIDTH),
        out_shape=jax.ShapeDtypeStruct((batch * seq, MIX_WIDTH), BF16),
        scratch_shapes=[pltpu.VMEM((POOL_TOP + tq, POOL_WIDTH), F32),
                        pltpu.VMEM((2, POOL_TOP + tq, POOL_GROUP), F32),
                        pltpu.VMEM((BLOCK + tq, 2 * KV_WIDTH), BF16),
                        pltpu.VMEM((KV_WIDTH, BLOCK + tq), BF16),
                        pltpu.VMEM((2, 2 * BLOCK, BLOCK), F32)],
        compiler_params=_params("parallel", "parallel"),
        name="mix_prompt",
    )(sinks, u, u, q, kd, kd, vt, vt, wpool, pscale)


def _sink_softmax(s, mask, sink):
    s = jnp.where(mask, s, -jnp.inf)
    m = jnp.maximum(jnp.max(s, axis=1, keepdims=True), sink)
    p = jnp.exp(s - m)
    den = jnp.sum(p, axis=1, keepdims=True) + jnp.exp(sink - m)
    return p / den


def _mix_sample_kernel(sink_ref, u_ref, q_ref, kn_ref, vn_ref, st_ref, ck_ref, cv_ref,
                       wpool_ref, pscale_ref, mix_ref, npool_ref, nk_ref, nv_ref,
                       kall_ref, vall_ref, *, dec_seq):
    sb = SEQ_BLOCK
    rows_per_seq = N_HEADS * dec_seq
    nrow = sb * rows_per_seq
    ntok = sb * dec_seq
    win = ck_ref.shape[1]

    @pl.when(pl.program_id(0) == 0)
    def _():
        kall_ref[...] = jnp.zeros_like(kall_ref)
        vall_ref[...] = jnp.zeros_like(vall_ref)

    ext = [st_ref[h] for h in range(POOL_HIST)]
    ext += [u_ref[t] for t in range(dec_seq)]
    for h in range(POOL_HIST):
        npool_ref[h] = ext[h + dec_seq]
    r_out = lax.broadcasted_iota(jnp.int32, (ntok, ntok), 0)
    r_in = lax.broadcasted_iota(jnp.int32, (ntok, ntok), 1)
    to_seq_major = ((r_in % sb) * dec_seq + r_in // sb == r_out).astype(BF16)
    for gi, w in enumerate(POOL_WINDOWS):
        cols = slice(gi * POOL_GROUP, (gi + 1) * POOL_GROUP)
        ds = []
        for t in range(dec_seq):
            tok = ext[POOL_HIST + t][:, cols]
            acc = tok
            for j in range(1, w):
                acc = acc + ext[POOL_HIST + t - j][:, cols]
            ds.append(acc * (1.0 / w) - tok)
        d = jnp.concatenate(ds, axis=0).astype(BF16)
        po = jnp.dot(d, wpool_ref[gi], preferred_element_type=F32) * pscale_ref[:, cols]
        po = jnp.dot(to_seq_major, po.astype(BF16), preferred_element_type=F32)
        mix_ref[:, cols] = po.astype(BF16)

    for b in range(sb):
        tok_rows = slice(b * dec_seq, (b + 1) * dec_seq)
        kall_ref[b, 0:win, :] = ck_ref[b]
        kall_ref[b, win:win + dec_seq, :] = kn_ref[tok_rows, :]
        vall_ref[b, 0:win, :] = cv_ref[b]
        vall_ref[b, win:win + dec_seq, :] = vn_ref[tok_rows, :]
        nk_ref[b] = kall_ref[b, dec_seq:dec_seq + win, :]
        nv_ref[b] = vall_ref[b, dec_seq:dec_seq + win, :]

    r = lax.broadcasted_iota(jnp.int32, (nrow, ntok), 0)
    c = lax.broadcasted_iota(jnp.int32, (nrow, ntok), 1)
    pick = (c == (r // rows_per_seq) * dec_seq + r % dec_seq)
    qrep = jnp.dot(pick.astype(BF16), q_ref[...], preferred_element_type=F32)
    rr = lax.broadcasted_iota(jnp.int32, (nrow, LANES), 0) % rows_per_seq
    row_g = rr // (GQA_GROUP * dec_seq)
    row_j = (rr // dec_seq) % GQA_GROUP
    lane_half = lax.broadcasted_iota(jnp.int32, (nrow, LANES), 1) // HEAD_DIM
    lhs_cols = [jnp.zeros((nrow, LANES), F32) for _ in range(KV_WIDTH // LANES)]
    for qc in range(ATTN_WIDTH // LANES):
        g = qc * HEADS_PER_COL // GQA_GROUP
        src = qrep[:, qc * LANES:(qc + 1) * LANES]
        src_swapped = _swap_halves(src)
        kv_col, kv_half = g // HEADS_PER_COL, g % HEADS_PER_COL
        for hh in range(HEADS_PER_COL):
            j = (qc * HEADS_PER_COL + hh) % GQA_GROUP
            here = (row_g == g) & (row_j == j) & (lane_half == kv_half)
            lhs_cols[kv_col] = lhs_cols[kv_col] + jnp.where(
                here, src if hh == kv_half else src_swapped, 0.0)
    lhs = jnp.concatenate(lhs_cols, axis=1).astype(BF16)
    scores = []
    for b in range(sb):
        rows = slice(b * rows_per_seq, (b + 1) * rows_per_seq)
        scores.append(lax.dot_general(lhs[rows], kall_ref[b].astype(BF16),
                                      (((1,), (1,)), ((), ())), preferred_element_type=F32))
    s = jnp.concatenate(scores, axis=0)
    t_row = lax.broadcasted_iota(jnp.int32, (nrow, SAMPLE_KEYS), 0) % dec_seq
    kj = lax.broadcasted_iota(jnp.int32, (nrow, SAMPLE_KEYS), 1)
    mask = (kj >= t_row + (win - WINDOW)) & (kj <= t_row + win)
    p = _sink_softmax(s, mask, sink_ref[...]).astype(BF16)
    outs = []
    for b in range(sb):
        rows = slice(b * rows_per_seq, (b + 1) * rows_per_seq)
        outs.append(jnp.dot(p[rows], vall_ref[b].astype(BF16), preferred_element_type=F32))
    o = jnp.concatenate(outs, axis=0)
    z_cols = []
    for qc in range(ATTN_WIDTH // LANES):
        g = qc * HEADS_PER_COL // GQA_GROUP
        kv_col, kv_half = g // HEADS_PER_COL, g % HEADS_PER_COL
        src = o[:, kv_col * LANES:(kv_col + 1) * LANES]
        src_swapped = _swap_halves(src)
        zc = jnp.zeros((nrow, LANES), F32)
        for hh in range(HEADS_PER_COL):
            j = (qc * HEADS_PER_COL + hh) % GQA_GROUP
            here = (row_g == g) & (row_j == j) & (lane_half == hh)
            zc = zc + jnp.where(here, src if hh == kv_half else src_swapped, 0.0)
        z_cols.append(zc)
    z = jnp.concatenate(z_cols, axis=1).astype(BF16)
    rt = lax.broadcasted_iota(jnp.int32, (ntok, nrow), 0)
    ct = lax.broadcasted_iota(jnp.int32, (ntok, nrow), 1)
    unpick = (rt == (ct // rows_per_seq) * dec_seq + ct % dec_seq)
    attn = jnp.dot(unpick.astype(BF16), z, preferred_element_type=F32)
    mix_ref[:, POOL_WIDTH:] = attn.astype(BF16)


def _mix_sample(sink_rows, u, q, k_new, v_new, state_t, cache_k, cache_v, wpool, pscale, dec_seq):
    nseq, win, _ = cache_k.shape
    sb = SEQ_BLOCK
    ntok = sb * dec_seq
    row = lambda w: pl.BlockSpec((ntok, w), lambda i: (i, 0))
    slab = pl.BlockSpec((POOL_HIST, sb, POOL_WIDTH), lambda i: (0, i, 0))
    u_slab = pl.BlockSpec((dec_seq, sb, POOL_WIDTH), lambda i: (0, i, 0))
    cache = pl.BlockSpec((sb, win, KV_WIDTH), lambda i: (i, 0, 0))
    full = lambda a: pl.BlockSpec(a.shape, lambda i: (0,) * a.ndim)
    return pl.pallas_call(
        functools.partial(_mix_sample_kernel, dec_seq=dec_seq),
        grid=(nseq // sb,),
        in_specs=[full(sink_rows), u_slab, row(ATTN_WIDTH), row(KV_WIDTH), row(KV_WIDTH),
                  slab, cache, cache, full(wpool), full(pscale)],
        out_specs=[row(MIX_WIDTH), slab, cache, cache],
        out_shape=[jax.ShapeDtypeStruct((nseq * dec_seq, MIX_WIDTH), BF16),
                   jax.ShapeDtypeStruct((POOL_HIST, nseq, POOL_WIDTH), F32),
                   jax.ShapeDtypeStruct(cache_k.shape, F32),
                   jax.ShapeDtypeStruct(cache_v.shape, F32)],
        scratch_shapes=[pltpu.VMEM((sb, SAMPLE_KEYS, KV_WIDTH), F32),
                        pltpu.VMEM((sb, SAMPLE_KEYS, KV_WIDTH), F32)],
        compiler_params=_params("arbitrary"),
        name="mix_sample",
    )(sink_rows, u, q, k_new, v_new, state_t, cache_k, cache_v, wpool, pscale)


def _outproj_kernel(x_ref, mix_ref, w_ref, g_ref, x1_ref, h2_ref):
    x1 = x_ref[...] + jnp.dot(mix_ref[...], w_ref[...], preferred_element_type=F32)
    x1_ref[...] = x1
    h2_ref[...] = _rms(x1, g_ref[...]).astype(BF16)


def _outproj(x, mix, w_out, g):
    t = x.shape[0]
    tm = TOKEN_TILE
    row = lambda w: pl.BlockSpec((tm, w), lambda i: (i, 0))
    full = lambda a: pl.BlockSpec(a.shape, lambda i: (0,) * a.ndim)
    return pl.pallas_call(
        _outproj_kernel,
        grid=(t // tm,),
        in_specs=[row(D_MODEL), row(MIX_WIDTH), full(w_out), full(g)],
        out_specs=[row(D_MODEL), row(D_MODEL)],
        out_shape=[jax.ShapeDtypeStruct((t, D_MODEL), F32),
                   jax.ShapeDtypeStruct((t, D_MODEL), BF16)],
        compiler_params=_params("parallel"),
        name="outproj",
    )(x, mix, w_out, g)


def _ffn_kernel(x1_ref, h2_ref, gf_ref, wgu_hbm, wd_hbm, y_ref,
                wgu_buf, wd_buf, act_ref, gu_sem, wd_sem, *, final_norm):
    nchunks = wgu_hbm.shape[0]
    tf = wd_hbm.shape[1]
    half = tf // 2
    i = pl.program_id(0)
    base = i * nchunks

    def gu_copy(n):
        s = lax.rem(base + n, 3)
        return pltpu.make_async_copy(wgu_hbm.at[n % nchunks], wgu_buf.at[s], gu_sem.at[s])

    def down_copy(n):
        s = lax.rem(base + n, 2)
        return pltpu.make_async_copy(wd_hbm.at[n % nchunks], wd_buf.at[s], wd_sem.at[s])

    def gate_up_half(c, hh):
        s = lax.rem(base + c, 3)
        return jnp.dot(h2_ref[...], wgu_buf[s, :, hh * tf:(hh + 1) * tf],
                       preferred_element_type=F32)

    def store_act(c, gu):
        s = lax.rem(base + c, 2)
        for hh in range(2):
            act = jax.nn.silu(gu[hh][:, :half]) * gu[hh][:, half:]
            act_ref[s, :, hh * half:(hh + 1) * half] = act.astype(BF16)

    def down(c):
        s = lax.rem(base + c, 2)
        y_ref[...] += jnp.dot(act_ref[s], wd_buf[s], preferred_element_type=F32)

    def refill(c):
        if c >= 0:
            down_copy(c).wait()
        if c + 2 < nchunks:
            gu_copy(c + 2).wait()
        if 2 <= c + 3 < nchunks + 2:
            gu_copy(c + 3).start()
        if c >= 0:
            down_copy(c + 1).start()

    @pl.when(i == 0)
    def _():
        gu_copy(0).start()
        gu_copy(1).start()
        down_copy(0).start()

    y_ref[...] = x1_ref[...]
    gu_copy(0).wait()
    for c in range(-1, nchunks):
        if c + 1 < nchunks:
            first = gate_up_half(c + 1, 0)
            refill(c)
            gu = [first, gate_up_half(c + 1, 1)]
        else:
            refill(c)
        if c >= 0:
            down(c)
        if c + 1 < nchunks:
            store_act(c + 1, gu)
    if final_norm:
        y_ref[...] = _rms(y_ref[...], gf_ref[...])

    @pl.when(i + 1 == pl.num_programs(0))
    def _():
        gu_copy(nchunks).wait()
        gu_copy(nchunks + 1).wait()
        down_copy(nchunks).wait()


def _ffn(x1, h2, wgu, wd, gf, final_norm):
    t = x1.shape[0]
    tm = TOKEN_TILE
    nchunks, tf, _ = wd.shape
    assert nchunks >= 3
    row = pl.BlockSpec((tm, D_MODEL), lambda i: (i, 0))
    hbm = pl.BlockSpec(memory_space=pl.ANY)
    return pl.pallas_call(
        functools.partial(_ffn_kernel, final_norm=final_norm),
        grid=(t // tm,),
        in_specs=[row, row, pl.BlockSpec(gf.shape, lambda i: (0, 0)), hbm, hbm],
        out_specs=row,
        out_shape=jax.ShapeDtypeStruct((t, D_MODEL), F32),
        scratch_shapes=[pltpu.VMEM((3, D_MODEL, 2 * tf), BF16),
                        pltpu.VMEM((2, tf, D_MODEL), BF16),
                        pltpu.VMEM((2, tm, tf), BF16),
                        pltpu.SemaphoreType.DMA((3,)),
                        pltpu.SemaphoreType.DMA((2,))],
        compiler_params=_params("arbitrary"),
        name="ffn",
    )(x1, h2, gf, wgu, wd)


def _rope_tables(pos):
    inv = ROPE_THETA ** (-jnp.arange(0, ROPE_DIM, 2, dtype=F32) / ROPE_DIM)
    ang = pos.astype(F32)[:, None] * inv[None, :]
    cos, sin = jnp.cos(ang), jnp.sin(ang)
    n = pos.shape[0]
    rest = HEAD_DIM - ROPE_DIM
    one, zero = jnp.ones((n, rest), F32), jnp.zeros((n, rest), F32)
    zh = jnp.zeros((n, ROPE_HALF), F32)
    c = jnp.tile(jnp.concatenate([cos, cos, one], axis=1), (1, HEADS_PER_COL))
    s1 = jnp.tile(jnp.concatenate([-sin, zh, zero], axis=1), (1, HEADS_PER_COL))
    s2 = jnp.tile(jnp.concatenate([zh, sin, zero], axis=1), (1, HEADS_PER_COL))
    return c, s1, s2


def kernel(x_prompt, x_sample, state_pool, cache_k_win, cache_v_win, g_mix, w_in, w_pool,
           pool_scale, attn_sinks, w_out, g_ffn, w_gate, w_up, w_down, g_final):
    batch, seq, _ = x_prompt.shape
    nseq, dec_seq, _ = x_sample.shape
    depth = w_in.shape[0]
    win_s = cache_k_win.shape[2]
    ntok_s = nseq * dec_seq
    assert seq % TOKEN_TILE == 0 and ntok_s == TOKEN_TILE and nseq % SEQ_BLOCK == 0
    assert win_s == WINDOW and win_s + dec_seq <= SAMPLE_KEYS and seq >= WINDOW
    assert PAST_LEN >= max(POOL_HIST, WINDOW)

    xp = x_prompt.reshape(batch * seq, D_MODEL)
    xs = x_sample.reshape(ntok_s, D_MODEL)
    tabs_p = _rope_tables(jnp.arange(seq))
    tabs_s = _rope_tables(PAST_LEN + jnp.arange(ntok_s) % dec_seq)
    gf = g_final.reshape(1, D_MODEL)

    outs = [[] for _ in range(6)]
    for l in range(depth):
        w_in_l, w_out_l, wpool_l = w_in[l].astype(BF16), w_out[l].astype(BF16), w_pool[l].astype(BF16)
        nch = D_FF // FF_TILE
        halves = lambda w: w.astype(BF16).reshape(D_MODEL, nch, 2, FF_TILE // 2)
        wgu_l = jnp.stack([halves(w_gate[l]), halves(w_up[l])], axis=3)
        wgu_l = jnp.transpose(wgu_l, (1, 0, 2, 3, 4)).reshape(nch, D_MODEL, 2 * FF_TILE)
        wd_l = w_down[l].astype(BF16).reshape(nch, FF_TILE, D_MODEL)
        gm = g_mix[l].reshape(1, D_MODEL)
        gn = g_ffn[l].reshape(1, D_MODEL)
        pscale = pool_scale[l].reshape(1, POOL_WIDTH)
        sink_rows = jnp.tile(jnp.repeat(attn_sinks[l], dec_seq), SEQ_BLOCK)[:, None]

        u, q, kd, vt, u_tail, k_tail, v_tail = _proj(xp, gm, w_in_l, tabs_p, seq // TOKEN_TILE, WINDOW)
        mix = _mix_prompt(attn_sinks[l], u, q, kd, vt, wpool_l, pscale, batch, seq)
        x1, h2 = _outproj(xp, mix, w_out_l, gn)
        xp = _ffn(x1, h2, wgu_l, wd_l, gf, l == depth - 1)
        outs[0].append(u_tail.reshape(batch, HALO, POOL_WIDTH)[:, HALO - POOL_HIST:])
        outs[1].append(k_tail.reshape(batch, WINDOW, N_KV_HEADS, HEAD_DIM))
        outs[2].append(v_tail.reshape(batch, WINDOW, N_KV_HEADS, HEAD_DIM))

        u, q, _, _, _, k_new, v_new = _proj(xs, gm, w_in_l, tabs_s, 1, TOKEN_TILE)
        state_t = jnp.transpose(state_pool[l], (1, 0, 2))
        ck = cache_k_win[l].reshape(nseq, win_s, KV_WIDTH)
        cv = cache_v_win[l].reshape(nseq, win_s, KV_WIDTH)
        u_t = jnp.transpose(u.reshape(nseq, dec_seq, POOL_WIDTH), (1, 0, 2))
        mix, npool, nk, nv = _mix_sample(sink_rows, u_t, q, k_new, v_new, state_t, ck, cv, wpool_l,
                                         pscale, dec_seq)
        x1, h2 = _outproj(xs, mix, w_out_l, gn)
        xs = _ffn(x1, h2, wgu_l, wd_l, gf, l == depth - 1)
        outs[3].append(jnp.transpose(npool, (1, 0, 2)))
        outs[4].append(nk.reshape(nseq, win_s, N_KV_HEADS, HEAD_DIM))
        outs[5].append(nv.reshape(nseq, win_s, N_KV_HEADS, HEAD_DIM))

    y_prompt = xp.reshape(batch, seq, D_MODEL)
    y_sample = xs.reshape(nseq, dec_seq, D_MODEL)
    return (y_prompt, y_sample) + tuple(jnp.stack(o) for o in outs)
```

```python
import functools

import jax
import jax.numpy as jnp
from jax import lax
from jax.experimental import pallas as pl
from jax.experimental.pallas import tpu as pltpu

F32 = jnp.float32
BF16 = jnp.bfloat16

D_MODEL = 2048
POOL_WIDTH = D_MODEL // 2
POOL_WINDOWS = (2, 4, 8, 16)
N_POOL_GROUPS = len(POOL_WINDOWS)
POOL_GROUP = POOL_WIDTH // N_POOL_GROUPS
POOL_HIST = max(POOL_WINDOWS) - 1
HEAD_DIM = 64
N_HEADS = (D_MODEL - POOL_WIDTH) // HEAD_DIM
N_KV_HEADS = 4
GQA_GROUP = N_HEADS // N_KV_HEADS
ATTN_WIDTH = N_HEADS * HEAD_DIM
KV_WIDTH = N_KV_HEADS * HEAD_DIM
IN_WIDTH = POOL_WIDTH + ATTN_WIDTH + 2 * KV_WIDTH
MIX_WIDTH = POOL_WIDTH + ATTN_WIDTH
WINDOW = 128
BLOCK = 128
ROPE_DIM = HEAD_DIM // 4
ROPE_HALF = ROPE_DIM // 2
ROPE_THETA = 500000.0
D_FF = ((8 * D_MODEL // 3 + 255) // 256) * 256
EPS = 1e-5
PAST_LEN = 16384
Q_SCALE = HEAD_DIM ** -0.5

LANES = 128
SUBLANES = 8
HEADS_PER_COL = LANES // HEAD_DIM
HALO = 16
POOL_TOP = 2 * HALO
TOKEN_TILE = 512
FF_TILE = 512
SEQ_BLOCK = 8
SAMPLE_KEYS = 256
VMEM_LIMIT = 56 * 1024 * 1024


def _params(*semantics):
    return pltpu.CompilerParams(dimension_semantics=semantics, vmem_limit_bytes=VMEM_LIMIT)


def _rms(x, g):
    ms = jnp.mean(x * x, axis=-1, keepdims=True)
    return x * lax.rsqrt(ms + EPS) * g


def _swap_halves(z):
    return pltpu.roll(z, HEAD_DIM, 1)


def _proj_kernel(x_ref, g_ref, w_ref, c_ref, s1_ref, s2_ref,
                 u_ref, q_ref, kd_ref, vt_ref, utail_ref, ktail_ref, vtail_ref):
    tm = x_ref.shape[0]
    tail = ktail_ref.shape[0]
    h = _rms(x_ref[...], g_ref[...]).astype(BF16)
    proj = jnp.dot(h, w_ref[...], preferred_element_type=F32)
    u = proj[:, :POOL_WIDTH]
    u_ref[...] = u
    utail_ref[...] = u[tm - HALO:, :]
    c, s1, s2 = c_ref[...], s1_ref[...], s2_ref[...]

    def rope(z):
        return (z * c + pltpu.roll(z, LANES - ROPE_HALF, 1) * s1
                + pltpu.roll(z, ROPE_HALF, 1) * s2)

    for col in range(ATTN_WIDTH // LANES):
        z = proj[:, POOL_WIDTH + col * LANES:POOL_WIDTH + (col + 1) * LANES]
        q_ref[:, col * LANES:(col + 1) * LANES] = (rope(z) * Q_SCALE).astype(BF16)
    low_half = lax.broadcasted_iota(jnp.int32, (1, LANES), 1) < HEAD_DIM
    k0 = POOL_WIDTH + ATTN_WIDTH
    for a in range(KV_WIDTH // LANES):
        z = rope(proj[:, k0 + a * LANES:k0 + (a + 1) * LANES])
        ktail_ref[:, a * LANES:(a + 1) * LANES] = z[tm - tail:, :]
        zr = _swap_halves(z)
        kd_ref[:, (2 * a) * LANES:(2 * a + 1) * LANES] = jnp.where(low_half, z, zr).astype(BF16)
        kd_ref[:, (2 * a + 1) * LANES:(2 * a + 2) * LANES] = jnp.where(low_half, zr, z).astype(BF16)
    v = proj[:, k0 + KV_WIDTH:]
    vtail_ref[...] = v[tm - tail:, :]
    vt_ref[...] = v.T.astype(BF16)


def _proj(x, g, w_in, tabs, tiles_per_seq, tail):
    t = x.shape[0]
    tm = TOKEN_TILE
    nseq = t // (tm * tiles_per_seq)
    row = lambda w: pl.BlockSpec((tm, w), lambda i: (i, 0))
    tab = pl.BlockSpec((tm, LANES), lambda i: (i % tiles_per_seq, 0))
    full = lambda a: pl.BlockSpec(a.shape, lambda i: (0,) * a.ndim)
    per_seq = lambda r, w: pl.BlockSpec((r, w), lambda i: (i // tiles_per_seq, 0))
    return pl.pallas_call(
        _proj_kernel,
        grid=(t // tm,),
        in_specs=[row(D_MODEL), full(g), full(w_in), tab, tab, tab],
        out_specs=[row(POOL_WIDTH), row(ATTN_WIDTH), row(2 * KV_WIDTH),
                   pl.BlockSpec((KV_WIDTH, tm), lambda i: (0, i)),
                   per_seq(HALO, POOL_WIDTH), per_seq(tail, KV_WIDTH), per_seq(tail, KV_WIDTH)],
        out_shape=[jax.ShapeDtypeStruct((t, POOL_WIDTH), F32),
                   jax.ShapeDtypeStruct((t, ATTN_WIDTH), BF16),
                   jax.ShapeDtypeStruct((t, 2 * KV_WIDTH), BF16),
                   jax.ShapeDtypeStruct((KV_WIDTH, t), BF16),
                   jax.ShapeDtypeStruct((nseq * HALO, POOL_WIDTH), F32),
                   jax.ShapeDtypeStruct((nseq * tail, KV_WIDTH), F32),
                   jax.ShapeDtypeStruct((nseq * tail, KV_WIDTH), F32)],
        compiler_params=_params("arbitrary"),
        name="proj",
    )(x, g, w_in, *tabs)


def _mix_prompt_kernel(sink_ref, u_ref, halo_ref, q_ref, kdc_ref, kdp_ref, vtc_ref, vtp_ref,
                       wpool_ref, pscale_ref, mix_ref, ext_ref, lvl_ref, kwin_ref, vtwin_ref, bias_ref):
    tq = u_ref.shape[0]
    i = pl.program_id(1)
    first = i == 0

    top = POOL_TOP
    ext_ref[0:top - HALO, :] = jnp.zeros((top - HALO, POOL_WIDTH), F32)
    ext_ref[top - HALO:top, :] = halo_ref[...] * jnp.where(first, 0.0, 1.0)
    ext_ref[top:, :] = u_ref[...]
    lvl_ref[:, 0:SUBLANES, :] = jnp.zeros((2, SUBLANES, POOL_GROUP), F32)
    nlive = top + tq - SUBLANES
    pos1 = i * tq + lax.broadcasted_iota(jnp.int32, (tq, 1), 0) + 1
    for gi, w in enumerate(POOL_WINDOWS):
        cols = slice(gi * POOL_GROUP, (gi + 1) * POOL_GROUP)
        src = ext_ref.at[:, cols]
        shift, slot = 1, 0
        while shift < w:
            dst = lvl_ref.at[slot]
            dst[SUBLANES:, :] = (src[SUBLANES:SUBLANES + nlive, :]
                                 + src[SUBLANES - shift:SUBLANES - shift + nlive, :])
            src, shift, slot = dst, 2 * shift, 1 - slot
        tok = ext_ref[top:, cols]
        inv_cnt = 1.0 / jnp.minimum(pos1, w).astype(F32)
        d = (src[top:, :] * inv_cnt - tok).astype(BF16)
        po = jnp.dot(d, wpool_ref[gi], preferred_element_type=F32) * pscale_ref[:, cols]
        mix_ref[:, cols] = po.astype(BF16)

    kwin_ref[0:BLOCK, :] = kdp_ref[...]
    kwin_ref[BLOCK:, :] = kdc_ref[...]
    vtwin_ref[:, 0:BLOCK] = vtp_ref[...]
    vtwin_ref[:, BLOCK:] = vtc_ref[...]
    kj = lax.broadcasted_iota(jnp.int32, (2 * BLOCK, BLOCK), 0)
    qi = lax.broadcasted_iota(jnp.int32, (2 * BLOCK, BLOCK), 1)
    band = (kj >= qi) & (kj <= qi + WINDOW)
    first_lo = jnp.where(first, BLOCK, 0)
    bias_ref[0] = jnp.where(band, 0.0, -jnp.inf)
    bias_ref[1] = jnp.where(band & (kj >= first_lo), 0.0, -jnp.inf)
    lane = lax.broadcasted_iota(jnp.int32, (1, LANES), 1)
    keep_half = [(lane // HEAD_DIM == hh).astype(BF16) for hh in range(HEADS_PER_COL)]

    def scores(n, g):
        rows = slice(n * BLOCK, (n + 1) * BLOCK)
        qs = jnp.concatenate(
            [q_ref[rows, (h // HEADS_PER_COL) * LANES:(h // HEADS_PER_COL + 1) * LANES]
             * keep_half[h % HEADS_PER_COL]
             for h in range(GQA_GROUP * g, GQA_GROUP * (g + 1))], axis=0)
        kd = kwin_ref[n * BLOCK:(n + 2) * BLOCK, g * LANES:(g + 1) * LANES]
        return lax.dot_general(kd, qs, (((1,), (1,)), ((), ())), preferred_element_type=F32)

    def attend(n, g, s_t):
        p_t = []
        for j in range(GQA_GROUP):
            s = s_t[:, j * BLOCK:(j + 1) * BLOCK] + bias_ref[1 if n == 0 else 0]
            m = jnp.max(s, axis=0, keepdims=True)
            p = jnp.exp(s - m)
            den = jnp.sum(p, axis=0, keepdims=True) + jnp.exp(sink_ref[GQA_GROUP * g + j] - m)
            p_t.append((p * (1.0 / den)).astype(BF16))
        vt = vtwin_ref[g * HEAD_DIM:(g + 1) * HEAD_DIM, n * BLOCK:(n + 2) * BLOCK]
        o_t = jnp.dot(vt, jnp.concatenate(p_t, axis=1), preferred_element_type=F32)
        for c in range(GQA_GROUP // HEADS_PER_COL):
            col_t = jnp.concatenate(
                [o_t[:, (HEADS_PER_COL * c + hh) * BLOCK:(HEADS_PER_COL * c + hh + 1) * BLOCK]
                 for hh in range(HEADS_PER_COL)], axis=0)
            col = POOL_WIDTH + (GQA_GROUP // HEADS_PER_COL * g + c) * LANES
            mix_ref[n * BLOCK:(n + 1) * BLOCK, col:col + LANES] = col_t.T.astype(BF16)

    work = [(n, g) for n in range(tq // BLOCK) for g in range(N_KV_HEADS)]
    s_next = scores(*work[0])
    for idx, (n, g) in enumerate(work):
        s_t = s_next
        if idx + 1 < len(work):
            s_next = scores(*work[idx + 1])
        attend(n, g, s_t)


def _mix_prompt(sinks, u, q, kd, vt, wpool, pscale, batch, seq):
    tq = TOKEN_TILE
    nt = seq // tq
    bpt = tq // BLOCK
    hpt = tq // HALO
    prev_block = lambda b, i: jnp.maximum((b * nt + i) * bpt - 1, 0)
    row = lambda w: pl.BlockSpec((tq, w), lambda b, i: (b * nt + i, 0))
    halo = pl.BlockSpec((HALO, POOL_WIDTH), lambda b, i: (jnp.maximum((b * nt + i) * hpt - 1, 0), 0))
    full = lambda a: pl.BlockSpec(a.shape, lambda b, i: (0,) * a.ndim)
    return pl.pallas_call(
        _mix_prompt_kernel,
        grid=(batch, nt),
        in_specs=[pl.BlockSpec(memory_space=pltpu.SMEM), row(POOL_WIDTH), halo, row(ATTN_WIDTH),
                  row(2 * KV_WIDTH),
                  pl.BlockSpec((BLOCK, 2 * KV_WIDTH), lambda b, i: (prev_block(b, i), 0)),
                  pl.BlockSpec((KV_WIDTH, tq), lambda b, i: (0, b * nt + i)),
                  pl.BlockSpec((KV_WIDTH, BLOCK), lambda b, i: (0, prev_block(b, i))),
                  full(wpool), full(pscale)],
        out_specs=row(MIX_WIDTH),
        out_shape=jax.ShapeDtypeStruct((batch * seq, MIX_WIDTH), BF16),
        scratch_shapes=[pltpu.VMEM((POOL_TOP + tq, POOL_WIDTH), F32),
                        pltpu.VMEM((2, POOL_TOP + tq, POOL_GROUP), F32),
                        pltpu.VMEM((BLOCK + tq, 2 * KV_WIDTH), BF16),
                        pltpu.VMEM((KV_WIDTH, BLOCK + tq), BF16),
                        pltpu.VMEM((2, 2 * BLOCK, BLOCK), F32)],
        compiler_params=_params("parallel", "parallel"),
        name="mix_prompt",
    )(sinks, u, u, q, kd, kd, vt, vt, wpool, pscale)


def _sink_softmax(s, mask, sink):
    s = jnp.where(mask, s, -jnp.inf)
    m = jnp.maximum(jnp.max(s, axis=1, keepdims=True), sink)
    p = jnp.exp(s - m)
    den = jnp.sum(p, axis=1, keepdims=True) + jnp.exp(sink - m)
    return p / den


def _mix_sample_kernel(sink_ref, u_ref, q_ref, kn_ref, vn_ref, st_ref, ck_ref, cv_ref,
                       wpool_ref, pscale_ref, mix_ref, npool_ref, nk_ref, nv_ref,
                       kall_ref, vall_ref, *, dec_seq):
    sb = SEQ_BLOCK
    rows_per_seq = N_HEADS * dec_seq
    nrow = sb * rows_per_seq
    ntok = sb * dec_seq
    win = ck_ref.shape[1]

    @pl.when(pl.program_id(0) == 0)
    def _():
        kall_ref[...] = jnp.zeros_like(kall_ref)
        vall_ref[...] = jnp.zeros_like(vall_ref)

    ext = [st_ref[h] for h in range(POOL_HIST)]
    ext += [u_ref[t] for t in range(dec_seq)]
    for h in range(POOL_HIST):
        npool_ref[h] = ext[h + dec_seq]
    r_out = lax.broadcasted_iota(jnp.int32, (ntok, ntok), 0)
    r_in = lax.broadcasted_iota(jnp.int32, (ntok, ntok), 1)
    to_seq_major = ((r_in % sb) * dec_seq + r_in // sb == r_out).astype(BF16)
    for gi, w in enumerate(POOL_WINDOWS):
        cols = slice(gi * POOL_GROUP, (gi + 1) * POOL_GROUP)
        ds = []
        for t in range(dec_seq):
            tok = ext[POOL_HIST + t][:, cols]
            acc = tok
            for j in range(1, w):
                acc = acc + ext[POOL_HIST + t - j][:, cols]
            ds.append(acc * (1.0 / w) - tok)
        d = jnp.concatenate(ds, axis=0).astype(BF16)
        po = jnp.dot(d, wpool_ref[gi], preferred_element_type=F32) * pscale_ref[:, cols]
        po = jnp.dot(to_seq_major, po.astype(BF16), preferred_element_type=F32)
        mix_ref[:, cols] = po.astype(BF16)

    for b in range(sb):
        tok_rows = slice(b * dec_seq, (b + 1) * dec_seq)
        kall_ref[b, 0:win, :] = ck_ref[b]
        kall_ref[b, win:win + dec_seq, :] = kn_ref[tok_rows, :]
        vall_ref[b, 0:win, :] = cv_ref[b]
        vall_ref[b, win:win + dec_seq, :] = vn_ref[tok_rows, :]
        nk_ref[b] = kall_ref[b, dec_seq:dec_seq + win, :]
        nv_ref[b] = vall_ref[b, dec_seq:dec_seq + win, :]

    r = lax.broadcasted_iota(jnp.int32, (nrow, ntok), 0)
    c = lax.broadcasted_iota(jnp.int32, (nrow, ntok), 1)
    pick = (c == (r // rows_per_seq) * dec_seq + r % dec_seq)
    qrep = jnp.dot(pick.astype(BF16), q_ref[...], preferred_element_type=F32)
    rr = lax.broadcasted_iota(jnp.int32, (nrow, LANES), 0) % rows_per_seq
    row_g = rr // (GQA_GROUP * dec_seq)
    row_j = (rr // dec_seq) % GQA_GROUP
    lane_half = lax.broadcasted_iota(jnp.int32, (nrow, LANES), 1) // HEAD_DIM
    lhs_cols = [jnp.zeros((nrow, LANES), F32) for _ in range(KV_WIDTH // LANES)]
    for qc in range(ATTN_WIDTH // LANES):
        g = qc * HEADS_PER_COL // GQA_GROUP
        src = qrep[:, qc * LANES:(qc + 1) * LANES]
        src_swapped = _swap_halves(src)
        kv_col, kv_half = g // HEADS_PER_COL, g % HEADS_PER_COL
        for hh in range(HEADS_PER_COL):
            j = (qc * HEADS_PER_COL + hh) % GQA_GROUP
            here = (row_g == g) & (row_j == j) & (lane_half == kv_half)
            lhs_cols[kv_col] = lhs_cols[kv_col] + jnp.where(
                here, src if hh == kv_half else src_swapped, 0.0)
    lhs = jnp.concatenate(lhs_cols, axis=1).astype(BF16)
    scores = []
    for b in range(sb):
        rows = slice(b * rows_per_seq, (b + 1) * rows_per_seq)
        scores.append(lax.dot_general(lhs[rows], kall_ref[b].astype(BF16),
                                      (((1,), (1,)), ((), ())), preferred_element_type=F32))
    s = jnp.concatenate(scores, axis=0)
    t_row = lax.broadcasted_iota(jnp.int32, (nrow, SAMPLE_KEYS), 0) % dec_seq
    kj = lax.broadcasted_iota(jnp.int32, (nrow, SAMPLE_KEYS), 1)
    mask = (kj >= t_row + (win - WINDOW)) & (kj <= t_row + win)
    p = _sink_softmax(s, mask, sink_ref[...]).astype(BF16)
    outs = []
    for b in range(sb):
        rows = slice(b * rows_per_seq, (b + 1) * rows_per_seq)
        outs.append(jnp.dot(p[rows], vall_ref[b].astype(BF16), preferred_element_type=F32))
    o = jnp.concatenate(outs, axis=0)
    z_cols = []
    for qc in range(ATTN_WIDTH // LANES):
        g = qc * HEADS_PER_COL // GQA_GROUP
        kv_col, kv_half = g // HEADS_PER_COL, g % HEADS_PER_COL
        src = o[:, kv_col * LANES:(kv_col + 1) * LANES]
        src_swapped = _swap_halves(src)
        zc = jnp.zeros((nrow, LANES), F32)
        for hh in range(HEADS_PER_COL):
            j = (qc * HEADS_PER_COL + hh) % GQA_GROUP
            here = (row_g == g) & (row_j == j) & (lane_half == hh)
            zc = zc + jnp.where(here, src if hh == kv_half else src_swapped, 0.0)
        z_cols.append(zc)
    z = jnp.concatenate(z_cols, axis=1).astype(BF16)
    rt = lax.broadcasted_iota(jnp.int32, (ntok, nrow), 0)
    ct = lax.broadcasted_iota(jnp.int32, (ntok, nrow), 1)
    unpick = (rt == (ct // rows_per_seq) * dec_seq + ct % dec_seq)
    attn = jnp.dot(unpick.astype(BF16), z, preferred_element_type=F32)
    mix_ref[:, POOL_WIDTH:] = attn.astype(BF16)


def _mix_sample(sink_rows, u, q, k_new, v_new, state_t, cache_k, cache_v, wpool, pscale, dec_seq):
    nseq, win, _ = cache_k.shape
    sb = SEQ_BLOCK
    ntok = sb * dec_seq
    row = lambda w: pl.BlockSpec((ntok, w), lambda i: (i, 0))
    slab = pl.BlockSpec((POOL_HIST, sb, POOL_WIDTH), lambda i: (0, i, 0))
    u_slab = pl.BlockSpec((dec_seq, sb, POOL_WIDTH), lambda i: (0, i, 0))
    cache = pl.BlockSpec((sb, win, KV_WIDTH), lambda i: (i, 0, 0))
    full = lambda a: pl.BlockSpec(a.shape, lambda i: (0,) * a.ndim)
    return pl.pallas_call(
        functools.partial(_mix_sample_kernel, dec_seq=dec_seq),
        grid=(nseq // sb,),
        in_specs=[full(sink_rows), u_slab, row(ATTN_WIDTH), row(KV_WIDTH), row(KV_WIDTH),
                  slab, cache, cache, full(wpool), full(pscale)],
        out_specs=[row(MIX_WIDTH), slab, cache, cache],
        out_shape=[jax.ShapeDtypeStruct((nseq * dec_seq, MIX_WIDTH), BF16),
                   jax.ShapeDtypeStruct((POOL_HIST, nseq, POOL_WIDTH), F32),
                   jax.ShapeDtypeStruct(cache_k.shape, F32),
                   jax.ShapeDtypeStruct(cache_v.shape, F32)],
        scratch_shapes=[pltpu.VMEM((sb, SAMPLE_KEYS, KV_WIDTH), F32),
                        pltpu.VMEM((sb, SAMPLE_KEYS, KV_WIDTH), F32)],
        compiler_params=_params("arbitrary"),
        name="mix_sample",
    )(sink_rows, u, q, k_new, v_new, state_t, cache_k, cache_v, wpool, pscale)


def _outproj_kernel(x_ref, mix_ref, w_ref, g_ref, x1_ref, h2_ref):
    x1 = x_ref[...] + jnp.dot(mix_ref[...], w_ref[...], preferred_element_type=F32)
    x1_ref[...] = x1
    h2_ref[...] = _rms(x1, g_ref[...]).astype(BF16)


def _outproj(x, mix, w_out, g):
    t = x.shape[0]
    tm = TOKEN_TILE
    row = lambda w: pl.BlockSpec((tm, w), lambda i: (i, 0))
    full = lambda a: pl.BlockSpec(a.shape, lambda i: (0,) * a.ndim)
    return pl.pallas_call(
        _outproj_kernel,
        grid=(t // tm,),
        in_specs=[row(D_MODEL), row(MIX_WIDTH), full(w_out), full(g)],
        out_specs=[row(D_MODEL), row(D_MODEL)],
        out_shape=[jax.ShapeDtypeStruct((t, D_MODEL), F32),
                   jax.ShapeDtypeStruct((t, D_MODEL), BF16)],
        compiler_params=_params("parallel"),
        name="outproj",
    )(x, mix, w_out, g)


FFN_GU_SLOTS = 4
FFN_DOWN_SLOTS = 3


def _ffn_kernel(x1_ref, h2_ref, gf_ref, wg_hbm, wu_hbm, wd_hbm, y_ref,
                wg_buf, wu_buf, wd_buf, act_ref, gu_sem, wd_sem, *, final_norm):
    tf = wd_buf.shape[1]
    nchunks = wd_hbm.shape[0] // tf
    half = tf // 2
    i = pl.program_id(0)
    base = i * nchunks

    def gu_copies(n):
        s = lax.rem(base + n, FFN_GU_SLOTS)
        cols = pl.ds((n % nchunks) * tf, tf)
        return (pltpu.make_async_copy(wg_hbm.at[:, cols], wg_buf.at[s], gu_sem.at[0, s]),
                pltpu.make_async_copy(wu_hbm.at[:, cols], wu_buf.at[s], gu_sem.at[1, s]))

    def down_copy(n):
        s = lax.rem(base + n, FFN_DOWN_SLOTS)
        rows = pl.ds((n % nchunks) * tf, tf)
        return pltpu.make_async_copy(wd_hbm.at[rows, :], wd_buf.at[s], wd_sem.at[s])

    def start(copies):
        for cp in copies:
            cp.start()

    def wait(copies):
        for cp in copies:
            cp.wait()

    def gate_up_half(c, hh):
        s = lax.rem(base + c, FFN_GU_SLOTS)
        h = h2_ref[...]
        cols = slice(hh * half, (hh + 1) * half)
        return (jnp.dot(h, wg_buf[s, :, cols], preferred_element_type=F32),
                jnp.dot(h, wu_buf[s, :, cols], preferred_element_type=F32))

    def store_act(c, gu):
        s = lax.rem(base + c, 2)
        for hh, (gate, up) in enumerate(gu):
            act_ref[s, :, hh * half:(hh + 1) * half] = (jax.nn.silu(gate) * up).astype(BF16)

    def down(c):
        y_ref[...] += jnp.dot(act_ref[lax.rem(base + c, 2)], wd_buf[lax.rem(base + c, FFN_DOWN_SLOTS)],
                              preferred_element_type=F32)

    def refill(c):
        if c >= 0:
            down_copy(c).wait()
        if c + 2 < nchunks:
            wait(gu_copies(c + 2))
        if FFN_GU_SLOTS - 1 <= c + FFN_GU_SLOTS < nchunks + FFN_GU_SLOTS - 1:
            start(gu_copies(c + FFN_GU_SLOTS))
        if c >= 0:
            down_copy(c + FFN_DOWN_SLOTS - 1).start()

    @pl.when(i == 0)
    def _():
        for n in range(FFN_GU_SLOTS - 1):
            start(gu_copies(n))
        for n in range(FFN_DOWN_SLOTS - 1):
            down_copy(n).start()

    y_ref[...] = x1_ref[...]
    wait(gu_copies(0))
    for c in range(-1, nchunks):
        if c + 1 < nchunks:
            first = gate_up_half(c + 1, 0)
            refill(c)
            gu = [first, gate_up_half(c + 1, 1)]
        else:
            refill(c)
        if c >= 0:
            down(c)
        if c + 1 < nchunks:
            store_act(c + 1, gu)
    if final_norm:
        y_ref[...] = _rms(y_ref[...], gf_ref[...])

    @pl.when(i + 1 == pl.num_programs(0))
    def _():
        for n in range(FFN_GU_SLOTS - 1):
            wait(gu_copies(nchunks + n))
        for n in range(FFN_DOWN_SLOTS - 1):
            down_copy(nchunks + n).wait()


def _ffn(x1, h2, wg, wu, wd, gf, final_norm):
    t = x1.shape[0]
    tm, tf = TOKEN_TILE, FF_TILE
    assert D_FF % tf == 0 and D_FF // tf >= FFN_GU_SLOTS
    row = pl.BlockSpec((tm, D_MODEL), lambda i: (i, 0))
    hbm = pl.BlockSpec(memory_space=pl.ANY)
    return pl.pallas_call(
        functools.partial(_ffn_kernel, final_norm=final_norm),
        grid=(t // tm,),
        in_specs=[row, row, pl.BlockSpec(gf.shape, lambda i: (0, 0)), hbm, hbm, hbm],
        out_specs=row,
        out_shape=jax.ShapeDtypeStruct((t, D_MODEL), F32),
        scratch_shapes=[pltpu.VMEM((FFN_GU_SLOTS, D_MODEL, tf), BF16),
                        pltpu.VMEM((FFN_GU_SLOTS, D_MODEL, tf), BF16),
                        pltpu.VMEM((FFN_DOWN_SLOTS, tf, D_MODEL), BF16),
                        pltpu.VMEM((2, tm, tf), BF16),
                        pltpu.SemaphoreType.DMA((2, FFN_GU_SLOTS)),
                        pltpu.SemaphoreType.DMA((FFN_DOWN_SLOTS,))],
        compiler_params=_params("arbitrary"),
        name="ffn",
    )(x1, h2, gf, wg, wu, wd)


def _rope_tables(pos):
    inv = ROPE_THETA ** (-jnp.arange(0, ROPE_DIM, 2, dtype=F32) / ROPE_DIM)
    ang = pos.astype(F32)[:, None] * inv[None, :]
    cos, sin = jnp.cos(ang), jnp.sin(ang)
    n = pos.shape[0]
    rest = HEAD_DIM - ROPE_DIM
    one, zero = jnp.ones((n, rest), F32), jnp.zeros((n, rest), F32)
    zh = jnp.zeros((n, ROPE_HALF), F32)
    c = jnp.tile(jnp.concatenate([cos, cos, one], axis=1), (1, HEADS_PER_COL))
    s1 = jnp.tile(jnp.concatenate([-sin, zh, zero], axis=1), (1, HEADS_PER_COL))
    s2 = jnp.tile(jnp.concatenate([zh, sin, zero], axis=1), (1, HEADS_PER_COL))
    return c, s1, s2


def kernel(x_prompt, x_sample, state_pool, cache_k_win, cache_v_win, g_mix, w_in, w_pool,
           pool_scale, attn_sinks, w_out, g_ffn, w_gate, w_up, w_down, g_final):
    batch, seq, _ = x_prompt.shape
    nseq, dec_seq, _ = x_sample.shape
    depth = w_in.shape[0]
    win_s = cache_k_win.shape[2]
    ntok_s = nseq * dec_seq
    assert seq % TOKEN_TILE == 0 and ntok_s == TOKEN_TILE and nseq % SEQ_BLOCK == 0
    assert win_s == WINDOW and win_s + dec_seq <= SAMPLE_KEYS and seq >= WINDOW
    assert PAST_LEN >= max(POOL_HIST, WINDOW)

    xp = x_prompt.reshape(batch * seq, D_MODEL)
    xs = x_sample.reshape(ntok_s, D_MODEL)
    tabs_p = _rope_tables(jnp.arange(seq))
    tabs_s = _rope_tables(PAST_LEN + jnp.arange(ntok_s) % dec_seq)
    gf = g_final.reshape(1, D_MODEL)

    outs = [[] for _ in range(6)]
    for l in range(depth):
        w_in_l, w_out_l, wpool_l = w_in[l].astype(BF16), w_out[l].astype(BF16), w_pool[l].astype(BF16)
        wg_l, wu_l, wd_l = w_gate[l].astype(BF16), w_up[l].astype(BF16), w_down[l].astype(BF16)
        gm = g_mix[l].reshape(1, D_MODEL)
        gn = g_ffn[l].reshape(1, D_MODEL)
        pscale = pool_scale[l].reshape(1, POOL_WIDTH)
        sink_rows = jnp.tile(jnp.repeat(attn_sinks[l], dec_seq), SEQ_BLOCK)[:, None]

        u, q, kd, vt, u_tail, k_tail, v_tail = _proj(xp, gm, w_in_l, tabs_p, seq // TOKEN_TILE, WINDOW)
        mix = _mix_prompt(attn_sinks[l], u, q, kd, vt, wpool_l, pscale, batch, seq)
        x1, h2 = _outproj(xp, mix, w_out_l, gn)
        xp = _ffn(x1, h2, wg_l, wu_l, wd_l, gf, l == depth - 1)
        outs[0].append(u_tail.reshape(batch, HALO, POOL_WIDTH)[:, HALO - POOL_HIST:])
        outs[1].append(k_tail.reshape(batch, WINDOW, N_KV_HEADS, HEAD_DIM))
        outs[2].append(v_tail.reshape(batch, WINDOW, N_KV_HEADS, HEAD_DIM))

        u, q, _, _, _, k_new, v_new = _proj(xs, gm, w_in_l, tabs_s, 1, TOKEN_TILE)
        state_t = jnp.transpose(state_pool[l], (1, 0, 2))
        ck = cache_k_win[l].reshape(nseq, win_s, KV_WIDTH)
        cv = cache_v_win[l].reshape(nseq, win_s, KV_WIDTH)
        u_t = jnp.transpose(u.reshape(nseq, dec_seq, POOL_WIDTH), (1, 0, 2))
        mix, npool, nk, nv = _mix_sample(sink_rows, u_t, q, k_new, v_new, state_t, ck, cv, wpool_l,
                                         pscale, dec_seq)
        x1, h2 = _outproj(xs, mix, w_out_l, gn)
        xs = _ffn(x1, h2, wg_l, wu_l, wd_l, gf, l == depth - 1)
        outs[3].append(jnp.transpose(npool, (1, 0, 2)))
        outs[4].append(nk.reshape(nseq, win_s, N_KV_HEADS, HEAD_DIM))
        outs[5].append(nv.reshape(nseq, win_s, N_KV_HEADS, HEAD_DIM))

    y_prompt = xp.reshape(batch, seq, D_MODEL)
    y_sample = xs.reshape(nseq, dec_seq, D_MODEL)
    return (y_prompt, y_sample) + tuple(jnp.stack(o) for o in outs)
```

```python
import functools

import jax
import jax.numpy as jnp
from jax import lax
from jax.experimental import pallas as pl
from jax.experimental.pallas import tpu as pltpu

F32 = jnp.float32
BF16 = jnp.bfloat16

D_MODEL = 2048
POOL_WIDTH = D_MODEL // 2
POOL_WINDOWS = (2, 4, 8, 16)
N_POOL_GROUPS = len(POOL_WINDOWS)
POOL_GROUP = POOL_WIDTH // N_POOL_GROUPS
POOL_HIST = max(POOL_WINDOWS) - 1
HEAD_DIM = 64
N_HEADS = (D_MODEL - POOL_WIDTH) // HEAD_DIM
N_KV_HEADS = 4
GQA_GROUP = N_HEADS // N_KV_HEADS
ATTN_WIDTH = N_HEADS * HEAD_DIM
KV_WIDTH = N_KV_HEADS * HEAD_DIM
IN_WIDTH = POOL_WIDTH + ATTN_WIDTH + 2 * KV_WIDTH
MIX_WIDTH = POOL_WIDTH + ATTN_WIDTH
WINDOW = 128
BLOCK = 128
ROPE_DIM = HEAD_DIM // 4
ROPE_HALF = ROPE_DIM // 2
ROPE_THETA = 500000.0
D_FF = ((8 * D_MODEL // 3 + 255) // 256) * 256
EPS = 1e-5
PAST_LEN = 16384
Q_SCALE = HEAD_DIM ** -0.5

LANES = 128
SUBLANES = 8
HEADS_PER_COL = LANES // HEAD_DIM
HALO = 16
POOL_TOP = 2 * HALO
TOKEN_TILE = 512
FF_TILE = 512
SEQ_BLOCK = 8
SAMPLE_KEYS = 256
VMEM_LIMIT = 56 * 1024 * 1024


def _params(*semantics):
    return pltpu.CompilerParams(dimension_semantics=semantics, vmem_limit_bytes=VMEM_LIMIT)


def _rms(x, g):
    ms = jnp.mean(x * x, axis=-1, keepdims=True)
    return x * lax.rsqrt(ms + EPS) * g


def _swap_halves(z):
    return pltpu.roll(z, HEAD_DIM, 1)


def _cast_rows(in_refs, out_refs):
    for src, dst in zip(in_refs, out_refs):
        dst[...] = src[...].astype(BF16)


def _cast_specs(weights, steps):
    for w in weights:
        assert w.shape[0] % (steps * 2 * SUBLANES) == 0
    specs = [pl.BlockSpec((w.shape[0] // steps, w.shape[1]), lambda i: (i, 0)) for w in weights]
    return specs, [jax.ShapeDtypeStruct(w.shape, BF16) for w in weights]


def _proj_kernel(x_ref, g_ref, w_ref, c_ref, s1_ref, s2_ref, *refs):
    ncast = (len(refs) - 7) // 2
    u_ref, q_ref, kd_ref, vt_ref, utail_ref, ktail_ref, vtail_ref = refs[ncast:ncast + 7]
    _cast_rows(refs[:ncast], refs[ncast + 7:])
    tm = x_ref.shape[0]
    tail = ktail_ref.shape[0]
    h = _rms(x_ref[...], g_ref[...]).astype(BF16)
    proj = jnp.dot(h, w_ref[...], preferred_element_type=F32)
    u = proj[:, :POOL_WIDTH]
    u_ref[...] = u
    utail_ref[...] = u[tm - HALO:, :]
    c, s1, s2 = c_ref[...], s1_ref[...], s2_ref[...]

    def rope(z):
        return (z * c + pltpu.roll(z, LANES - ROPE_HALF, 1) * s1
                + pltpu.roll(z, ROPE_HALF, 1) * s2)

    for col in range(ATTN_WIDTH // LANES):
        z = proj[:, POOL_WIDTH + col * LANES:POOL_WIDTH + (col + 1) * LANES]
        q_ref[:, col * LANES:(col + 1) * LANES] = (rope(z) * Q_SCALE).astype(BF16)
    low_half = lax.broadcasted_iota(jnp.int32, (1, LANES), 1) < HEAD_DIM
    k0 = POOL_WIDTH + ATTN_WIDTH
    for a in range(KV_WIDTH // LANES):
        z = rope(proj[:, k0 + a * LANES:k0 + (a + 1) * LANES])
        ktail_ref[:, a * LANES:(a + 1) * LANES] = z[tm - tail:, :]
        zr = _swap_halves(z)
        kd_ref[:, (2 * a) * LANES:(2 * a + 1) * LANES] = jnp.where(low_half, z, zr).astype(BF16)
        kd_ref[:, (2 * a + 1) * LANES:(2 * a + 2) * LANES] = jnp.where(low_half, zr, z).astype(BF16)
    v = proj[:, k0 + KV_WIDTH:]
    vtail_ref[...] = v[tm - tail:, :]
    vt_ref[...] = v.T.astype(BF16)


def _proj(x, g, w_in, tabs, tiles_per_seq, tail, cast=()):
    t = x.shape[0]
    tm = TOKEN_TILE
    steps = t // tm
    nseq = steps // tiles_per_seq
    row = lambda w: pl.BlockSpec((tm, w), lambda i: (i, 0))
    tab = pl.BlockSpec((tm, LANES), lambda i: (i % tiles_per_seq, 0))
    const = lambda a: pl.BlockSpec(a.shape, lambda i: (0,) * a.ndim, pipeline_mode=pl.Buffered(1))
    per_seq = lambda r, w: pl.BlockSpec((r, w), lambda i: (i // tiles_per_seq, 0))
    cast_specs, cast_shapes = _cast_specs(cast, steps)
    return pl.pallas_call(
        _proj_kernel,
        grid=(steps,),
        in_specs=[row(D_MODEL), const(g), const(w_in), tab, tab, tab] + cast_specs,
        out_specs=[row(POOL_WIDTH), row(ATTN_WIDTH), row(2 * KV_WIDTH),
                   pl.BlockSpec((KV_WIDTH, tm), lambda i: (0, i)),
                   per_seq(HALO, POOL_WIDTH), per_seq(tail, KV_WIDTH), per_seq(tail, KV_WIDTH)]
        + cast_specs,
        out_shape=[jax.ShapeDtypeStruct((t, POOL_WIDTH), F32),
                   jax.ShapeDtypeStruct((t, ATTN_WIDTH), BF16),
                   jax.ShapeDtypeStruct((t, 2 * KV_WIDTH), BF16),
                   jax.ShapeDtypeStruct((KV_WIDTH, t), BF16),
                   jax.ShapeDtypeStruct((nseq * HALO, POOL_WIDTH), F32),
                   jax.ShapeDtypeStruct((nseq * tail, KV_WIDTH), F32),
                   jax.ShapeDtypeStruct((nseq * tail, KV_WIDTH), F32)] + cast_shapes,
        compiler_params=_params("arbitrary"),
        name="proj",
    )(x, g, w_in, *tabs, *cast)


def _mix_prompt_kernel(sink_ref, u_ref, halo_ref, q_ref, kdc_ref, kdp_ref, vtc_ref, vtp_ref,
                       wpool_ref, pscale_ref, mix_ref, ext_ref, lvl_ref, kwin_ref, vtwin_ref, bias_ref):
    tq = u_ref.shape[0]
    i = pl.program_id(1)
    first = i == 0

    top = POOL_TOP
    ext_ref[0:top - HALO, :] = jnp.zeros((top - HALO, POOL_WIDTH), F32)
    ext_ref[top - HALO:top, :] = halo_ref[...] * jnp.where(first, 0.0, 1.0)
    ext_ref[top:, :] = u_ref[...]
    lvl_ref[:, 0:SUBLANES, :] = jnp.zeros((2, SUBLANES, POOL_GROUP), F32)
    nlive = top + tq - SUBLANES
    pos1 = i * tq + lax.broadcasted_iota(jnp.int32, (tq, 1), 0) + 1
    for gi, w in enumerate(POOL_WINDOWS):
        cols = slice(gi * POOL_GROUP, (gi + 1) * POOL_GROUP)
        src = ext_ref.at[:, cols]
        shift, slot = 1, 0
        while shift < w:
            dst = lvl_ref.at[slot]
            dst[SUBLANES:, :] = (src[SUBLANES:SUBLANES + nlive, :]
                                 + src[SUBLANES - shift:SUBLANES - shift + nlive, :])
            src, shift, slot = dst, 2 * shift, 1 - slot
        tok = ext_ref[top:, cols]
        inv_cnt = 1.0 / jnp.minimum(pos1, w).astype(F32)
        d = (src[top:, :] * inv_cnt - tok).astype(BF16)
        po = jnp.dot(d, wpool_ref[gi], preferred_element_type=F32) * pscale_ref[:, cols]
        mix_ref[:, cols] = po.astype(BF16)

    kwin_ref[0:BLOCK, :] = kdp_ref[...]
    kwin_ref[BLOCK:, :] = kdc_ref[...]
    vtwin_ref[:, 0:BLOCK] = vtp_ref[...]
    vtwin_ref[:, BLOCK:] = vtc_ref[...]
    kj = lax.broadcasted_iota(jnp.int32, (2 * BLOCK, BLOCK), 0)
    qi = lax.broadcasted_iota(jnp.int32, (2 * BLOCK, BLOCK), 1)
    band = (kj >= qi) & (kj <= qi + WINDOW)
    first_lo = jnp.where(first, BLOCK, 0)
    bias_ref[0] = jnp.where(band, 0.0, -jnp.inf)
    bias_ref[1] = jnp.where(band & (kj >= first_lo), 0.0, -jnp.inf)
    lane = lax.broadcasted_iota(jnp.int32, (1, LANES), 1)
    keep_half = [(lane // HEAD_DIM == hh).astype(BF16) for hh in range(HEADS_PER_COL)]

    def scores(n, g):
        rows = slice(n * BLOCK, (n + 1) * BLOCK)
        qs = jnp.concatenate(
            [q_ref[rows, (h // HEADS_PER_COL) * LANES:(h // HEADS_PER_COL + 1) * LANES]
             * keep_half[h % HEADS_PER_COL]
             for h in range(GQA_GROUP * g, GQA_GROUP * (g + 1))], axis=0)
        kd = kwin_ref[n * BLOCK:(n + 2) * BLOCK, g * LANES:(g + 1) * LANES]
        return lax.dot_general(kd, qs, (((1,), (1,)), ((), ())), preferred_element_type=F32)

    def attend(n, g, s_t):
        p_t = []
        for j in range(GQA_GROUP):
            s = s_t[:, j * BLOCK:(j + 1) * BLOCK] + bias_ref[1 if n == 0 else 0]
            m = jnp.max(s, axis=0, keepdims=True)
            p = jnp.exp(s - m)
            den = jnp.sum(p, axis=0, keepdims=True) + jnp.exp(sink_ref[GQA_GROUP * g + j] - m)
            p_t.append((p * (1.0 / den)).astype(BF16))
        vt = vtwin_ref[g * HEAD_DIM:(g + 1) * HEAD_DIM, n * BLOCK:(n + 2) * BLOCK]
        o_t = jnp.dot(vt, jnp.concatenate(p_t, axis=1), preferred_element_type=F32)
        for c in range(GQA_GROUP // HEADS_PER_COL):
            col_t = jnp.concatenate(
                [o_t[:, (HEADS_PER_COL * c + hh) * BLOCK:(HEADS_PER_COL * c + hh + 1) * BLOCK]
                 for hh in range(HEADS_PER_COL)], axis=0)
            col = POOL_WIDTH + (GQA_GROUP // HEADS_PER_COL * g + c) * LANES
            mix_ref[n * BLOCK:(n + 1) * BLOCK, col:col + LANES] = col_t.T.astype(BF16)

    work = [(n, g) for n in range(tq // BLOCK) for g in range(N_KV_HEADS)]
    s_next = scores(*work[0])
    for idx, (n, g) in enumerate(work):
        s_t = s_next
        if idx + 1 < len(work):
            s_next = scores(*work[idx + 1])
        attend(n, g, s_t)


def _mix_prompt(sinks, u, q, kd, vt, wpool, pscale, batch, seq):
    tq = TOKEN_TILE
    nt = seq // tq
    bpt = tq // BLOCK
    hpt = tq // HALO
    prev_block = lambda b, i: jnp.maximum((b * nt + i) * bpt - 1, 0)
    row = lambda w: pl.BlockSpec((tq, w), lambda b, i: (b * nt + i, 0))
    halo = pl.BlockSpec((HALO, POOL_WIDTH), lambda b, i: (jnp.maximum((b * nt + i) * hpt - 1, 0), 0))
    full = lambda a: pl.BlockSpec(a.shape, lambda b, i: (0,) * a.ndim)
    return pl.pallas_call(
        _mix_prompt_kernel,
        grid=(batch, nt),
        in_specs=[pl.BlockSpec(memory_space=pltpu.SMEM), row(POOL_WIDTH), halo, row(ATTN_WIDTH),
                  row(2 * KV_WIDTH),
                  pl.BlockSpec((BLOCK, 2 * KV_WIDTH), lambda b, i: (prev_block(b, i), 0)),
                  pl.BlockSpec((KV_WIDTH, tq), lambda b, i: (0, b * nt + i)),
                  pl.BlockSpec((KV_WIDTH, BLOCK), lambda b, i: (0, prev_block(b, i))),
                  full(wpool), full(pscale)],
        out_specs=row(MIX_WIDTH),
        out_shape=jax.ShapeDtypeStruct((batch * seq, MIX_WIDTH), BF16),
        scratch_shapes=[pltpu.VMEM((POOL_TOP + tq, POOL_WIDTH), F32),
                        pltpu.VMEM((2, POOL_TOP + tq, POOL_GROUP), F32),
                        pltpu.VMEM((BLOCK + tq, 2 * KV_WIDTH), BF16),
                        pltpu.VMEM((KV_WIDTH, BLOCK + tq), BF16),
                        pltpu.VMEM((2, 2 * BLOCK, BLOCK), F32)],
        compiler_params=_params("parallel", "parallel"),
        name="mix_prompt",
    )(sinks, u, u, q, kd, kd, vt, vt, wpool, pscale)


def _sink_softmax(s, mask, sink):
    s = jnp.where(mask, s, -jnp.inf)
    m = jnp.maximum(jnp.max(s, axis=1, keepdims=True), sink)
    p = jnp.exp(s - m)
    den = jnp.sum(p, axis=1, keepdims=True) + jnp.exp(sink - m)
    return p / den


def _mix_sample_kernel(sink_ref, u_ref, q_ref, kn_ref, vn_ref, st_ref, ck_ref, cv_ref,
                       wpool_ref, pscale_ref, mix_ref, npool_ref, nk_ref, nv_ref,
                       kall_ref, vall_ref, *, dec_seq):
    sb = SEQ_BLOCK
    rows_per_seq = N_HEADS * dec_seq
    nrow = sb * rows_per_seq
    ntok = sb * dec_seq
    win = ck_ref.shape[1]

    @pl.when(pl.program_id(0) == 0)
    def _():
        kall_ref[...] = jnp.zeros_like(kall_ref)
        vall_ref[...] = jnp.zeros_like(vall_ref)

    ext = [st_ref[h] for h in range(POOL_HIST)]
    ext += [u_ref[t] for t in range(dec_seq)]
    for h in range(POOL_HIST):
        npool_ref[h] = ext[h + dec_seq]
    r_out = lax.broadcasted_iota(jnp.int32, (ntok, ntok), 0)
    r_in = lax.broadcasted_iota(jnp.int32, (ntok, ntok), 1)
    to_seq_major = ((r_in % sb) * dec_seq + r_in // sb == r_out).astype(BF16)
    for gi, w in enumerate(POOL_WINDOWS):
        cols = slice(gi * POOL_GROUP, (gi + 1) * POOL_GROUP)
        ds = []
        for t in range(dec_seq):
            tok = ext[POOL_HIST + t][:, cols]
            acc = tok
            for j in range(1, w):
                acc = acc + ext[POOL_HIST + t - j][:, cols]
            ds.append(acc * (1.0 / w) - tok)
        d = jnp.concatenate(ds, axis=0).astype(BF16)
        po = jnp.dot(d, wpool_ref[gi], preferred_element_type=F32) * pscale_ref[:, cols]
        po = jnp.dot(to_seq_major, po.astype(BF16), preferred_element_type=F32)
        mix_ref[:, cols] = po.astype(BF16)

    for b in range(sb):
        tok_rows = slice(b * dec_seq, (b + 1) * dec_seq)
        kall_ref[b, 0:win, :] = ck_ref[b]
        kall_ref[b, win:win + dec_seq, :] = kn_ref[tok_rows, :]
        vall_ref[b, 0:win, :] = cv_ref[b]
        vall_ref[b, win:win + dec_seq, :] = vn_ref[tok_rows, :]
        nk_ref[b] = kall_ref[b, dec_seq:dec_seq + win, :]
        nv_ref[b] = vall_ref[b, dec_seq:dec_seq + win, :]

    r = lax.broadcasted_iota(jnp.int32, (nrow, ntok), 0)
    c = lax.broadcasted_iota(jnp.int32, (nrow, ntok), 1)
    pick = (c == (r // rows_per_seq) * dec_seq + r % dec_seq)
    qrep = jnp.dot(pick.astype(BF16), q_ref[...], preferred_element_type=F32)
    rr = lax.broadcasted_iota(jnp.int32, (nrow, LANES), 0) % rows_per_seq
    row_g = rr // (GQA_GROUP * dec_seq)
    row_j = (rr // dec_seq) % GQA_GROUP
    lane_half = lax.broadcasted_iota(jnp.int32, (nrow, LANES), 1) // HEAD_DIM
    lhs_cols = [jnp.zeros((nrow, LANES), F32) for _ in range(KV_WIDTH // LANES)]
    for qc in range(ATTN_WIDTH // LANES):
        g = qc * HEADS_PER_COL // GQA_GROUP
        src = qrep[:, qc * LANES:(qc + 1) * LANES]
        src_swapped = _swap_halves(src)
        kv_col, kv_half = g // HEADS_PER_COL, g % HEADS_PER_COL
        for hh in range(HEADS_PER_COL):
            j = (qc * HEADS_PER_COL + hh) % GQA_GROUP
            here = (row_g == g) & (row_j == j) & (lane_half == kv_half)
            lhs_cols[kv_col] = lhs_cols[kv_col] + jnp.where(
                here, src if hh == kv_half else src_swapped, 0.0)
    lhs = jnp.concatenate(lhs_cols, axis=1).astype(BF16)
    scores = []
    for b in range(sb):
        rows = slice(b * rows_per_seq, (b + 1) * rows_per_seq)
        scores.append(lax.dot_general(lhs[rows], kall_ref[b].astype(BF16),
                                      (((1,), (1,)), ((), ())), preferred_element_type=F32))
    s = jnp.concatenate(scores, axis=0)
    t_row = lax.broadcasted_iota(jnp.int32, (nrow, SAMPLE_KEYS), 0) % dec_seq
    kj = lax.broadcasted_iota(jnp.int32, (nrow, SAMPLE_KEYS), 1)
    mask = (kj >= t_row + (win - WINDOW)) & (kj <= t_row + win)
    p = _sink_softmax(s, mask, sink_ref[...]).astype(BF16)
    outs = []
    for b in range(sb):
        rows = slice(b * rows_per_seq, (b + 1) * rows_per_seq)
        outs.append(jnp.dot(p[rows], vall_ref[b].astype(BF16), preferred_element_type=F32))
    o = jnp.concatenate(outs, axis=0)
    z_cols = []
    for qc in range(ATTN_WIDTH // LANES):
        g = qc * HEADS_PER_COL // GQA_GROUP
        kv_col, kv_half = g // HEADS_PER_COL, g % HEADS_PER_COL
        src = o[:, kv_col * LANES:(kv_col + 1) * LANES]
        src_swapped = _swap_halves(src)
        zc = jnp.zeros((nrow, LANES), F32)
        for hh in range(HEADS_PER_COL):
            j = (qc * HEADS_PER_COL + hh) % GQA_GROUP
            here = (row_g == g) & (row_j == j) & (lane_half == hh)
            zc = zc + jnp.where(here, src if hh == kv_half else src_swapped, 0.0)
        z_cols.append(zc)
    z = jnp.concatenate(z_cols, axis=1).astype(BF16)
    rt = lax.broadcasted_iota(jnp.int32, (ntok, nrow), 0)
    ct = lax.broadcasted_iota(jnp.int32, (ntok, nrow), 1)
    unpick = (rt == (ct // rows_per_seq) * dec_seq + ct % dec_seq)
    attn = jnp.dot(unpick.astype(BF16), z, preferred_element_type=F32)
    mix_ref[:, POOL_WIDTH:] = attn.astype(BF16)


def _mix_sample(sink_rows, u, q, k_new, v_new, state_t, cache_k, cache_v, wpool, pscale, dec_seq):
    nseq, win, _ = cache_k.shape
    sb = SEQ_BLOCK
    ntok = sb * dec_seq
    row = lambda w: pl.BlockSpec((ntok, w), lambda i: (i, 0))
    slab = pl.BlockSpec((POOL_HIST, sb, POOL_WIDTH), lambda i: (0, i, 0))
    u_slab = pl.BlockSpec((dec_seq, sb, POOL_WIDTH), lambda i: (0, i, 0))
    cache = pl.BlockSpec((sb, win, KV_WIDTH), lambda i: (i, 0, 0))
    full = lambda a: pl.BlockSpec(a.shape, lambda i: (0,) * a.ndim)
    return pl.pallas_call(
        functools.partial(_mix_sample_kernel, dec_seq=dec_seq),
        grid=(nseq // sb,),
        in_specs=[full(sink_rows), u_slab, row(ATTN_WIDTH), row(KV_WIDTH), row(KV_WIDTH),
                  slab, cache, cache, full(wpool), full(pscale)],
        out_specs=[row(MIX_WIDTH), slab, cache, cache],
        out_shape=[jax.ShapeDtypeStruct((nseq * dec_seq, MIX_WIDTH), BF16),
                   jax.ShapeDtypeStruct((POOL_HIST, nseq, POOL_WIDTH), F32),
                   jax.ShapeDtypeStruct(cache_k.shape, F32),
                   jax.ShapeDtypeStruct(cache_v.shape, F32)],
        scratch_shapes=[pltpu.VMEM((sb, SAMPLE_KEYS, KV_WIDTH), F32),
                        pltpu.VMEM((sb, SAMPLE_KEYS, KV_WIDTH), F32)],
        compiler_params=_params("arbitrary"),
        name="mix_sample",
    )(sink_rows, u, q, k_new, v_new, state_t, cache_k, cache_v, wpool, pscale)


def _outproj_kernel(x_ref, mix_ref, w_ref, g_ref, *refs):
    ncast = (len(refs) - 2) // 2
    x1_ref, h2_ref = refs[ncast:ncast + 2]
    _cast_rows(refs[:ncast], refs[ncast + 2:])
    x1 = x_ref[...] + jnp.dot(mix_ref[...], w_ref[...], preferred_element_type=F32)
    x1_ref[...] = x1
    h2_ref[...] = _rms(x1, g_ref[...]).astype(BF16)


def _outproj(x, mix, w_out, g, cast=()):
    t = x.shape[0]
    tm = TOKEN_TILE
    steps = t // tm
    row = lambda w: pl.BlockSpec((tm, w), lambda i: (i, 0))
    const = lambda a: pl.BlockSpec(a.shape, lambda i: (0,) * a.ndim, pipeline_mode=pl.Buffered(1))
    cast_specs, cast_shapes = _cast_specs(cast, steps)
    return pl.pallas_call(
        _outproj_kernel,
        grid=(steps,),
        in_specs=[row(D_MODEL), row(MIX_WIDTH), const(w_out), const(g)] + cast_specs,
        out_specs=[row(D_MODEL), row(D_MODEL)] + cast_specs,
        out_shape=[jax.ShapeDtypeStruct((t, D_MODEL), F32),
                   jax.ShapeDtypeStruct((t, D_MODEL), BF16)] + cast_shapes,
        compiler_params=_params("parallel"),
        name="outproj",
    )(x, mix, w_out, g, *cast)


def _ffn_kernel(x1_ref, h2_ref, wg_ref, wu_ref, wd_ref, gf_ref, y_ref, acc_ref, *, final_norm):
    j = pl.program_id(1)

    @pl.when(j == 0)
    def _():
        acc_ref[...] = x1_ref[...]

    h = h2_ref[...]
    gate = jnp.dot(h, wg_ref[...], preferred_element_type=F32)
    up = jnp.dot(h, wu_ref[...], preferred_element_type=F32)
    act = (jax.nn.silu(gate) * up).astype(BF16)
    acc_ref[...] += jnp.dot(act, wd_ref[...], preferred_element_type=F32)

    @pl.when(j == pl.num_programs(1) - 1)
    def _():
        y = acc_ref[...]
        y_ref[...] = _rms(y, gf_ref[...]) if final_norm else y


def _ffn(x1, h2, wg, wu, wd, gf, final_norm):
    t = x1.shape[0]
    tm, tf = TOKEN_TILE, FF_TILE
    row = pl.BlockSpec((tm, D_MODEL), lambda i, j: (i, 0))
    return pl.pallas_call(
        functools.partial(_ffn_kernel, final_norm=final_norm),
        grid=(t // tm, D_FF // tf),
        in_specs=[row, row,
                  pl.BlockSpec((D_MODEL, tf), lambda i, j: (0, j)),
                  pl.BlockSpec((D_MODEL, tf), lambda i, j: (0, j)),
                  pl.BlockSpec((tf, D_MODEL), lambda i, j: (j, 0)),
                  pl.BlockSpec(gf.shape, lambda i, j: (0, 0))],
        out_specs=row,
        out_shape=jax.ShapeDtypeStruct((t, D_MODEL), F32),
        scratch_shapes=[pltpu.VMEM((tm, D_MODEL), F32)],
        compiler_params=_params("parallel", "arbitrary"),
        name="ffn",
    )(x1, h2, wg, wu, wd, gf)


def _rope_tables(pos):
    inv = ROPE_THETA ** (-jnp.arange(0, ROPE_DIM, 2, dtype=F32) / ROPE_DIM)
    ang = pos.astype(F32)[:, None] * inv[None, :]
    cos, sin = jnp.cos(ang), jnp.sin(ang)
    n = pos.shape[0]
    rest = HEAD_DIM - ROPE_DIM
    one, zero = jnp.ones((n, rest), F32), jnp.zeros((n, rest), F32)
    zh = jnp.zeros((n, ROPE_HALF), F32)
    c = jnp.tile(jnp.concatenate([cos, cos, one], axis=1), (1, HEADS_PER_COL))
    s1 = jnp.tile(jnp.concatenate([-sin, zh, zero], axis=1), (1, HEADS_PER_COL))
    s2 = jnp.tile(jnp.concatenate([zh, sin, zero], axis=1), (1, HEADS_PER_COL))
    return c, s1, s2


def kernel(x_prompt, x_sample, state_pool, cache_k_win, cache_v_win, g_mix, w_in, w_pool,
           pool_scale, attn_sinks, w_out, g_ffn, w_gate, w_up, w_down, g_final):
    batch, seq, _ = x_prompt.shape
    nseq, dec_seq, _ = x_sample.shape
    depth = w_in.shape[0]
    win_s = cache_k_win.shape[2]
    ntok_s = nseq * dec_seq
    assert seq % TOKEN_TILE == 0 and ntok_s == TOKEN_TILE and nseq % SEQ_BLOCK == 0
    assert win_s == WINDOW and win_s + dec_seq <= SAMPLE_KEYS and seq >= WINDOW
    assert PAST_LEN >= max(POOL_HIST, WINDOW)

    xp = x_prompt.reshape(batch * seq, D_MODEL)
    xs = x_sample.reshape(ntok_s, D_MODEL)
    tabs_p = _rope_tables(jnp.arange(seq))
    tabs_s = _rope_tables(PAST_LEN + jnp.arange(ntok_s) % dec_seq)
    gf = g_final.reshape(1, D_MODEL)

    outs = [[] for _ in range(6)]
    for l in range(depth):
        w_in_l, wpool_l = w_in[l].astype(BF16), w_pool[l].astype(BF16)
        gm = g_mix[l].reshape(1, D_MODEL)
        gn = g_ffn[l].reshape(1, D_MODEL)
        pscale = pool_scale[l].reshape(1, POOL_WIDTH)
        sink_rows = jnp.tile(jnp.repeat(attn_sinks[l], dec_seq), SEQ_BLOCK)[:, None]

        u, q, kd, vt, u_tail, k_tail, v_tail, w_out_l, wg_l = _proj(
            xp, gm, w_in_l, tabs_p, seq // TOKEN_TILE, WINDOW, cast=(w_out[l], w_gate[l]))
        mix = _mix_prompt(attn_sinks[l], u, q, kd, vt, wpool_l, pscale, batch, seq)
        x1, h2, wu_l, wd_l = _outproj(xp, mix, w_out_l, gn, cast=(w_up[l], w_down[l]))
        xp = _ffn(x1, h2, wg_l, wu_l, wd_l, gf, l == depth - 1)
        outs[0].append(u_tail.reshape(batch, HALO, POOL_WIDTH)[:, HALO - POOL_HIST:])
        outs[1].append(k_tail.reshape(batch, WINDOW, N_KV_HEADS, HEAD_DIM))
        outs[2].append(v_tail.reshape(batch, WINDOW, N_KV_HEADS, HEAD_DIM))

        u, q, _, _, _, k_new, v_new = _proj(xs, gm, w_in_l, tabs_s, 1, TOKEN_TILE)
        state_t = jnp.transpose(state_pool[l], (1, 0, 2))
        ck = cache_k_win[l].reshape(nseq, win_s, KV_WIDTH)
        cv = cache_v_win[l].reshape(nseq, win_s, KV_WIDTH)
        u_t = jnp.transpose(u.reshape(nseq, dec_seq, POOL_WIDTH), (1, 0, 2))
        mix, npool, nk, nv = _mix_sample(sink_rows, u_t, q, k_new, v_new, state_t, ck, cv, wpool_l,
                                         pscale, dec_seq)
        x1, h2 = _outproj(xs, mix, w_out_l, gn)
        xs = _ffn(x1, h2, wg_l, wu_l, wd_l, gf, l == depth - 1)
        outs[3].append(jnp.transpose(npool, (1, 0, 2)))
        outs[4].append(nk.reshape(nseq, win_s, N_KV_HEADS, HEAD_DIM))
        outs[5].append(nv.reshape(nseq, win_s, N_KV_HEADS, HEAD_DIM))

    y_prompt = xp.reshape(batch, seq, D_MODEL)
    y_sample = xs.reshape(nseq, dec_seq, D_MODEL)
    return (y_prompt, y_sample) + tuple(jnp.stack(o) for o in outs)
```

```python
import functools

import jax
import jax.numpy as jnp
from jax import lax
from jax.experimental import pallas as pl
from jax.experimental.pallas import tpu as pltpu

F32 = jnp.float32
BF16 = jnp.bfloat16

D_MODEL = 2048
POOL_WIDTH = D_MODEL // 2
POOL_WINDOWS = (2, 4, 8, 16)
N_POOL_GROUPS = len(POOL_WINDOWS)
POOL_GROUP = POOL_WIDTH // N_POOL_GROUPS
POOL_HIST = max(POOL_WINDOWS) - 1
HEAD_DIM = 64
N_HEADS = (D_MODEL - POOL_WIDTH) // HEAD_DIM
N_KV_HEADS = 4
GQA_GROUP = N_HEADS // N_KV_HEADS
ATTN_WIDTH = N_HEADS * HEAD_DIM
KV_WIDTH = N_KV_HEADS * HEAD_DIM
IN_WIDTH = POOL_WIDTH + ATTN_WIDTH + 2 * KV_WIDTH
MIX_WIDTH = POOL_WIDTH + ATTN_WIDTH
WINDOW = 128
BLOCK = 128
ROPE_DIM = HEAD_DIM // 4
ROPE_HALF = ROPE_DIM // 2
ROPE_THETA = 500000.0
D_FF = ((8 * D_MODEL // 3 + 255) // 256) * 256
EPS = 1e-5
PAST_LEN = 16384
Q_SCALE = HEAD_DIM ** -0.5

LANES = 128
SUBLANES = 8
HEADS_PER_COL = LANES // HEAD_DIM
HALO = 16
POOL_TOP = 2 * HALO
TOKEN_TILE = 512
FF_TILE = 512
SEQ_BLOCK = 8
SAMPLE_KEYS = 256
VMEM_LIMIT = 56 * 1024 * 1024


def _params(*semantics):
    return pltpu.CompilerParams(dimension_semantics=semantics, vmem_limit_bytes=VMEM_LIMIT)


def _rms(x, g):
    ms = jnp.mean(x * x, axis=-1, keepdims=True)
    return x * lax.rsqrt(ms + EPS) * g


def _swap_halves(z):
    return pltpu.roll(z, HEAD_DIM, 1)


def _cast_rows(in_refs, out_refs):
    for src, dst in zip(in_refs, out_refs):
        dst[...] = src[...].astype(BF16)


def _cast_specs(weights, steps, step_of=lambda i: i):
    for w in weights:
        assert w.shape[0] % (steps * 2 * SUBLANES) == 0
    specs = [pl.BlockSpec((w.shape[0] // steps, w.shape[1]), lambda *idx: (step_of(*idx), 0))
             for w in weights]
    return specs, [jax.ShapeDtypeStruct(w.shape, BF16) for w in weights]


def _proj_kernel(x_ref, g_ref, w_ref, c_ref, s1_ref, s2_ref, *refs):
    ncast = (len(refs) - 7) // 2
    u_ref, q_ref, kd_ref, vt_ref, utail_ref, ktail_ref, vtail_ref = refs[ncast:ncast + 7]
    _cast_rows(refs[:ncast], refs[ncast + 7:])
    tm = x_ref.shape[0]
    tail = ktail_ref.shape[0]
    h = _rms(x_ref[...], g_ref[...]).astype(BF16)
    proj = jnp.dot(h, w_ref[...], preferred_element_type=F32)
    u = proj[:, :POOL_WIDTH]
    u_ref[...] = u
    utail_ref[...] = u[tm - HALO:, :]
    c, s1, s2 = c_ref[...], s1_ref[...], s2_ref[...]

    def rope(z):
        return (z * c + pltpu.roll(z, LANES - ROPE_HALF, 1) * s1
                + pltpu.roll(z, ROPE_HALF, 1) * s2)

    for col in range(ATTN_WIDTH // LANES):
        z = proj[:, POOL_WIDTH + col * LANES:POOL_WIDTH + (col + 1) * LANES]
        q_ref[:, col * LANES:(col + 1) * LANES] = (rope(z) * Q_SCALE).astype(BF16)
    low_half = lax.broadcasted_iota(jnp.int32, (1, LANES), 1) < HEAD_DIM
    k0 = POOL_WIDTH + ATTN_WIDTH
    for a in range(KV_WIDTH // LANES):
        z = rope(proj[:, k0 + a * LANES:k0 + (a + 1) * LANES])
        ktail_ref[:, a * LANES:(a + 1) * LANES] = z[tm - tail:, :]
        zr = _swap_halves(z)
        kd_ref[:, (2 * a) * LANES:(2 * a + 1) * LANES] = jnp.where(low_half, z, zr).astype(BF16)
        kd_ref[:, (2 * a + 1) * LANES:(2 * a + 2) * LANES] = jnp.where(low_half, zr, z).astype(BF16)
    v = proj[:, k0 + KV_WIDTH:]
    vtail_ref[...] = v[tm - tail:, :]
    vt_ref[...] = v.T.astype(BF16)


def _proj(x, g, w_in, tabs, tiles_per_seq, tail, cast=()):
    t = x.shape[0]
    tm = TOKEN_TILE
    steps = t // tm
    nseq = steps // tiles_per_seq
    row = lambda w: pl.BlockSpec((tm, w), lambda i: (i, 0))
    tab = pl.BlockSpec((tm, LANES), lambda i: (i % tiles_per_seq, 0))
    const = lambda a: pl.BlockSpec(a.shape, lambda i: (0,) * a.ndim, pipeline_mode=pl.Buffered(1))
    per_seq = lambda r, w: pl.BlockSpec((r, w), lambda i: (i // tiles_per_seq, 0))
    cast_specs, cast_shapes = _cast_specs(cast, steps)
    return pl.pallas_call(
        _proj_kernel,
        grid=(steps,),
        in_specs=[row(D_MODEL), const(g), const(w_in), tab, tab, tab] + cast_specs,
        out_specs=[row(POOL_WIDTH), row(ATTN_WIDTH), row(2 * KV_WIDTH),
                   pl.BlockSpec((KV_WIDTH, tm), lambda i: (0, i)),
                   per_seq(HALO, POOL_WIDTH), per_seq(tail, KV_WIDTH), per_seq(tail, KV_WIDTH)]
        + cast_specs,
        out_shape=[jax.ShapeDtypeStruct((t, POOL_WIDTH), F32),
                   jax.ShapeDtypeStruct((t, ATTN_WIDTH), BF16),
                   jax.ShapeDtypeStruct((t, 2 * KV_WIDTH), BF16),
                   jax.ShapeDtypeStruct((KV_WIDTH, t), BF16),
                   jax.ShapeDtypeStruct((nseq * HALO, POOL_WIDTH), F32),
                   jax.ShapeDtypeStruct((nseq * tail, KV_WIDTH), F32),
                   jax.ShapeDtypeStruct((nseq * tail, KV_WIDTH), F32)] + cast_shapes,
        compiler_params=_params("arbitrary"),
        name="proj",
    )(x, g, w_in, *tabs, *cast)


def _mix_prompt_kernel(sink_ref, u_ref, halo_ref, q_ref, kdc_ref, kdp_ref, vtc_ref, vtp_ref,
                       wpool_ref, pscale_ref, *refs):
    ncast = (len(refs) - 6) // 2
    mix_ref = refs[ncast]
    ext_ref, lvl_ref, kwin_ref, vtwin_ref, bias_ref = refs[2 * ncast + 1:]
    _cast_rows(refs[:ncast], refs[ncast + 1:2 * ncast + 1])
    tq = u_ref.shape[0]
    i = pl.program_id(1)
    first = i == 0

    top = POOL_TOP
    ext_ref[0:top - HALO, :] = jnp.zeros((top - HALO, POOL_WIDTH), F32)
    ext_ref[top - HALO:top, :] = halo_ref[...] * jnp.where(first, 0.0, 1.0)
    ext_ref[top:, :] = u_ref[...]
    lvl_ref[:, 0:SUBLANES, :] = jnp.zeros((2, SUBLANES, POOL_GROUP), F32)
    nlive = top + tq - SUBLANES
    pos1 = i * tq + lax.broadcasted_iota(jnp.int32, (tq, 1), 0) + 1
    for gi, w in enumerate(POOL_WINDOWS):
        cols = slice(gi * POOL_GROUP, (gi + 1) * POOL_GROUP)
        src = ext_ref.at[:, cols]
        shift, slot = 1, 0
        while shift < w:
            dst = lvl_ref.at[slot]
            dst[SUBLANES:, :] = (src[SUBLANES:SUBLANES + nlive, :]
                                 + src[SUBLANES - shift:SUBLANES - shift + nlive, :])
            src, shift, slot = dst, 2 * shift, 1 - slot
        tok = ext_ref[top:, cols]
        inv_cnt = 1.0 / jnp.minimum(pos1, w).astype(F32)
        d = (src[top:, :] * inv_cnt - tok).astype(BF16)
        po = jnp.dot(d, wpool_ref[gi], preferred_element_type=F32) * pscale_ref[:, cols]
        mix_ref[:, cols] = po.astype(BF16)

    kwin_ref[0:BLOCK, :] = kdp_ref[...]
    kwin_ref[BLOCK:, :] = kdc_ref[...]
    vtwin_ref[:, 0:BLOCK] = vtp_ref[...]
    vtwin_ref[:, BLOCK:] = vtc_ref[...]
    kj = lax.broadcasted_iota(jnp.int32, (2 * BLOCK, BLOCK), 0)
    qi = lax.broadcasted_iota(jnp.int32, (2 * BLOCK, BLOCK), 1)
    band = (kj >= qi) & (kj <= qi + WINDOW)
    first_lo = jnp.where(first, BLOCK, 0)
    bias_ref[0] = jnp.where(band, 0.0, -jnp.inf)
    bias_ref[1] = jnp.where(band & (kj >= first_lo), 0.0, -jnp.inf)
    lane = lax.broadcasted_iota(jnp.int32, (1, LANES), 1)
    keep_half = [(lane // HEAD_DIM == hh).astype(BF16) for hh in range(HEADS_PER_COL)]

    def scores(n, g):
        rows = slice(n * BLOCK, (n + 1) * BLOCK)
        qs = jnp.concatenate(
            [q_ref[rows, (h // HEADS_PER_COL) * LANES:(h // HEADS_PER_COL + 1) * LANES]
             * keep_half[h % HEADS_PER_COL]
             for h in range(GQA_GROUP * g, GQA_GROUP * (g + 1))], axis=0)
        kd = kwin_ref[n * BLOCK:(n + 2) * BLOCK, g * LANES:(g + 1) * LANES]
        return lax.dot_general(kd, qs, (((1,), (1,)), ((), ())), preferred_element_type=F32)

    def attend(n, g, s_t):
        p_t = []
        for j in range(GQA_GROUP):
            s = s_t[:, j * BLOCK:(j + 1) * BLOCK] + bias_ref[1 if n == 0 else 0]
            m = jnp.max(s, axis=0, keepdims=True)
            p = jnp.exp(s - m)
            den = jnp.sum(p, axis=0, keepdims=True) + jnp.exp(sink_ref[GQA_GROUP * g + j] - m)
            p_t.append((p * (1.0 / den)).astype(BF16))
        vt = vtwin_ref[g * HEAD_DIM:(g + 1) * HEAD_DIM, n * BLOCK:(n + 2) * BLOCK]
        o_t = jnp.dot(vt, jnp.concatenate(p_t, axis=1), preferred_element_type=F32)
        for c in range(GQA_GROUP // HEADS_PER_COL):
            col_t = jnp.concatenate(
                [o_t[:, (HEADS_PER_COL * c + hh) * BLOCK:(HEADS_PER_COL * c + hh + 1) * BLOCK]
                 for hh in range(HEADS_PER_COL)], axis=0)
            col = POOL_WIDTH + (GQA_GROUP // HEADS_PER_COL * g + c) * LANES
            mix_ref[n * BLOCK:(n + 1) * BLOCK, col:col + LANES] = col_t.T.astype(BF16)

    work = [(n, g) for n in range(tq // BLOCK) for g in range(N_KV_HEADS)]
    s_next = scores(*work[0])
    for idx, (n, g) in enumerate(work):
        s_t = s_next
        if idx + 1 < len(work):
            s_next = scores(*work[idx + 1])
        attend(n, g, s_t)


def _mix_prompt(sinks, u, q, kd, vt, wpool, pscale, batch, seq, cast=()):
    tq = TOKEN_TILE
    nt = seq // tq
    bpt = tq // BLOCK
    hpt = tq // HALO
    prev_block = lambda b, i: jnp.maximum((b * nt + i) * bpt - 1, 0)
    row = lambda w: pl.BlockSpec((tq, w), lambda b, i: (b * nt + i, 0))
    halo = pl.BlockSpec((HALO, POOL_WIDTH), lambda b, i: (jnp.maximum((b * nt + i) * hpt - 1, 0), 0))
    full = lambda a: pl.BlockSpec(a.shape, lambda b, i: (0,) * a.ndim)
    cast_specs, cast_shapes = _cast_specs(cast, batch * nt, lambda b, i: b * nt + i)
    return pl.pallas_call(
        _mix_prompt_kernel,
        grid=(batch, nt),
        in_specs=[pl.BlockSpec(memory_space=pltpu.SMEM), row(POOL_WIDTH), halo, row(ATTN_WIDTH),
                  row(2 * KV_WIDTH),
                  pl.BlockSpec((BLOCK, 2 * KV_WIDTH), lambda b, i: (prev_block(b, i), 0)),
                  pl.BlockSpec((KV_WIDTH, tq), lambda b, i: (0, b * nt + i)),
                  pl.BlockSpec((KV_WIDTH, BLOCK), lambda b, i: (0, prev_block(b, i))),
                  full(wpool), full(pscale)] + cast_specs,
        out_specs=[row(MIX_WIDTH)] + cast_specs,
        out_shape=[jax.ShapeDtypeStruct((batch * seq, MIX_WIDTH), BF16)] + cast_shapes,
        scratch_shapes=[pltpu.VMEM((POOL_TOP + tq, POOL_WIDTH), F32),
                        pltpu.VMEM((2, POOL_TOP + tq, POOL_GROUP), F32),
                        pltpu.VMEM((BLOCK + tq, 2 * KV_WIDTH), BF16),
                        pltpu.VMEM((KV_WIDTH, BLOCK + tq), BF16),
                        pltpu.VMEM((2, 2 * BLOCK, BLOCK), F32)],
        compiler_params=_params("parallel", "parallel"),
        name="mix_prompt",
    )(sinks, u, u, q, kd, kd, vt, vt, wpool, pscale, *cast)


def _sink_softmax(s, mask, sink):
    s = jnp.where(mask, s, -jnp.inf)
    m = jnp.maximum(jnp.max(s, axis=1, keepdims=True), sink)
    p = jnp.exp(s - m)
    den = jnp.sum(p, axis=1, keepdims=True) + jnp.exp(sink - m)
    return p / den


def _mix_sample_kernel(sink_ref, u_ref, q_ref, kn_ref, vn_ref, st_ref, ck_ref, cv_ref,
                       wpool_ref, pscale_ref, mix_ref, npool_ref, nk_ref, nv_ref,
                       kall_ref, vall_ref, *, dec_seq):
    sb = SEQ_BLOCK
    rows_per_seq = N_HEADS * dec_seq
    nrow = sb * rows_per_seq
    ntok = sb * dec_seq
    win = ck_ref.shape[1]

    @pl.when(pl.program_id(0) == 0)
    def _():
        kall_ref[...] = jnp.zeros_like(kall_ref)
        vall_ref[...] = jnp.zeros_like(vall_ref)

    ext = [st_ref[h] for h in range(POOL_HIST)]
    ext += [u_ref[t] for t in range(dec_seq)]
    for h in range(POOL_HIST):
        npool_ref[h] = ext[h + dec_seq]
    r_out = lax.broadcasted_iota(jnp.int32, (ntok, ntok), 0)
    r_in = lax.broadcasted_iota(jnp.int32, (ntok, ntok), 1)
    to_seq_major = ((r_in % sb) * dec_seq + r_in // sb == r_out).astype(BF16)
    for gi, w in enumerate(POOL_WINDOWS):
        cols = slice(gi * POOL_GROUP, (gi + 1) * POOL_GROUP)
        ds = []
        for t in range(dec_seq):
            tok = ext[POOL_HIST + t][:, cols]
            acc = tok
            for j in range(1, w):
                acc = acc + ext[POOL_HIST + t - j][:, cols]
            ds.append(acc * (1.0 / w) - tok)
        d = jnp.concatenate(ds, axis=0).astype(BF16)
        po = jnp.dot(d, wpool_ref[gi], preferred_element_type=F32) * pscale_ref[:, cols]
        po = jnp.dot(to_seq_major, po.astype(BF16), preferred_element_type=F32)
        mix_ref[:, cols] = po.astype(BF16)

    for b in range(sb):
        tok_rows = slice(b * dec_seq, (b + 1) * dec_seq)
        kall_ref[b, 0:win, :] = ck_ref[b]
        kall_ref[b, win:win + dec_seq, :] = kn_ref[tok_rows, :]
        vall_ref[b, 0:win, :] = cv_ref[b]
        vall_ref[b, win:win + dec_seq, :] = vn_ref[tok_rows, :]
        nk_ref[b] = kall_ref[b, dec_seq:dec_seq + win, :]
        nv_ref[b] = vall_ref[b, dec_seq:dec_seq + win, :]

    r = lax.broadcasted_iota(jnp.int32, (nrow, ntok), 0)
    c = lax.broadcasted_iota(jnp.int32, (nrow, ntok), 1)
    pick = (c == (r // rows_per_seq) * dec_seq + r % dec_seq)
    qrep = jnp.dot(pick.astype(BF16), q_ref[...], preferred_element_type=F32)
    rr = lax.broadcasted_iota(jnp.int32, (nrow, LANES), 0) % rows_per_seq
    row_g = rr // (GQA_GROUP * dec_seq)
    row_j = (rr // dec_seq) % GQA_GROUP
    lane_half = lax.broadcasted_iota(jnp.int32, (nrow, LANES), 1) // HEAD_DIM
    lhs_cols = [jnp.zeros((nrow, LANES), F32) for _ in range(KV_WIDTH // LANES)]
    for qc in range(ATTN_WIDTH // LANES):
        g = qc * HEADS_PER_COL // GQA_GROUP
        src = qrep[:, qc * LANES:(qc + 1) * LANES]
        src_swapped = _swap_halves(src)
        kv_col, kv_half = g // HEADS_PER_COL, g % HEADS_PER_COL
        for hh in range(HEADS_PER_COL):
            j = (qc * HEADS_PER_COL + hh) % GQA_GROUP
            here = (row_g == g) & (row_j == j) & (lane_half == kv_half)
            lhs_cols[kv_col] = lhs_cols[kv_col] + jnp.where(
                here, src if hh == kv_half else src_swapped, 0.0)
    lhs = jnp.concatenate(lhs_cols, axis=1).astype(BF16)
    scores = []
    for b in range(sb):
        rows = slice(b * rows_per_seq, (b + 1) * rows_per_seq)
        scores.append(lax.dot_general(lhs[rows], kall_ref[b].astype(BF16),
                                      (((1,), (1,)), ((), ())), preferred_element_type=F32))
    s = jnp.concatenate(scores, axis=0)
    t_row = lax.broadcasted_iota(jnp.int32, (nrow, SAMPLE_KEYS), 0) % dec_seq
    kj = lax.broadcasted_iota(jnp.int32, (nrow, SAMPLE_KEYS), 1)
    mask = (kj >= t_row + (win - WINDOW)) & (kj <= t_row + win)
    p = _sink_softmax(s, mask, sink_ref[...]).astype(BF16)
    outs = []
    for b in range(sb):
        rows = slice(b * rows_per_seq, (b + 1) * rows_per_seq)
        outs.append(jnp.dot(p[rows], vall_ref[b].astype(BF16), preferred_element_type=F32))
    o = jnp.concatenate(outs, axis=0)
    z_cols = []
    for qc in range(ATTN_WIDTH // LANES):
        g = qc * HEADS_PER_COL // GQA_GROUP
        kv_col, kv_half = g // HEADS_PER_COL, g % HEADS_PER_COL
        src = o[:, kv_col * LANES:(kv_col + 1) * LANES]
        src_swapped = _swap_halves(src)
        zc = jnp.zeros((nrow, LANES), F32)
        for hh in range(HEADS_PER_COL):
            j = (qc * HEADS_PER_COL + hh) % GQA_GROUP
            here = (row_g == g) & (row_j == j) & (lane_half == hh)
            zc = zc + jnp.where(here, src if hh == kv_half else src_swapped, 0.0)
        z_cols.append(zc)
    z = jnp.concatenate(z_cols, axis=1).astype(BF16)
    rt = lax.broadcasted_iota(jnp.int32, (ntok, nrow), 0)
    ct = lax.broadcasted_iota(jnp.int32, (ntok, nrow), 1)
    unpick = (rt == (ct // rows_per_seq) * dec_seq + ct % dec_seq)
    attn = jnp.dot(unpick.astype(BF16), z, preferred_element_type=F32)
    mix_ref[:, POOL_WIDTH:] = attn.astype(BF16)


def _mix_sample(sink_rows, u, q, k_new, v_new, state_t, cache_k, cache_v, wpool, pscale, dec_seq):
    nseq, win, _ = cache_k.shape
    sb = SEQ_BLOCK
    ntok = sb * dec_seq
    row = lambda w: pl.BlockSpec((ntok, w), lambda i: (i, 0))
    slab = pl.BlockSpec((POOL_HIST, sb, POOL_WIDTH), lambda i: (0, i, 0))
    u_slab = pl.BlockSpec((dec_seq, sb, POOL_WIDTH), lambda i: (0, i, 0))
    cache = pl.BlockSpec((sb, win, KV_WIDTH), lambda i: (i, 0, 0))
    full = lambda a: pl.BlockSpec(a.shape, lambda i: (0,) * a.ndim)
    return pl.pallas_call(
        functools.partial(_mix_sample_kernel, dec_seq=dec_seq),
        grid=(nseq // sb,),
        in_specs=[full(sink_rows), u_slab, row(ATTN_WIDTH), row(KV_WIDTH), row(KV_WIDTH),
                  slab, cache, cache, full(wpool), full(pscale)],
        out_specs=[row(MIX_WIDTH), slab, cache, cache],
        out_shape=[jax.ShapeDtypeStruct((nseq * dec_seq, MIX_WIDTH), BF16),
                   jax.ShapeDtypeStruct((POOL_HIST, nseq, POOL_WIDTH), F32),
                   jax.ShapeDtypeStruct(cache_k.shape, F32),
                   jax.ShapeDtypeStruct(cache_v.shape, F32)],
        scratch_shapes=[pltpu.VMEM((sb, SAMPLE_KEYS, KV_WIDTH), F32),
                        pltpu.VMEM((sb, SAMPLE_KEYS, KV_WIDTH), F32)],
        compiler_params=_params("arbitrary"),
        name="mix_sample",
    )(sink_rows, u, q, k_new, v_new, state_t, cache_k, cache_v, wpool, pscale)


def _outproj_kernel(x_ref, mix_ref, w_ref, g_ref, *refs):
    ncast = (len(refs) - 2) // 2
    x1_ref, h2_ref = refs[ncast:ncast + 2]
    _cast_rows(refs[:ncast], refs[ncast + 2:])
    x1 = x_ref[...] + jnp.dot(mix_ref[...], w_ref[...], preferred_element_type=F32)
    x1_ref[...] = x1
    h2_ref[...] = _rms(x1, g_ref[...]).astype(BF16)


def _outproj(x, mix, w_out, g, cast=()):
    t = x.shape[0]
    tm = TOKEN_TILE
    steps = t // tm
    row = lambda w: pl.BlockSpec((tm, w), lambda i: (i, 0))
    const = lambda a: pl.BlockSpec(a.shape, lambda i: (0,) * a.ndim, pipeline_mode=pl.Buffered(1))
    cast_specs, cast_shapes = _cast_specs(cast, steps)
    return pl.pallas_call(
        _outproj_kernel,
        grid=(steps,),
        in_specs=[row(D_MODEL), row(MIX_WIDTH), const(w_out), const(g)] + cast_specs,
        out_specs=[row(D_MODEL), row(D_MODEL)] + cast_specs,
        out_shape=[jax.ShapeDtypeStruct((t, D_MODEL), F32),
                   jax.ShapeDtypeStruct((t, D_MODEL), BF16)] + cast_shapes,
        compiler_params=_params("parallel"),
        name="outproj",
    )(x, mix, w_out, g, *cast)


def _ffn_kernel(x1_ref, h2_ref, wg_ref, wu_ref, wd_ref, gf_ref, y_ref, acc_ref, *, final_norm):
    j = pl.program_id(1)

    @pl.when(j == 0)
    def _():
        acc_ref[...] = x1_ref[...]

    h = h2_ref[...]
    gate = jnp.dot(h, wg_ref[...], preferred_element_type=F32)
    up = jnp.dot(h, wu_ref[...], preferred_element_type=F32)
    act = (jax.nn.silu(gate) * up).astype(BF16)
    acc_ref[...] += jnp.dot(act, wd_ref[...], preferred_element_type=F32)

    @pl.when(j == pl.num_programs(1) - 1)
    def _():
        y = acc_ref[...]
        y_ref[...] = _rms(y, gf_ref[...]) if final_norm else y


def _ffn(x1, h2, wg, wu, wd, gf, final_norm):
    t = x1.shape[0]
    tm, tf = TOKEN_TILE, FF_TILE
    row = pl.BlockSpec((tm, D_MODEL), lambda i, j: (i, 0))
    return pl.pallas_call(
        functools.partial(_ffn_kernel, final_norm=final_norm),
        grid=(t // tm, D_FF // tf),
        in_specs=[row, row,
                  pl.BlockSpec((D_MODEL, tf), lambda i, j: (0, j)),
                  pl.BlockSpec((D_MODEL, tf), lambda i, j: (0, j)),
                  pl.BlockSpec((tf, D_MODEL), lambda i, j: (j, 0)),
                  pl.BlockSpec(gf.shape, lambda i, j: (0, 0))],
        out_specs=row,
        out_shape=jax.ShapeDtypeStruct((t, D_MODEL), F32),
        scratch_shapes=[pltpu.VMEM((tm, D_MODEL), F32)],
        compiler_params=_params("parallel", "arbitrary"),
        name="ffn",
    )(x1, h2, wg, wu, wd, gf)


def _rope_tables(pos):
    inv = ROPE_THETA ** (-jnp.arange(0, ROPE_DIM, 2, dtype=F32) / ROPE_DIM)
    ang = pos.astype(F32)[:, None] * inv[None, :]
    cos, sin = jnp.cos(ang), jnp.sin(ang)
    n = pos.shape[0]
    rest = HEAD_DIM - ROPE_DIM
    one, zero = jnp.ones((n, rest), F32), jnp.zeros((n, rest), F32)
    zh = jnp.zeros((n, ROPE_HALF), F32)
    c = jnp.tile(jnp.concatenate([cos, cos, one], axis=1), (1, HEADS_PER_COL))
    s1 = jnp.tile(jnp.concatenate([-sin, zh, zero], axis=1), (1, HEADS_PER_COL))
    s2 = jnp.tile(jnp.concatenate([zh, sin, zero], axis=1), (1, HEADS_PER_COL))
    return c, s1, s2


def kernel(x_prompt, x_sample, state_pool, cache_k_win, cache_v_win, g_mix, w_in, w_pool,
           pool_scale, attn_sinks, w_out, g_ffn, w_gate, w_up, w_down, g_final):
    batch, seq, _ = x_prompt.shape
    nseq, dec_seq, _ = x_sample.shape
    depth = w_in.shape[0]
    win_s = cache_k_win.shape[2]
    ntok_s = nseq * dec_seq
    assert seq % TOKEN_TILE == 0 and ntok_s == TOKEN_TILE and nseq % SEQ_BLOCK == 0
    assert win_s == WINDOW and win_s + dec_seq <= SAMPLE_KEYS and seq >= WINDOW
    assert PAST_LEN >= max(POOL_HIST, WINDOW)

    xp = x_prompt.reshape(batch * seq, D_MODEL)
    xs = x_sample.reshape(ntok_s, D_MODEL)
    tabs_p = _rope_tables(jnp.arange(seq))
    tabs_s = _rope_tables(PAST_LEN + jnp.arange(ntok_s) % dec_seq)
    gf = g_final.reshape(1, D_MODEL)

    outs = [[] for _ in range(6)]
    for l in range(depth):
        w_in_l, wpool_l = w_in[l].astype(BF16), w_pool[l].astype(BF16)
        gm = g_mix[l].reshape(1, D_MODEL)
        gn = g_ffn[l].reshape(1, D_MODEL)
        pscale = pool_scale[l].reshape(1, POOL_WIDTH)
        sink_rows = jnp.tile(jnp.repeat(attn_sinks[l], dec_seq), SEQ_BLOCK)[:, None]

        u, q, kd, vt, u_tail, k_tail, v_tail, w_out_l, wg_l = _proj(
            xp, gm, w_in_l, tabs_p, seq // TOKEN_TILE, WINDOW, cast=(w_out[l], w_gate[l]))
        mix, wu_l, wd_l = _mix_prompt(attn_sinks[l], u, q, kd, vt, wpool_l, pscale, batch, seq,
                                      cast=(w_up[l], w_down[l]))
        x1, h2 = _outproj(xp, mix, w_out_l, gn)
        xp = _ffn(x1, h2, wg_l, wu_l, wd_l, gf, l == depth - 1)
        outs[0].append(u_tail.reshape(batch, HALO, POOL_WIDTH)[:, HALO - POOL_HIST:])
        outs[1].append(k_tail.reshape(batch, WINDOW, N_KV_HEADS, HEAD_DIM))
        outs[2].append(v_tail.reshape(batch, WINDOW, N_KV_HEADS, HEAD_DIM))

        u, q, _, _, _, k_new, v_new = _proj(xs, gm, w_in_l, tabs_s, 1, TOKEN_TILE)
        state_t = jnp.transpose(state_pool[l], (1, 0, 2))
        ck = cache_k_win[l].reshape(nseq, win_s, KV_WIDTH)
        cv = cache_v_win[l].reshape(nseq, win_s, KV_WIDTH)
        u_t = jnp.transpose(u.reshape(nseq, dec_seq, POOL_WIDTH), (1, 0, 2))
        mix, npool, nk, nv = _mix_sample(sink_rows, u_t, q, k_new, v_new, state_t, ck, cv, wpool_l,
                                         pscale, dec_seq)
        x1, h2 = _outproj(xs, mix, w_out_l, gn)
        xs = _ffn(x1, h2, wg_l, wu_l, wd_l, gf, l == depth - 1)
        outs[3].append(jnp.transpose(npool, (1, 0, 2)))
        outs[4].append(nk.reshape(nseq, win_s, N_KV_HEADS, HEAD_DIM))
        outs[5].append(nv.reshape(nseq, win_s, N_KV_HEADS, HEAD_DIM))

    y_prompt = xp.reshape(batch, seq, D_MODEL)
    y_sample = xs.reshape(nseq, dec_seq, D_MODEL)
    return (y_prompt, y_sample) + tuple(jnp.stack(o) for o in outs)
```

```python
import functools

import jax
import jax.numpy as jnp
from jax import lax
from jax.experimental import pallas as pl
from jax.experimental.pallas import tpu as pltpu

F32 = jnp.float32
BF16 = jnp.bfloat16

D_MODEL = 2048
POOL_WIDTH = D_MODEL // 2
POOL_WINDOWS = (2, 4, 8, 16)
N_POOL_GROUPS = len(POOL_WINDOWS)
POOL_GROUP = POOL_WIDTH // N_POOL_GROUPS
POOL_HIST = max(POOL_WINDOWS) - 1
HEAD_DIM = 64
N_HEADS = (D_MODEL - POOL_WIDTH) // HEAD_DIM
N_KV_HEADS = 4
GQA_GROUP = N_HEADS // N_KV_HEADS
ATTN_WIDTH = N_HEADS * HEAD_DIM
KV_WIDTH = N_KV_HEADS * HEAD_DIM
IN_WIDTH = POOL_WIDTH + ATTN_WIDTH + 2 * KV_WIDTH
MIX_WIDTH = POOL_WIDTH + ATTN_WIDTH
WINDOW = 128
BLOCK = 128
ROPE_DIM = HEAD_DIM // 4
ROPE_HALF = ROPE_DIM // 2
ROPE_THETA = 500000.0
D_FF = ((8 * D_MODEL // 3 + 255) // 256) * 256
EPS = 1e-5
PAST_LEN = 16384
Q_SCALE = HEAD_DIM ** -0.5

LANES = 128
SUBLANES = 8
HEADS_PER_COL = LANES // HEAD_DIM
HALO = 16
POOL_TOP = 2 * HALO
TOKEN_TILE = 512
FFN_TOKEN_TILE = 1024
FF_TILE = 512
SEQ_BLOCK = 8
SAMPLE_KEYS = 256
VMEM_LIMIT = 60 * 1024 * 1024


def _params(*semantics):
    return pltpu.CompilerParams(dimension_semantics=semantics, vmem_limit_bytes=VMEM_LIMIT)


def _rms(x, g):
    ms = jnp.mean(x * x, axis=-1, keepdims=True)
    return x * lax.rsqrt(ms + EPS) * g


def _swap_halves(z):
    return pltpu.roll(z, HEAD_DIM, 1)


def _cast_rows(in_refs, out_refs):
    for src, dst in zip(in_refs, out_refs):
        dst[...] = src[...].astype(BF16)


def _cast_specs(weights, steps, step_of=lambda i: i):
    for w in weights:
        assert w.shape[0] % (steps * 2 * SUBLANES) == 0
    specs = [pl.BlockSpec((w.shape[0] // steps, w.shape[1]), lambda *idx: (step_of(*idx), 0))
             for w in weights]
    return specs, [jax.ShapeDtypeStruct(w.shape, BF16) for w in weights]


def _proj_kernel(x_ref, g_ref, w_ref, c_ref, s1_ref, s2_ref, *refs):
    ncast = (len(refs) - 7) // 2
    u_ref, q_ref, kd_ref, vt_ref, utail_ref, ktail_ref, vtail_ref = refs[ncast:ncast + 7]
    _cast_rows(refs[:ncast], refs[ncast + 7:])
    tm = x_ref.shape[0]
    tail = ktail_ref.shape[0]
    h = _rms(x_ref[...], g_ref[...]).astype(BF16)
    proj = jnp.dot(h, w_ref[...], preferred_element_type=F32)
    u = proj[:, :POOL_WIDTH]
    u_ref[...] = u
    utail_ref[...] = u[tm - HALO:, :]
    c, s1, s2 = c_ref[...], s1_ref[...], s2_ref[...]

    def rope(z):
        return (z * c + pltpu.roll(z, LANES - ROPE_HALF, 1) * s1
                + pltpu.roll(z, ROPE_HALF, 1) * s2)

    for col in range(ATTN_WIDTH // LANES):
        z = proj[:, POOL_WIDTH + col * LANES:POOL_WIDTH + (col + 1) * LANES]
        q_ref[:, col * LANES:(col + 1) * LANES] = (rope(z) * Q_SCALE).astype(BF16)
    low_half = lax.broadcasted_iota(jnp.int32, (1, LANES), 1) < HEAD_DIM
    k0 = POOL_WIDTH + ATTN_WIDTH
    for a in range(KV_WIDTH // LANES):
        z = rope(proj[:, k0 + a * LANES:k0 + (a + 1) * LANES])
        ktail_ref[:, a * LANES:(a + 1) * LANES] = z[tm - tail:, :]
        zr = _swap_halves(z)
        kd_ref[:, (2 * a) * LANES:(2 * a + 1) * LANES] = jnp.where(low_half, z, zr).astype(BF16)
        kd_ref[:, (2 * a + 1) * LANES:(2 * a + 2) * LANES] = jnp.where(low_half, zr, z).astype(BF16)
    v = proj[:, k0 + KV_WIDTH:]
    vtail_ref[...] = v[tm - tail:, :]
    vt_ref[...] = v.T.astype(BF16)


def _proj(x, g, w_in, tabs, tiles_per_seq, tail, cast=()):
    t = x.shape[0]
    tm = TOKEN_TILE
    steps = t // tm
    nseq = steps // tiles_per_seq
    row = lambda w: pl.BlockSpec((tm, w), lambda i: (i, 0))
    tab = pl.BlockSpec((tm, LANES), lambda i: (i % tiles_per_seq, 0))
    const = lambda a: pl.BlockSpec(a.shape, lambda i: (0,) * a.ndim, pipeline_mode=pl.Buffered(1))
    per_seq = lambda r, w: pl.BlockSpec((r, w), lambda i: (i // tiles_per_seq, 0))
    cast_specs, cast_shapes = _cast_specs(cast, steps)
    return pl.pallas_call(
        _proj_kernel,
        grid=(steps,),
        in_specs=[row(D_MODEL), const(g), const(w_in), tab, tab, tab] + cast_specs,
        out_specs=[row(POOL_WIDTH), row(ATTN_WIDTH), row(2 * KV_WIDTH),
                   pl.BlockSpec((KV_WIDTH, tm), lambda i: (0, i)),
                   per_seq(HALO, POOL_WIDTH), per_seq(tail, KV_WIDTH), per_seq(tail, KV_WIDTH)]
        + cast_specs,
        out_shape=[jax.ShapeDtypeStruct((t, POOL_WIDTH), F32),
                   jax.ShapeDtypeStruct((t, ATTN_WIDTH), BF16),
                   jax.ShapeDtypeStruct((t, 2 * KV_WIDTH), BF16),
                   jax.ShapeDtypeStruct((KV_WIDTH, t), BF16),
                   jax.ShapeDtypeStruct((nseq * HALO, POOL_WIDTH), F32),
                   jax.ShapeDtypeStruct((nseq * tail, KV_WIDTH), F32),
                   jax.ShapeDtypeStruct((nseq * tail, KV_WIDTH), F32)] + cast_shapes,
        compiler_params=_params("arbitrary"),
        name="proj",
    )(x, g, w_in, *tabs, *cast)


def _mix_prompt_kernel(sink_ref, u_ref, halo_ref, q_ref, kdc_ref, kdp_ref, vtc_ref, vtp_ref,
                       wpool_ref, pscale_ref, *refs):
    ncast = (len(refs) - 6) // 2
    mix_ref = refs[ncast]
    ext_ref, lvl_ref, kwin_ref, vtwin_ref, bias_ref = refs[2 * ncast + 1:]
    _cast_rows(refs[:ncast], refs[ncast + 1:2 * ncast + 1])
    tq = u_ref.shape[0]
    i = pl.program_id(1)
    first = i == 0

    top = POOL_TOP
    ext_ref[0:top - HALO, :] = jnp.zeros((top - HALO, POOL_WIDTH), F32)
    ext_ref[top - HALO:top, :] = halo_ref[...] * jnp.where(first, 0.0, 1.0)
    ext_ref[top:, :] = u_ref[...]
    lvl_ref[:, 0:SUBLANES, :] = jnp.zeros((2, SUBLANES, POOL_GROUP), F32)
    nlive = top + tq - SUBLANES
    pos1 = i * tq + lax.broadcasted_iota(jnp.int32, (tq, 1), 0) + 1
    for gi, w in enumerate(POOL_WINDOWS):
        cols = slice(gi * POOL_GROUP, (gi + 1) * POOL_GROUP)
        src = ext_ref.at[:, cols]
        shift, slot = 1, 0
        while shift < w:
            dst = lvl_ref.at[slot]
            dst[SUBLANES:, :] = (src[SUBLANES:SUBLANES + nlive, :]
                                 + src[SUBLANES - shift:SUBLANES - shift + nlive, :])
            src, shift, slot = dst, 2 * shift, 1 - slot
        tok = ext_ref[top:, cols]
        inv_cnt = 1.0 / jnp.minimum(pos1, w).astype(F32)
        d = (src[top:, :] * inv_cnt - tok).astype(BF16)
        po = jnp.dot(d, wpool_ref[gi], preferred_element_type=F32) * pscale_ref[:, cols]
        mix_ref[:, cols] = po.astype(BF16)

    kwin_ref[0:BLOCK, :] = kdp_ref[...]
    kwin_ref[BLOCK:, :] = kdc_ref[...]
    vtwin_ref[:, 0:BLOCK] = vtp_ref[...]
    vtwin_ref[:, BLOCK:] = vtc_ref[...]
    kj = lax.broadcasted_iota(jnp.int32, (2 * BLOCK, BLOCK), 0)
    qi = lax.broadcasted_iota(jnp.int32, (2 * BLOCK, BLOCK), 1)
    band = (kj >= qi) & (kj <= qi + WINDOW)
    first_lo = jnp.where(first, BLOCK, 0)
    bias_ref[0] = jnp.where(band, 0.0, -jnp.inf)
    bias_ref[1] = jnp.where(band & (kj >= first_lo), 0.0, -jnp.inf)
    lane = lax.broadcasted_iota(jnp.int32, (1, LANES), 1)
    keep_half = [(lane // HEAD_DIM == hh).astype(BF16) for hh in range(HEADS_PER_COL)]

    def scores(n, g):
        rows = slice(n * BLOCK, (n + 1) * BLOCK)
        qs = jnp.concatenate(
            [q_ref[rows, (h // HEADS_PER_COL) * LANES:(h // HEADS_PER_COL + 1) * LANES]
             * keep_half[h % HEADS_PER_COL]
             for h in range(GQA_GROUP * g, GQA_GROUP * (g + 1))], axis=0)
        kd = kwin_ref[n * BLOCK:(n + 2) * BLOCK, g * LANES:(g + 1) * LANES]
        return lax.dot_general(kd, qs, (((1,), (1,)), ((), ())), preferred_element_type=F32)

    def attend(n, g, s_t):
        p_t = []
        for j in range(GQA_GROUP):
            s = s_t[:, j * BLOCK:(j + 1) * BLOCK] + bias_ref[1 if n == 0 else 0]
            m = jnp.max(s, axis=0, keepdims=True)
            p = jnp.exp(s - m)
            den = jnp.sum(p, axis=0, keepdims=True) + jnp.exp(sink_ref[GQA_GROUP * g + j] - m)
            p_t.append((p * (1.0 / den)).astype(BF16))
        vt = vtwin_ref[g * HEAD_DIM:(g + 1) * HEAD_DIM, n * BLOCK:(n + 2) * BLOCK]
        o_t = jnp.dot(vt, jnp.concatenate(p_t, axis=1), preferred_element_type=F32)
        for c in range(GQA_GROUP // HEADS_PER_COL):
            col_t = jnp.concatenate(
                [o_t[:, (HEADS_PER_COL * c + hh) * BLOCK:(HEADS_PER_COL * c + hh + 1) * BLOCK]
                 for hh in range(HEADS_PER_COL)], axis=0)
            col = POOL_WIDTH + (GQA_GROUP // HEADS_PER_COL * g + c) * LANES
            mix_ref[n * BLOCK:(n + 1) * BLOCK, col:col + LANES] = col_t.T.astype(BF16)

    work = [(n, g) for n in range(tq // BLOCK) for g in range(N_KV_HEADS)]
    s_next = scores(*work[0])
    for idx, (n, g) in enumerate(work):
        s_t = s_next
        if idx + 1 < len(work):
            s_next = scores(*work[idx + 1])
        attend(n, g, s_t)


def _mix_prompt(sinks, u, q, kd, vt, wpool, pscale, batch, seq, cast=()):
    tq = TOKEN_TILE
    nt = seq // tq
    bpt = tq // BLOCK
    hpt = tq // HALO
    prev_block = lambda b, i: jnp.maximum((b * nt + i) * bpt - 1, 0)
    row = lambda w: pl.BlockSpec((tq, w), lambda b, i: (b * nt + i, 0))
    halo = pl.BlockSpec((HALO, POOL_WIDTH), lambda b, i: (jnp.maximum((b * nt + i) * hpt - 1, 0), 0))
    full = lambda a: pl.BlockSpec(a.shape, lambda b, i: (0,) * a.ndim)
    cast_specs, cast_shapes = _cast_specs(cast, batch * nt, lambda b, i: b * nt + i)
    return pl.pallas_call(
        _mix_prompt_kernel,
        grid=(batch, nt),
        in_specs=[pl.BlockSpec(memory_space=pltpu.SMEM), row(POOL_WIDTH), halo, row(ATTN_WIDTH),
                  row(2 * KV_WIDTH),
                  pl.BlockSpec((BLOCK, 2 * KV_WIDTH), lambda b, i: (prev_block(b, i), 0)),
                  pl.BlockSpec((KV_WIDTH, tq), lambda b, i: (0, b * nt + i)),
                  pl.BlockSpec((KV_WIDTH, BLOCK), lambda b, i: (0, prev_block(b, i))),
                  full(wpool), full(pscale)] + cast_specs,
        out_specs=[row(MIX_WIDTH)] + cast_specs,
        out_shape=[jax.ShapeDtypeStruct((batch * seq, MIX_WIDTH), BF16)] + cast_shapes,
        scratch_shapes=[pltpu.VMEM((POOL_TOP + tq, POOL_WIDTH), F32),
                        pltpu.VMEM((2, POOL_TOP + tq, POOL_GROUP), F32),
                        pltpu.VMEM((BLOCK + tq, 2 * KV_WIDTH), BF16),
                        pltpu.VMEM((KV_WIDTH, BLOCK + tq), BF16),
                        pltpu.VMEM((2, 2 * BLOCK, BLOCK), F32)],
        compiler_params=_params("parallel", "parallel"),
        name="mix_prompt",
    )(sinks, u, u, q, kd, kd, vt, vt, wpool, pscale, *cast)


def _sink_softmax(s, mask, sink):
    s = jnp.where(mask, s, -jnp.inf)
    m = jnp.maximum(jnp.max(s, axis=1, keepdims=True), sink)
    p = jnp.exp(s - m)
    den = jnp.sum(p, axis=1, keepdims=True) + jnp.exp(sink - m)
    return p / den


def _mix_sample_kernel(sink_ref, u_ref, q_ref, kn_ref, vn_ref, st_ref, ck_ref, cv_ref,
                       wpool_ref, pscale_ref, mix_ref, npool_ref, nk_ref, nv_ref,
                       kall_ref, vall_ref, *, dec_seq):
    sb = SEQ_BLOCK
    rows_per_seq = N_HEADS * dec_seq
    nrow = sb * rows_per_seq
    ntok = sb * dec_seq
    win = ck_ref.shape[1]

    @pl.when(pl.program_id(0) == 0)
    def _():
        kall_ref[...] = jnp.zeros_like(kall_ref)
        vall_ref[...] = jnp.zeros_like(vall_ref)

    ext = [st_ref[h] for h in range(POOL_HIST)]
    ext += [u_ref[t] for t in range(dec_seq)]
    for h in range(POOL_HIST):
        npool_ref[h] = ext[h + dec_seq]
    r_out = lax.broadcasted_iota(jnp.int32, (ntok, ntok), 0)
    r_in = lax.broadcasted_iota(jnp.int32, (ntok, ntok), 1)
    to_seq_major = ((r_in % sb) * dec_seq + r_in // sb == r_out).astype(BF16)
    for gi, w in enumerate(POOL_WINDOWS):
        cols = slice(gi * POOL_GROUP, (gi + 1) * POOL_GROUP)
        ds = []
        for t in range(dec_seq):
            tok = ext[POOL_HIST + t][:, cols]
            acc = tok
            for j in range(1, w):
                acc = acc + ext[POOL_HIST + t - j][:, cols]
            ds.append(acc * (1.0 / w) - tok)
        d = jnp.concatenate(ds, axis=0).astype(BF16)
        po = jnp.dot(d, wpool_ref[gi], preferred_element_type=F32) * pscale_ref[:, cols]
        po = jnp.dot(to_seq_major, po.astype(BF16), preferred_element_type=F32)
        mix_ref[:, cols] = po.astype(BF16)

    for b in range(sb):
        tok_rows = slice(b * dec_seq, (b + 1) * dec_seq)
        kall_ref[b, 0:win, :] = ck_ref[b]
        kall_ref[b, win:win + dec_seq, :] = kn_ref[tok_rows, :]
        vall_ref[b, 0:win, :] = cv_ref[b]
        vall_ref[b, win:win + dec_seq, :] = vn_ref[tok_rows, :]
        nk_ref[b] = kall_ref[b, dec_seq:dec_seq + win, :]
        nv_ref[b] = vall_ref[b, dec_seq:dec_seq + win, :]

    r = lax.broadcasted_iota(jnp.int32, (nrow, ntok), 0)
    c = lax.broadcasted_iota(jnp.int32, (nrow, ntok), 1)
    pick = (c == (r // rows_per_seq) * dec_seq + r % dec_seq)
    qrep = jnp.dot(pick.astype(BF16), q_ref[...], preferred_element_type=F32)
    rr = lax.broadcasted_iota(jnp.int32, (nrow, LANES), 0) % rows_per_seq
    row_g = rr // (GQA_GROUP * dec_seq)
    row_j = (rr // dec_seq) % GQA_GROUP
    lane_half = lax.broadcasted_iota(jnp.int32, (nrow, LANES), 1) // HEAD_DIM
    lhs_cols = [jnp.zeros((nrow, LANES), F32) for _ in range(KV_WIDTH // LANES)]
    for qc in range(ATTN_WIDTH // LANES):
        g = qc * HEADS_PER_COL // GQA_GROUP
        src = qrep[:, qc * LANES:(qc + 1) * LANES]
        src_swapped = _swap_halves(src)
        kv_col, kv_half = g // HEADS_PER_COL, g % HEADS_PER_COL
        for hh in range(HEADS_PER_COL):
            j = (qc * HEADS_PER_COL + hh) % GQA_GROUP
            here = (row_g == g) & (row_j == j) & (lane_half == kv_half)
            lhs_cols[kv_col] = lhs_cols[kv_col] + jnp.where(
                here, src if hh == kv_half else src_swapped, 0.0)
    lhs = jnp.concatenate(lhs_cols, axis=1).astype(BF16)
    scores = []
    for b in range(sb):
        rows = slice(b * rows_per_seq, (b + 1) * rows_per_seq)
        scores.append(lax.dot_general(lhs[rows], kall_ref[b].astype(BF16),
                                      (((1,), (1,)), ((), ())), preferred_element_type=F32))
    s = jnp.concatenate(scores, axis=0)
    t_row = lax.broadcasted_iota(jnp.int32, (nrow, SAMPLE_KEYS), 0) % dec_seq
    kj = lax.broadcasted_iota(jnp.int32, (nrow, SAMPLE_KEYS), 1)
    mask = (kj >= t_row + (win - WINDOW)) & (kj <= t_row + win)
    p = _sink_softmax(s, mask, sink_ref[...]).astype(BF16)
    outs = []
    for b in range(sb):
        rows = slice(b * rows_per_seq, (b + 1) * rows_per_seq)
        outs.append(jnp.dot(p[rows], vall_ref[b].astype(BF16), preferred_element_type=F32))
    o = jnp.concatenate(outs, axis=0)
    z_cols = []
    for qc in range(ATTN_WIDTH // LANES):
        g = qc * HEADS_PER_COL // GQA_GROUP
        kv_col, kv_half = g // HEADS_PER_COL, g % HEADS_PER_COL
        src = o[:, kv_col * LANES:(kv_col + 1) * LANES]
        src_swapped = _swap_halves(src)
        zc = jnp.zeros((nrow, LANES), F32)
        for hh in range(HEADS_PER_COL):
            j = (qc * HEADS_PER_COL + hh) % GQA_GROUP
            here = (row_g == g) & (row_j == j) & (lane_half == hh)
            zc = zc + jnp.where(here, src if hh == kv_half else src_swapped, 0.0)
        z_cols.append(zc)
    z = jnp.concatenate(z_cols, axis=1).astype(BF16)
    rt = lax.broadcasted_iota(jnp.int32, (ntok, nrow), 0)
    ct = lax.broadcasted_iota(jnp.int32, (ntok, nrow), 1)
    unpick = (rt == (ct // rows_per_seq) * dec_seq + ct % dec_seq)
    attn = jnp.dot(unpick.astype(BF16), z, preferred_element_type=F32)
    mix_ref[:, POOL_WIDTH:] = attn.astype(BF16)


def _mix_sample(sink_rows, u, q, k_new, v_new, state_t, cache_k, cache_v, wpool, pscale, dec_seq):
    nseq, win, _ = cache_k.shape
    sb = SEQ_BLOCK
    ntok = sb * dec_seq
    row = lambda w: pl.BlockSpec((ntok, w), lambda i: (i, 0))
    slab = pl.BlockSpec((POOL_HIST, sb, POOL_WIDTH), lambda i: (0, i, 0))
    u_slab = pl.BlockSpec((dec_seq, sb, POOL_WIDTH), lambda i: (0, i, 0))
    cache = pl.BlockSpec((sb, win, KV_WIDTH), lambda i: (i, 0, 0))
    full = lambda a: pl.BlockSpec(a.shape, lambda i: (0,) * a.ndim)
    return pl.pallas_call(
        functools.partial(_mix_sample_kernel, dec_seq=dec_seq),
        grid=(nseq // sb,),
        in_specs=[full(sink_rows), u_slab, row(ATTN_WIDTH), row(KV_WIDTH), row(KV_WIDTH),
                  slab, cache, cache, full(wpool), full(pscale)],
        out_specs=[row(MIX_WIDTH), slab, cache, cache],
        out_shape=[jax.ShapeDtypeStruct((nseq * dec_seq, MIX_WIDTH), BF16),
                   jax.ShapeDtypeStruct((POOL_HIST, nseq, POOL_WIDTH), F32),
                   jax.ShapeDtypeStruct(cache_k.shape, F32),
                   jax.ShapeDtypeStruct(cache_v.shape, F32)],
        scratch_shapes=[pltpu.VMEM((sb, SAMPLE_KEYS, KV_WIDTH), F32),
                        pltpu.VMEM((sb, SAMPLE_KEYS, KV_WIDTH), F32)],
        compiler_params=_params("arbitrary"),
        name="mix_sample",
    )(sink_rows, u, q, k_new, v_new, state_t, cache_k, cache_v, wpool, pscale)


def _outproj_kernel(x_ref, mix_ref, w_ref, g_ref, x1_ref, h2_ref):
    x1 = x_ref[...] + jnp.dot(mix_ref[...], w_ref[...], preferred_element_type=F32)
    x1_ref[...] = x1
    h2_ref[...] = _rms(x1, g_ref[...]).astype(BF16)


def _outproj(x, mix, w_out, g):
    t = x.shape[0]
    tm = TOKEN_TILE
    row = lambda w: pl.BlockSpec((tm, w), lambda i: (i, 0))
    const = lambda a: pl.BlockSpec(a.shape, lambda i: (0,) * a.ndim, pipeline_mode=pl.Buffered(1))
    return pl.pallas_call(
        _outproj_kernel,
        grid=(t // tm,),
        in_specs=[row(D_MODEL), row(MIX_WIDTH), const(w_out), const(g)],
        out_specs=[row(D_MODEL), row(D_MODEL)],
        out_shape=[jax.ShapeDtypeStruct((t, D_MODEL), F32),
                   jax.ShapeDtypeStruct((t, D_MODEL), BF16)],
        compiler_params=_params("parallel"),
        name="outproj",
    )(x, mix, w_out, g)


def _ffn_kernel(x1_hbm, h2_ref, wg_ref, wu_ref, wd_ref, gf_ref, y_ref, x1_buf, sem, *, final_norm):
    i, j = pl.program_id(0), pl.program_id(1)
    tm = y_ref.shape[0]
    residual = pltpu.make_async_copy(x1_hbm.at[pl.ds(pl.multiple_of(i * tm, tm), tm)], x1_buf, sem)

    @pl.when(j == 0)
    def _():
        residual.start()
        y_ref[...] = jnp.zeros_like(y_ref)

    h = h2_ref[...]
    gate = jnp.dot(h, wg_ref[...], preferred_element_type=F32)
    up = jnp.dot(h, wu_ref[...], preferred_element_type=F32)
    act = (jax.nn.silu(gate) * up).astype(BF16)
    y_ref[...] += jnp.dot(act, wd_ref[...], preferred_element_type=F32)

    @pl.when(j == pl.num_programs(1) - 1)
    def _():
        residual.wait()
        y = x1_buf[...] + y_ref[...]
        y_ref[...] = _rms(y, gf_ref[...]) if final_norm else y


def _ffn(x1, h2, wg, wu, wd, gf, final_norm, tm):
    t = x1.shape[0]
    tf = FF_TILE
    row = pl.BlockSpec((tm, D_MODEL), lambda i, j: (i, 0))
    return pl.pallas_call(
        functools.partial(_ffn_kernel, final_norm=final_norm),
        grid=(t // tm, D_FF // tf),
        in_specs=[pl.BlockSpec(memory_space=pl.ANY), row,
                  pl.BlockSpec((D_MODEL, tf), lambda i, j: (0, j)),
                  pl.BlockSpec((D_MODEL, tf), lambda i, j: (0, j)),
                  pl.BlockSpec((tf, D_MODEL), lambda i, j: (j, 0)),
                  pl.BlockSpec(gf.shape, lambda i, j: (0, 0))],
        out_specs=row,
        out_shape=jax.ShapeDtypeStruct((t, D_MODEL), F32),
        scratch_shapes=[pltpu.VMEM((tm, D_MODEL), F32), pltpu.SemaphoreType.DMA(())],
        compiler_params=_params("arbitrary", "arbitrary"),
        name="ffn",
    )(x1, h2, wg, wu, wd, gf)


def _rope_tables(pos):
    inv = ROPE_THETA ** (-jnp.arange(0, ROPE_DIM, 2, dtype=F32) / ROPE_DIM)
    ang = pos.astype(F32)[:, None] * inv[None, :]
    cos, sin = jnp.cos(ang), jnp.sin(ang)
    n = pos.shape[0]
    rest = HEAD_DIM - ROPE_DIM
    one, zero = jnp.ones((n, rest), F32), jnp.zeros((n, rest), F32)
    zh = jnp.zeros((n, ROPE_HALF), F32)
    c = jnp.tile(jnp.concatenate([cos, cos, one], axis=1), (1, HEADS_PER_COL))
    s1 = jnp.tile(jnp.concatenate([-sin, zh, zero], axis=1), (1, HEADS_PER_COL))
    s2 = jnp.tile(jnp.concatenate([zh, sin, zero], axis=1), (1, HEADS_PER_COL))
    return c, s1, s2


def kernel(x_prompt, x_sample, state_pool, cache_k_win, cache_v_win, g_mix, w_in, w_pool,
           pool_scale, attn_sinks, w_out, g_ffn, w_gate, w_up, w_down, g_final):
    batch, seq, _ = x_prompt.shape
    nseq, dec_seq, _ = x_sample.shape
    depth = w_in.shape[0]
    win_s = cache_k_win.shape[2]
    ntok_s = nseq * dec_seq
    assert seq % TOKEN_TILE == 0 and ntok_s == TOKEN_TILE and nseq % SEQ_BLOCK == 0
    assert win_s == WINDOW and win_s + dec_seq <= SAMPLE_KEYS and seq >= WINDOW
    assert PAST_LEN >= max(POOL_HIST, WINDOW)

    xp = x_prompt.reshape(batch * seq, D_MODEL)
    xs = x_sample.reshape(ntok_s, D_MODEL)
    tabs_p = _rope_tables(jnp.arange(seq))
    tabs_s = _rope_tables(PAST_LEN + jnp.arange(ntok_s) % dec_seq)
    gf = g_final.reshape(1, D_MODEL)

    outs = [[] for _ in range(6)]
    for l in range(depth):
        w_in_l, wpool_l = w_in[l].astype(BF16), w_pool[l].astype(BF16)
        gm = g_mix[l].reshape(1, D_MODEL)
        gn = g_ffn[l].reshape(1, D_MODEL)
        pscale = pool_scale[l].reshape(1, POOL_WIDTH)
        sink_rows = jnp.tile(jnp.repeat(attn_sinks[l], dec_seq), SEQ_BLOCK)[:, None]

        u, q, kd, vt, u_tail, k_tail, v_tail, w_out_l, wg_l = _proj(
            xp, gm, w_in_l, tabs_p, seq // TOKEN_TILE, WINDOW, cast=(w_out[l], w_gate[l]))
        mix, wu_l, wd_l = _mix_prompt(attn_sinks[l], u, q, kd, vt, wpool_l, pscale, batch, seq,
                                      cast=(w_up[l], w_down[l]))
        x1, h2 = _outproj(xp, mix, w_out_l, gn)
        xp = _ffn(x1, h2, wg_l, wu_l, wd_l, gf, l == depth - 1, FFN_TOKEN_TILE)
        outs[0].append(u_tail.reshape(batch, HALO, POOL_WIDTH)[:, HALO - POOL_HIST:])
        outs[1].append(k_tail.reshape(batch, WINDOW, N_KV_HEADS, HEAD_DIM))
        outs[2].append(v_tail.reshape(batch, WINDOW, N_KV_HEADS, HEAD_DIM))

        u, q, _, _, _, k_new, v_new = _proj(xs, gm, w_in_l, tabs_s, 1, TOKEN_TILE)
        state_t = jnp.transpose(state_pool[l], (1, 0, 2))
        ck = cache_k_win[l].reshape(nseq, win_s, KV_WIDTH)
        cv = cache_v_win[l].reshape(nseq, win_s, KV_WIDTH)
        u_t = jnp.transpose(u.reshape(nseq, dec_seq, POOL_WIDTH), (1, 0, 2))
        mix, npool, nk, nv = _mix_sample(sink_rows, u_t, q, k_new, v_new, state_t, ck, cv, wpool_l,
                                         pscale, dec_seq)
        x1, h2 = _outproj(xs, mix, w_out_l, gn)
        xs = _ffn(x1, h2, wg_l, wu_l, wd_l, gf, l == depth - 1, min(FFN_TOKEN_TILE, ntok_s))
        outs[3].append(jnp.transpose(npool, (1, 0, 2)))
        outs[4].append(nk.reshape(nseq, win_s, N_KV_HEADS, HEAD_DIM))
        outs[5].append(nv.reshape(nseq, win_s, N_KV_HEADS, HEAD_DIM))

    y_prompt = xp.reshape(batch, seq, D_MODEL)
    y_sample = xs.reshape(nseq, dec_seq, D_MODEL)
    return (y_prompt, y_sample) + tuple(jnp.stack(o) for o in outs)
```

```python
import functools

import jax
import jax.numpy as jnp
from jax import lax
from jax.experimental import pallas as pl
from jax.experimental.pallas import tpu as pltpu

F32 = jnp.float32
BF16 = jnp.bfloat16

D_MODEL = 2048
POOL_WIDTH = D_MODEL // 2
POOL_WINDOWS = (2, 4, 8, 16)
N_POOL_GROUPS = len(POOL_WINDOWS)
POOL_GROUP = POOL_WIDTH // N_POOL_GROUPS
POOL_HIST = max(POOL_WINDOWS) - 1
HEAD_DIM = 64
N_HEADS = (D_MODEL - POOL_WIDTH) // HEAD_DIM
N_KV_HEADS = 4
GQA_GROUP = N_HEADS // N_KV_HEADS
ATTN_WIDTH = N_HEADS * HEAD_DIM
KV_WIDTH = N_KV_HEADS * HEAD_DIM
IN_WIDTH = POOL_WIDTH + ATTN_WIDTH + 2 * KV_WIDTH
MIX_WIDTH = POOL_WIDTH + ATTN_WIDTH
WINDOW = 128
BLOCK = 128
ROPE_DIM = HEAD_DIM // 4
ROPE_HALF = ROPE_DIM // 2
ROPE_THETA = 500000.0
D_FF = ((8 * D_MODEL // 3 + 255) // 256) * 256
EPS = 1e-5
PAST_LEN = 16384
Q_SCALE = HEAD_DIM ** -0.5

LANES = 128
SUBLANES = 8
HEADS_PER_COL = LANES // HEAD_DIM
HALO = 16
POOL_TOP = 2 * HALO
TOKEN_TILE = 512
FFN_TOKEN_TILE = 1024
OUT_PIECE = 256
FF_TILE = 512
SEQ_BLOCK = 8
SAMPLE_KEYS = 256
VMEM_LIMIT = 60 * 1024 * 1024


def _params(*semantics):
    return pltpu.CompilerParams(dimension_semantics=semantics, vmem_limit_bytes=VMEM_LIMIT)


def _rms(x, g):
    ms = jnp.mean(x * x, axis=-1, keepdims=True)
    return x * lax.rsqrt(ms + EPS) * g


def _swap_halves(z):
    return pltpu.roll(z, HEAD_DIM, 1)


def _cast_rows(in_refs, out_refs):
    for src, dst in zip(in_refs, out_refs):
        dst[...] = src[...].astype(BF16)


def _cast_specs(weights, steps, step_of=lambda i: i):
    for w in weights:
        assert w.shape[0] % (steps * 2 * SUBLANES) == 0
    specs = [pl.BlockSpec((w.shape[0] // steps, w.shape[1]), lambda *idx: (step_of(*idx), 0))
             for w in weights]
    return specs, [jax.ShapeDtypeStruct(w.shape, BF16) for w in weights]


def _proj_kernel(x_ref, g_ref, w_ref, c_ref, s1_ref, s2_ref, *refs):
    ncast = (len(refs) - 7) // 2
    u_ref, q_ref, kd_ref, vt_ref, utail_ref, ktail_ref, vtail_ref = refs[ncast:ncast + 7]
    _cast_rows(refs[:ncast], refs[ncast + 7:])
    tm = x_ref.shape[0]
    tail = ktail_ref.shape[0]
    h = _rms(x_ref[...], g_ref[...]).astype(BF16)
    proj = jnp.dot(h, w_ref[...], preferred_element_type=F32)
    u = proj[:, :POOL_WIDTH]
    u_ref[...] = u
    utail_ref[...] = u[tm - HALO:, :]
    c, s1, s2 = c_ref[...], s1_ref[...], s2_ref[...]

    def rope(z):
        return (z * c + pltpu.roll(z, LANES - ROPE_HALF, 1) * s1
                + pltpu.roll(z, ROPE_HALF, 1) * s2)

    for col in range(ATTN_WIDTH // LANES):
        z = proj[:, POOL_WIDTH + col * LANES:POOL_WIDTH + (col + 1) * LANES]
        q_ref[:, col * LANES:(col + 1) * LANES] = (rope(z) * Q_SCALE).astype(BF16)
    low_half = lax.broadcasted_iota(jnp.int32, (1, LANES), 1) < HEAD_DIM
    k0 = POOL_WIDTH + ATTN_WIDTH
    for a in range(KV_WIDTH // LANES):
        z = rope(proj[:, k0 + a * LANES:k0 + (a + 1) * LANES])
        ktail_ref[:, a * LANES:(a + 1) * LANES] = z[tm - tail:, :]
        zr = _swap_halves(z)
        kd_ref[:, (2 * a) * LANES:(2 * a + 1) * LANES] = jnp.where(low_half, z, zr).astype(BF16)
        kd_ref[:, (2 * a + 1) * LANES:(2 * a + 2) * LANES] = jnp.where(low_half, zr, z).astype(BF16)
    v = proj[:, k0 + KV_WIDTH:]
    vtail_ref[...] = v[tm - tail:, :]
    vt_ref[...] = v.T.astype(BF16)


def _proj(x, g, w_in, tabs, tiles_per_seq, tail, cast=()):
    t = x.shape[0]
    tm = TOKEN_TILE
    steps = t // tm
    nseq = steps // tiles_per_seq
    row = lambda w: pl.BlockSpec((tm, w), lambda i: (i, 0))
    tab = pl.BlockSpec((tm, LANES), lambda i: (i % tiles_per_seq, 0))
    const = lambda a: pl.BlockSpec(a.shape, lambda i: (0,) * a.ndim, pipeline_mode=pl.Buffered(1))
    per_seq = lambda r, w: pl.BlockSpec((r, w), lambda i: (i // tiles_per_seq, 0))
    cast_specs, cast_shapes = _cast_specs(cast, steps)
    return pl.pallas_call(
        _proj_kernel,
        grid=(steps,),
        in_specs=[row(D_MODEL), const(g), const(w_in), tab, tab, tab] + cast_specs,
        out_specs=[row(POOL_WIDTH), row(ATTN_WIDTH), row(2 * KV_WIDTH),
                   pl.BlockSpec((KV_WIDTH, tm), lambda i: (0, i)),
                   per_seq(HALO, POOL_WIDTH), per_seq(tail, KV_WIDTH), per_seq(tail, KV_WIDTH)]
        + cast_specs,
        out_shape=[jax.ShapeDtypeStruct((t, POOL_WIDTH), F32),
                   jax.ShapeDtypeStruct((t, ATTN_WIDTH), BF16),
                   jax.ShapeDtypeStruct((t, 2 * KV_WIDTH), BF16),
                   jax.ShapeDtypeStruct((KV_WIDTH, t), BF16),
                   jax.ShapeDtypeStruct((nseq * HALO, POOL_WIDTH), F32),
                   jax.ShapeDtypeStruct((nseq * tail, KV_WIDTH), F32),
                   jax.ShapeDtypeStruct((nseq * tail, KV_WIDTH), F32)] + cast_shapes,
        compiler_params=_params("arbitrary"),
        name="proj",
    )(x, g, w_in, *tabs, *cast)


def _mix_prompt_kernel(sink_ref, u_ref, halo_ref, q_ref, kdc_ref, kdp_ref, vtc_ref, vtp_ref,
                       wpool_ref, pscale_ref, x_ref, wout_ref, gffn_ref, *refs, tiles_per_seq):
    ncast = (len(refs) - 8) // 2
    x1_ref, h2_ref = refs[ncast:ncast + 2]
    ext_ref, lvl_ref, kwin_ref, vtwin_ref, bias_ref, mixbuf_ref = refs[2 * ncast + 2:]
    _cast_rows(refs[:ncast], refs[ncast + 2:2 * ncast + 2])
    tq = u_ref.shape[0]
    step = pl.program_id(0)
    tile = jnp.minimum(step, pl.num_programs(0) - 2)
    i = lax.rem(tile, tiles_per_seq)
    first = i == 0
    mix_ref = mixbuf_ref.at[lax.rem(step, 2)]
    mix_prev_ref = mixbuf_ref.at[lax.rem(step + 1, 2)]

    @pl.when(step == 0)
    def _():
        mixbuf_ref[1] = jnp.zeros(mixbuf_ref.shape[1:], BF16)

    def out_projection(piece):
        cols = slice(piece * OUT_PIECE, (piece + 1) * OUT_PIECE)
        x1_ref[:, cols] = x_ref[:, cols] + jnp.dot(mix_prev_ref[...], wout_ref[:, cols],
                                                   preferred_element_type=F32)

    top = POOL_TOP
    ext_ref[0:top - HALO, :] = jnp.zeros((top - HALO, POOL_WIDTH), F32)
    ext_ref[top - HALO:top, :] = halo_ref[...] * jnp.where(first, 0.0, 1.0)
    ext_ref[top:, :] = u_ref[...]
    lvl_ref[:, 0:SUBLANES, :] = jnp.zeros((2, SUBLANES, POOL_GROUP), F32)
    nlive = top + tq - SUBLANES
    pos1 = i * tq + lax.broadcasted_iota(jnp.int32, (tq, 1), 0) + 1
    for gi, w in enumerate(POOL_WINDOWS):
        cols = slice(gi * POOL_GROUP, (gi + 1) * POOL_GROUP)
        src = ext_ref.at[:, cols]
        shift, slot = 1, 0
        while shift < w:
            dst = lvl_ref.at[slot]
            dst[SUBLANES:, :] = (src[SUBLANES:SUBLANES + nlive, :]
                                 + src[SUBLANES - shift:SUBLANES - shift + nlive, :])
            src, shift, slot = dst, 2 * shift, 1 - slot
        tok = ext_ref[top:, cols]
        inv_cnt = 1.0 / jnp.minimum(pos1, w).astype(F32)
        d = (src[top:, :] * inv_cnt - tok).astype(BF16)
        po = jnp.dot(d, wpool_ref[gi], preferred_element_type=F32) * pscale_ref[:, cols]
        mix_ref[:, cols] = po.astype(BF16)

    kwin_ref[0:BLOCK, :] = kdp_ref[...]
    kwin_ref[BLOCK:, :] = kdc_ref[...]
    vtwin_ref[:, 0:BLOCK] = vtp_ref[...]
    vtwin_ref[:, BLOCK:] = vtc_ref[...]
    kj = lax.broadcasted_iota(jnp.int32, (2 * BLOCK, BLOCK), 0)
    qi = lax.broadcasted_iota(jnp.int32, (2 * BLOCK, BLOCK), 1)
    band = (kj >= qi) & (kj <= qi + WINDOW)
    first_lo = jnp.where(first, BLOCK, 0)
    bias_ref[0] = jnp.where(band, 0.0, -jnp.inf)
    bias_ref[1] = jnp.where(band & (kj >= first_lo), 0.0, -jnp.inf)
    lane = lax.broadcasted_iota(jnp.int32, (1, LANES), 1)
    keep_half = [(lane // HEAD_DIM == hh).astype(BF16) for hh in range(HEADS_PER_COL)]

    def scores(n, g):
        rows = slice(n * BLOCK, (n + 1) * BLOCK)
        qs = jnp.concatenate(
            [q_ref[rows, (h // HEADS_PER_COL) * LANES:(h // HEADS_PER_COL + 1) * LANES]
             * keep_half[h % HEADS_PER_COL]
             for h in range(GQA_GROUP * g, GQA_GROUP * (g + 1))], axis=0)
        kd = kwin_ref[n * BLOCK:(n + 2) * BLOCK, g * LANES:(g + 1) * LANES]
        return lax.dot_general(kd, qs, (((1,), (1,)), ((), ())), preferred_element_type=F32)

    def attend(n, g, s_t):
        p_t = []
        for j in range(GQA_GROUP):
            s = s_t[:, j * BLOCK:(j + 1) * BLOCK] + bias_ref[1 if n == 0 else 0]
            m = jnp.max(s, axis=0, keepdims=True)
            p = jnp.exp(s - m)
            den = jnp.sum(p, axis=0, keepdims=True) + jnp.exp(sink_ref[GQA_GROUP * g + j] - m)
            p_t.append((p * (1.0 / den)).astype(BF16))
        vt = vtwin_ref[g * HEAD_DIM:(g + 1) * HEAD_DIM, n * BLOCK:(n + 2) * BLOCK]
        o_t = jnp.dot(vt, jnp.concatenate(p_t, axis=1), preferred_element_type=F32)
        for c in range(GQA_GROUP // HEADS_PER_COL):
            col_t = jnp.concatenate(
                [o_t[:, (HEADS_PER_COL * c + hh) * BLOCK:(HEADS_PER_COL * c + hh + 1) * BLOCK]
                 for hh in range(HEADS_PER_COL)], axis=0)
            col = POOL_WIDTH + (GQA_GROUP // HEADS_PER_COL * g + c) * LANES
            mix_ref[n * BLOCK:(n + 1) * BLOCK, col:col + LANES] = col_t.T.astype(BF16)

    work = [(n, g) for n in range(tq // BLOCK) for g in range(N_KV_HEADS)]
    pieces = list(range(D_MODEL // OUT_PIECE))
    s_next = scores(*work[0])
    for idx, (n, g) in enumerate(work):
        s_t = s_next
        if idx + 1 < len(work):
            s_next = scores(*work[idx + 1])
        if idx % 2 == 0 and pieces:
            out_projection(pieces.pop(0))
        attend(n, g, s_t)
    assert not pieces
    h2_ref[...] = _rms(x1_ref[...], gffn_ref[...]).astype(BF16)


def _mix_prompt(sinks, u, q, kd, vt, wpool, pscale, x, w_out, g_ffn, batch, seq, cast=()):
    tq = TOKEN_TILE
    nt = seq // tq
    ntiles = batch * nt
    bpt = tq // BLOCK
    hpt = tq // HALO
    tile = lambda s: jnp.minimum(s, ntiles - 1)
    prev_block = lambda s: jnp.maximum(tile(s) * bpt - 1, 0)
    row = lambda w: pl.BlockSpec((tq, w), lambda s: (tile(s), 0))
    late_row = lambda w: pl.BlockSpec((tq, w), lambda s: (jnp.maximum(s - 1, 0), 0))
    halo = pl.BlockSpec((HALO, POOL_WIDTH), lambda s: (jnp.maximum(tile(s) * hpt - 1, 0), 0))
    const = lambda a: pl.BlockSpec(a.shape, lambda s: (0,) * a.ndim, pipeline_mode=pl.Buffered(1))
    cast_specs, cast_shapes = _cast_specs(cast, ntiles, tile)
    return pl.pallas_call(
        functools.partial(_mix_prompt_kernel, tiles_per_seq=nt),
        grid=(ntiles + 1,),
        in_specs=[pl.BlockSpec(memory_space=pltpu.SMEM), row(POOL_WIDTH), halo, row(ATTN_WIDTH),
                  row(2 * KV_WIDTH),
                  pl.BlockSpec((BLOCK, 2 * KV_WIDTH), lambda s: (prev_block(s), 0)),
                  pl.BlockSpec((KV_WIDTH, tq), lambda s: (0, tile(s))),
                  pl.BlockSpec((KV_WIDTH, BLOCK), lambda s: (0, prev_block(s))),
                  const(wpool), const(pscale), late_row(D_MODEL), const(w_out), const(g_ffn)]
        + cast_specs,
        out_specs=[late_row(D_MODEL), late_row(D_MODEL)] + cast_specs,
        out_shape=[jax.ShapeDtypeStruct((batch * seq, D_MODEL), F32),
                   jax.ShapeDtypeStruct((batch * seq, D_MODEL), BF16)] + cast_shapes,
        scratch_shapes=[pltpu.VMEM((POOL_TOP + tq, POOL_WIDTH), F32),
                        pltpu.VMEM((2, POOL_TOP + tq, POOL_GROUP), F32),
                        pltpu.VMEM((BLOCK + tq, 2 * KV_WIDTH), BF16),
                        pltpu.VMEM((KV_WIDTH, BLOCK + tq), BF16),
                        pltpu.VMEM((2, 2 * BLOCK, BLOCK), F32),
                        pltpu.VMEM((2, tq, MIX_WIDTH), BF16)],
        compiler_params=_params("arbitrary"),
        name="mix_prompt",
    )(sinks, u, u, q, kd, kd, vt, vt, wpool, pscale, x, w_out, g_ffn, *cast)


def _sink_softmax(s, mask, sink):
    s = jnp.where(mask, s, -jnp.inf)
    m = jnp.maximum(jnp.max(s, axis=1, keepdims=True), sink)
    p = jnp.exp(s - m)
    den = jnp.sum(p, axis=1, keepdims=True) + jnp.exp(sink - m)
    return p / den


def _mix_sample_kernel(sink_ref, u_ref, q_ref, kn_ref, vn_ref, st_ref, ck_ref, cv_ref,
                       wpool_ref, pscale_ref, mix_ref, npool_ref, nk_ref, nv_ref,
                       kall_ref, vall_ref, *, dec_seq):
    sb = SEQ_BLOCK
    rows_per_seq = N_HEADS * dec_seq
    nrow = sb * rows_per_seq
    ntok = sb * dec_seq
    win = ck_ref.shape[1]

    @pl.when(pl.program_id(0) == 0)
    def _():
        kall_ref[...] = jnp.zeros_like(kall_ref)
        vall_ref[...] = jnp.zeros_like(vall_ref)

    ext = [st_ref[h] for h in range(POOL_HIST)]
    ext += [u_ref[t] for t in range(dec_seq)]
    for h in range(POOL_HIST):
        npool_ref[h] = ext[h + dec_seq]
    r_out = lax.broadcasted_iota(jnp.int32, (ntok, ntok), 0)
    r_in = lax.broadcasted_iota(jnp.int32, (ntok, ntok), 1)
    to_seq_major = ((r_in % sb) * dec_seq + r_in // sb == r_out).astype(BF16)
    for gi, w in enumerate(POOL_WINDOWS):
        cols = slice(gi * POOL_GROUP, (gi + 1) * POOL_GROUP)
        ds = []
        for t in range(dec_seq):
            tok = ext[POOL_HIST + t][:, cols]
            acc = tok
            for j in range(1, w):
                acc = acc + ext[POOL_HIST + t - j][:, cols]
            ds.append(acc * (1.0 / w) - tok)
        d = jnp.concatenate(ds, axis=0).astype(BF16)
        po = jnp.dot(d, wpool_ref[gi], preferred_element_type=F32) * pscale_ref[:, cols]
        po = jnp.dot(to_seq_major, po.astype(BF16), preferred_element_type=F32)
        mix_ref[:, cols] = po.astype(BF16)

    for b in range(sb):
        tok_rows = slice(b * dec_seq, (b + 1) * dec_seq)
        kall_ref[b, 0:win, :] = ck_ref[b]
        kall_ref[b, win:win + dec_seq, :] = kn_ref[tok_rows, :]
        vall_ref[b, 0:win, :] = cv_ref[b]
        vall_ref[b, win:win + dec_seq, :] = vn_ref[tok_rows, :]
        nk_ref[b] = kall_ref[b, dec_seq:dec_seq + win, :]
        nv_ref[b] = vall_ref[b, dec_seq:dec_seq + win, :]

    r = lax.broadcasted_iota(jnp.int32, (nrow, ntok), 0)
    c = lax.broadcasted_iota(jnp.int32, (nrow, ntok), 1)
    pick = (c == (r // rows_per_seq) * dec_seq + r % dec_seq)
    qrep = jnp.dot(pick.astype(BF16), q_ref[...], preferred_element_type=F32)
    rr = lax.broadcasted_iota(jnp.int32, (nrow, LANES), 0) % rows_per_seq
    row_g = rr // (GQA_GROUP * dec_seq)
    row_j = (rr // dec_seq) % GQA_GROUP
    lane_half = lax.broadcasted_iota(jnp.int32, (nrow, LANES), 1) // HEAD_DIM
    lhs_cols = [jnp.zeros((nrow, LANES), F32) for _ in range(KV_WIDTH // LANES)]
    for qc in range(ATTN_WIDTH // LANES):
        g = qc * HEADS_PER_COL // GQA_GROUP
        src = qrep[:, qc * LANES:(qc + 1) * LANES]
        src_swapped = _swap_halves(src)
        kv_col, kv_half = g // HEADS_PER_COL, g % HEADS_PER_COL
        for hh in range(HEADS_PER_COL):
            j = (qc * HEADS_PER_COL + hh) % GQA_GROUP
            here = (row_g == g) & (row_j == j) & (lane_half == kv_half)
            lhs_cols[kv_col] = lhs_cols[kv_col] + jnp.where(
                here, src if hh == kv_half else src_swapped, 0.0)
    lhs = jnp.concatenate(lhs_cols, axis=1).astype(BF16)
    scores = []
    for b in range(sb):
        rows = slice(b * rows_per_seq, (b + 1) * rows_per_seq)
        scores.append(lax.dot_general(lhs[rows], kall_ref[b].astype(BF16),
                                      (((1,), (1,)), ((), ())), preferred_element_type=F32))
    s = jnp.concatenate(scores, axis=0)
    t_row = lax.broadcasted_iota(jnp.int32, (nrow, SAMPLE_KEYS), 0) % dec_seq
    kj = lax.broadcasted_iota(jnp.int32, (nrow, SAMPLE_KEYS), 1)
    mask = (kj >= t_row + (win - WINDOW)) & (kj <= t_row + win)
    p = _sink_softmax(s, mask, sink_ref[...]).astype(BF16)
    outs = []
    for b in range(sb):
        rows = slice(b * rows_per_seq, (b + 1) * rows_per_seq)
        outs.append(jnp.dot(p[rows], vall_ref[b].astype(BF16), preferred_element_type=F32))
    o = jnp.concatenate(outs, axis=0)
    z_cols = []
    for qc in range(ATTN_WIDTH // LANES):
        g = qc * HEADS_PER_COL // GQA_GROUP
        kv_col, kv_half = g // HEADS_PER_COL, g % HEADS_PER_COL
        src = o[:, kv_col * LANES:(kv_col + 1) * LANES]
        src_swapped = _swap_halves(src)
        zc = jnp.zeros((nrow, LANES), F32)
        for hh in range(HEADS_PER_COL):
            j = (qc * HEADS_PER_COL + hh) % GQA_GROUP
            here = (row_g == g) & (row_j == j) & (lane_half == hh)
            zc = zc + jnp.where(here, src if hh == kv_half else src_swapped, 0.0)
        z_cols.append(zc)
    z = jnp.concatenate(z_cols, axis=1).astype(BF16)
    rt = lax.broadcasted_iota(jnp.int32, (ntok, nrow), 0)
    ct = lax.broadcasted_iota(jnp.int32, (ntok, nrow), 1)
    unpick = (rt == (ct // rows_per_seq) * dec_seq + ct % dec_seq)
    attn = jnp.dot(unpick.astype(BF16), z, preferred_element_type=F32)
    mix_ref[:, POOL_WIDTH:] = attn.astype(BF16)


def _mix_sample(sink_rows, u, q, k_new, v_new, state_t, cache_k, cache_v, wpool, pscale, dec_seq):
    nseq, win, _ = cache_k.shape
    sb = SEQ_BLOCK
    ntok = sb * dec_seq
    row = lambda w: pl.BlockSpec((ntok, w), lambda i: (i, 0))
    slab = pl.BlockSpec((POOL_HIST, sb, POOL_WIDTH), lambda i: (0, i, 0))
    u_slab = pl.BlockSpec((dec_seq, sb, POOL_WIDTH), lambda i: (0, i, 0))
    cache = pl.BlockSpec((sb, win, KV_WIDTH), lambda i: (i, 0, 0))
    full = lambda a: pl.BlockSpec(a.shape, lambda i: (0,) * a.ndim)
    return pl.pallas_call(
        functools.partial(_mix_sample_kernel, dec_seq=dec_seq),
        grid=(nseq // sb,),
        in_specs=[full(sink_rows), u_slab, row(ATTN_WIDTH), row(KV_WIDTH), row(KV_WIDTH),
                  slab, cache, cache, full(wpool), full(pscale)],
        out_specs=[row(MIX_WIDTH), slab, cache, cache],
        out_shape=[jax.ShapeDtypeStruct((nseq * dec_seq, MIX_WIDTH), BF16),
                   jax.ShapeDtypeStruct((POOL_HIST, nseq, POOL_WIDTH), F32),
                   jax.ShapeDtypeStruct(cache_k.shape, F32),
                   jax.ShapeDtypeStruct(cache_v.shape, F32)],
        scratch_shapes=[pltpu.VMEM((sb, SAMPLE_KEYS, KV_WIDTH), F32),
                        pltpu.VMEM((sb, SAMPLE_KEYS, KV_WIDTH), F32)],
        compiler_params=_params("arbitrary"),
        name="mix_sample",
    )(sink_rows, u, q, k_new, v_new, state_t, cache_k, cache_v, wpool, pscale)


def _outproj_kernel(x_ref, mix_ref, w_ref, g_ref, x1_ref, h2_ref):
    x1 = x_ref[...] + jnp.dot(mix_ref[...], w_ref[...], preferred_element_type=F32)
    x1_ref[...] = x1
    h2_ref[...] = _rms(x1, g_ref[...]).astype(BF16)


def _outproj(x, mix, w_out, g):
    t = x.shape[0]
    tm = TOKEN_TILE
    row = lambda w: pl.BlockSpec((tm, w), lambda i: (i, 0))
    const = lambda a: pl.BlockSpec(a.shape, lambda i: (0,) * a.ndim, pipeline_mode=pl.Buffered(1))
    return pl.pallas_call(
        _outproj_kernel,
        grid=(t // tm,),
        in_specs=[row(D_MODEL), row(MIX_WIDTH), const(w_out), const(g)],
        out_specs=[row(D_MODEL), row(D_MODEL)],
        out_shape=[jax.ShapeDtypeStruct((t, D_MODEL), F32),
                   jax.ShapeDtypeStruct((t, D_MODEL), BF16)],
        compiler_params=_params("parallel"),
        name="outproj",
    )(x, mix, w_out, g)


def _ffn_kernel(x1_hbm, h2_ref, wg_ref, wu_ref, wd_ref, gf_ref, y_ref, x1_buf, sem, *, final_norm):
    i, j = pl.program_id(0), pl.program_id(1)
    tm = y_ref.shape[0]
    residual = pltpu.make_async_copy(x1_hbm.at[pl.ds(pl.multiple_of(i * tm, tm), tm)], x1_buf, sem)

    @pl.when(j == 0)
    def _():
        residual.start()
        y_ref[...] = jnp.zeros_like(y_ref)

    h = h2_ref[...]
    gate = jnp.dot(h, wg_ref[...], preferred_element_type=F32)
    up = jnp.dot(h, wu_ref[...], preferred_element_type=F32)
    act = (jax.nn.silu(gate) * up).astype(BF16)
    y_ref[...] += jnp.dot(act, wd_ref[...], preferred_element_type=F32)

    @pl.when(j == pl.num_programs(1) - 1)
    def _():
        residual.wait()
        y = x1_buf[...] + y_ref[...]
        y_ref[...] = _rms(y, gf_ref[...]) if final_norm else y


def _ffn(x1, h2, wg, wu, wd, gf, final_norm, tm):
    t = x1.shape[0]
    tf = FF_TILE
    row = pl.BlockSpec((tm, D_MODEL), lambda i, j: (i, 0))
    return pl.pallas_call(
        functools.partial(_ffn_kernel, final_norm=final_norm),
        grid=(t // tm, D_FF // tf),
        in_specs=[pl.BlockSpec(memory_space=pl.ANY), row,
                  pl.BlockSpec((D_MODEL, tf), lambda i, j: (0, j)),
                  pl.BlockSpec((D_MODEL, tf), lambda i, j: (0, j)),
                  pl.BlockSpec((tf, D_MODEL), lambda i, j: (j, 0)),
                  pl.BlockSpec(gf.shape, lambda i, j: (0, 0))],
        out_specs=row,
        out_shape=jax.ShapeDtypeStruct((t, D_MODEL), F32),
        scratch_shapes=[pltpu.VMEM((tm, D_MODEL), F32), pltpu.SemaphoreType.DMA(())],
        compiler_params=_params("arbitrary", "arbitrary"),
        name="ffn",
    )(x1, h2, wg, wu, wd, gf)


def _rope_tables(pos):
    inv = ROPE_THETA ** (-jnp.arange(0, ROPE_DIM, 2, dtype=F32) / ROPE_DIM)
    ang = pos.astype(F32)[:, None] * inv[None, :]
    cos, sin = jnp.cos(ang), jnp.sin(ang)
    n = pos.shape[0]
    rest = HEAD_DIM - ROPE_DIM
    one, zero = jnp.ones((n, rest), F32), jnp.zeros((n, rest), F32)
    zh = jnp.zeros((n, ROPE_HALF), F32)
    c = jnp.tile(jnp.concatenate([cos, cos, one], axis=1), (1, HEADS_PER_COL))
    s1 = jnp.tile(jnp.concatenate([-sin, zh, zero], axis=1), (1, HEADS_PER_COL))
    s2 = jnp.tile(jnp.concatenate([zh, sin, zero], axis=1), (1, HEADS_PER_COL))
    return c, s1, s2


def kernel(x_prompt, x_sample, state_pool, cache_k_win, cache_v_win, g_mix, w_in, w_pool,
           pool_scale, attn_sinks, w_out, g_ffn, w_gate, w_up, w_down, g_final):
    batch, seq, _ = x_prompt.shape
    nseq, dec_seq, _ = x_sample.shape
    depth = w_in.shape[0]
    win_s = cache_k_win.shape[2]
    ntok_s = nseq * dec_seq
    assert seq % TOKEN_TILE == 0 and ntok_s == TOKEN_TILE and nseq % SEQ_BLOCK == 0
    assert win_s == WINDOW and win_s + dec_seq <= SAMPLE_KEYS and seq >= WINDOW
    assert PAST_LEN >= max(POOL_HIST, WINDOW)

    xp = x_prompt.reshape(batch * seq, D_MODEL)
    xs = x_sample.reshape(ntok_s, D_MODEL)
    tabs_p = _rope_tables(jnp.arange(seq))
    tabs_s = _rope_tables(PAST_LEN + jnp.arange(ntok_s) % dec_seq)
    gf = g_final.reshape(1, D_MODEL)

    outs = [[] for _ in range(6)]
    for l in range(depth):
        w_in_l, wpool_l = w_in[l].astype(BF16), w_pool[l].astype(BF16)
        gm = g_mix[l].reshape(1, D_MODEL)
        gn = g_ffn[l].reshape(1, D_MODEL)
        pscale = pool_scale[l].reshape(1, POOL_WIDTH)
        sink_rows = jnp.tile(jnp.repeat(attn_sinks[l], dec_seq), SEQ_BLOCK)[:, None]

        u, q, kd, vt, u_tail, k_tail, v_tail, w_out_l, wg_l, wd_l = _proj(
            xp, gm, w_in_l, tabs_p, seq // TOKEN_TILE, WINDOW, cast=(w_out[l], w_gate[l], w_down[l]))
        x1, h2, wu_l = _mix_prompt(attn_sinks[l], u, q, kd, vt, wpool_l, pscale, xp, w_out_l, gn,
                                   batch, seq, cast=(w_up[l],))
        xp = _ffn(x1, h2, wg_l, wu_l, wd_l, gf, l == depth - 1, FFN_TOKEN_TILE)
        outs[0].append(u_tail.reshape(batch, HALO, POOL_WIDTH)[:, HALO - POOL_HIST:])
        outs[1].append(k_tail.reshape(batch, WINDOW, N_KV_HEADS, HEAD_DIM))
        outs[2].append(v_tail.reshape(batch, WINDOW, N_KV_HEADS, HEAD_DIM))

        u, q, _, _, _, k_new, v_new = _proj(xs, gm, w_in_l, tabs_s, 1, TOKEN_TILE)
        state_t = jnp.transpose(state_pool[l], (1, 0, 2))
        ck = cache_k_win[l].reshape(nseq, win_s, KV_WIDTH)
        cv = cache_v_win[l].reshape(nseq, win_s, KV_WIDTH)
        u_t = jnp.transpose(u.reshape(nseq, dec_seq, POOL_WIDTH), (1, 0, 2))
        mix, npool, nk, nv = _mix_sample(sink_rows, u_t, q, k_new, v_new, state_t, ck, cv, wpool_l,
                                         pscale, dec_seq)
        x1, h2 = _outproj(xs, mix, w_out_l, gn)
        xs = _ffn(x1, h2, wg_l, wu_l, wd_l, gf, l == depth - 1, min(FFN_TOKEN_TILE, ntok_s))
        outs[3].append(jnp.transpose(npool, (1, 0, 2)))
        outs[4].append(nk.reshape(nseq, win_s, N_KV_HEADS, HEAD_DIM))
        outs[5].append(nv.reshape(nseq, win_s, N_KV_HEADS, HEAD_DIM))

    y_prompt = xp.reshape(batch, seq, D_MODEL)
    y_sample = xs.reshape(nseq, dec_seq, D_MODEL)
    return (y_prompt, y_sample) + tuple(jnp.stack(o) for o in outs)
```

```python
import functools

import jax
import jax.numpy as jnp
import numpy as np
from jax import lax
from jax.experimental import pallas as pl
from jax.experimental.pallas import tpu as pltpu

F32 = jnp.float32
BF16 = jnp.bfloat16

D_MODEL = 2048
POOL_WIDTH = D_MODEL // 2
POOL_WINDOWS = (2, 4, 8, 16)
N_POOL_GROUPS = len(POOL_WINDOWS)
POOL_GROUP = POOL_WIDTH // N_POOL_GROUPS
POOL_HIST = max(POOL_WINDOWS) - 1
HEAD_DIM = 64
N_HEADS = (D_MODEL - POOL_WIDTH) // HEAD_DIM
N_KV_HEADS = 4
GQA_GROUP = N_HEADS // N_KV_HEADS
ATTN_WIDTH = N_HEADS * HEAD_DIM
KV_WIDTH = N_KV_HEADS * HEAD_DIM
IN_WIDTH = POOL_WIDTH + ATTN_WIDTH + 2 * KV_WIDTH
MIX_WIDTH = POOL_WIDTH + ATTN_WIDTH
WINDOW = 128
BLOCK = 128
ROPE_DIM = HEAD_DIM // 4
ROPE_HALF = ROPE_DIM // 2
ROPE_THETA = 500000.0
D_FF = ((8 * D_MODEL // 3 + 255) // 256) * 256
EPS = 1e-5
PAST_LEN = 16384
Q_SCALE = HEAD_DIM ** -0.5

LANES = 128
SUBLANES = 8
HEADS_PER_COL = LANES // HEAD_DIM
HALO = 16
POOL_TOP = 2 * HALO
TOKEN_TILE = 512
FFN_TOKEN_TILE = 1024
PROJ_ROW_SPLITS = 2
OUT_COL_PIECES = 4
FF_TILE = 512
SEQ_BLOCK = 8
SAMPLE_KEYS = 256
VMEM_LIMIT = 60 * 1024 * 1024


def _params(*semantics):
    return pltpu.CompilerParams(dimension_semantics=semantics, vmem_limit_bytes=VMEM_LIMIT)


def _rms(x, g):
    ms = jnp.mean(x * x, axis=-1, keepdims=True)
    return x * lax.rsqrt(ms + EPS) * g


def _swap_halves(z):
    return pltpu.roll(z, HEAD_DIM, 1)


def _cast_rows(in_refs, out_refs):
    for src, dst in zip(in_refs, out_refs):
        dst[...] = src[...].astype(BF16)


def _cast_specs(weights, steps, step_of=lambda i: i):
    for w in weights:
        assert w.shape[0] % (steps * 2 * SUBLANES) == 0
    specs = [pl.BlockSpec((w.shape[0] // steps, w.shape[1]), lambda *idx: (step_of(*idx), 0))
             for w in weights]
    return specs, [jax.ShapeDtypeStruct(w.shape, BF16) for w in weights]


def _proj_kernel(x_ref, g_ref, w_ref, tab_ref, *refs):
    ncast = (len(refs) - 7) // 2
    u_ref, q_ref, kd_ref, vt_ref, utail_ref, ktail_ref, vtail_ref = refs[ncast:ncast + 7]
    _cast_rows(refs[:ncast], refs[ncast + 7:])
    tm = x_ref.shape[0]
    tail = ktail_ref.shape[0]
    low_half = lax.broadcasted_iota(jnp.int32, (1, LANES), 1) < HEAD_DIM
    k0 = POOL_WIDTH + ATTN_WIDTH

    def finish(r0, r1, proj):
        u = proj[:, :POOL_WIDTH]
        u_ref[r0:r1, :] = u
        if r1 == tm:
            utail_ref[...] = u[r1 - r0 - HALO:, :]
        c, s1, s2 = (tab_ref[r0:r1, k * LANES:(k + 1) * LANES] for k in range(3))
        t0 = max(r0, tm - tail)

        def rope(z):
            return (z * c + pltpu.roll(z, LANES - ROPE_HALF, 1) * s1
                    + pltpu.roll(z, ROPE_HALF, 1) * s2)

        for col in range(ATTN_WIDTH // LANES):
            z = proj[:, POOL_WIDTH + col * LANES:POOL_WIDTH + (col + 1) * LANES]
            q_ref[r0:r1, col * LANES:(col + 1) * LANES] = (rope(z) * Q_SCALE).astype(BF16)
        for a in range(KV_WIDTH // LANES):
            z = rope(proj[:, k0 + a * LANES:k0 + (a + 1) * LANES])
            if t0 < r1:
                ktail_ref[t0 - (tm - tail):r1 - (tm - tail), a * LANES:(a + 1) * LANES] = z[t0 - r0:, :]
            zr = _swap_halves(z)
            kd_ref[r0:r1, (2 * a) * LANES:(2 * a + 1) * LANES] = jnp.where(low_half, z, zr).astype(BF16)
            kd_ref[r0:r1, (2 * a + 1) * LANES:(2 * a + 2) * LANES] = jnp.where(low_half, zr, z).astype(BF16)
        v = proj[:, k0 + KV_WIDTH:]
        if t0 < r1:
            vtail_ref[t0 - (tm - tail):r1 - (tm - tail), :] = v[t0 - r0:, :]
        vt_ref[:, r0:r1] = v.T.astype(BF16)

    bounds = [(r * tm // PROJ_ROW_SPLITS, (r + 1) * tm // PROJ_ROW_SPLITS) for r in range(PROJ_ROW_SPLITS)]
    hs = [_rms(x_ref[r0:r1, :], g_ref[...]).astype(BF16) for r0, r1 in bounds]
    projs = [jnp.dot(h, w_ref[...], preferred_element_type=F32) for h in hs]
    for (r0, r1), proj in zip(bounds, projs):
        finish(r0, r1, proj)


def _proj(x, g, w_in, tabs, tiles_per_seq, tail, cast=()):
    t = x.shape[0]
    tm = TOKEN_TILE
    steps = t // tm
    nseq = steps // tiles_per_seq
    row = lambda w: pl.BlockSpec((tm, w), lambda i: (i, 0))
    tab = pl.BlockSpec((tm, 3 * LANES), lambda i: (i % tiles_per_seq, 0))
    const = lambda a: pl.BlockSpec(a.shape, lambda i: (0,) * a.ndim, pipeline_mode=pl.Buffered(1))
    per_seq = lambda r, w: pl.BlockSpec((r, w), lambda i: (i // tiles_per_seq, 0))
    cast_specs, cast_shapes = _cast_specs(cast, steps)
    return pl.pallas_call(
        _proj_kernel,
        grid=(steps,),
        in_specs=[row(D_MODEL), const(g), const(w_in), tab] + cast_specs,
        out_specs=[row(POOL_WIDTH), row(ATTN_WIDTH), row(2 * KV_WIDTH),
                   pl.BlockSpec((KV_WIDTH, tm), lambda i: (0, i)),
                   per_seq(HALO, POOL_WIDTH), per_seq(tail, KV_WIDTH), per_seq(tail, KV_WIDTH)]
        + cast_specs,
        out_shape=[jax.ShapeDtypeStruct((t, POOL_WIDTH), F32),
                   jax.ShapeDtypeStruct((t, ATTN_WIDTH), BF16),
                   jax.ShapeDtypeStruct((t, 2 * KV_WIDTH), BF16),
                   jax.ShapeDtypeStruct((KV_WIDTH, t), BF16),
                   jax.ShapeDtypeStruct((nseq * HALO, POOL_WIDTH), F32),
                   jax.ShapeDtypeStruct((nseq * tail, KV_WIDTH), F32),
                   jax.ShapeDtypeStruct((nseq * tail, KV_WIDTH), F32)] + cast_shapes,
        compiler_params=_params("arbitrary"),
        name="proj",
    )(x, g, w_in, tabs, *cast)


def _mix_prompt_kernel(sink_ref, u_ref, halo_ref, q_ref, kdc_ref, kdp_ref, vtc_ref, vtp_ref,
                       wpool_ref, pscale_ref, x_ref, wout_ref, gffn_ref, *refs, tiles_per_seq):
    ncast = (len(refs) - 8) // 2
    x1_ref, h2_ref = refs[ncast:ncast + 2]
    ext_ref, lvl_ref, kwin_ref, vtwin_ref, bias_ref, mixbuf_ref = refs[2 * ncast + 2:]
    _cast_rows(refs[:ncast], refs[ncast + 2:2 * ncast + 2])
    tq = u_ref.shape[0]
    step = pl.program_id(0)
    tile = jnp.minimum(step, pl.num_programs(0) - 2)
    i = lax.rem(tile, tiles_per_seq)
    first = i == 0
    mix_ref = mixbuf_ref.at[lax.rem(step, 2)]
    mix_prev_ref = mixbuf_ref.at[lax.rem(step + 1, 2)]

    @pl.when(step == 0)
    def _():
        mixbuf_ref[1] = jnp.zeros(mixbuf_ref.shape[1:], BF16)

    def out_projection(piece):
        r, c = divmod(piece, OUT_COL_PIECES)
        rows = slice(r * (tq // 2), (r + 1) * (tq // 2))
        cols = slice(c * (D_MODEL // OUT_COL_PIECES), (c + 1) * (D_MODEL // OUT_COL_PIECES))
        x1_ref[rows, cols] = x_ref[rows, cols] + jnp.dot(mix_prev_ref[rows, :], wout_ref[:, cols],
                                                         preferred_element_type=F32)
        if c == OUT_COL_PIECES - 1:
            h2_ref[rows, :] = _rms(x1_ref[rows, :], gffn_ref[...]).astype(BF16)

    top = POOL_TOP
    ext_ref[0:top - HALO, :] = jnp.zeros((top - HALO, POOL_WIDTH), F32)
    ext_ref[top - HALO:top, :] = halo_ref[...] * jnp.where(first, 0.0, 1.0)
    ext_ref[top:, :] = u_ref[...]
    lvl_ref[:, 0:SUBLANES, :] = jnp.zeros((2, SUBLANES, POOL_GROUP), F32)
    nlive = top + tq - SUBLANES
    pos1 = i * tq + lax.broadcasted_iota(jnp.int32, (tq, 1), 0) + 1

    def pool_group(gi):
        w = POOL_WINDOWS[gi]
        cols = slice(gi * POOL_GROUP, (gi + 1) * POOL_GROUP)
        src = ext_ref.at[:, cols]
        shift, slot = 1, 0
        while shift < w:
            dst = lvl_ref.at[slot]
            dst[SUBLANES:, :] = (src[SUBLANES:SUBLANES + nlive, :]
                                 + src[SUBLANES - shift:SUBLANES - shift + nlive, :])
            src, shift, slot = dst, 2 * shift, 1 - slot
        tok = ext_ref[top:, cols]
        inv_cnt = 1.0 / jnp.minimum(pos1, w).astype(F32)
        d = (src[top:, :] * inv_cnt - tok).astype(BF16)
        po = jnp.dot(d, wpool_ref[gi], preferred_element_type=F32) * pscale_ref[:, cols]
        mix_ref[:, cols] = po.astype(BF16)

    kwin_ref[0:BLOCK, :] = kdp_ref[...]
    kwin_ref[BLOCK:, :] = kdc_ref[...]
    vtwin_ref[:, 0:BLOCK] = vtp_ref[...]
    vtwin_ref[:, BLOCK:] = vtc_ref[...]
    kj = lax.broadcasted_iota(jnp.int32, (2 * BLOCK, BLOCK), 0)
    qi = lax.broadcasted_iota(jnp.int32, (2 * BLOCK, BLOCK), 1)
    band = (kj >= qi) & (kj <= qi + WINDOW)
    first_lo = jnp.where(first, BLOCK, 0)
    bias_ref[0] = jnp.where(band, 0.0, -jnp.inf)
    bias_ref[1] = jnp.where(band & (kj >= first_lo), 0.0, -jnp.inf)
    lane = lax.broadcasted_iota(jnp.int32, (1, LANES), 1)
    keep_half = [(lane // HEAD_DIM == hh).astype(BF16) for hh in range(HEADS_PER_COL)]

    def scores(n, g):
        rows = slice(n * BLOCK, (n + 1) * BLOCK)
        qs = jnp.concatenate(
            [q_ref[rows, (h // HEADS_PER_COL) * LANES:(h // HEADS_PER_COL + 1) * LANES]
             * keep_half[h % HEADS_PER_COL]
             for h in range(GQA_GROUP * g, GQA_GROUP * (g + 1))], axis=0)
        kd = kwin_ref[n * BLOCK:(n + 2) * BLOCK, g * LANES:(g + 1) * LANES]
        return lax.dot_general(kd, qs, (((1,), (1,)), ((), ())), preferred_element_type=F32)

    def attend(n, g, s_t):
        p_t = []
        for j in range(GQA_GROUP):
            s = s_t[:, j * BLOCK:(j + 1) * BLOCK] + bias_ref[1 if n == 0 else 0]
            m = jnp.max(s, axis=0, keepdims=True)
            p = jnp.exp(s - m)
            den = jnp.sum(p, axis=0, keepdims=True) + jnp.exp(sink_ref[GQA_GROUP * g + j] - m)
            p_t.append((p * (1.0 / den)).astype(BF16))
        vt = vtwin_ref[g * HEAD_DIM:(g + 1) * HEAD_DIM, n * BLOCK:(n + 2) * BLOCK]
        o_t = jnp.dot(vt, jnp.concatenate(p_t, axis=1), preferred_element_type=F32)
        for c in range(GQA_GROUP // HEADS_PER_COL):
            col_t = jnp.concatenate(
                [o_t[:, (HEADS_PER_COL * c + hh) * BLOCK:(HEADS_PER_COL * c + hh + 1) * BLOCK]
                 for hh in range(HEADS_PER_COL)], axis=0)
            col = POOL_WIDTH + (GQA_GROUP // HEADS_PER_COL * g + c) * LANES
            mix_ref[n * BLOCK:(n + 1) * BLOCK, col:col + LANES] = col_t.T.astype(BF16)

    work = [(n, g) for n in range(tq // BLOCK) for g in range(N_KV_HEADS)]
    pieces = list(range(2 * OUT_COL_PIECES))
    groups = list(range(N_POOL_GROUPS))
    s_next = scores(*work[0])
    for idx, (n, g) in enumerate(work):
        s_t = s_next
        if idx + 1 < len(work):
            s_next = scores(*work[idx + 1])
        if idx % 2 == 0:
            out_projection(pieces.pop(0))
        attend(n, g, s_t)
        if idx % 4 == 1:
            pool_group(groups.pop(0))
    assert not pieces and not groups


def _mix_prompt(sinks, u, q, kd, vt, wpool, pscale, x, w_out, g_ffn, batch, seq, cast=()):
    tq = TOKEN_TILE
    nt = seq // tq
    ntiles = batch * nt
    bpt = tq // BLOCK
    hpt = tq // HALO
    tile = lambda s: jnp.minimum(s, ntiles - 1)
    prev_block = lambda s: jnp.maximum(tile(s) * bpt - 1, 0)
    row = lambda w: pl.BlockSpec((tq, w), lambda s: (tile(s), 0))
    late_row = lambda w: pl.BlockSpec((tq, w), lambda s: (jnp.maximum(s - 1, 0), 0))
    halo = pl.BlockSpec((HALO, POOL_WIDTH), lambda s: (jnp.maximum(tile(s) * hpt - 1, 0), 0))
    const = lambda a: pl.BlockSpec(a.shape, lambda s: (0,) * a.ndim, pipeline_mode=pl.Buffered(1))
    cast_specs, cast_shapes = _cast_specs(cast, ntiles, tile)
    return pl.pallas_call(
        functools.partial(_mix_prompt_kernel, tiles_per_seq=nt),
        grid=(ntiles + 1,),
        in_specs=[pl.BlockSpec(memory_space=pltpu.SMEM), row(POOL_WIDTH), halo, row(ATTN_WIDTH),
                  row(2 * KV_WIDTH),
                  pl.BlockSpec((BLOCK, 2 * KV_WIDTH), lambda s: (prev_block(s), 0)),
                  pl.BlockSpec((KV_WIDTH, tq), lambda s: (0, tile(s))),
                  pl.BlockSpec((KV_WIDTH, BLOCK), lambda s: (0, prev_block(s))),
                  const(wpool), const(pscale), late_row(D_MODEL), const(w_out), const(g_ffn)]
        + cast_specs,
        out_specs=[late_row(D_MODEL), late_row(D_MODEL)] + cast_specs,
        out_shape=[jax.ShapeDtypeStruct((batch * seq, D_MODEL), F32),
                   jax.ShapeDtypeStruct((batch * seq, D_MODEL), BF16)] + cast_shapes,
        scratch_shapes=[pltpu.VMEM((POOL_TOP + tq, POOL_WIDTH), F32),
                        pltpu.VMEM((2, POOL_TOP + tq, POOL_GROUP), F32),
                        pltpu.VMEM((BLOCK + tq, 2 * KV_WIDTH), BF16),
                        pltpu.VMEM((KV_WIDTH, BLOCK + tq), BF16),
                        pltpu.VMEM((2, 2 * BLOCK, BLOCK), F32),
                        pltpu.VMEM((2, tq, MIX_WIDTH), BF16)],
        compiler_params=_params("arbitrary"),
        name="mix_prompt",
    )(sinks, u, u, q, kd, kd, vt, vt, wpool, pscale, x, w_out, g_ffn, *cast)


def _sink_softmax(s, mask, sink):
    s = jnp.where(mask, s, -jnp.inf)
    m = jnp.maximum(jnp.max(s, axis=1, keepdims=True), sink)
    p = jnp.exp(s - m)
    den = jnp.sum(p, axis=1, keepdims=True) + jnp.exp(sink - m)
    return p / den


def _mix_sample_kernel(sink_ref, u_ref, q_ref, kn_ref, vn_ref, st_ref, ck_ref, cv_ref,
                       wpool_ref, pscale_ref, mix_ref, npool_ref, nk_ref, nv_ref,
                       kall_ref, vall_ref, *, dec_seq):
    sb = SEQ_BLOCK
    rows_per_seq = N_HEADS * dec_seq
    nrow = sb * rows_per_seq
    ntok = sb * dec_seq
    win = ck_ref.shape[1]

    @pl.when(pl.program_id(0) == 0)
    def _():
        kall_ref[...] = jnp.zeros_like(kall_ref)
        vall_ref[...] = jnp.zeros_like(vall_ref)

    ext = [st_ref[h] for h in range(POOL_HIST)]
    ext += [u_ref[t] for t in range(dec_seq)]
    for h in range(POOL_HIST):
        npool_ref[h] = ext[h + dec_seq]
    r_out = lax.broadcasted_iota(jnp.int32, (ntok, ntok), 0)
    r_in = lax.broadcasted_iota(jnp.int32, (ntok, ntok), 1)
    to_seq_major = ((r_in % sb) * dec_seq + r_in // sb == r_out).astype(BF16)
    for gi, w in enumerate(POOL_WINDOWS):
        cols = slice(gi * POOL_GROUP, (gi + 1) * POOL_GROUP)
        ds = []
        for t in range(dec_seq):
            tok = ext[POOL_HIST + t][:, cols]
            acc = tok
            for j in range(1, w):
                acc = acc + ext[POOL_HIST + t - j][:, cols]
            ds.append(acc * (1.0 / w) - tok)
        d = jnp.concatenate(ds, axis=0).astype(BF16)
        po = jnp.dot(d, wpool_ref[gi], preferred_element_type=F32) * pscale_ref[:, cols]
        po = jnp.dot(to_seq_major, po.astype(BF16), preferred_element_type=F32)
        mix_ref[:, cols] = po.astype(BF16)

    for b in range(sb):
        tok_rows = slice(b * dec_seq, (b + 1) * dec_seq)
        kall_ref[b, 0:win, :] = ck_ref[b]
        kall_ref[b, win:win + dec_seq, :] = kn_ref[tok_rows, :]
        vall_ref[b, 0:win, :] = cv_ref[b]
        vall_ref[b, win:win + dec_seq, :] = vn_ref[tok_rows, :]
        nk_ref[b] = kall_ref[b, dec_seq:dec_seq + win, :]
        nv_ref[b] = vall_ref[b, dec_seq:dec_seq + win, :]

    r = lax.broadcasted_iota(jnp.int32, (nrow, ntok), 0)
    c = lax.broadcasted_iota(jnp.int32, (nrow, ntok), 1)
    pick = (c == (r // rows_per_seq) * dec_seq + r % dec_seq)
    qrep = jnp.dot(pick.astype(BF16), q_ref[...], preferred_element_type=F32)
    rr = lax.broadcasted_iota(jnp.int32, (nrow, LANES), 0) % rows_per_seq
    row_g = rr // (GQA_GROUP * dec_seq)
    row_j = (rr // dec_seq) % GQA_GROUP
    lane_half = lax.broadcasted_iota(jnp.int32, (nrow, LANES), 1) // HEAD_DIM
    lhs_cols = [jnp.zeros((nrow, LANES), F32) for _ in range(KV_WIDTH // LANES)]
    for qc in range(ATTN_WIDTH // LANES):
        g = qc * HEADS_PER_COL // GQA_GROUP
        src = qrep[:, qc * LANES:(qc + 1) * LANES]
        src_swapped = _swap_halves(src)
        kv_col, kv_half = g // HEADS_PER_COL, g % HEADS_PER_COL
        for hh in range(HEADS_PER_COL):
            j = (qc * HEADS_PER_COL + hh) % GQA_GROUP
            here = (row_g == g) & (row_j == j) & (lane_half == kv_half)
            lhs_cols[kv_col] = lhs_cols[kv_col] + jnp.where(
                here, src if hh == kv_half else src_swapped, 0.0)
    lhs = jnp.concatenate(lhs_cols, axis=1).astype(BF16)
    scores = []
    for b in range(sb):
        rows = slice(b * rows_per_seq, (b + 1) * rows_per_seq)
        scores.append(lax.dot_general(lhs[rows], kall_ref[b].astype(BF16),
                                      (((1,), (1,)), ((), ())), preferred_element_type=F32))
    s = jnp.concatenate(scores, axis=0)
    t_row = lax.broadcasted_iota(jnp.int32, (nrow, SAMPLE_KEYS), 0) % dec_seq
    kj = lax.broadcasted_iota(jnp.int32, (nrow, SAMPLE_KEYS), 1)
    mask = (kj >= t_row + (win - WINDOW)) & (kj <= t_row + win)
    p = _sink_softmax(s, mask, sink_ref[...]).astype(BF16)
    outs = []
    for b in range(sb):
        rows = slice(b * rows_per_seq, (b + 1) * rows_per_seq)
        outs.append(jnp.dot(p[rows], vall_ref[b].astype(BF16), preferred_element_type=F32))
    o = jnp.concatenate(outs, axis=0)
    z_cols = []
    for qc in range(ATTN_WIDTH // LANES):
        g = qc * HEADS_PER_COL // GQA_GROUP
        kv_col, kv_half = g // HEADS_PER_COL, g % HEADS_PER_COL
        src = o[:, kv_col * LANES:(kv_col + 1) * LANES]
        src_swapped = _swap_halves(src)
        zc = jnp.zeros((nrow, LANES), F32)
        for hh in range(HEADS_PER_COL):
            j = (qc * HEADS_PER_COL + hh) % GQA_GROUP
            here = (row_g == g) & (row_j == j) & (lane_half == hh)
            zc = zc + jnp.where(here, src if hh == kv_half else src_swapped, 0.0)
        z_cols.append(zc)
    z = jnp.concatenate(z_cols, axis=1).astype(BF16)
    rt = lax.broadcasted_iota(jnp.int32, (ntok, nrow), 0)
    ct = lax.broadcasted_iota(jnp.int32, (ntok, nrow), 1)
    unpick = (rt == (ct // rows_per_seq) * dec_seq + ct % dec_seq)
    attn = jnp.dot(unpick.astype(BF16), z, preferred_element_type=F32)
    mix_ref[:, POOL_WIDTH:] = attn.astype(BF16)


def _mix_sample(sink_rows, u, q, k_new, v_new, state_t, cache_k, cache_v, wpool, pscale, dec_seq):
    nseq, win, _ = cache_k.shape
    sb = SEQ_BLOCK
    ntok = sb * dec_seq
    row = lambda w: pl.BlockSpec((ntok, w), lambda i: (i, 0))
    slab = pl.BlockSpec((POOL_HIST, sb, POOL_WIDTH), lambda i: (0, i, 0))
    u_slab = pl.BlockSpec((dec_seq, sb, POOL_WIDTH), lambda i: (0, i, 0))
    cache = pl.BlockSpec((sb, win, KV_WIDTH), lambda i: (i, 0, 0))
    full = lambda a: pl.BlockSpec(a.shape, lambda i: (0,) * a.ndim)
    return pl.pallas_call(
        functools.partial(_mix_sample_kernel, dec_seq=dec_seq),
        grid=(nseq // sb,),
        in_specs=[full(sink_rows), u_slab, row(ATTN_WIDTH), row(KV_WIDTH), row(KV_WIDTH),
                  slab, cache, cache, full(wpool), full(pscale)],
        out_specs=[row(MIX_WIDTH), slab, cache, cache],
        out_shape=[jax.ShapeDtypeStruct((nseq * dec_seq, MIX_WIDTH), BF16),
                   jax.ShapeDtypeStruct((POOL_HIST, nseq, POOL_WIDTH), F32),
                   jax.ShapeDtypeStruct(cache_k.shape, F32),
                   jax.ShapeDtypeStruct(cache_v.shape, F32)],
        scratch_shapes=[pltpu.VMEM((sb, SAMPLE_KEYS, KV_WIDTH), F32),
                        pltpu.VMEM((sb, SAMPLE_KEYS, KV_WIDTH), F32)],
        compiler_params=_params("arbitrary"),
        name="mix_sample",
    )(sink_rows, u, q, k_new, v_new, state_t, cache_k, cache_v, wpool, pscale)


def _outproj_kernel(x_ref, mix_ref, w_ref, g_ref, x1_ref, h2_ref):
    x1 = x_ref[...] + jnp.dot(mix_ref[...], w_ref[...], preferred_element_type=F32)
    x1_ref[...] = x1
    h2_ref[...] = _rms(x1, g_ref[...]).astype(BF16)


def _outproj(x, mix, w_out, g):
    t = x.shape[0]
    tm = TOKEN_TILE
    row = lambda w: pl.BlockSpec((tm, w), lambda i: (i, 0))
    const = lambda a: pl.BlockSpec(a.shape, lambda i: (0,) * a.ndim, pipeline_mode=pl.Buffered(1))
    return pl.pallas_call(
        _outproj_kernel,
        grid=(t // tm,),
        in_specs=[row(D_MODEL), row(MIX_WIDTH), const(w_out), const(g)],
        out_specs=[row(D_MODEL), row(D_MODEL)],
        out_shape=[jax.ShapeDtypeStruct((t, D_MODEL), F32),
                   jax.ShapeDtypeStruct((t, D_MODEL), BF16)],
        compiler_params=_params("parallel"),
        name="outproj",
    )(x, mix, w_out, g)


def _ffn_kernel(x1_hbm, h2_ref, wg_ref, wu_ref, wd_ref, gf_ref, y_ref, x1_buf, sem, *, final_norm):
    i, j = pl.program_id(0), pl.program_id(1)
    tm = y_ref.shape[0]
    residual = pltpu.make_async_copy(x1_hbm.at[pl.ds(pl.multiple_of(i * tm, tm), tm)], x1_buf, sem)

    @pl.when(j == 0)
    def _():
        residual.start()
        y_ref[...] = jnp.zeros_like(y_ref)

    h = h2_ref[...]
    gate = jnp.dot(h, wg_ref[...], preferred_element_type=F32)
    up = jnp.dot(h, wu_ref[...], preferred_element_type=F32)
    act = (jax.nn.silu(gate) * up).astype(BF16)
    y_ref[...] += jnp.dot(act, wd_ref[...], preferred_element_type=F32)

    @pl.when(j == pl.num_programs(1) - 1)
    def _():
        residual.wait()
        y = x1_buf[...] + y_ref[...]
        y_ref[...] = _rms(y, gf_ref[...]) if final_norm else y


def _ffn(x1, h2, wg, wu, wd, gf, final_norm, tm):
    t = x1.shape[0]
    tf = FF_TILE
    row = pl.BlockSpec((tm, D_MODEL), lambda i, j: (i, 0))
    return pl.pallas_call(
        functools.partial(_ffn_kernel, final_norm=final_norm),
        grid=(t // tm, D_FF // tf),
        in_specs=[pl.BlockSpec(memory_space=pl.ANY), row,
                  pl.BlockSpec((D_MODEL, tf), lambda i, j: (0, j)),
                  pl.BlockSpec((D_MODEL, tf), lambda i, j: (0, j)),
                  pl.BlockSpec((tf, D_MODEL), lambda i, j: (j, 0)),
                  pl.BlockSpec(gf.shape, lambda i, j: (0, 0))],
        out_specs=row,
        out_shape=jax.ShapeDtypeStruct((t, D_MODEL), F32),
        scratch_shapes=[pltpu.VMEM((tm, D_MODEL), F32), pltpu.SemaphoreType.DMA(())],
        compiler_params=_params("arbitrary", "arbitrary"),
        name="ffn",
    )(x1, h2, wg, wu, wd, gf)


def _rope_tables(pos):
    inv = ROPE_THETA ** (-np.arange(0, ROPE_DIM, 2, dtype=np.float64) / ROPE_DIM)
    ang = np.asarray(pos, np.float64)[:, None] * inv[None, :]
    cos, sin = np.cos(ang), np.sin(ang)
    n = ang.shape[0]
    rest = HEAD_DIM - ROPE_DIM
    one, zero, zh = np.ones((n, rest)), np.zeros((n, rest)), np.zeros((n, ROPE_HALF))
    per_head = [np.concatenate(parts, axis=1) for parts in
                ([cos, cos, one], [-sin, zh, zero], [zh, sin, zero])]
    return jnp.asarray(np.concatenate([np.tile(t, (1, HEADS_PER_COL)) for t in per_head], axis=1), F32)


def kernel(x_prompt, x_sample, state_pool, cache_k_win, cache_v_win, g_mix, w_in, w_pool,
           pool_scale, attn_sinks, w_out, g_ffn, w_gate, w_up, w_down, g_final):
    batch, seq, _ = x_prompt.shape
    nseq, dec_seq, _ = x_sample.shape
    depth = w_in.shape[0]
    win_s = cache_k_win.shape[2]
    ntok_s = nseq * dec_seq
    assert seq % TOKEN_TILE == 0 and ntok_s == TOKEN_TILE and nseq % SEQ_BLOCK == 0
    assert win_s == WINDOW and win_s + dec_seq <= SAMPLE_KEYS and seq >= WINDOW
    assert PAST_LEN >= max(POOL_HIST, WINDOW)

    xp = x_prompt.reshape(batch * seq, D_MODEL)
    xs = x_sample.reshape(ntok_s, D_MODEL)
    tabs_p = _rope_tables(np.arange(seq))
    tabs_s = _rope_tables(PAST_LEN + np.arange(ntok_s) % dec_seq)
    gf = g_final.reshape(1, D_MODEL)

    outs = [[] for _ in range(6)]
    for l in range(depth):
        w_in_l, wpool_l = w_in[l].astype(BF16), w_pool[l].astype(BF16)
        gm = g_mix[l].reshape(1, D_MODEL)
        gn = g_ffn[l].reshape(1, D_MODEL)
        pscale = pool_scale[l].reshape(1, POOL_WIDTH)
        sink_rows = jnp.tile(jnp.repeat(attn_sinks[l], dec_seq), SEQ_BLOCK)[:, None]

        u, q, kd, vt, u_tail, k_tail, v_tail, w_out_l, wg_l, wd_l = _proj(
            xp, gm, w_in_l, tabs_p, seq // TOKEN_TILE, WINDOW, cast=(w_out[l], w_gate[l], w_down[l]))
        x1, h2, wu_l = _mix_prompt(attn_sinks[l], u, q, kd, vt, wpool_l, pscale, xp, w_out_l, gn,
                                   batch, seq, cast=(w_up[l],))
        xp = _ffn(x1, h2, wg_l, wu_l, wd_l, gf, l == depth - 1, FFN_TOKEN_TILE)
        outs[0].append(u_tail.reshape(batch, HALO, POOL_WIDTH)[:, HALO - POOL_HIST:])
        outs[1].append(k_tail.reshape(batch, WINDOW, N_KV_HEADS, HEAD_DIM))
        outs[2].append(v_tail.reshape(batch, WINDOW, N_KV_HEADS, HEAD_DIM))

        u, q, _, _, _, k_new, v_new = _proj(xs, gm, w_in_l, tabs_s, 1, TOKEN_TILE)
        state_t = jnp.transpose(state_pool[l], (1, 0, 2))
        ck = cache_k_win[l].reshape(nseq, win_s, KV_WIDTH)
        cv = cache_v_win[l].reshape(nseq, win_s, KV_WIDTH)
        u_t = jnp.transpose(u.reshape(nseq, dec_seq, POOL_WIDTH), (1, 0, 2))
        mix, npool, nk, nv = _mix_sample(sink_rows, u_t, q, k_new, v_new, state_t, ck, cv, wpool_l,
                                         pscale, dec_seq)
        x1, h2 = _outproj(xs, mix, w_out_l, gn)
        xs = _ffn(x1, h2, wg_l, wu_l, wd_l, gf, l == depth - 1, min(FFN_TOKEN_TILE, ntok_s))
        outs[3].append(jnp.transpose(npool, (1, 0, 2)))
        outs[4].append(nk.reshape(nseq, win_s, N_KV_HEADS, HEAD_DIM))
        outs[5].append(nv.reshape(nseq, win_s, N_KV_HEADS, HEAD_DIM))

    y_prompt = xp.reshape(batch, seq, D_MODEL)
    y_sample = xs.reshape(nseq, dec_seq, D_MODEL)
    return (y_prompt, y_sample) + tuple(jnp.stack(o) for o in outs)
```

```python
import functools

import jax
import jax.numpy as jnp
import numpy as np
from jax import lax
from jax.experimental import pallas as pl
from jax.experimental.pallas import tpu as pltpu

F32 = jnp.float32
BF16 = jnp.bfloat16

D_MODEL = 2048
POOL_WIDTH = D_MODEL // 2
POOL_WINDOWS = (2, 4, 8, 16)
N_POOL_GROUPS = len(POOL_WINDOWS)
POOL_GROUP = POOL_WIDTH // N_POOL_GROUPS
POOL_HIST = max(POOL_WINDOWS) - 1
HEAD_DIM = 64
N_HEADS = (D_MODEL - POOL_WIDTH) // HEAD_DIM
N_KV_HEADS = 4
GQA_GROUP = N_HEADS // N_KV_HEADS
ATTN_WIDTH = N_HEADS * HEAD_DIM
KV_WIDTH = N_KV_HEADS * HEAD_DIM
IN_WIDTH = POOL_WIDTH + ATTN_WIDTH + 2 * KV_WIDTH
MIX_WIDTH = POOL_WIDTH + ATTN_WIDTH
WINDOW = 128
BLOCK = 128
ROPE_DIM = HEAD_DIM // 4
ROPE_HALF = ROPE_DIM // 2
ROPE_THETA = 500000.0
D_FF = ((8 * D_MODEL // 3 + 255) // 256) * 256
EPS = 1e-5
PAST_LEN = 16384
Q_SCALE = HEAD_DIM ** -0.5

LANES = 128
SUBLANES = 8
HEADS_PER_COL = LANES // HEAD_DIM
HALO = 16
POOL_TOP = 2 * HALO
TOKEN_TILE = 512
FFN_TOKEN_TILE = 1024
OUT_COL_PIECES = 4
FF_TILE = 512
SEQ_BLOCK = 8
SAMPLE_KEYS = 256
VMEM_LIMIT = 60 * 1024 * 1024


def _params(*semantics):
    return pltpu.CompilerParams(dimension_semantics=semantics, vmem_limit_bytes=VMEM_LIMIT)


def _rms(x, g):
    ms = jnp.mean(x * x, axis=-1, keepdims=True)
    return x * lax.rsqrt(ms + EPS) * g


def _swap_halves(z):
    return pltpu.roll(z, HEAD_DIM, 1)


def _cast_rows(in_refs, out_refs):
    for src, dst in zip(in_refs, out_refs):
        dst[...] = src[...].astype(BF16)


def _cast_specs(weights, steps, step_of=lambda i: i):
    for w in weights:
        assert w.shape[0] % (steps * 2 * SUBLANES) == 0
    specs = [pl.BlockSpec((w.shape[0] // steps, w.shape[1]), lambda *idx: (step_of(*idx), 0))
             for w in weights]
    return specs, [jax.ShapeDtypeStruct(w.shape, BF16) for w in weights]


def _proj_kernel(x_ref, g_ref, w_ref, tab_ref, *refs):
    ncast = (len(refs) - 7) // 2
    u_ref, q_ref, kd_ref, vt_ref, utail_ref, ktail_ref, vtail_ref = refs[ncast:ncast + 7]
    _cast_rows(refs[:ncast], refs[ncast + 7:])
    tm = x_ref.shape[0]
    tail = ktail_ref.shape[0]
    low_half = lax.broadcasted_iota(jnp.int32, (1, LANES), 1) < HEAD_DIM
    k0 = POOL_WIDTH + ATTN_WIDTH

    def finish(r0, r1, proj):
        u = proj[:, :POOL_WIDTH]
        u_ref[r0:r1, :] = u
        if r1 == tm:
            utail_ref[...] = u[r1 - r0 - HALO:, :]
        c, s1, s2 = (tab_ref[r0:r1, k * LANES:(k + 1) * LANES] for k in range(3))
        t0 = max(r0, tm - tail)

        def rope(z):
            return (z * c + pltpu.roll(z, LANES - ROPE_HALF, 1) * s1
                    + pltpu.roll(z, ROPE_HALF, 1) * s2)

        for col in range(ATTN_WIDTH // LANES):
            z = proj[:, POOL_WIDTH + col * LANES:POOL_WIDTH + (col + 1) * LANES]
            q_ref[r0:r1, col * LANES:(col + 1) * LANES] = (rope(z) * Q_SCALE).astype(BF16)
        for a in range(KV_WIDTH // LANES):
            z = rope(proj[:, k0 + a * LANES:k0 + (a + 1) * LANES])
            if t0 < r1:
                ktail_ref[t0 - (tm - tail):r1 - (tm - tail), a * LANES:(a + 1) * LANES] = z[t0 - r0:, :]
            zr = _swap_halves(z)
            kd_ref[r0:r1, (2 * a) * LANES:(2 * a + 1) * LANES] = jnp.where(low_half, z, zr).astype(BF16)
            kd_ref[r0:r1, (2 * a + 1) * LANES:(2 * a + 2) * LANES] = jnp.where(low_half, zr, z).astype(BF16)
        v = proj[:, k0 + KV_WIDTH:]
        if t0 < r1:
            vtail_ref[t0 - (tm - tail):r1 - (tm - tail), :] = v[t0 - r0:, :]
        vt_ref[:, r0:r1] = v.T.astype(BF16)

    h = _rms(x_ref[...], g_ref[...]).astype(BF16)
    finish(0, tm, jnp.dot(h, w_ref[...], preferred_element_type=F32))


def _proj(x, g, w_in, tabs, tiles_per_seq, tail, cast=()):
    t = x.shape[0]
    tm = TOKEN_TILE
    steps = t // tm
    nseq = steps // tiles_per_seq
    row = lambda w: pl.BlockSpec((tm, w), lambda i: (i, 0))
    tab = pl.BlockSpec((tm, 3 * LANES), lambda i: (i % tiles_per_seq, 0))
    const = lambda a: pl.BlockSpec(a.shape, lambda i: (0,) * a.ndim, pipeline_mode=pl.Buffered(1))
    per_seq = lambda r, w: pl.BlockSpec((r, w), lambda i: (i // tiles_per_seq, 0))
    cast_specs, cast_shapes = _cast_specs(cast, steps)
    return pl.pallas_call(
        _proj_kernel,
        grid=(steps,),
        in_specs=[row(D_MODEL), const(g), const(w_in), tab] + cast_specs,
        out_specs=[row(POOL_WIDTH), row(ATTN_WIDTH), row(2 * KV_WIDTH),
                   pl.BlockSpec((KV_WIDTH, tm), lambda i: (0, i)),
                   per_seq(HALO, POOL_WIDTH), per_seq(tail, KV_WIDTH), per_seq(tail, KV_WIDTH)]
        + cast_specs,
        out_shape=[jax.ShapeDtypeStruct((t, POOL_WIDTH), F32),
                   jax.ShapeDtypeStruct((t, ATTN_WIDTH), BF16),
                   jax.ShapeDtypeStruct((t, 2 * KV_WIDTH), BF16),
                   jax.ShapeDtypeStruct((KV_WIDTH, t), BF16),
                   jax.ShapeDtypeStruct((nseq * HALO, POOL_WIDTH), F32),
                   jax.ShapeDtypeStruct((nseq * tail, KV_WIDTH), F32),
                   jax.ShapeDtypeStruct((nseq * tail, KV_WIDTH), F32)] + cast_shapes,
        compiler_params=_params("arbitrary"),
        name="proj",
    )(x, g, w_in, tabs, *cast)


def _mix_prompt_kernel(sink_ref, u_ref, halo_ref, q_ref, kdc_ref, kdp_ref, vtc_ref, vtp_ref,
                       wpool_ref, pscale_ref, x_ref, wout_ref, gffn_ref, *refs, tiles_per_seq):
    ncast = (len(refs) - 8) // 2
    x1_ref, h2_ref = refs[ncast:ncast + 2]
    ext_ref, lvl_ref, kwin_ref, vtwin_ref, bias_ref, mixbuf_ref = refs[2 * ncast + 2:]
    _cast_rows(refs[:ncast], refs[ncast + 2:2 * ncast + 2])
    tq = u_ref.shape[0]
    step = pl.program_id(0)
    tile = jnp.minimum(step, pl.num_programs(0) - 2)
    i = lax.rem(tile, tiles_per_seq)
    first = i == 0
    mix_ref = mixbuf_ref.at[lax.rem(step, 2)]
    mix_prev_ref = mixbuf_ref.at[lax.rem(step + 1, 2)]

    @pl.when(step == 0)
    def _():
        mixbuf_ref[1] = jnp.zeros(mixbuf_ref.shape[1:], BF16)

    def out_projection(piece):
        r, c = divmod(piece, OUT_COL_PIECES)
        rows = slice(r * (tq // 2), (r + 1) * (tq // 2))
        cols = slice(c * (D_MODEL // OUT_COL_PIECES), (c + 1) * (D_MODEL // OUT_COL_PIECES))
        x1_ref[rows, cols] = x_ref[rows, cols] + jnp.dot(mix_prev_ref[rows, :], wout_ref[:, cols],
                                                         preferred_element_type=F32)
        if c == OUT_COL_PIECES - 1:
            h2_ref[rows, :] = _rms(x1_ref[rows, :], gffn_ref[...]).astype(BF16)

    top = POOL_TOP
    ext_ref[0:top - HALO, :] = jnp.zeros((top - HALO, POOL_WIDTH), F32)
    ext_ref[top - HALO:top, :] = halo_ref[...] * jnp.where(first, 0.0, 1.0)
    ext_ref[top:, :] = u_ref[...]
    lvl_ref[:, 0:SUBLANES, :] = jnp.zeros((2, SUBLANES, POOL_GROUP), F32)
    nlive = top + tq - SUBLANES
    pos1 = i * tq + lax.broadcasted_iota(jnp.int32, (tq, 1), 0) + 1

    def pool_group(gi):
        w = POOL_WINDOWS[gi]
        cols = slice(gi * POOL_GROUP, (gi + 1) * POOL_GROUP)
        src = ext_ref.at[:, cols]
        shift, slot = 1, 0
        while shift < w:
            dst = lvl_ref.at[slot]
            dst[SUBLANES:, :] = (src[SUBLANES:SUBLANES + nlive, :]
                                 + src[SUBLANES - shift:SUBLANES - shift + nlive, :])
            src, shift, slot = dst, 2 * shift, 1 - slot
        tok = ext_ref[top:, cols]
        inv_cnt = 1.0 / jnp.minimum(pos1, w).astype(F32)
        d = (src[top:, :] * inv_cnt - tok).astype(BF16)
        po = jnp.dot(d, wpool_ref[gi], preferred_element_type=F32) * pscale_ref[:, cols]
        mix_ref[:, cols] = po.astype(BF16)

    kwin_ref[0:BLOCK, :] = kdp_ref[...]
    kwin_ref[BLOCK:, :] = kdc_ref[...]
    vtwin_ref[:, 0:BLOCK] = vtp_ref[...]
    vtwin_ref[:, BLOCK:] = vtc_ref[...]
    kj = lax.broadcasted_iota(jnp.int32, (2 * BLOCK, BLOCK), 0)
    qi = lax.broadcasted_iota(jnp.int32, (2 * BLOCK, BLOCK), 1)
    band = (kj >= qi) & (kj <= qi + WINDOW)
    first_lo = jnp.where(first, BLOCK, 0)
    bias_ref[0] = jnp.where(band, 0.0, -jnp.inf)
    bias_ref[1] = jnp.where(band & (kj >= first_lo), 0.0, -jnp.inf)
    lane = lax.broadcasted_iota(jnp.int32, (1, LANES), 1)
    keep_half = [(lane // HEAD_DIM == hh).astype(BF16) for hh in range(HEADS_PER_COL)]

    def scores(n, g):
        rows = slice(n * BLOCK, (n + 1) * BLOCK)
        qs = jnp.concatenate(
            [q_ref[rows, (h // HEADS_PER_COL) * LANES:(h // HEADS_PER_COL + 1) * LANES]
             * keep_half[h % HEADS_PER_COL]
             for h in range(GQA_GROUP * g, GQA_GROUP * (g + 1))], axis=0)
        kd = kwin_ref[n * BLOCK:(n + 2) * BLOCK, g * LANES:(g + 1) * LANES]
        return lax.dot_general(kd, qs, (((1,), (1,)), ((), ())), preferred_element_type=F32)

    def attend(n, g, s_t):
        p_t = []
        for j in range(GQA_GROUP):
            s = s_t[:, j * BLOCK:(j + 1) * BLOCK] + bias_ref[1 if n == 0 else 0]
            m = jnp.max(s, axis=0, keepdims=True)
            p = jnp.exp(s - m)
            den = jnp.sum(p, axis=0, keepdims=True) + jnp.exp(sink_ref[GQA_GROUP * g + j] - m)
            p_t.append((p * (1.0 / den)).astype(BF16))
        vt = vtwin_ref[g * HEAD_DIM:(g + 1) * HEAD_DIM, n * BLOCK:(n + 2) * BLOCK]
        o_t = jnp.dot(vt, jnp.concatenate(p_t, axis=1), preferred_element_type=F32)
        for c in range(GQA_GROUP // HEADS_PER_COL):
            col_t = jnp.concatenate(
                [o_t[:, (HEADS_PER_COL * c + hh) * BLOCK:(HEADS_PER_COL * c + hh + 1) * BLOCK]
                 for hh in range(HEADS_PER_COL)], axis=0)
            col = POOL_WIDTH + (GQA_GROUP // HEADS_PER_COL * g + c) * LANES
            mix_ref[n * BLOCK:(n + 1) * BLOCK, col:col + LANES] = col_t.T.astype(BF16)

    work = [(n, g) for n in range(tq // BLOCK) for g in range(N_KV_HEADS)]
    pieces = list(range(2 * OUT_COL_PIECES))
    groups = list(range(N_POOL_GROUPS))
    s_next = scores(*work[0])
    for idx, (n, g) in enumerate(work):
        s_t = s_next
        if idx + 1 < len(work):
            s_next = scores(*work[idx + 1])
        if idx % 2 == 0:
            out_projection(pieces.pop(0))
        attend(n, g, s_t)
        if idx % 4 == 1:
            pool_group(groups.pop(0))
    assert not pieces and not groups


def _mix_prompt(sinks, u, q, kd, vt, wpool, pscale, x, w_out, g_ffn, batch, seq, cast=()):
    tq = TOKEN_TILE
    nt = seq // tq
    ntiles = batch * nt
    bpt = tq // BLOCK
    hpt = tq // HALO
    tile = lambda s: jnp.minimum(s, ntiles - 1)
    prev_block = lambda s: jnp.maximum(tile(s) * bpt - 1, 0)
    row = lambda w: pl.BlockSpec((tq, w), lambda s: (tile(s), 0))
    late_row = lambda w: pl.BlockSpec((tq, w), lambda s: (jnp.maximum(s - 1, 0), 0))
    halo = pl.BlockSpec((HALO, POOL_WIDTH), lambda s: (jnp.maximum(tile(s) * hpt - 1, 0), 0))
    const = lambda a: pl.BlockSpec(a.shape, lambda s: (0,) * a.ndim, pipeline_mode=pl.Buffered(1))
    cast_specs, cast_shapes = _cast_specs(cast, ntiles, tile)
    return pl.pallas_call(
        functools.partial(_mix_prompt_kernel, tiles_per_seq=nt),
        grid=(ntiles + 1,),
        in_specs=[pl.BlockSpec(memory_space=pltpu.SMEM), row(POOL_WIDTH), halo, row(ATTN_WIDTH),
                  row(2 * KV_WIDTH),
                  pl.BlockSpec((BLOCK, 2 * KV_WIDTH), lambda s: (prev_block(s), 0)),
                  pl.BlockSpec((KV_WIDTH, tq), lambda s: (0, tile(s))),
                  pl.BlockSpec((KV_WIDTH, BLOCK), lambda s: (0, prev_block(s))),
                  const(wpool), const(pscale), late_row(D_MODEL), const(w_out), const(g_ffn)]
        + cast_specs,
        out_specs=[late_row(D_MODEL), late_row(D_MODEL)] + cast_specs,
        out_shape=[jax.ShapeDtypeStruct((batch * seq, D_MODEL), F32),
                   jax.ShapeDtypeStruct((batch * seq, D_MODEL), BF16)] + cast_shapes,
        scratch_shapes=[pltpu.VMEM((POOL_TOP + tq, POOL_WIDTH), F32),
                        pltpu.VMEM((2, POOL_TOP + tq, POOL_GROUP), F32),
                        pltpu.VMEM((BLOCK + tq, 2 * KV_WIDTH), BF16),
                        pltpu.VMEM((KV_WIDTH, BLOCK + tq), BF16),
                        pltpu.VMEM((2, 2 * BLOCK, BLOCK), F32),
                        pltpu.VMEM((2, tq, MIX_WIDTH), BF16)],
        compiler_params=_params("arbitrary"),
        name="mix_prompt",
    )(sinks, u, u, q, kd, kd, vt, vt, wpool, pscale, x, w_out, g_ffn, *cast)


def _sink_softmax(s, mask, sink):
    s = jnp.where(mask, s, -jnp.inf)
    m = jnp.maximum(jnp.max(s, axis=1, keepdims=True), sink)
    p = jnp.exp(s - m)
    den = jnp.sum(p, axis=1, keepdims=True) + jnp.exp(sink - m)
    return p / den


def _mix_sample_kernel(sink_ref, u_ref, q_ref, kn_ref, vn_ref, st_ref, ck_ref, cv_ref,
                       wpool_ref, pscale_ref, mix_ref, npool_ref, nk_ref, nv_ref,
                       kall_ref, vall_ref, *, dec_seq):
    sb = SEQ_BLOCK
    rows_per_seq = N_HEADS * dec_seq
    nrow = sb * rows_per_seq
    ntok = sb * dec_seq
    win = ck_ref.shape[1]

    @pl.when(pl.program_id(0) == 0)
    def _():
        kall_ref[...] = jnp.zeros_like(kall_ref)
        vall_ref[...] = jnp.zeros_like(vall_ref)

    ext = [st_ref[h] for h in range(POOL_HIST)]
    ext += [u_ref[t] for t in range(dec_seq)]
    for h in range(POOL_HIST):
        npool_ref[h] = ext[h + dec_seq]
    r_out = lax.broadcasted_iota(jnp.int32, (ntok, ntok), 0)
    r_in = lax.broadcasted_iota(jnp.int32, (ntok, ntok), 1)
    to_seq_major = ((r_in % sb) * dec_seq + r_in // sb == r_out).astype(BF16)
    for gi, w in enumerate(POOL_WINDOWS):
        cols = slice(gi * POOL_GROUP, (gi + 1) * POOL_GROUP)
        ds = []
        for t in range(dec_seq):
            tok = ext[POOL_HIST + t][:, cols]
            acc = tok
            for j in range(1, w):
                acc = acc + ext[POOL_HIST + t - j][:, cols]
            ds.append(acc * (1.0 / w) - tok)
        d = jnp.concatenate(ds, axis=0).astype(BF16)
        po = jnp.dot(d, wpool_ref[gi], preferred_element_type=F32) * pscale_ref[:, cols]
        po = jnp.dot(to_seq_major, po.astype(BF16), preferred_element_type=F32)
        mix_ref[:, cols] = po.astype(BF16)

    for b in range(sb):
        tok_rows = slice(b * dec_seq, (b + 1) * dec_seq)
        kall_ref[b, 0:win, :] = ck_ref[b]
        kall_ref[b, win:win + dec_seq, :] = kn_ref[tok_rows, :]
        vall_ref[b, 0:win, :] = cv_ref[b]
        vall_ref[b, win:win + dec_seq, :] = vn_ref[tok_rows, :]
        nk_ref[b] = kall_ref[b, dec_seq:dec_seq + win, :]
        nv_ref[b] = vall_ref[b, dec_seq:dec_seq + win, :]

    nhalf = 2
    hseq, hrow, htok = sb // nhalf, nrow // nhalf, ntok // nhalf
    r = lax.broadcasted_iota(jnp.int32, (hrow, htok), 0)
    c = lax.broadcasted_iota(jnp.int32, (hrow, htok), 1)
    pick = (c == (r // rows_per_seq) * dec_seq + r % dec_seq).astype(BF16)
    rt = lax.broadcasted_iota(jnp.int32, (htok, hrow), 0)
    ct = lax.broadcasted_iota(jnp.int32, (htok, hrow), 1)
    unpick = (rt == (ct // rows_per_seq) * dec_seq + ct % dec_seq).astype(BF16)
    rr = lax.broadcasted_iota(jnp.int32, (hrow, LANES), 0) % rows_per_seq
    row_g = rr // (GQA_GROUP * dec_seq)
    row_j = (rr // dec_seq) % GQA_GROUP
    lane_half = lax.broadcasted_iota(jnp.int32, (hrow, LANES), 1) // HEAD_DIM
    t_row = lax.broadcasted_iota(jnp.int32, (hrow, SAMPLE_KEYS), 0) % dec_seq
    kj = lax.broadcasted_iota(jnp.int32, (hrow, SAMPLE_KEYS), 1)
    mask = (kj >= t_row + (win - WINDOW)) & (kj <= t_row + win)

    def build_lhs(hf):
        qrep = jnp.dot(pick, q_ref[hf * htok:(hf + 1) * htok, :], preferred_element_type=F32)
        lhs_cols = [jnp.zeros((hrow, LANES), F32) for _ in range(KV_WIDTH // LANES)]
        for qc in range(ATTN_WIDTH // LANES):
            g = qc * HEADS_PER_COL // GQA_GROUP
            src = qrep[:, qc * LANES:(qc + 1) * LANES]
            src_swapped = _swap_halves(src)
            kv_col, kv_half = g // HEADS_PER_COL, g % HEADS_PER_COL
            for hh in range(HEADS_PER_COL):
                j = (qc * HEADS_PER_COL + hh) % GQA_GROUP
                here = (row_g == g) & (row_j == j) & (lane_half == kv_half)
                lhs_cols[kv_col] = lhs_cols[kv_col] + jnp.where(
                    here, src if hh == kv_half else src_swapped, 0.0)
        return jnp.concatenate(lhs_cols, axis=1).astype(BF16)

    def score(hf, lhs):
        return jnp.concatenate(
            [lax.dot_general(lhs[b * rows_per_seq:(b + 1) * rows_per_seq],
                             kall_ref[hf * hseq + b].astype(BF16),
                             (((1,), (1,)), ((), ())), preferred_element_type=F32)
             for b in range(hseq)], axis=0)

    def weigh(hf, s):
        return _sink_softmax(s, mask, sink_ref[hf * hrow:(hf + 1) * hrow, :]).astype(BF16)

    def gather_values(hf, p):
        return jnp.concatenate(
            [jnp.dot(p[b * rows_per_seq:(b + 1) * rows_per_seq], vall_ref[hf * hseq + b].astype(BF16),
                     preferred_element_type=F32) for b in range(hseq)], axis=0)

    def emit(hf, o):
        z_cols = []
        for qc in range(ATTN_WIDTH // LANES):
            g = qc * HEADS_PER_COL // GQA_GROUP
            kv_col, kv_half = g // HEADS_PER_COL, g % HEADS_PER_COL
            src = o[:, kv_col * LANES:(kv_col + 1) * LANES]
            src_swapped = _swap_halves(src)
            zc = jnp.zeros((hrow, LANES), F32)
            for hh in range(HEADS_PER_COL):
                j = (qc * HEADS_PER_COL + hh) % GQA_GROUP
                here = (row_g == g) & (row_j == j) & (lane_half == hh)
                zc = zc + jnp.where(here, src if hh == kv_half else src_swapped, 0.0)
            z_cols.append(zc)
        z = jnp.concatenate(z_cols, axis=1).astype(BF16)
        attn = jnp.dot(unpick, z, preferred_element_type=F32)
        mix_ref[hf * htok:(hf + 1) * htok, POOL_WIDTH:] = attn.astype(BF16)

    state = [build_lhs(hf) for hf in range(nhalf)]
    for stage in (score, weigh, gather_values, emit):
        state = [stage(hf, x) for hf, x in enumerate(state)]


def _mix_sample(sink_rows, u, q, k_new, v_new, state_t, cache_k, cache_v, wpool, pscale, dec_seq):
    nseq, win, _ = cache_k.shape
    sb = SEQ_BLOCK
    ntok = sb * dec_seq
    row = lambda w: pl.BlockSpec((ntok, w), lambda i: (i, 0))
    slab = pl.BlockSpec((POOL_HIST, sb, POOL_WIDTH), lambda i: (0, i, 0))
    u_slab = pl.BlockSpec((dec_seq, sb, POOL_WIDTH), lambda i: (0, i, 0))
    cache = pl.BlockSpec((sb, win, KV_WIDTH), lambda i: (i, 0, 0))
    full = lambda a: pl.BlockSpec(a.shape, lambda i: (0,) * a.ndim)
    return pl.pallas_call(
        functools.partial(_mix_sample_kernel, dec_seq=dec_seq),
        grid=(nseq // sb,),
        in_specs=[full(sink_rows), u_slab, row(ATTN_WIDTH), row(KV_WIDTH), row(KV_WIDTH),
                  slab, cache, cache, full(wpool), full(pscale)],
        out_specs=[row(MIX_WIDTH), slab, cache, cache],
        out_shape=[jax.ShapeDtypeStruct((nseq * dec_seq, MIX_WIDTH), BF16),
                   jax.ShapeDtypeStruct((POOL_HIST, nseq, POOL_WIDTH), F32),
                   jax.ShapeDtypeStruct(cache_k.shape, F32),
                   jax.ShapeDtypeStruct(cache_v.shape, F32)],
        scratch_shapes=[pltpu.VMEM((sb, SAMPLE_KEYS, KV_WIDTH), F32),
                        pltpu.VMEM((sb, SAMPLE_KEYS, KV_WIDTH), F32)],
        compiler_params=_params("arbitrary"),
        name="mix_sample",
    )(sink_rows, u, q, k_new, v_new, state_t, cache_k, cache_v, wpool, pscale)


def _outproj_kernel(x_ref, mix_ref, w_ref, g_ref, x1_ref, h2_ref):
    x1 = x_ref[...] + jnp.dot(mix_ref[...], w_ref[...], preferred_element_type=F32)
    x1_ref[...] = x1
    h2_ref[...] = _rms(x1, g_ref[...]).astype(BF16)


def _outproj(x, mix, w_out, g):
    t = x.shape[0]
    tm = TOKEN_TILE
    row = lambda w: pl.BlockSpec((tm, w), lambda i: (i, 0))
    const = lambda a: pl.BlockSpec(a.shape, lambda i: (0,) * a.ndim, pipeline_mode=pl.Buffered(1))
    return pl.pallas_call(
        _outproj_kernel,
        grid=(t // tm,),
        in_specs=[row(D_MODEL), row(MIX_WIDTH), const(w_out), const(g)],
        out_specs=[row(D_MODEL), row(D_MODEL)],
        out_shape=[jax.ShapeDtypeStruct((t, D_MODEL), F32),
                   jax.ShapeDtypeStruct((t, D_MODEL), BF16)],
        compiler_params=_params("parallel"),
        name="outproj",
    )(x, mix, w_out, g)


def _ffn_kernel(x1_hbm, h2_ref, wg_ref, wu_ref, wd_ref, gf_ref, y_ref, x1_buf, sem, *, final_norm):
    i, j = pl.program_id(0), pl.program_id(1)
    tm = y_ref.shape[0]
    residual = pltpu.make_async_copy(x1_hbm.at[pl.ds(pl.multiple_of(i * tm, tm), tm)], x1_buf, sem)

    @pl.when(j == 0)
    def _():
        residual.start()
        y_ref[...] = jnp.zeros_like(y_ref)

    h = h2_ref[...]
    gate = jnp.dot(h, wg_ref[...], preferred_element_type=F32)
    up = jnp.dot(h, wu_ref[...], preferred_element_type=F32)
    act = (jax.nn.silu(gate) * up).astype(BF16)
    y_ref[...] += jnp.dot(act, wd_ref[...], preferred_element_type=F32)

    @pl.when(j == pl.num_programs(1) - 1)
    def _():
        residual.wait()
        y = x1_buf[...] + y_ref[...]
        y_ref[...] = _rms(y, gf_ref[...]) if final_norm else y


def _ffn(x1, h2, wg, wu, wd, gf, final_norm, tm):
    t = x1.shape[0]
    tf = FF_TILE
    row = pl.BlockSpec((tm, D_MODEL), lambda i, j: (i, 0))
    return pl.pallas_call(
        functools.partial(_ffn_kernel, final_norm=final_norm),
        grid=(t // tm, D_FF // tf),
        in_specs=[pl.BlockSpec(memory_space=pl.ANY), row,
                  pl.BlockSpec((D_MODEL, tf), lambda i, j: (0, j)),
                  pl.BlockSpec((D_MODEL, tf), lambda i, j: (0, j)),
                  pl.BlockSpec((tf, D_MODEL), lambda i, j: (j, 0)),
                  pl.BlockSpec(gf.shape, lambda i, j: (0, 0))],
        out_specs=row,
        out_shape=jax.ShapeDtypeStruct((t, D_MODEL), F32),
        scratch_shapes=[pltpu.VMEM((tm, D_MODEL), F32), pltpu.SemaphoreType.DMA(())],
        compiler_params=_params("arbitrary", "arbitrary"),
        name="ffn",
    )(x1, h2, wg, wu, wd, gf)


def _rope_tables(pos):
    inv = ROPE_THETA ** (-np.arange(0, ROPE_DIM, 2, dtype=np.float64) / ROPE_DIM)
    ang = np.asarray(pos, np.float64)[:, None] * inv[None, :]
    cos, sin = np.cos(ang), np.sin(ang)
    n = ang.shape[0]
    rest = HEAD_DIM - ROPE_DIM
    one, zero, zh = np.ones((n, rest)), np.zeros((n, rest)), np.zeros((n, ROPE_HALF))
    per_head = [np.concatenate(parts, axis=1) for parts in
                ([cos, cos, one], [-sin, zh, zero], [zh, sin, zero])]
    return jnp.asarray(np.concatenate([np.tile(t, (1, HEADS_PER_COL)) for t in per_head], axis=1), F32)


def kernel(x_prompt, x_sample, state_pool, cache_k_win, cache_v_win, g_mix, w_in, w_pool,
           pool_scale, attn_sinks, w_out, g_ffn, w_gate, w_up, w_down, g_final):
    batch, seq, _ = x_prompt.shape
    nseq, dec_seq, _ = x_sample.shape
    depth = w_in.shape[0]
    win_s = cache_k_win.shape[2]
    ntok_s = nseq * dec_seq
    assert seq % TOKEN_TILE == 0 and ntok_s == TOKEN_TILE and nseq % SEQ_BLOCK == 0
    assert win_s == WINDOW and win_s + dec_seq <= SAMPLE_KEYS and seq >= WINDOW
    assert PAST_LEN >= max(POOL_HIST, WINDOW)

    xp = x_prompt.reshape(batch * seq, D_MODEL)
    xs = x_sample.reshape(ntok_s, D_MODEL)
    tabs_p = _rope_tables(np.arange(seq))
    tabs_s = _rope_tables(PAST_LEN + np.arange(ntok_s) % dec_seq)
    gf = g_final.reshape(1, D_MODEL)

    outs = [[] for _ in range(6)]
    for l in range(depth):
        w_in_l, wpool_l = w_in[l].astype(BF16), w_pool[l].astype(BF16)
        gm = g_mix[l].reshape(1, D_MODEL)
        gn = g_ffn[l].reshape(1, D_MODEL)
        pscale = pool_scale[l].reshape(1, POOL_WIDTH)
        sink_rows = jnp.tile(jnp.repeat(attn_sinks[l], dec_seq), SEQ_BLOCK)[:, None]

        u, q, kd, vt, u_tail, k_tail, v_tail, w_out_l, wg_l, wd_l = _proj(
            xp, gm, w_in_l, tabs_p, seq // TOKEN_TILE, WINDOW, cast=(w_out[l], w_gate[l], w_down[l]))
        x1, h2, wu_l = _mix_prompt(attn_sinks[l], u, q, kd, vt, wpool_l, pscale, xp, w_out_l, gn,
                                   batch, seq, cast=(w_up[l],))
        xp = _ffn(x1, h2, wg_l, wu_l, wd_l, gf, l == depth - 1, FFN_TOKEN_TILE)
        outs[0].append(u_tail.reshape(batch, HALO, POOL_WIDTH)[:, HALO - POOL_HIST:])
        outs[1].append(k_tail.reshape(batch, WINDOW, N_KV_HEADS, HEAD_DIM))
        outs[2].append(v_tail.reshape(batch, WINDOW, N_KV_HEADS, HEAD_DIM))

        u, q, _, _, _, k_new, v_new = _proj(xs, gm, w_in_l, tabs_s, 1, TOKEN_TILE)
        state_t = jnp.transpose(state_pool[l], (1, 0, 2))
        ck = cache_k_win[l].reshape(nseq, win_s, KV_WIDTH)
        cv = cache_v_win[l].reshape(nseq, win_s, KV_WIDTH)
        u_t = jnp.transpose(u.reshape(nseq, dec_seq, POOL_WIDTH), (1, 0, 2))
        mix, npool, nk, nv = _mix_sample(sink_rows, u_t, q, k_new, v_new, state_t, ck, cv, wpool_l,
                                         pscale, dec_seq)
        x1, h2 = _outproj(xs, mix, w_out_l, gn)
        xs = _ffn(x1, h2, wg_l, wu_l, wd_l, gf, l == depth - 1, min(FFN_TOKEN_TILE, ntok_s))
        outs[3].append(jnp.transpose(npool, (1, 0, 2)))
        outs[4].append(nk.reshape(nseq, win_s, N_KV_HEADS, HEAD_DIM))
        outs[5].append(nv.reshape(nseq, win_s, N_KV_HEADS, HEAD_DIM))

    y_prompt = xp.reshape(batch, seq, D_MODEL)
    y_sample = xs.reshape(nseq, dec_seq, D_MODEL)
    return (y_prompt, y_sample) + tuple(jnp.stack(o) for o in outs)
```

```python
import functools

import jax
import jax.numpy as jnp
import numpy as np
from jax import lax
from jax.experimental import pallas as pl
from jax.experimental.pallas import tpu as pltpu

F32 = jnp.float32
BF16 = jnp.bfloat16

D_MODEL = 2048
POOL_WIDTH = D_MODEL // 2
POOL_WINDOWS = (2, 4, 8, 16)
N_POOL_GROUPS = len(POOL_WINDOWS)
POOL_GROUP = POOL_WIDTH // N_POOL_GROUPS
POOL_HIST = max(POOL_WINDOWS) - 1
HEAD_DIM = 64
N_HEADS = (D_MODEL - POOL_WIDTH) // HEAD_DIM
N_KV_HEADS = 4
GQA_GROUP = N_HEADS // N_KV_HEADS
ATTN_WIDTH = N_HEADS * HEAD_DIM
KV_WIDTH = N_KV_HEADS * HEAD_DIM
IN_WIDTH = POOL_WIDTH + ATTN_WIDTH + 2 * KV_WIDTH
MIX_WIDTH = POOL_WIDTH + ATTN_WIDTH
WINDOW = 128
BLOCK = 128
ROPE_DIM = HEAD_DIM // 4
ROPE_HALF = ROPE_DIM // 2
ROPE_THETA = 500000.0
D_FF = ((8 * D_MODEL // 3 + 255) // 256) * 256
EPS = 1e-5
PAST_LEN = 16384
LOG2E = 1.4426950408889634
Q_SCALE = HEAD_DIM ** -0.5 * LOG2E

LANES = 128
SUBLANES = 8
HEADS_PER_COL = LANES // HEAD_DIM
HALO = 16
POOL_TOP = 2 * HALO
TOKEN_TILE = 512
FFN_TOKEN_TILE = 1024
OUT_COL_PIECES = 4
FF_TILE = 512
SEQ_BLOCK = 8
SAMPLE_KEYS = 256
VMEM_LIMIT = 60 * 1024 * 1024


def _params(*semantics):
    return pltpu.CompilerParams(dimension_semantics=semantics, vmem_limit_bytes=VMEM_LIMIT)


def _rms(x, g):
    ms = jnp.mean(x * x, axis=-1, keepdims=True)
    return x * lax.rsqrt(ms + EPS) * g


def _swap_halves(z):
    return pltpu.roll(z, HEAD_DIM, 1)


def _cast_rows(in_refs, out_refs):
    for src, dst in zip(in_refs, out_refs):
        dst[...] = src[...].astype(BF16)


def _cast_specs(weights, steps, step_of=lambda i: i):
    for w in weights:
        assert w.shape[0] % (steps * 2 * SUBLANES) == 0
    specs = [pl.BlockSpec((w.shape[0] // steps, w.shape[1]), lambda *idx: (step_of(*idx), 0))
             for w in weights]
    return specs, [jax.ShapeDtypeStruct(w.shape, BF16) for w in weights]


def _proj_kernel(x_ref, g_ref, w_ref, c_ref, s1_ref, s2_ref, *refs):
    ncast = (len(refs) - 7) // 2
    u_ref, q_ref, kd_ref, vt_ref, utail_ref, ktail_ref, vtail_ref = refs[ncast:ncast + 7]
    _cast_rows(refs[:ncast], refs[ncast + 7:])
    tm = x_ref.shape[0]
    tail = ktail_ref.shape[0]
    x = x_ref[...]
    inv_rms = lax.rsqrt(jnp.mean(x * x, axis=-1, keepdims=True) + EPS)
    xg = (x * g_ref[...]).astype(BF16)
    k0 = POOL_WIDTH + ATTN_WIDTH
    project = lambda c0, c1: jnp.dot(xg, w_ref[:, c0:c1], preferred_element_type=F32)
    raw_kv, raw_q, raw_u = project(k0, IN_WIDTH), project(POOL_WIDTH, k0), project(0, POOL_WIDTH)
    c, s1, s2 = c_ref[...], s1_ref[...], s2_ref[...]

    def rope(z):
        return (z * c + pltpu.roll(z, LANES - ROPE_HALF, 1) * s1
                + pltpu.roll(z, ROPE_HALF, 1) * s2)

    low_half = lax.broadcasted_iota(jnp.int32, (1, LANES), 1) < HEAD_DIM
    for a in range(KV_WIDTH // LANES):
        z = rope(raw_kv[:, a * LANES:(a + 1) * LANES] * inv_rms)
        ktail_ref[:, a * LANES:(a + 1) * LANES] = z[tm - tail:, :]
        zr = _swap_halves(z)
        kd_ref[:, (2 * a) * LANES:(2 * a + 1) * LANES] = jnp.where(low_half, z, zr).astype(BF16)
        kd_ref[:, (2 * a + 1) * LANES:(2 * a + 2) * LANES] = jnp.where(low_half, zr, z).astype(BF16)
    v = raw_kv[:, KV_WIDTH:] * inv_rms
    vtail_ref[...] = v[tm - tail:, :]
    vt_ref[...] = v.T.astype(BF16)
    for col in range(ATTN_WIDTH // LANES):
        z = raw_q[:, col * LANES:(col + 1) * LANES] * inv_rms
        q_ref[:, col * LANES:(col + 1) * LANES] = (rope(z) * Q_SCALE).astype(BF16)
    u = raw_u * inv_rms
    u_ref[...] = u
    utail_ref[...] = u[tm - HALO:, :]


def _proj(x, g, w_in, tabs, tiles_per_seq, tail, cast=()):
    t = x.shape[0]
    tm = TOKEN_TILE
    steps = t // tm
    nseq = steps // tiles_per_seq
    row = lambda w: pl.BlockSpec((tm, w), lambda i: (i, 0))
    tab = pl.BlockSpec((tm, LANES), lambda i: (i % tiles_per_seq, 0))
    const = lambda a: pl.BlockSpec(a.shape, lambda i: (0,) * a.ndim, pipeline_mode=pl.Buffered(1))
    per_seq = lambda r, w: pl.BlockSpec((r, w), lambda i: (i // tiles_per_seq, 0))
    cast_specs, cast_shapes = _cast_specs(cast, steps)
    return pl.pallas_call(
        _proj_kernel,
        grid=(steps,),
        in_specs=[row(D_MODEL), const(g), const(w_in), tab, tab, tab] + cast_specs,
        out_specs=[row(POOL_WIDTH), row(ATTN_WIDTH), row(2 * KV_WIDTH),
                   pl.BlockSpec((KV_WIDTH, tm), lambda i: (0, i)),
                   per_seq(HALO, POOL_WIDTH), per_seq(tail, KV_WIDTH), per_seq(tail, KV_WIDTH)]
        + cast_specs,
        out_shape=[jax.ShapeDtypeStruct((t, POOL_WIDTH), F32),
                   jax.ShapeDtypeStruct((t, ATTN_WIDTH), BF16),
                   jax.ShapeDtypeStruct((t, 2 * KV_WIDTH), BF16),
                   jax.ShapeDtypeStruct((KV_WIDTH, t), BF16),
                   jax.ShapeDtypeStruct((nseq * HALO, POOL_WIDTH), F32),
                   jax.ShapeDtypeStruct((nseq * tail, KV_WIDTH), F32),
                   jax.ShapeDtypeStruct((nseq * tail, KV_WIDTH), F32)] + cast_shapes,
        compiler_params=_params("arbitrary"),
        name="proj",
    )(x, g, w_in, *tabs, *cast)


def _mix_prompt_kernel(sink_ref, u_ref, halo_ref, q_ref, kdc_ref, kdp_ref, vtc_ref, vtp_ref,
                       wpool_ref, pscale_ref, x_ref, wout_ref, gffn_ref, *refs, tiles_per_seq):
    ncast = (len(refs) - 8) // 2
    x1_ref, h2_ref = refs[ncast:ncast + 2]
    ext_ref, lvl_ref, kwin_ref, vtwin_ref, bias_ref, mixbuf_ref = refs[2 * ncast + 2:]
    _cast_rows(refs[:ncast], refs[ncast + 2:2 * ncast + 2])
    tq = u_ref.shape[0]
    step = pl.program_id(0)
    tile = jnp.minimum(step, pl.num_programs(0) - 2)
    i = lax.rem(tile, tiles_per_seq)
    first = i == 0
    mix_ref = mixbuf_ref.at[lax.rem(step, 2)]
    mix_prev_ref = mixbuf_ref.at[lax.rem(step + 1, 2)]

    @pl.when(step == 0)
    def _():
        mixbuf_ref[1] = jnp.zeros(mixbuf_ref.shape[1:], BF16)

    def out_projection(piece):
        r, c = divmod(piece, OUT_COL_PIECES)
        rows = slice(r * (tq // 2), (r + 1) * (tq // 2))
        cols = slice(c * (D_MODEL // OUT_COL_PIECES), (c + 1) * (D_MODEL // OUT_COL_PIECES))
        x1_ref[rows, cols] = x_ref[rows, cols] + jnp.dot(mix_prev_ref[rows, :], wout_ref[:, cols],
                                                         preferred_element_type=F32)
        if c == OUT_COL_PIECES - 1:
            h2_ref[rows, :] = _rms(x1_ref[rows, :], gffn_ref[...]).astype(BF16)

    top = POOL_TOP
    ext_ref[0:top - HALO, :] = jnp.zeros((top - HALO, POOL_WIDTH), F32)
    ext_ref[top - HALO:top, :] = halo_ref[...] * jnp.where(first, 0.0, 1.0)
    ext_ref[top:, :] = u_ref[...]
    lvl_ref[:, 0:SUBLANES, :] = jnp.zeros((2, SUBLANES, POOL_GROUP), F32)
    nlive = top + tq - SUBLANES
    pos1 = i * tq + lax.broadcasted_iota(jnp.int32, (tq, 1), 0) + 1

    def pool_group(gi):
        w = POOL_WINDOWS[gi]
        cols = slice(gi * POOL_GROUP, (gi + 1) * POOL_GROUP)
        src = ext_ref.at[:, cols]
        shift, slot = 1, 0
        while shift < w:
            dst = lvl_ref.at[slot]
            dst[SUBLANES:, :] = (src[SUBLANES:SUBLANES + nlive, :]
                                 + src[SUBLANES - shift:SUBLANES - shift + nlive, :])
            src, shift, slot = dst, 2 * shift, 1 - slot
        tok = ext_ref[top:, cols]
        inv_cnt = 1.0 / jnp.minimum(pos1, w).astype(F32)
        d = (src[top:, :] * inv_cnt - tok).astype(BF16)
        po = jnp.dot(d, wpool_ref[gi], preferred_element_type=F32) * pscale_ref[:, cols]
        mix_ref[:, cols] = po.astype(BF16)

    kwin_ref[0:BLOCK, :] = kdp_ref[...]
    kwin_ref[BLOCK:, :] = kdc_ref[...]
    vtwin_ref[:, 0:BLOCK] = vtp_ref[...]
    vtwin_ref[:, BLOCK:] = vtc_ref[...]
    kj = lax.broadcasted_iota(jnp.int32, (2 * BLOCK, BLOCK), 0)
    qi = lax.broadcasted_iota(jnp.int32, (2 * BLOCK, BLOCK), 1)
    band = (kj >= qi) & (kj <= qi + WINDOW)
    first_lo = jnp.where(first, BLOCK, 0)
    bias_ref[0] = jnp.where(band, 0.0, -jnp.inf)
    bias_ref[1] = jnp.where(band & (kj >= first_lo), 0.0, -jnp.inf)
    lane = lax.broadcasted_iota(jnp.int32, (1, LANES), 1)
    keep_half = [(lane // HEAD_DIM == hh).astype(BF16) for hh in range(HEADS_PER_COL)]

    def scores(n, g):
        rows = slice(n * BLOCK, (n + 1) * BLOCK)
        qs = jnp.concatenate(
            [q_ref[rows, (h // HEADS_PER_COL) * LANES:(h // HEADS_PER_COL + 1) * LANES]
             * keep_half[h % HEADS_PER_COL]
             for h in range(GQA_GROUP * g, GQA_GROUP * (g + 1))], axis=0)
        kd = kwin_ref[n * BLOCK:(n + 2) * BLOCK, g * LANES:(g + 1) * LANES]
        return lax.dot_general(kd, qs, (((1,), (1,)), ((), ())), preferred_element_type=F32)

    def attend(n, g, s_t):
        p_t = []
        for j in range(GQA_GROUP):
            s = s_t[:, j * BLOCK:(j + 1) * BLOCK] + bias_ref[1 if n == 0 else 0]
            m = jnp.max(s, axis=0, keepdims=True)
            p = jnp.exp2(s - m)
            den = jnp.sum(p, axis=0, keepdims=True) + jnp.exp2(sink_ref[GQA_GROUP * g + j] * LOG2E - m)
            p_t.append((p * (1.0 / den)).astype(BF16))
        vt = vtwin_ref[g * HEAD_DIM:(g + 1) * HEAD_DIM, n * BLOCK:(n + 2) * BLOCK]
        o_t = jnp.dot(vt, jnp.concatenate(p_t, axis=1), preferred_element_type=F32)
        for c in range(GQA_GROUP // HEADS_PER_COL):
            col_t = jnp.concatenate(
                [o_t[:, (HEADS_PER_COL * c + hh) * BLOCK:(HEADS_PER_COL * c + hh + 1) * BLOCK]
                 for hh in range(HEADS_PER_COL)], axis=0)
            col = POOL_WIDTH + (GQA_GROUP // HEADS_PER_COL * g + c) * LANES
            mix_ref[n * BLOCK:(n + 1) * BLOCK, col:col + LANES] = col_t.T.astype(BF16)

    work = [(n, g) for n in range(tq // BLOCK) for g in range(N_KV_HEADS)]
    pieces = list(range(2 * OUT_COL_PIECES))
    groups = list(range(N_POOL_GROUPS))
    s_next = scores(*work[0])
    for idx, (n, g) in enumerate(work):
        s_t = s_next
        if idx + 1 < len(work):
            s_next = scores(*work[idx + 1])
        if idx % 2 == 0:
            out_projection(pieces.pop(0))
        attend(n, g, s_t)
        if idx % 4 == 1:
            pool_group(groups.pop(0))
    assert not pieces and not groups


def _mix_prompt(sinks, u, q, kd, vt, wpool, pscale, x, w_out, g_ffn, batch, seq, cast=()):
    tq = TOKEN_TILE
    nt = seq // tq
    ntiles = batch * nt
    bpt = tq // BLOCK
    hpt = tq // HALO
    tile = lambda s: jnp.minimum(s, ntiles - 1)
    prev_block = lambda s: jnp.maximum(tile(s) * bpt - 1, 0)
    row = lambda w: pl.BlockSpec((tq, w), lambda s: (tile(s), 0))
    late_row = lambda w: pl.BlockSpec((tq, w), lambda s: (jnp.maximum(s - 1, 0), 0))
    halo = pl.BlockSpec((HALO, POOL_WIDTH), lambda s: (jnp.maximum(tile(s) * hpt - 1, 0), 0))
    const = lambda a: pl.BlockSpec(a.shape, lambda s: (0,) * a.ndim, pipeline_mode=pl.Buffered(1))
    cast_specs, cast_shapes = _cast_specs(cast, ntiles, tile)
    return pl.pallas_call(
        functools.partial(_mix_prompt_kernel, tiles_per_seq=nt),
        grid=(ntiles + 1,),
        in_specs=[pl.BlockSpec(memory_space=pltpu.SMEM), row(POOL_WIDTH), halo, row(ATTN_WIDTH),
                  row(2 * KV_WIDTH),
                  pl.BlockSpec((BLOCK, 2 * KV_WIDTH), lambda s: (prev_block(s), 0)),
                  pl.BlockSpec((KV_WIDTH, tq), lambda s: (0, tile(s))),
                  pl.BlockSpec((KV_WIDTH, BLOCK), lambda s: (0, prev_block(s))),
                  const(wpool), const(pscale), late_row(D_MODEL), const(w_out), const(g_ffn)]
        + cast_specs,
        out_specs=[late_row(D_MODEL), late_row(D_MODEL)] + cast_specs,
        out_shape=[jax.ShapeDtypeStruct((batch * seq, D_MODEL), F32),
                   jax.ShapeDtypeStruct((batch * seq, D_MODEL), BF16)] + cast_shapes,
        scratch_shapes=[pltpu.VMEM((POOL_TOP + tq, POOL_WIDTH), F32),
                        pltpu.VMEM((2, POOL_TOP + tq, POOL_GROUP), F32),
                        pltpu.VMEM((BLOCK + tq, 2 * KV_WIDTH), BF16),
                        pltpu.VMEM((KV_WIDTH, BLOCK + tq), BF16),
                        pltpu.VMEM((2, 2 * BLOCK, BLOCK), F32),
                        pltpu.VMEM((2, tq, MIX_WIDTH), BF16)],
        compiler_params=_params("arbitrary"),
        name="mix_prompt",
    )(sinks, u, u, q, kd, kd, vt, vt, wpool, pscale, x, w_out, g_ffn, *cast)


def _sink_softmax(s, mask, sink):
    s = jnp.where(mask, s, -jnp.inf)
    sink = sink * LOG2E
    m = jnp.maximum(jnp.max(s, axis=1, keepdims=True), sink)
    p = jnp.exp2(s - m)
    den = jnp.sum(p, axis=1, keepdims=True) + jnp.exp2(sink - m)
    return p / den


def _mix_sample_kernel(sink_ref, u_ref, q_ref, kn_ref, vn_ref, st_ref, ck_ref, cv_ref,
                       wpool_ref, pscale_ref, mix_ref, npool_ref, nk_ref, nv_ref,
                       kall_ref, vall_ref, *, dec_seq):
    sb = SEQ_BLOCK
    rows_per_seq = N_HEADS * dec_seq
    nrow = sb * rows_per_seq
    ntok = sb * dec_seq
    win = ck_ref.shape[1]

    @pl.when(pl.program_id(0) == 0)
    def _():
        kall_ref[...] = jnp.zeros_like(kall_ref)
        vall_ref[...] = jnp.zeros_like(vall_ref)

    ext = [st_ref[h] for h in range(POOL_HIST)]
    ext += [u_ref[t] for t in range(dec_seq)]
    for h in range(POOL_HIST):
        npool_ref[h] = ext[h + dec_seq]
    r_out = lax.broadcasted_iota(jnp.int32, (ntok, ntok), 0)
    r_in = lax.broadcasted_iota(jnp.int32, (ntok, ntok), 1)
    to_seq_major = ((r_in % sb) * dec_seq + r_in // sb == r_out).astype(BF16)
    for gi, w in enumerate(POOL_WINDOWS):
        cols = slice(gi * POOL_GROUP, (gi + 1) * POOL_GROUP)
        ds = []
        for t in range(dec_seq):
            tok = ext[POOL_HIST + t][:, cols]
            acc = tok
            for j in range(1, w):
                acc = acc + ext[POOL_HIST + t - j][:, cols]
            ds.append(acc * (1.0 / w) - tok)
        d = jnp.concatenate(ds, axis=0).astype(BF16)
        po = jnp.dot(d, wpool_ref[gi], preferred_element_type=F32) * pscale_ref[:, cols]
        po = jnp.dot(to_seq_major, po.astype(BF16), preferred_element_type=F32)
        mix_ref[:, cols] = po.astype(BF16)

    for b in range(sb):
        tok_rows = slice(b * dec_seq, (b + 1) * dec_seq)
        kall_ref[b, 0:win, :] = ck_ref[b]
        kall_ref[b, win:win + dec_seq, :] = kn_ref[tok_rows, :]
        vall_ref[b, 0:win, :] = cv_ref[b]
        vall_ref[b, win:win + dec_seq, :] = vn_ref[tok_rows, :]
        nk_ref[b] = kall_ref[b, dec_seq:dec_seq + win, :]
        nv_ref[b] = vall_ref[b, dec_seq:dec_seq + win, :]

    nhalf = 2
    hseq, hrow, htok = sb // nhalf, nrow // nhalf, ntok // nhalf
    r = lax.broadcasted_iota(jnp.int32, (hrow, htok), 0)
    c = lax.broadcasted_iota(jnp.int32, (hrow, htok), 1)
    pick = (c == (r // rows_per_seq) * dec_seq + r % dec_seq).astype(BF16)
    rt = lax.broadcasted_iota(jnp.int32, (htok, hrow), 0)
    ct = lax.broadcasted_iota(jnp.int32, (htok, hrow), 1)
    unpick = (rt == (ct // rows_per_seq) * dec_seq + ct % dec_seq).astype(BF16)
    rr = lax.broadcasted_iota(jnp.int32, (hrow, LANES), 0) % rows_per_seq
    row_g = rr // (GQA_GROUP * dec_seq)
    row_j = (rr // dec_seq) % GQA_GROUP
    lane_half = lax.broadcasted_iota(jnp.int32, (hrow, LANES), 1) // HEAD_DIM
    t_row = lax.broadcasted_iota(jnp.int32, (hrow, SAMPLE_KEYS), 0) % dec_seq
    kj = lax.broadcasted_iota(jnp.int32, (hrow, SAMPLE_KEYS), 1)
    mask = (kj >= t_row + (win - WINDOW)) & (kj <= t_row + win)

    def build_lhs(hf):
        qrep = jnp.dot(pick, q_ref[hf * htok:(hf + 1) * htok, :], preferred_element_type=F32)
        lhs_cols = [jnp.zeros((hrow, LANES), F32) for _ in range(KV_WIDTH // LANES)]
        for qc in range(ATTN_WIDTH // LANES):
            g = qc * HEADS_PER_COL // GQA_GROUP
            src = qrep[:, qc * LANES:(qc + 1) * LANES]
            src_swapped = _swap_halves(src)
            kv_col, kv_half = g // HEADS_PER_COL, g % HEADS_PER_COL
            for hh in range(HEADS_PER_COL):
                j = (qc * HEADS_PER_COL + hh) % GQA_GROUP
                here = (row_g == g) & (row_j == j) & (lane_half == kv_half)
                lhs_cols[kv_col] = lhs_cols[kv_col] + jnp.where(
                    here, src if hh == kv_half else src_swapped, 0.0)
        return jnp.concatenate(lhs_cols, axis=1).astype(BF16)

    def score(hf, lhs):
        return jnp.concatenate(
            [lax.dot_general(lhs[b * rows_per_seq:(b + 1) * rows_per_seq],
                             kall_ref[hf * hseq + b].astype(BF16),
                             (((1,), (1,)), ((), ())), preferred_element_type=F32)
             for b in range(hseq)], axis=0)

    def weigh(hf, s):
        return _sink_softmax(s, mask, sink_ref[hf * hrow:(hf + 1) * hrow, :]).astype(BF16)

    def gather_values(hf, p):
        return jnp.concatenate(
            [jnp.dot(p[b * rows_per_seq:(b + 1) * rows_per_seq], vall_ref[hf * hseq + b].astype(BF16),
                     preferred_element_type=F32) for b in range(hseq)], axis=0)

    def emit(hf, o):
        z_cols = []
        for qc in range(ATTN_WIDTH // LANES):
            g = qc * HEADS_PER_COL // GQA_GROUP
            kv_col, kv_half = g // HEADS_PER_COL, g % HEADS_PER_COL
            src = o[:, kv_col * LANES:(kv_col + 1) * LANES]
            src_swapped = _swap_halves(src)
            zc = jnp.zeros((hrow, LANES), F32)
            for hh in range(HEADS_PER_COL):
                j = (qc * HEADS_PER_COL + hh) % GQA_GROUP
                here = (row_g == g) & (row_j == j) & (lane_half == hh)
                zc = zc + jnp.where(here, src if hh == kv_half else src_swapped, 0.0)
            z_cols.append(zc)
        z = jnp.concatenate(z_cols, axis=1).astype(BF16)
        attn = jnp.dot(unpick, z, preferred_element_type=F32)
        mix_ref[hf * htok:(hf + 1) * htok, POOL_WIDTH:] = attn.astype(BF16)

    state = [build_lhs(hf) for hf in range(nhalf)]
    for stage in (score, weigh, gather_values, emit):
        state = [stage(hf, x) for hf, x in enumerate(state)]


def _mix_sample(sink_rows, u, q, k_new, v_new, state_t, cache_k, cache_v, wpool, pscale, dec_seq):
    nseq, win, _ = cache_k.shape
    sb = SEQ_BLOCK
    ntok = sb * dec_seq
    row = lambda w: pl.BlockSpec((ntok, w), lambda i: (i, 0))
    slab = pl.BlockSpec((POOL_HIST, sb, POOL_WIDTH), lambda i: (0, i, 0))
    u_slab = pl.BlockSpec((dec_seq, sb, POOL_WIDTH), lambda i: (0, i, 0))
    cache = pl.BlockSpec((sb, win, KV_WIDTH), lambda i: (i, 0, 0))
    full = lambda a: pl.BlockSpec(a.shape, lambda i: (0,) * a.ndim)
    return pl.pallas_call(
        functools.partial(_mix_sample_kernel, dec_seq=dec_seq),
        grid=(nseq // sb,),
        in_specs=[full(sink_rows), u_slab, row(ATTN_WIDTH), row(KV_WIDTH), row(KV_WIDTH),
                  slab, cache, cache, full(wpool), full(pscale)],
        out_specs=[row(MIX_WIDTH), slab, cache, cache],
        out_shape=[jax.ShapeDtypeStruct((nseq * dec_seq, MIX_WIDTH), BF16),
                   jax.ShapeDtypeStruct((POOL_HIST, nseq, POOL_WIDTH), F32),
                   jax.ShapeDtypeStruct(cache_k.shape, F32),
                   jax.ShapeDtypeStruct(cache_v.shape, F32)],
        scratch_shapes=[pltpu.VMEM((sb, SAMPLE_KEYS, KV_WIDTH), F32),
                        pltpu.VMEM((sb, SAMPLE_KEYS, KV_WIDTH), F32)],
        compiler_params=_params("arbitrary"),
        name="mix_sample",
    )(sink_rows, u, q, k_new, v_new, state_t, cache_k, cache_v, wpool, pscale)


def _outproj_kernel(x_ref, mix_ref, w_ref, g_ref, x1_ref, h2_ref):
    x1 = x_ref[...] + jnp.dot(mix_ref[...], w_ref[...], preferred_element_type=F32)
    x1_ref[...] = x1
    h2_ref[...] = _rms(x1, g_ref[...]).astype(BF16)


def _outproj(x, mix, w_out, g):
    t = x.shape[0]
    tm = TOKEN_TILE
    row = lambda w: pl.BlockSpec((tm, w), lambda i: (i, 0))
    const = lambda a: pl.BlockSpec(a.shape, lambda i: (0,) * a.ndim, pipeline_mode=pl.Buffered(1))
    return pl.pallas_call(
        _outproj_kernel,
        grid=(t // tm,),
        in_specs=[row(D_MODEL), row(MIX_WIDTH), const(w_out), const(g)],
        out_specs=[row(D_MODEL), row(D_MODEL)],
        out_shape=[jax.ShapeDtypeStruct((t, D_MODEL), F32),
                   jax.ShapeDtypeStruct((t, D_MODEL), BF16)],
        compiler_params=_params("parallel"),
        name="outproj",
    )(x, mix, w_out, g)


def _ffn_kernel(x1_hbm, h2_ref, wg_ref, wu_ref, wd_ref, gf_ref, y_ref, x1_buf, sem, *, final_norm):
    i, j = pl.program_id(0), pl.program_id(1)
    tm = y_ref.shape[0]
    residual = pltpu.make_async_copy(x1_hbm.at[pl.ds(pl.multiple_of(i * tm, tm), tm)], x1_buf, sem)

    @pl.when(j == 0)
    def _():
        residual.start()
        y_ref[...] = jnp.zeros_like(y_ref)

    h = h2_ref[...]
    gate = jnp.dot(h, wg_ref[...], preferred_element_type=F32)
    up = jnp.dot(h, wu_ref[...], preferred_element_type=F32)
    act = (jax.nn.silu(gate) * up).astype(BF16)
    y_ref[...] += jnp.dot(act, wd_ref[...], preferred_element_type=F32)

    @pl.when(j == pl.num_programs(1) - 1)
    def _():
        residual.wait()
        y = x1_buf[...] + y_ref[...]
        y_ref[...] = _rms(y, gf_ref[...]) if final_norm else y


def _ffn(x1, h2, wg, wu, wd, gf, final_norm, tm):
    t = x1.shape[0]
    tf = FF_TILE
    row = pl.BlockSpec((tm, D_MODEL), lambda i, j: (i, 0))
    return pl.pallas_call(
        functools.partial(_ffn_kernel, final_norm=final_norm),
        grid=(t // tm, D_FF // tf),
        in_specs=[pl.BlockSpec(memory_space=pl.ANY), row,
                  pl.BlockSpec((D_MODEL, tf), lambda i, j: (0, j)),
                  pl.BlockSpec((D_MODEL, tf), lambda i, j: (0, j)),
                  pl.BlockSpec((tf, D_MODEL), lambda i, j: (j, 0)),
                  pl.BlockSpec(gf.shape, lambda i, j: (0, 0))],
        out_specs=row,
        out_shape=jax.ShapeDtypeStruct((t, D_MODEL), F32),
        scratch_shapes=[pltpu.VMEM((tm, D_MODEL), F32), pltpu.SemaphoreType.DMA(())],
        compiler_params=_params("arbitrary", "arbitrary"),
        name="ffn",
    )(x1, h2, wg, wu, wd, gf)


def _rope_tables(pos):
    inv = ROPE_THETA ** (-np.arange(0, ROPE_DIM, 2, dtype=np.float64) / ROPE_DIM)
    ang = np.asarray(pos, np.float64)[:, None] * inv[None, :]
    cos, sin = np.cos(ang), np.sin(ang)
    n = ang.shape[0]
    rest = HEAD_DIM - ROPE_DIM
    one, zero, zh = np.ones((n, rest)), np.zeros((n, rest)), np.zeros((n, ROPE_HALF))
    per_head = [np.concatenate(parts, axis=1) for parts in
                ([cos, cos, one], [-sin, zh, zero], [zh, sin, zero])]
    return [jnp.asarray(np.tile(t, (1, HEADS_PER_COL)), F32) for t in per_head]


def kernel(x_prompt, x_sample, state_pool, cache_k_win, cache_v_win, g_mix, w_in, w_pool,
           pool_scale, attn_sinks, w_out, g_ffn, w_gate, w_up, w_down, g_final):
    batch, seq, _ = x_prompt.shape
    nseq, dec_seq, _ = x_sample.shape
    depth = w_in.shape[0]
    win_s = cache_k_win.shape[2]
    ntok_s = nseq * dec_seq
    assert seq % TOKEN_TILE == 0 and ntok_s == TOKEN_TILE and nseq % SEQ_BLOCK == 0
    assert win_s == WINDOW and win_s + dec_seq <= SAMPLE_KEYS and seq >= WINDOW
    assert PAST_LEN >= max(POOL_HIST, WINDOW)

    xp = x_prompt.reshape(batch * seq, D_MODEL)
    xs = x_sample.reshape(ntok_s, D_MODEL)
    tabs_p = _rope_tables(np.arange(seq))
    tabs_s = _rope_tables(PAST_LEN + np.arange(ntok_s) % dec_seq)
    gf = g_final.reshape(1, D_MODEL)

    outs = [[] for _ in range(6)]
    for l in range(depth):
        w_in_l, wpool_l = w_in[l].astype(BF16), w_pool[l].astype(BF16)
        gm = g_mix[l].reshape(1, D_MODEL)
        gn = g_ffn[l].reshape(1, D_MODEL)
        pscale = pool_scale[l].reshape(1, POOL_WIDTH)
        sink_rows = jnp.tile(jnp.repeat(attn_sinks[l], dec_seq), SEQ_BLOCK)[:, None]

        u, q, kd, vt, u_tail, k_tail, v_tail, w_out_l, wg_l, wd_l = _proj(
            xp, gm, w_in_l, tabs_p, seq // TOKEN_TILE, WINDOW, cast=(w_out[l], w_gate[l], w_down[l]))
        x1, h2, wu_l = _mix_prompt(attn_sinks[l], u, q, kd, vt, wpool_l, pscale, xp, w_out_l, gn,
                                   batch, seq, cast=(w_up[l],))
        xp = _ffn(x1, h2, wg_l, wu_l, wd_l, gf, l == depth - 1, FFN_TOKEN_TILE)
        outs[0].append(u_tail.reshape(batch, HALO, POOL_WIDTH)[:, HALO - POOL_HIST:])
        outs[1].append(k_tail.reshape(batch, WINDOW, N_KV_HEADS, HEAD_DIM))
        outs[2].append(v_tail.reshape(batch, WINDOW, N_KV_HEADS, HEAD_DIM))

        u, q, _, _, _, k_new, v_new = _proj(xs, gm, w_in_l, tabs_s, 1, TOKEN_TILE)
        state_t = jnp.transpose(state_pool[l], (1, 0, 2))
        ck = cache_k_win[l].reshape(nseq, win_s, KV_WIDTH)
        cv = cache_v_win[l].reshape(nseq, win_s, KV_WIDTH)
        u_t = jnp.transpose(u.reshape(nseq, dec_seq, POOL_WIDTH), (1, 0, 2))
        mix, npool, nk, nv = _mix_sample(sink_rows, u_t, q, k_new, v_new, state_t, ck, cv, wpool_l,
                                         pscale, dec_seq)
        x1, h2 = _outproj(xs, mix, w_out_l, gn)
        xs = _ffn(x1, h2, wg_l, wu_l, wd_l, gf, l == depth - 1, min(FFN_TOKEN_TILE, ntok_s))
        outs[3].append(jnp.transpose(npool, (1, 0, 2)))
        outs[4].append(nk.reshape(nseq, win_s, N_KV_HEADS, HEAD_DIM))
        outs[5].append(nv.reshape(nseq, win_s, N_KV_HEADS, HEAD_DIM))

    y_prompt = xp.reshape(batch, seq, D_MODEL)
    y_sample = xs.reshape(nseq, dec_seq, D_MODEL)
    return (y_prompt, y_sample) + tuple(jnp.stack(o) for o in outs)
```

```python
import functools

import jax
import jax.numpy as jnp
import numpy as np
from jax import lax
from jax.experimental import pallas as pl
from jax.experimental.pallas import tpu as pltpu

F32 = jnp.float32
BF16 = jnp.bfloat16

D_MODEL = 2048
POOL_WIDTH = D_MODEL // 2
POOL_WINDOWS = (2, 4, 8, 16)
N_POOL_GROUPS = len(POOL_WINDOWS)
POOL_GROUP = POOL_WIDTH // N_POOL_GROUPS
POOL_HIST = max(POOL_WINDOWS) - 1
HEAD_DIM = 64
N_HEADS = (D_MODEL - POOL_WIDTH) // HEAD_DIM
N_KV_HEADS = 4
GQA_GROUP = N_HEADS // N_KV_HEADS
ATTN_WIDTH = N_HEADS * HEAD_DIM
KV_WIDTH = N_KV_HEADS * HEAD_DIM
IN_WIDTH = POOL_WIDTH + ATTN_WIDTH + 2 * KV_WIDTH
MIX_WIDTH = POOL_WIDTH + ATTN_WIDTH
WINDOW = 128
BLOCK = 128
ROPE_DIM = HEAD_DIM // 4
ROPE_HALF = ROPE_DIM // 2
ROPE_THETA = 500000.0
D_FF = ((8 * D_MODEL // 3 + 255) // 256) * 256
EPS = 1e-5
PAST_LEN = 16384
LOG2E = 1.4426950408889634
Q_SCALE = HEAD_DIM ** -0.5 * LOG2E

LANES = 128
SUBLANES = 8
HEADS_PER_COL = LANES // HEAD_DIM
HALO = 16
POOL_TOP = 2 * HALO
TOKEN_TILE = 512
FFN_TOKEN_TILE = 1024
OUT_COL_PIECES = 4
FF_TILE = 512
SEQ_BLOCK = 8
SAMPLE_KEYS = 256
VMEM_LIMIT = 60 * 1024 * 1024


def _params(*semantics):
    return pltpu.CompilerParams(dimension_semantics=semantics, vmem_limit_bytes=VMEM_LIMIT)


def _rms(x, g):
    ms = jnp.mean(x * x, axis=-1, keepdims=True)
    return x * lax.rsqrt(ms + EPS) * g


def _swap_halves(z):
    return pltpu.roll(z, HEAD_DIM, 1)


def _cast_rows(in_refs, out_refs):
    for src, dst in zip(in_refs, out_refs):
        dst[...] = src[...].astype(BF16)


def _cast_specs(weights, steps, step_of=lambda i: i):
    for w in weights:
        assert w.shape[0] % (steps * 2 * SUBLANES) == 0
    specs = [pl.BlockSpec((w.shape[0] // steps, w.shape[1]), lambda *idx: (step_of(*idx), 0))
             for w in weights]
    return specs, [jax.ShapeDtypeStruct(w.shape, BF16) for w in weights]


def _proj_kernel(x_ref, g_ref, w_ref, c_ref, s1_ref, s2_ref, *refs):
    ncast = (len(refs) - 7) // 2
    u_ref, q_ref, kd_ref, vt_ref, utail_ref, ktail_ref, vtail_ref = refs[ncast:ncast + 7]
    _cast_rows(refs[:ncast], refs[ncast + 7:])
    tm = x_ref.shape[0]
    tail = ktail_ref.shape[0]
    x = x_ref[...]
    inv_rms = lax.rsqrt(jnp.mean(x * x, axis=-1, keepdims=True) + EPS)
    xg = (x * g_ref[...]).astype(BF16)
    k0 = POOL_WIDTH + ATTN_WIDTH
    project = lambda c0, c1: jnp.dot(xg, w_ref[:, c0:c1], preferred_element_type=F32)
    raw_kv, raw_q, raw_u = project(k0, IN_WIDTH), project(POOL_WIDTH, k0), project(0, POOL_WIDTH)
    c, s1, s2 = c_ref[...], s1_ref[...], s2_ref[...]

    def rope(z):
        return (z * c + pltpu.roll(z, LANES - ROPE_HALF, 1) * s1
                + pltpu.roll(z, ROPE_HALF, 1) * s2)

    low_half = lax.broadcasted_iota(jnp.int32, (1, LANES), 1) < HEAD_DIM
    for a in range(KV_WIDTH // LANES):
        z = rope(raw_kv[:, a * LANES:(a + 1) * LANES] * inv_rms)
        ktail_ref[:, a * LANES:(a + 1) * LANES] = z[tm - tail:, :]
        zr = _swap_halves(z)
        kd_ref[:, (2 * a) * LANES:(2 * a + 1) * LANES] = jnp.where(low_half, z, zr).astype(BF16)
        kd_ref[:, (2 * a + 1) * LANES:(2 * a + 2) * LANES] = jnp.where(low_half, zr, z).astype(BF16)
    v = raw_kv[:, KV_WIDTH:] * inv_rms
    vtail_ref[...] = v[tm - tail:, :]
    vt_ref[...] = v.T.astype(BF16)
    for col in range(ATTN_WIDTH // LANES):
        z = raw_q[:, col * LANES:(col + 1) * LANES] * inv_rms
        q_ref[:, col * LANES:(col + 1) * LANES] = (rope(z) * Q_SCALE).astype(BF16)
    u = raw_u * inv_rms
    u_ref[...] = u
    utail_ref[...] = u[tm - HALO:, :]


def _proj(x, g, w_in, tabs, tiles_per_seq, tail, cast=()):
    t = x.shape[0]
    tm = TOKEN_TILE
    steps = t // tm
    nseq = steps // tiles_per_seq
    row = lambda w: pl.BlockSpec((tm, w), lambda i: (i, 0))
    tab = pl.BlockSpec((tm, LANES), lambda i: (i % tiles_per_seq, 0))
    const = lambda a: pl.BlockSpec(a.shape, lambda i: (0,) * a.ndim, pipeline_mode=pl.Buffered(1))
    per_seq = lambda r, w: pl.BlockSpec((r, w), lambda i: (i // tiles_per_seq, 0))
    cast_specs, cast_shapes = _cast_specs(cast, steps)
    return pl.pallas_call(
        _proj_kernel,
        grid=(steps,),
        in_specs=[row(D_MODEL), const(g), const(w_in), tab, tab, tab] + cast_specs,
        out_specs=[row(POOL_WIDTH), row(ATTN_WIDTH), row(2 * KV_WIDTH),
                   pl.BlockSpec((KV_WIDTH, tm), lambda i: (0, i)),
                   per_seq(HALO, POOL_WIDTH), per_seq(tail, KV_WIDTH), per_seq(tail, KV_WIDTH)]
        + cast_specs,
        out_shape=[jax.ShapeDtypeStruct((t, POOL_WIDTH), F32),
                   jax.ShapeDtypeStruct((t, ATTN_WIDTH), BF16),
                   jax.ShapeDtypeStruct((t, 2 * KV_WIDTH), BF16),
                   jax.ShapeDtypeStruct((KV_WIDTH, t), BF16),
                   jax.ShapeDtypeStruct((nseq * HALO, POOL_WIDTH), F32),
                   jax.ShapeDtypeStruct((nseq * tail, KV_WIDTH), F32),
                   jax.ShapeDtypeStruct((nseq * tail, KV_WIDTH), F32)] + cast_shapes,
        compiler_params=_params("arbitrary"),
        name="proj",
    )(x, g, w_in, *tabs, *cast)


def _mix_prompt_kernel(sink_ref, u_ref, halo_ref, q_ref, kdc_ref, kdp_ref, vtc_ref, vtp_ref,
                       wpool_ref, pscale_ref, x_ref, wout_ref, gffn_ref, *refs, tiles_per_seq):
    ncast = (len(refs) - 8) // 2
    x1_ref, h2_ref = refs[ncast:ncast + 2]
    ext_ref, lvl_ref, kwin_ref, vtwin_ref, bias_ref, mixbuf_ref = refs[2 * ncast + 2:]
    _cast_rows(refs[:ncast], refs[ncast + 2:2 * ncast + 2])
    tq = u_ref.shape[0]
    step = pl.program_id(0)
    tile = jnp.minimum(step, pl.num_programs(0) - 2)
    i = lax.rem(tile, tiles_per_seq)
    first = i == 0
    mix_ref = mixbuf_ref.at[lax.rem(step, 2)]
    mix_prev_ref = mixbuf_ref.at[lax.rem(step + 1, 2)]

    @pl.when(step == 0)
    def _():
        mixbuf_ref[1] = jnp.zeros(mixbuf_ref.shape[1:], BF16)

    def out_projection(piece):
        r, c = divmod(piece, OUT_COL_PIECES)
        rows = slice(r * (tq // 2), (r + 1) * (tq // 2))
        cols = slice(c * (D_MODEL // OUT_COL_PIECES), (c + 1) * (D_MODEL // OUT_COL_PIECES))
        x1_ref[rows, cols] = x_ref[rows, cols] + jnp.dot(mix_prev_ref[rows, :], wout_ref[:, cols],
                                                         preferred_element_type=F32)
        if c == OUT_COL_PIECES - 1:
            h2_ref[rows, :] = _rms(x1_ref[rows, :], gffn_ref[...]).astype(BF16)

    top = POOL_TOP
    ext_ref[0:top - HALO, :] = jnp.zeros((top - HALO, POOL_WIDTH), F32)
    ext_ref[top - HALO:top, :] = halo_ref[...] * jnp.where(first, 0.0, 1.0)
    ext_ref[top:, :] = u_ref[...]
    lvl_ref[:, 0:SUBLANES, :] = jnp.zeros((2, SUBLANES, POOL_GROUP), F32)
    nlive = top + tq - SUBLANES
    pos1 = i * tq + lax.broadcasted_iota(jnp.int32, (tq, 1), 0) + 1

    def pool_group(gi):
        w = POOL_WINDOWS[gi]
        cols = slice(gi * POOL_GROUP, (gi + 1) * POOL_GROUP)
        src = ext_ref.at[:, cols]
        shift, slot = 1, 0
        while shift < w:
            dst = lvl_ref.at[slot]
            dst[SUBLANES:, :] = (src[SUBLANES:SUBLANES + nlive, :]
                                 + src[SUBLANES - shift:SUBLANES - shift + nlive, :])
            src, shift, slot = dst, 2 * shift, 1 - slot
        tok = ext_ref[top:, cols]
        inv_cnt = 1.0 / jnp.minimum(pos1, w).astype(F32)
        d = (src[top:, :] * inv_cnt - tok).astype(BF16)
        po = jnp.dot(d, wpool_ref[gi], preferred_element_type=F32) * pscale_ref[:, cols]
        mix_ref[:, cols] = po.astype(BF16)

    kwin_ref[0:BLOCK, :] = kdp_ref[...]
    kwin_ref[BLOCK:, :] = kdc_ref[...]
    vtwin_ref[:, 0:BLOCK] = vtp_ref[...]
    vtwin_ref[:, BLOCK:] = vtc_ref[...]
    kj = lax.broadcasted_iota(jnp.int32, (2 * BLOCK, BLOCK), 0)
    qi = lax.broadcasted_iota(jnp.int32, (2 * BLOCK, BLOCK), 1)
    band = (kj >= qi) & (kj <= qi + WINDOW)
    first_lo = jnp.where(first, BLOCK, 0)
    bias_ref[0] = jnp.where(band, 0.0, -jnp.inf)
    bias_ref[1] = jnp.where(band & (kj >= first_lo), 0.0, -jnp.inf)
    lane = lax.broadcasted_iota(jnp.int32, (1, LANES), 1)
    keep_half = [(lane // HEAD_DIM == hh).astype(BF16) for hh in range(HEADS_PER_COL)]

    def scores(n, g):
        rows = slice(n * BLOCK, (n + 1) * BLOCK)
        qs = jnp.concatenate(
            [q_ref[rows, (h // HEADS_PER_COL) * LANES:(h // HEADS_PER_COL + 1) * LANES]
             * keep_half[h % HEADS_PER_COL]
             for h in range(GQA_GROUP * g, GQA_GROUP * (g + 1))], axis=0)
        kd = kwin_ref[n * BLOCK:(n + 2) * BLOCK, g * LANES:(g + 1) * LANES]
        return lax.dot_general(kd, qs, (((1,), (1,)), ((), ())), preferred_element_type=F32)

    def attend(n, g, s_t):
        p_t = []
        for j in range(GQA_GROUP):
            s = s_t[:, j * BLOCK:(j + 1) * BLOCK] + bias_ref[1 if n == 0 else 0]
            m = jnp.max(s, axis=0, keepdims=True)
            p = jnp.exp2(s - m)
            den = jnp.sum(p, axis=0, keepdims=True) + jnp.exp2(sink_ref[GQA_GROUP * g + j] * LOG2E - m)
            p_t.append((p * (1.0 / den)).astype(BF16))
        vt = vtwin_ref[g * HEAD_DIM:(g + 1) * HEAD_DIM, n * BLOCK:(n + 2) * BLOCK]
        o_t = jnp.dot(vt, jnp.concatenate(p_t, axis=1), preferred_element_type=F32)
        for c in range(GQA_GROUP // HEADS_PER_COL):
            col_t = jnp.concatenate(
                [o_t[:, (HEADS_PER_COL * c + hh) * BLOCK:(HEADS_PER_COL * c + hh + 1) * BLOCK]
                 for hh in range(HEADS_PER_COL)], axis=0)
            col = POOL_WIDTH + (GQA_GROUP // HEADS_PER_COL * g + c) * LANES
            mix_ref[n * BLOCK:(n + 1) * BLOCK, col:col + LANES] = col_t.T.astype(BF16)

    work = [(n, g) for n in range(tq // BLOCK) for g in range(N_KV_HEADS)]
    pieces = list(range(2 * OUT_COL_PIECES))
    groups = list(range(N_POOL_GROUPS))
    s_next = scores(*work[0])
    for idx, (n, g) in enumerate(work):
        s_t = s_next
        if idx + 1 < len(work):
            s_next = scores(*work[idx + 1])
        if idx % 2 == 0:
            out_projection(pieces.pop(0))
        attend(n, g, s_t)
        if idx % 4 == 1:
            pool_group(groups.pop(0))
    assert not pieces and not groups


def _mix_prompt(sinks, u, q, kd, vt, wpool, pscale, x, w_out, g_ffn, batch, seq, cast=()):
    tq = TOKEN_TILE
    nt = seq // tq
    ntiles = batch * nt
    bpt = tq // BLOCK
    hpt = tq // HALO
    tile = lambda s: jnp.minimum(s, ntiles - 1)
    prev_block = lambda s: jnp.maximum(tile(s) * bpt - 1, 0)
    row = lambda w: pl.BlockSpec((tq, w), lambda s: (tile(s), 0))
    late_row = lambda w: pl.BlockSpec((tq, w), lambda s: (jnp.maximum(s - 1, 0), 0))
    halo = pl.BlockSpec((HALO, POOL_WIDTH), lambda s: (jnp.maximum(tile(s) * hpt - 1, 0), 0))
    const = lambda a: pl.BlockSpec(a.shape, lambda s: (0,) * a.ndim, pipeline_mode=pl.Buffered(1))
    cast_specs, cast_shapes = _cast_specs(cast, ntiles, tile)
    return pl.pallas_call(
        functools.partial(_mix_prompt_kernel, tiles_per_seq=nt),
        grid=(ntiles + 1,),
        in_specs=[pl.BlockSpec(memory_space=pltpu.SMEM), row(POOL_WIDTH), halo, row(ATTN_WIDTH),
                  row(2 * KV_WIDTH),
                  pl.BlockSpec((BLOCK, 2 * KV_WIDTH), lambda s: (prev_block(s), 0)),
                  pl.BlockSpec((KV_WIDTH, tq), lambda s: (0, tile(s))),
                  pl.BlockSpec((KV_WIDTH, BLOCK), lambda s: (0, prev_block(s))),
                  const(wpool), const(pscale), late_row(D_MODEL), const(w_out), const(g_ffn)]
        + cast_specs,
        out_specs=[late_row(D_MODEL), late_row(D_MODEL)] + cast_specs,
        out_shape=[jax.ShapeDtypeStruct((batch * seq, D_MODEL), F32),
                   jax.ShapeDtypeStruct((batch * seq, D_MODEL), BF16)] + cast_shapes,
        scratch_shapes=[pltpu.VMEM((POOL_TOP + tq, POOL_WIDTH), F32),
                        pltpu.VMEM((2, POOL_TOP + tq, POOL_GROUP), F32),
                        pltpu.VMEM((BLOCK + tq, 2 * KV_WIDTH), BF16),
                        pltpu.VMEM((KV_WIDTH, BLOCK + tq), BF16),
                        pltpu.VMEM((2, 2 * BLOCK, BLOCK), F32),
                        pltpu.VMEM((2, tq, MIX_WIDTH), BF16)],
        compiler_params=_params("arbitrary"),
        name="mix_prompt",
    )(sinks, u, u, q, kd, kd, vt, vt, wpool, pscale, x, w_out, g_ffn, *cast)


def _sink_softmax(s, mask, sink):
    s = jnp.where(mask, s, -jnp.inf)
    sink = sink * LOG2E
    m = jnp.maximum(jnp.max(s, axis=1, keepdims=True), sink)
    p = jnp.exp2(s - m)
    den = jnp.sum(p, axis=1, keepdims=True) + jnp.exp2(sink - m)
    return p / den


def _mix_sample_kernel(sink_ref, u_ref, q_ref, kn_ref, vn_ref, st_ref, ck_ref, cv_ref,
                       wpool_ref, pscale_ref, mix_ref, npool_ref, nk_ref, nv_ref,
                       kall_ref, vall_ref, *, dec_seq):
    sb = SEQ_BLOCK
    rows_per_seq = N_HEADS * dec_seq
    nrow = sb * rows_per_seq
    ntok = sb * dec_seq
    win = ck_ref.shape[1]

    @pl.when(pl.program_id(0) == 0)
    def _():
        kall_ref[...] = jnp.zeros_like(kall_ref)
        vall_ref[...] = jnp.zeros_like(vall_ref)

    ext = [st_ref[h] for h in range(POOL_HIST)]
    ext += [u_ref[t] for t in range(dec_seq)]
    for h in range(POOL_HIST):
        npool_ref[h] = ext[h + dec_seq]
    r_out = lax.broadcasted_iota(jnp.int32, (ntok, ntok), 0)
    r_in = lax.broadcasted_iota(jnp.int32, (ntok, ntok), 1)
    to_seq_major = ((r_in % sb) * dec_seq + r_in // sb == r_out).astype(BF16)
    for gi, w in enumerate(POOL_WINDOWS):
        cols = slice(gi * POOL_GROUP, (gi + 1) * POOL_GROUP)
        ds = []
        for t in range(dec_seq):
            tok = ext[POOL_HIST + t][:, cols]
            acc = tok
            for j in range(1, w):
                acc = acc + ext[POOL_HIST + t - j][:, cols]
            ds.append(acc * (1.0 / w) - tok)
        d = jnp.concatenate(ds, axis=0).astype(BF16)
        po = jnp.dot(d, wpool_ref[gi], preferred_element_type=F32) * pscale_ref[:, cols]
        po = jnp.dot(to_seq_major, po.astype(BF16), preferred_element_type=F32)
        mix_ref[:, cols] = po.astype(BF16)

    for b in range(sb):
        tok_rows = slice(b * dec_seq, (b + 1) * dec_seq)
        kall_ref[b, 0:win, :] = ck_ref[b].reshape(win, KV_WIDTH)
        kall_ref[b, win:win + dec_seq, :] = kn_ref[tok_rows, :]
        vall_ref[b, 0:win, :] = cv_ref[b].reshape(win, KV_WIDTH)
        vall_ref[b, win:win + dec_seq, :] = vn_ref[tok_rows, :]
        nk_ref[b] = kall_ref[b, dec_seq:dec_seq + win, :]
        nv_ref[b] = vall_ref[b, dec_seq:dec_seq + win, :]

    nhalf = 2
    hseq, hrow, htok = sb // nhalf, nrow // nhalf, ntok // nhalf
    r = lax.broadcasted_iota(jnp.int32, (hrow, htok), 0)
    c = lax.broadcasted_iota(jnp.int32, (hrow, htok), 1)
    pick = (c == (r // rows_per_seq) * dec_seq + r % dec_seq).astype(BF16)
    rt = lax.broadcasted_iota(jnp.int32, (htok, hrow), 0)
    ct = lax.broadcasted_iota(jnp.int32, (htok, hrow), 1)
    unpick = (rt == (ct // rows_per_seq) * dec_seq + ct % dec_seq).astype(BF16)
    rr = lax.broadcasted_iota(jnp.int32, (hrow, LANES), 0) % rows_per_seq
    row_g = rr // (GQA_GROUP * dec_seq)
    row_j = (rr // dec_seq) % GQA_GROUP
    lane_half = lax.broadcasted_iota(jnp.int32, (hrow, LANES), 1) // HEAD_DIM
    t_row = lax.broadcasted_iota(jnp.int32, (hrow, SAMPLE_KEYS), 0) % dec_seq
    kj = lax.broadcasted_iota(jnp.int32, (hrow, SAMPLE_KEYS), 1)
    mask = (kj >= t_row + (win - WINDOW)) & (kj <= t_row + win)

    def build_lhs(hf):
        qrep = jnp.dot(pick, q_ref[hf * htok:(hf + 1) * htok, :], preferred_element_type=F32)
        lhs_cols = [jnp.zeros((hrow, LANES), F32) for _ in range(KV_WIDTH // LANES)]
        for qc in range(ATTN_WIDTH // LANES):
            g = qc * HEADS_PER_COL // GQA_GROUP
            src = qrep[:, qc * LANES:(qc + 1) * LANES]
            src_swapped = _swap_halves(src)
            kv_col, kv_half = g // HEADS_PER_COL, g % HEADS_PER_COL
            for hh in range(HEADS_PER_COL):
                j = (qc * HEADS_PER_COL + hh) % GQA_GROUP
                here = (row_g == g) & (row_j == j) & (lane_half == kv_half)
                lhs_cols[kv_col] = lhs_cols[kv_col] + jnp.where(
                    here, src if hh == kv_half else src_swapped, 0.0)
        return jnp.concatenate(lhs_cols, axis=1).astype(BF16)

    def score(hf, lhs):
        return jnp.concatenate(
            [lax.dot_general(lhs[b * rows_per_seq:(b + 1) * rows_per_seq],
                             kall_ref[hf * hseq + b].astype(BF16),
                             (((1,), (1,)), ((), ())), preferred_element_type=F32)
             for b in range(hseq)], axis=0)

    def weigh(hf, s):
        return _sink_softmax(s, mask, sink_ref[hf * hrow:(hf + 1) * hrow, :]).astype(BF16)

    def gather_values(hf, p):
        return jnp.concatenate(
            [jnp.dot(p[b * rows_per_seq:(b + 1) * rows_per_seq], vall_ref[hf * hseq + b].astype(BF16),
                     preferred_element_type=F32) for b in range(hseq)], axis=0)

    def emit(hf, o):
        z_cols = []
        for qc in range(ATTN_WIDTH // LANES):
            g = qc * HEADS_PER_COL // GQA_GROUP
            kv_col, kv_half = g // HEADS_PER_COL, g % HEADS_PER_COL
            src = o[:, kv_col * LANES:(kv_col + 1) * LANES]
            src_swapped = _swap_halves(src)
            zc = jnp.zeros((hrow, LANES), F32)
            for hh in range(HEADS_PER_COL):
                j = (qc * HEADS_PER_COL + hh) % GQA_GROUP
                here = (row_g == g) & (row_j == j) & (lane_half == hh)
                zc = zc + jnp.where(here, src if hh == kv_half else src_swapped, 0.0)
            z_cols.append(zc)
        z = jnp.concatenate(z_cols, axis=1).astype(BF16)
        attn = jnp.dot(unpick, z, preferred_element_type=F32)
        mix_ref[hf * htok:(hf + 1) * htok, POOL_WIDTH:] = attn.astype(BF16)

    state = [build_lhs(hf) for hf in range(nhalf)]
    for stage in (score, weigh, gather_values, emit):
        state = [stage(hf, x) for hf, x in enumerate(state)]


def _mix_sample(sink_rows, u, q, k_new, v_new, state_t, cache_k, cache_v, wpool, pscale, dec_seq):
    nseq, win = cache_k.shape[:2]
    sb = SEQ_BLOCK
    ntok = sb * dec_seq
    row = lambda w: pl.BlockSpec((ntok, w), lambda i: (i, 0))
    slab = pl.BlockSpec((POOL_HIST, sb, POOL_WIDTH), lambda i: (0, i, 0))
    u_slab = pl.BlockSpec((dec_seq, sb, POOL_WIDTH), lambda i: (0, i, 0))
    cache_in = pl.BlockSpec((sb, win, N_KV_HEADS, HEAD_DIM), lambda i: (i, 0, 0, 0))
    cache = pl.BlockSpec((sb, win, KV_WIDTH), lambda i: (i, 0, 0))
    full = lambda a: pl.BlockSpec(a.shape, lambda i: (0,) * a.ndim)
    return pl.pallas_call(
        functools.partial(_mix_sample_kernel, dec_seq=dec_seq),
        grid=(nseq // sb,),
        in_specs=[full(sink_rows), u_slab, row(ATTN_WIDTH), row(KV_WIDTH), row(KV_WIDTH),
                  slab, cache_in, cache_in, full(wpool), full(pscale)],
        out_specs=[row(MIX_WIDTH), slab, cache, cache],
        out_shape=[jax.ShapeDtypeStruct((nseq * dec_seq, MIX_WIDTH), BF16),
                   jax.ShapeDtypeStruct((POOL_HIST, nseq, POOL_WIDTH), F32),
                   jax.ShapeDtypeStruct((nseq, win, KV_WIDTH), F32),
                   jax.ShapeDtypeStruct((nseq, win, KV_WIDTH), F32)],
        scratch_shapes=[pltpu.VMEM((sb, SAMPLE_KEYS, KV_WIDTH), F32),
                        pltpu.VMEM((sb, SAMPLE_KEYS, KV_WIDTH), F32)],
        compiler_params=_params("arbitrary"),
        name="mix_sample",
    )(sink_rows, u, q, k_new, v_new, state_t, cache_k, cache_v, wpool, pscale)


def _outproj_kernel(x_ref, mix_ref, w_ref, g_ref, x1_ref, h2_ref):
    x1 = x_ref[...] + jnp.dot(mix_ref[...], w_ref[...], preferred_element_type=F32)
    x1_ref[...] = x1
    h2_ref[...] = _rms(x1, g_ref[...]).astype(BF16)


def _outproj(x, mix, w_out, g):
    t = x.shape[0]
    tm = TOKEN_TILE
    row = lambda w: pl.BlockSpec((tm, w), lambda i: (i, 0))
    const = lambda a: pl.BlockSpec(a.shape, lambda i: (0,) * a.ndim, pipeline_mode=pl.Buffered(1))
    return pl.pallas_call(
        _outproj_kernel,
        grid=(t // tm,),
        in_specs=[row(D_MODEL), row(MIX_WIDTH), const(w_out), const(g)],
        out_specs=[row(D_MODEL), row(D_MODEL)],
        out_shape=[jax.ShapeDtypeStruct((t, D_MODEL), F32),
                   jax.ShapeDtypeStruct((t, D_MODEL), BF16)],
        compiler_params=_params("parallel"),
        name="outproj",
    )(x, mix, w_out, g)


def _ffn_kernel(x1_hbm, h2_ref, wg_ref, wu_ref, wd_ref, gf_ref, y_ref, x1_buf, sem, *, final_norm):
    i, j = pl.program_id(0), pl.program_id(1)
    tm = y_ref.shape[0]
    residual = pltpu.make_async_copy(x1_hbm.at[pl.ds(pl.multiple_of(i * tm, tm), tm)], x1_buf, sem)

    @pl.when(j == 0)
    def _():
        residual.start()
        y_ref[...] = jnp.zeros_like(y_ref)

    h = h2_ref[...]
    gate = jnp.dot(h, wg_ref[...], preferred_element_type=F32)
    up = jnp.dot(h, wu_ref[...], preferred_element_type=F32)
    act = (jax.nn.silu(gate) * up).astype(BF16)
    y_ref[...] += jnp.dot(act, wd_ref[...], preferred_element_type=F32)

    @pl.when(j == pl.num_programs(1) - 1)
    def _():
        residual.wait()
        y = x1_buf[...] + y_ref[...]
        y_ref[...] = _rms(y, gf_ref[...]) if final_norm else y


def _ffn(x1, h2, wg, wu, wd, gf, final_norm, tm):
    t = x1.shape[0]
    tf = FF_TILE
    row = pl.BlockSpec((tm, D_MODEL), lambda i, j: (i, 0))
    return pl.pallas_call(
        functools.partial(_ffn_kernel, final_norm=final_norm),
        grid=(t // tm, D_FF // tf),
        in_specs=[pl.BlockSpec(memory_space=pl.ANY), row,
                  pl.BlockSpec((D_MODEL, tf), lambda i, j: (0, j)),
                  pl.BlockSpec((D_MODEL, tf), lambda i, j: (0, j)),
                  pl.BlockSpec((tf, D_MODEL), lambda i, j: (j, 0)),
                  pl.BlockSpec(gf.shape, lambda i, j: (0, 0))],
        out_specs=row,
        out_shape=jax.ShapeDtypeStruct((t, D_MODEL), F32),
        scratch_shapes=[pltpu.VMEM((tm, D_MODEL), F32), pltpu.SemaphoreType.DMA(())],
        compiler_params=_params("arbitrary", "arbitrary"),
        name="ffn",
    )(x1, h2, wg, wu, wd, gf)


def _rope_tables(pos):
    inv = ROPE_THETA ** (-np.arange(0, ROPE_DIM, 2, dtype=np.float64) / ROPE_DIM)
    ang = np.asarray(pos, np.float64)[:, None] * inv[None, :]
    cos, sin = np.cos(ang), np.sin(ang)
    n = ang.shape[0]
    rest = HEAD_DIM - ROPE_DIM
    one, zero, zh = np.ones((n, rest)), np.zeros((n, rest)), np.zeros((n, ROPE_HALF))
    per_head = [np.concatenate(parts, axis=1) for parts in
                ([cos, cos, one], [-sin, zh, zero], [zh, sin, zero])]
    return [jnp.asarray(np.tile(t, (1, HEADS_PER_COL)), F32) for t in per_head]


def kernel(x_prompt, x_sample, state_pool, cache_k_win, cache_v_win, g_mix, w_in, w_pool,
           pool_scale, attn_sinks, w_out, g_ffn, w_gate, w_up, w_down, g_final):
    batch, seq, _ = x_prompt.shape
    nseq, dec_seq, _ = x_sample.shape
    depth = w_in.shape[0]
    win_s = cache_k_win.shape[2]
    ntok_s = nseq * dec_seq
    assert seq % TOKEN_TILE == 0 and ntok_s == TOKEN_TILE and nseq % SEQ_BLOCK == 0
    assert win_s == WINDOW and win_s + dec_seq <= SAMPLE_KEYS and seq >= WINDOW
    assert PAST_LEN >= max(POOL_HIST, WINDOW)

    xp = x_prompt.reshape(batch * seq, D_MODEL)
    xs = x_sample.reshape(ntok_s, D_MODEL)
    tabs_p = _rope_tables(np.arange(seq))
    tabs_s = _rope_tables(PAST_LEN + np.arange(ntok_s) % dec_seq)
    gf = g_final.reshape(1, D_MODEL)

    outs = [[] for _ in range(6)]
    for l in range(depth):
        w_in_l, wpool_l = w_in[l].astype(BF16), w_pool[l].astype(BF16)
        gm = g_mix[l].reshape(1, D_MODEL)
        gn = g_ffn[l].reshape(1, D_MODEL)
        pscale = pool_scale[l].reshape(1, POOL_WIDTH)
        sink_rows = jnp.tile(jnp.repeat(attn_sinks[l], dec_seq), SEQ_BLOCK)[:, None]

        u, q, kd, vt, u_tail, k_tail, v_tail, w_out_l, wg_l, wd_l = _proj(
            xp, gm, w_in_l, tabs_p, seq // TOKEN_TILE, WINDOW, cast=(w_out[l], w_gate[l], w_down[l]))
        x1, h2, wu_l = _mix_prompt(attn_sinks[l], u, q, kd, vt, wpool_l, pscale, xp, w_out_l, gn,
                                   batch, seq, cast=(w_up[l],))
        xp = _ffn(x1, h2, wg_l, wu_l, wd_l, gf, l == depth - 1, FFN_TOKEN_TILE)
        outs[0].append(u_tail.reshape(batch, HALO, POOL_WIDTH)[:, HALO - POOL_HIST:])
        outs[1].append(k_tail.reshape(batch, WINDOW, N_KV_HEADS, HEAD_DIM))
        outs[2].append(v_tail.reshape(batch, WINDOW, N_KV_HEADS, HEAD_DIM))

        u, q, _, _, _, k_new, v_new = _proj(xs, gm, w_in_l, tabs_s, 1, TOKEN_TILE)
        state_t = jnp.transpose(state_pool[l], (1, 0, 2))
        ck, cv = cache_k_win[l], cache_v_win[l]
        u_t = jnp.transpose(u.reshape(nseq, dec_seq, POOL_WIDTH), (1, 0, 2))
        mix, npool, nk, nv = _mix_sample(sink_rows, u_t, q, k_new, v_new, state_t, ck, cv, wpool_l,
                                         pscale, dec_seq)
        x1, h2 = _outproj(xs, mix, w_out_l, gn)
        xs = _ffn(x1, h2, wg_l, wu_l, wd_l, gf, l == depth - 1, min(FFN_TOKEN_TILE, ntok_s))
        outs[3].append(jnp.transpose(npool, (1, 0, 2)))
        outs[4].append(nk.reshape(nseq, win_s, N_KV_HEADS, HEAD_DIM))
        outs[5].append(nv.reshape(nseq, win_s, N_KV_HEADS, HEAD_DIM))

    y_prompt = xp.reshape(batch, seq, D_MODEL)
    y_sample = xs.reshape(nseq, dec_seq, D_MODEL)
    return (y_prompt, y_sample) + tuple(jnp.stack(o) for o in outs)
```

```python
import functools

import jax
import jax.numpy as jnp
import numpy as np
from jax import lax
from jax.experimental import pallas as pl
from jax.experimental.pallas import tpu as pltpu

F32 = jnp.float32
BF16 = jnp.bfloat16

D_MODEL = 2048
POOL_WIDTH = D_MODEL // 2
POOL_WINDOWS = (2, 4, 8, 16)
N_POOL_GROUPS = len(POOL_WINDOWS)
POOL_GROUP = POOL_WIDTH // N_POOL_GROUPS
POOL_HIST = max(POOL_WINDOWS) - 1
HEAD_DIM = 64
N_HEADS = (D_MODEL - POOL_WIDTH) // HEAD_DIM
N_KV_HEADS = 4
GQA_GROUP = N_HEADS // N_KV_HEADS
ATTN_WIDTH = N_HEADS * HEAD_DIM
KV_WIDTH = N_KV_HEADS * HEAD_DIM
IN_WIDTH = POOL_WIDTH + ATTN_WIDTH + 2 * KV_WIDTH
MIX_WIDTH = POOL_WIDTH + ATTN_WIDTH
WINDOW = 128
BLOCK = 128
ROPE_DIM = HEAD_DIM // 4
ROPE_HALF = ROPE_DIM // 2
ROPE_THETA = 500000.0
D_FF = ((8 * D_MODEL // 3 + 255) // 256) * 256
EPS = 1e-5
PAST_LEN = 16384
LOG2E = 1.4426950408889634
Q_SCALE = HEAD_DIM ** -0.5 * LOG2E

LANES = 128
SUBLANES = 8
HEADS_PER_COL = LANES // HEAD_DIM
HALO = 16
POOL_TOP = 2 * HALO
TOKEN_TILE = 512
FFN_TOKEN_TILE = 1024
OUT_COL_PIECES = 4
FF_TILE = 512
SEQ_BLOCK = 8
VMEM_LIMIT = 60 * 1024 * 1024


def _params(*semantics):
    return pltpu.CompilerParams(dimension_semantics=semantics, vmem_limit_bytes=VMEM_LIMIT)


def _rms(x, g):
    ms = jnp.mean(x * x, axis=-1, keepdims=True)
    return x * lax.rsqrt(ms + EPS) * g


def _swap_halves(z):
    return pltpu.roll(z, HEAD_DIM, 1)


def _cast_rows(in_refs, out_refs):
    for src, dst in zip(in_refs, out_refs):
        dst[...] = src[...].astype(BF16)


def _cast_specs(weights, steps, step_of=lambda i: i):
    for w in weights:
        assert w.shape[0] % (steps * 2 * SUBLANES) == 0
    specs = [pl.BlockSpec((w.shape[0] // steps, w.shape[1]), lambda *idx: (step_of(*idx), 0))
             for w in weights]
    return specs, [jax.ShapeDtypeStruct(w.shape, BF16) for w in weights]


def _proj_kernel(x_ref, g_ref, w_ref, c_ref, s1_ref, s2_ref, *refs):
    ncast = (len(refs) - 7) // 2
    u_ref, q_ref, kd_ref, vt_ref, utail_ref, ktail_ref, vtail_ref = refs[ncast:ncast + 7]
    _cast_rows(refs[:ncast], refs[ncast + 7:])
    tm = x_ref.shape[0]
    tail = ktail_ref.shape[0]
    x = x_ref[...]
    inv_rms = lax.rsqrt(jnp.mean(x * x, axis=-1, keepdims=True) + EPS)
    xg = (x * g_ref[...]).astype(BF16)
    k0 = POOL_WIDTH + ATTN_WIDTH
    project = lambda c0, c1: jnp.dot(xg, w_ref[:, c0:c1], preferred_element_type=F32)
    raw_kv, raw_q, raw_u = project(k0, IN_WIDTH), project(POOL_WIDTH, k0), project(0, POOL_WIDTH)
    c, s1, s2 = c_ref[...], s1_ref[...], s2_ref[...]

    def rope(z):
        return (z * c + pltpu.roll(z, LANES - ROPE_HALF, 1) * s1
                + pltpu.roll(z, ROPE_HALF, 1) * s2)

    low_half = lax.broadcasted_iota(jnp.int32, (1, LANES), 1) < HEAD_DIM
    for a in range(KV_WIDTH // LANES):
        z = rope(raw_kv[:, a * LANES:(a + 1) * LANES] * inv_rms)
        ktail_ref[:, a * LANES:(a + 1) * LANES] = z[tm - tail:, :]
        zr = _swap_halves(z)
        kd_ref[:, (2 * a) * LANES:(2 * a + 1) * LANES] = jnp.where(low_half, z, zr).astype(BF16)
        kd_ref[:, (2 * a + 1) * LANES:(2 * a + 2) * LANES] = jnp.where(low_half, zr, z).astype(BF16)
    v = raw_kv[:, KV_WIDTH:] * inv_rms
    vtail_ref[...] = v[tm - tail:, :]
    vt_ref[...] = v.T.astype(BF16)
    for col in range(ATTN_WIDTH // LANES):
        z = raw_q[:, col * LANES:(col + 1) * LANES] * inv_rms
        q_ref[:, col * LANES:(col + 1) * LANES] = (rope(z) * Q_SCALE).astype(BF16)
    u = raw_u * inv_rms
    u_ref[...] = u
    utail_ref[...] = u[tm - HALO:, :]


def _proj(x, g, w_in, tabs, tiles_per_seq, tail, cast=()):
    t = x.shape[0]
    tm = TOKEN_TILE
    steps = t // tm
    nseq = steps // tiles_per_seq
    row = lambda w: pl.BlockSpec((tm, w), lambda i: (i, 0))
    tab = pl.BlockSpec((tm, LANES), lambda i: (i % tiles_per_seq, 0))
    const = lambda a: pl.BlockSpec(a.shape, lambda i: (0,) * a.ndim, pipeline_mode=pl.Buffered(1))
    per_seq = lambda r, w: pl.BlockSpec((r, w), lambda i: (i // tiles_per_seq, 0))
    cast_specs, cast_shapes = _cast_specs(cast, steps)
    return pl.pallas_call(
        _proj_kernel,
        grid=(steps,),
        in_specs=[row(D_MODEL), const(g), const(w_in), tab, tab, tab] + cast_specs,
        out_specs=[row(POOL_WIDTH), row(ATTN_WIDTH), row(2 * KV_WIDTH),
                   pl.BlockSpec((KV_WIDTH, tm), lambda i: (0, i)),
                   per_seq(HALO, POOL_WIDTH), per_seq(tail, KV_WIDTH), per_seq(tail, KV_WIDTH)]
        + cast_specs,
        out_shape=[jax.ShapeDtypeStruct((t, POOL_WIDTH), F32),
                   jax.ShapeDtypeStruct((t, ATTN_WIDTH), BF16),
                   jax.ShapeDtypeStruct((t, 2 * KV_WIDTH), BF16),
                   jax.ShapeDtypeStruct((KV_WIDTH, t), BF16),
                   jax.ShapeDtypeStruct((nseq * HALO, POOL_WIDTH), F32),
                   jax.ShapeDtypeStruct((nseq * tail, KV_WIDTH), F32),
                   jax.ShapeDtypeStruct((nseq * tail, KV_WIDTH), F32)] + cast_shapes,
        compiler_params=_params("arbitrary"),
        name="proj",
    )(x, g, w_in, *tabs, *cast)


def _mix_prompt_kernel(sink_ref, u_ref, halo_ref, q_ref, kdc_ref, kdp_ref, vtc_ref, vtp_ref,
                       wpool_ref, pscale_ref, x_ref, wout_ref, gffn_ref, *refs, tiles_per_seq):
    ncast = (len(refs) - 8) // 2
    x1_ref, h2_ref = refs[ncast:ncast + 2]
    ext_ref, lvl_ref, kwin_ref, vtwin_ref, bias_ref, mixbuf_ref = refs[2 * ncast + 2:]
    _cast_rows(refs[:ncast], refs[ncast + 2:2 * ncast + 2])
    tq = u_ref.shape[0]
    step = pl.program_id(0)
    tile = jnp.minimum(step, pl.num_programs(0) - 2)
    i = lax.rem(tile, tiles_per_seq)
    first = i == 0
    mix_ref = mixbuf_ref.at[lax.rem(step, 2)]
    mix_prev_ref = mixbuf_ref.at[lax.rem(step + 1, 2)]

    @pl.when(step == 0)
    def _():
        mixbuf_ref[1] = jnp.zeros(mixbuf_ref.shape[1:], BF16)

    def out_projection(piece):
        r, c = divmod(piece, OUT_COL_PIECES)
        rows = slice(r * (tq // 2), (r + 1) * (tq // 2))
        cols = slice(c * (D_MODEL // OUT_COL_PIECES), (c + 1) * (D_MODEL // OUT_COL_PIECES))
        x1_ref[rows, cols] = x_ref[rows, cols] + jnp.dot(mix_prev_ref[rows, :], wout_ref[:, cols],
                                                         preferred_element_type=F32)
        if c == OUT_COL_PIECES - 1:
            h2_ref[rows, :] = _rms(x1_ref[rows, :], gffn_ref[...]).astype(BF16)

    top = POOL_TOP
    ext_ref[0:top - HALO, :] = jnp.zeros((top - HALO, POOL_WIDTH), F32)
    ext_ref[top - HALO:top, :] = halo_ref[...] * jnp.where(first, 0.0, 1.0)
    ext_ref[top:, :] = u_ref[...]
    lvl_ref[:, 0:SUBLANES, :] = jnp.zeros((2, SUBLANES, POOL_GROUP), F32)
    nlive = top + tq - SUBLANES
    pos1 = i * tq + lax.broadcasted_iota(jnp.int32, (tq, 1), 0) + 1

    def pool_group(gi):
        w = POOL_WINDOWS[gi]
        cols = slice(gi * POOL_GROUP, (gi + 1) * POOL_GROUP)
        src = ext_ref.at[:, cols]
        shift, slot = 1, 0
        while shift < w:
            dst = lvl_ref.at[slot]
            dst[SUBLANES:, :] = (src[SUBLANES:SUBLANES + nlive, :]
                                 + src[SUBLANES - shift:SUBLANES - shift + nlive, :])
            src, shift, slot = dst, 2 * shift, 1 - slot
        tok = ext_ref[top:, cols]
        inv_cnt = 1.0 / jnp.minimum(pos1, w).astype(F32)
        d = (src[top:, :] * inv_cnt - tok).astype(BF16)
        po = jnp.dot(d, wpool_ref[gi], preferred_element_type=F32) * pscale_ref[:, cols]
        mix_ref[:, cols] = po.astype(BF16)

    kwin_ref[0:BLOCK, :] = kdp_ref[...]
    kwin_ref[BLOCK:, :] = kdc_ref[...]
    vtwin_ref[:, 0:BLOCK] = vtp_ref[...]
    vtwin_ref[:, BLOCK:] = vtc_ref[...]
    kj = lax.broadcasted_iota(jnp.int32, (2 * BLOCK, BLOCK), 0)
    qi = lax.broadcasted_iota(jnp.int32, (2 * BLOCK, BLOCK), 1)
    band = (kj >= qi) & (kj <= qi + WINDOW)
    first_lo = jnp.where(first, BLOCK, 0)
    bias_ref[0] = jnp.where(band, 0.0, -jnp.inf)
    bias_ref[1] = jnp.where(band & (kj >= first_lo), 0.0, -jnp.inf)
    lane = lax.broadcasted_iota(jnp.int32, (1, LANES), 1)
    keep_half = [(lane // HEAD_DIM == hh).astype(BF16) for hh in range(HEADS_PER_COL)]

    def scores(n, g):
        rows = slice(n * BLOCK, (n + 1) * BLOCK)
        qs = jnp.concatenate(
            [q_ref[rows, (h // HEADS_PER_COL) * LANES:(h // HEADS_PER_COL + 1) * LANES]
             * keep_half[h % HEADS_PER_COL]
             for h in range(GQA_GROUP * g, GQA_GROUP * (g + 1))], axis=0)
        kd = kwin_ref[n * BLOCK:(n + 2) * BLOCK, g * LANES:(g + 1) * LANES]
        return lax.dot_general(kd, qs, (((1,), (1,)), ((), ())), preferred_element_type=F32)

    def attend(n, g, s_t):
        p_t = []
        for j in range(GQA_GROUP):
            s = s_t[:, j * BLOCK:(j + 1) * BLOCK] + bias_ref[1 if n == 0 else 0]
            m = jnp.max(s, axis=0, keepdims=True)
            p = jnp.exp2(s - m)
            den = jnp.sum(p, axis=0, keepdims=True) + jnp.exp2(sink_ref[GQA_GROUP * g + j] * LOG2E - m)
            p_t.append((p * (1.0 / den)).astype(BF16))
        vt = vtwin_ref[g * HEAD_DIM:(g + 1) * HEAD_DIM, n * BLOCK:(n + 2) * BLOCK]
        o_t = jnp.dot(vt, jnp.concatenate(p_t, axis=1), preferred_element_type=F32)
        for c in range(GQA_GROUP // HEADS_PER_COL):
            col_t = jnp.concatenate(
                [o_t[:, (HEADS_PER_COL * c + hh) * BLOCK:(HEADS_PER_COL * c + hh + 1) * BLOCK]
                 for hh in range(HEADS_PER_COL)], axis=0)
            col = POOL_WIDTH + (GQA_GROUP // HEADS_PER_COL * g + c) * LANES
            mix_ref[n * BLOCK:(n + 1) * BLOCK, col:col + LANES] = col_t.T.astype(BF16)

    work = [(n, g) for n in range(tq // BLOCK) for g in range(N_KV_HEADS)]
    pieces = list(range(2 * OUT_COL_PIECES))
    groups = list(range(N_POOL_GROUPS))
    s_next = scores(*work[0])
    for idx, (n, g) in enumerate(work):
        s_t = s_next
        if idx + 1 < len(work):
            s_next = scores(*work[idx + 1])
        if idx % 2 == 0:
            out_projection(pieces.pop(0))
        attend(n, g, s_t)
        if idx % 4 == 1:
            pool_group(groups.pop(0))
    assert not pieces and not groups


def _mix_prompt(sinks, u, q, kd, vt, wpool, pscale, x, w_out, g_ffn, batch, seq, cast=()):
    tq = TOKEN_TILE
    nt = seq // tq
    ntiles = batch * nt
    bpt = tq // BLOCK
    hpt = tq // HALO
    tile = lambda s: jnp.minimum(s, ntiles - 1)
    prev_block = lambda s: jnp.maximum(tile(s) * bpt - 1, 0)
    row = lambda w: pl.BlockSpec((tq, w), lambda s: (tile(s), 0))
    late_row = lambda w: pl.BlockSpec((tq, w), lambda s: (jnp.maximum(s - 1, 0), 0))
    halo = pl.BlockSpec((HALO, POOL_WIDTH), lambda s: (jnp.maximum(tile(s) * hpt - 1, 0), 0))
    const = lambda a: pl.BlockSpec(a.shape, lambda s: (0,) * a.ndim, pipeline_mode=pl.Buffered(1))
    cast_specs, cast_shapes = _cast_specs(cast, ntiles, tile)
    return pl.pallas_call(
        functools.partial(_mix_prompt_kernel, tiles_per_seq=nt),
        grid=(ntiles + 1,),
        in_specs=[pl.BlockSpec(memory_space=pltpu.SMEM), row(POOL_WIDTH), halo, row(ATTN_WIDTH),
                  row(2 * KV_WIDTH),
                  pl.BlockSpec((BLOCK, 2 * KV_WIDTH), lambda s: (prev_block(s), 0)),
                  pl.BlockSpec((KV_WIDTH, tq), lambda s: (0, tile(s))),
                  pl.BlockSpec((KV_WIDTH, BLOCK), lambda s: (0, prev_block(s))),
                  const(wpool), const(pscale), late_row(D_MODEL), const(w_out), const(g_ffn)]
        + cast_specs,
        out_specs=[late_row(D_MODEL), late_row(D_MODEL)] + cast_specs,
        out_shape=[jax.ShapeDtypeStruct((batch * seq, D_MODEL), F32),
                   jax.ShapeDtypeStruct((batch * seq, D_MODEL), BF16)] + cast_shapes,
        scratch_shapes=[pltpu.VMEM((POOL_TOP + tq, POOL_WIDTH), F32),
                        pltpu.VMEM((2, POOL_TOP + tq, POOL_GROUP), F32),
                        pltpu.VMEM((BLOCK + tq, 2 * KV_WIDTH), BF16),
                        pltpu.VMEM((KV_WIDTH, BLOCK + tq), BF16),
                        pltpu.VMEM((2, 2 * BLOCK, BLOCK), F32),
                        pltpu.VMEM((2, tq, MIX_WIDTH), BF16)],
        compiler_params=_params("arbitrary"),
        name="mix_prompt",
    )(sinks, u, u, q, kd, kd, vt, vt, wpool, pscale, x, w_out, g_ffn, *cast)


def _mix_sample_t_kernel(sink_ref, u_ref, q_ref, kn_ref, vn_ref, st_ref, ckt_ref, cvt_ref,
                         wpool_ref, pscale_ref, mix_ref, npool_ref, nkt_ref, nvt_ref, *, dec_seq):
    sb = SEQ_BLOCK
    rows_per_seq = N_HEADS * dec_seq
    nrow = sb * rows_per_seq
    ntok = sb * dec_seq
    win = ckt_ref.shape[3]
    pair_rows = HEADS_PER_COL * GQA_GROUP * dec_seq
    assert win == LANES and ntok <= LANES

    ext = [st_ref[h] for h in range(POOL_HIST)]
    ext += [u_ref[t] for t in range(dec_seq)]
    for h in range(POOL_HIST):
        npool_ref[h] = ext[h + dec_seq]
    r_out = lax.broadcasted_iota(jnp.int32, (ntok, ntok), 0)
    r_in = lax.broadcasted_iota(jnp.int32, (ntok, ntok), 1)
    to_seq_major = ((r_in % sb) * dec_seq + r_in // sb == r_out).astype(BF16)
    for gi, w in enumerate(POOL_WINDOWS):
        cols = slice(gi * POOL_GROUP, (gi + 1) * POOL_GROUP)
        ds = []
        for t in range(dec_seq):
            tok = ext[POOL_HIST + t][:, cols]
            acc = tok
            for j in range(1, w):
                acc = acc + ext[POOL_HIST + t - j][:, cols]
            ds.append(acc * (1.0 / w) - tok)
        d = jnp.concatenate(ds, axis=0).astype(BF16)
        po = jnp.dot(d, wpool_ref[gi], preferred_element_type=F32) * pscale_ref[:, cols]
        po = jnp.dot(to_seq_major, po.astype(BF16), preferred_element_type=F32)
        mix_ref[:, cols] = po.astype(BF16)

    pad_rows = jnp.zeros((LANES - ntok, KV_WIDTH), F32)
    k_new = jnp.concatenate([kn_ref[...], pad_rows], axis=0)
    v_new = jnp.concatenate([vn_ref[...], pad_rows], axis=0)
    k_new_t, v_new_t = k_new.T, v_new.T
    newest = lax.broadcasted_iota(jnp.int32, (HEAD_DIM, LANES), 1) >= win - dec_seq
    for b in range(sb):
        for g in range(N_KV_HEADS):
            dims = slice(g * HEAD_DIM, (g + 1) * HEAD_DIM)
            for old_ref, new_t, out_ref in ((ckt_ref, k_new_t, nkt_ref), (cvt_ref, v_new_t, nvt_ref)):
                kept = pltpu.roll(old_ref[b, g], win - dec_seq, 1)
                fresh = pltpu.roll(new_t[dims, :], win - dec_seq - b * dec_seq, 1)
                out_ref[b, g] = jnp.where(newest, fresh, kept)

    nhalf = 2
    hseq, hrow, htok = sb // nhalf, nrow // nhalf, ntok // nhalf
    r = lax.broadcasted_iota(jnp.int32, (hrow, htok), 0)
    c = lax.broadcasted_iota(jnp.int32, (hrow, htok), 1)
    pick = (c == (r // rows_per_seq) * dec_seq + r % dec_seq).astype(BF16)
    rt = lax.broadcasted_iota(jnp.int32, (htok, hrow), 0)
    ct = lax.broadcasted_iota(jnp.int32, (htok, hrow), 1)
    unpick = (rt == (ct // rows_per_seq) * dec_seq + ct % dec_seq).astype(BF16)
    row = lax.broadcasted_iota(jnp.int32, (hrow, LANES), 0)
    lane = lax.broadcasted_iota(jnp.int32, (hrow, LANES), 1)
    row_seq = row // rows_per_seq
    row_g = (row % rows_per_seq) // (GQA_GROUP * dec_seq)
    row_j = (row // dec_seq) % GQA_GROUP
    row_t = row % dec_seq
    lane_half = lane // HEAD_DIM
    own_half = lane_half == row_g % HEADS_PER_COL
    first_pair = row_g < HEADS_PER_COL
    cache_ok = lane >= row_t + (win - WINDOW)
    k_new_b, v_new_b = k_new.astype(BF16), v_new.astype(BF16)

    def new_ok(hf):
        return (lane // dec_seq == row_seq + hf * hseq) & (lane % dec_seq <= row_t)

    def build_lhs(hf):
        qrep = jnp.dot(pick, q_ref[hf * htok:(hf + 1) * htok, :], preferred_element_type=F32)
        lhs = jnp.zeros((hrow, LANES), F32)
        for qc in range(ATTN_WIDTH // LANES):
            g = qc * HEADS_PER_COL // GQA_GROUP
            src = qrep[:, qc * LANES:(qc + 1) * LANES]
            src_swapped = _swap_halves(src)
            for hh in range(HEADS_PER_COL):
                j = (qc * HEADS_PER_COL + hh) % GQA_GROUP
                here = (row_g == g) & (row_j == j) & own_half
                lhs = lhs + jnp.where(here, src if hh == g % HEADS_PER_COL else src_swapped, 0.0)
        return lhs.astype(BF16)

    def pair_tile(ref, hf, b, a):
        return ref[hf * hseq + b, HEADS_PER_COL * a:HEADS_PER_COL * (a + 1)].reshape(LANES, win).astype(BF16)

    def score(hf, lhs):
        s_cache = jnp.concatenate(
            [jnp.dot(lhs[b * rows_per_seq + a * pair_rows:b * rows_per_seq + (a + 1) * pair_rows],
                     pair_tile(ckt_ref, hf, b, a), preferred_element_type=F32)
             for b in range(hseq) for a in range(N_KV_HEADS // HEADS_PER_COL)], axis=0)
        s_new = [lax.dot_general(lhs, k_new_b[:, a * LANES:(a + 1) * LANES], (((1,), (1,)), ((), ())),
                                 preferred_element_type=F32) for a in range(N_KV_HEADS // HEADS_PER_COL)]
        return s_cache, jnp.where(first_pair, s_new[0], s_new[1])

    def weigh(hf, scores):
        s_cache = jnp.where(cache_ok, scores[0], -jnp.inf)
        s_new = jnp.where(new_ok(hf), scores[1], -jnp.inf)
        sink = sink_ref[hf * hrow:(hf + 1) * hrow, :] * LOG2E
        m = jnp.maximum(jnp.maximum(jnp.max(s_cache, axis=1, keepdims=True),
                                    jnp.max(s_new, axis=1, keepdims=True)), sink)
        p_cache, p_new = jnp.exp2(s_cache - m), jnp.exp2(s_new - m)
        den = (jnp.sum(p_cache, axis=1, keepdims=True) + jnp.sum(p_new, axis=1, keepdims=True)
               + jnp.exp2(sink - m))
        return (p_cache / den).astype(BF16), (p_new / den).astype(BF16)

    def gather_values(hf, probs):
        p_cache, p_new = probs
        o_cache = jnp.concatenate(
            [lax.dot_general(p_cache[b * rows_per_seq + a * pair_rows:b * rows_per_seq + (a + 1) * pair_rows],
                             pair_tile(cvt_ref, hf, b, a), (((1,), (1,)), ((), ())),
                             preferred_element_type=F32)
             for b in range(hseq) for a in range(N_KV_HEADS // HEADS_PER_COL)], axis=0)
        o_new = [jnp.dot(p_new, v_new_b[:, a * LANES:(a + 1) * LANES], preferred_element_type=F32)
                 for a in range(N_KV_HEADS // HEADS_PER_COL)]
        return o_cache + jnp.where(first_pair, o_new[0], o_new[1])

    def emit(hf, o):
        o_swapped = _swap_halves(o)
        z_cols = []
        for qc in range(ATTN_WIDTH // LANES):
            g = qc * HEADS_PER_COL // GQA_GROUP
            zc = jnp.zeros((hrow, LANES), F32)
            for hh in range(HEADS_PER_COL):
                j = (qc * HEADS_PER_COL + hh) % GQA_GROUP
                here = (row_g == g) & (row_j == j) & (lane_half == hh)
                zc = zc + jnp.where(here, o if hh == g % HEADS_PER_COL else o_swapped, 0.0)
            z_cols.append(zc)
        z = jnp.concatenate(z_cols, axis=1).astype(BF16)
        attn = jnp.dot(unpick, z, preferred_element_type=F32)
        mix_ref[hf * htok:(hf + 1) * htok, POOL_WIDTH:] = attn.astype(BF16)

    state = [build_lhs(hf) for hf in range(nhalf)]
    for stage in (score, weigh, gather_values, emit):
        state = [stage(hf, x) for hf, x in enumerate(state)]


def _mix_sample_t(sink_rows, u, q, k_new, v_new, state_t, cache_kt, cache_vt, wpool, pscale, dec_seq):
    nseq = cache_kt.shape[0]
    sb = SEQ_BLOCK
    ntok = sb * dec_seq
    row = lambda w: pl.BlockSpec((ntok, w), lambda i: (i, 0))
    slab = pl.BlockSpec((POOL_HIST, sb, POOL_WIDTH), lambda i: (0, i, 0))
    u_slab = pl.BlockSpec((dec_seq, sb, POOL_WIDTH), lambda i: (0, i, 0))
    cache = pl.BlockSpec((sb,) + cache_kt.shape[1:], lambda i: (i, 0, 0, 0))
    full = lambda a: pl.BlockSpec(a.shape, lambda i: (0,) * a.ndim)
    return pl.pallas_call(
        functools.partial(_mix_sample_t_kernel, dec_seq=dec_seq),
        grid=(nseq // sb,),
        in_specs=[full(sink_rows), u_slab, row(ATTN_WIDTH), row(KV_WIDTH), row(KV_WIDTH),
                  slab, cache, cache, full(wpool), full(pscale)],
        out_specs=[row(MIX_WIDTH), slab, cache, cache],
        out_shape=[jax.ShapeDtypeStruct((nseq * dec_seq, MIX_WIDTH), BF16),
                   jax.ShapeDtypeStruct((POOL_HIST, nseq, POOL_WIDTH), F32),
                   jax.ShapeDtypeStruct(cache_kt.shape, F32),
                   jax.ShapeDtypeStruct(cache_vt.shape, F32)],
        compiler_params=_params("parallel"),
        name="mix_sample",
    )(sink_rows, u, q, k_new, v_new, state_t, cache_kt, cache_vt, wpool, pscale)


def _outproj_kernel(x_ref, mix_ref, w_ref, g_ref, x1_ref, h2_ref):
    x1 = x_ref[...] + jnp.dot(mix_ref[...], w_ref[...], preferred_element_type=F32)
    x1_ref[...] = x1
    h2_ref[...] = _rms(x1, g_ref[...]).astype(BF16)


def _outproj(x, mix, w_out, g):
    t = x.shape[0]
    tm = TOKEN_TILE
    row = lambda w: pl.BlockSpec((tm, w), lambda i: (i, 0))
    const = lambda a: pl.BlockSpec(a.shape, lambda i: (0,) * a.ndim, pipeline_mode=pl.Buffered(1))
    return pl.pallas_call(
        _outproj_kernel,
        grid=(t // tm,),
        in_specs=[row(D_MODEL), row(MIX_WIDTH), const(w_out), const(g)],
        out_specs=[row(D_MODEL), row(D_MODEL)],
        out_shape=[jax.ShapeDtypeStruct((t, D_MODEL), F32),
                   jax.ShapeDtypeStruct((t, D_MODEL), BF16)],
        compiler_params=_params("parallel"),
        name="outproj",
    )(x, mix, w_out, g)


def _ffn_kernel(x1_hbm, h2_ref, wg_ref, wu_ref, wd_ref, gf_ref, y_ref, x1_buf, sem, *, final_norm):
    i, j = pl.program_id(0), pl.program_id(1)
    tm = y_ref.shape[0]
    residual = pltpu.make_async_copy(x1_hbm.at[pl.ds(pl.multiple_of(i * tm, tm), tm)], x1_buf, sem)

    @pl.when(j == 0)
    def _():
        residual.start()
        y_ref[...] = jnp.zeros_like(y_ref)

    h = h2_ref[...]
    gate = jnp.dot(h, wg_ref[...], preferred_element_type=F32)
    up = jnp.dot(h, wu_ref[...], preferred_element_type=F32)
    act = (jax.nn.silu(gate) * up).astype(BF16)
    y_ref[...] += jnp.dot(act, wd_ref[...], preferred_element_type=F32)

    @pl.when(j == pl.num_programs(1) - 1)
    def _():
        residual.wait()
        y = x1_buf[...] + y_ref[...]
        y_ref[...] = _rms(y, gf_ref[...]) if final_norm else y


def _ffn(x1, h2, wg, wu, wd, gf, final_norm, tm):
    t = x1.shape[0]
    tf = FF_TILE
    row = pl.BlockSpec((tm, D_MODEL), lambda i, j: (i, 0))
    return pl.pallas_call(
        functools.partial(_ffn_kernel, final_norm=final_norm),
        grid=(t // tm, D_FF // tf),
        in_specs=[pl.BlockSpec(memory_space=pl.ANY), row,
                  pl.BlockSpec((D_MODEL, tf), lambda i, j: (0, j)),
                  pl.BlockSpec((D_MODEL, tf), lambda i, j: (0, j)),
                  pl.BlockSpec((tf, D_MODEL), lambda i, j: (j, 0)),
                  pl.BlockSpec(gf.shape, lambda i, j: (0, 0))],
        out_specs=row,
        out_shape=jax.ShapeDtypeStruct((t, D_MODEL), F32),
        scratch_shapes=[pltpu.VMEM((tm, D_MODEL), F32), pltpu.SemaphoreType.DMA(())],
        compiler_params=_params("arbitrary", "arbitrary"),
        name="ffn",
    )(x1, h2, wg, wu, wd, gf)


def _rope_tables(pos):
    inv = ROPE_THETA ** (-np.arange(0, ROPE_DIM, 2, dtype=np.float64) / ROPE_DIM)
    ang = np.asarray(pos, np.float64)[:, None] * inv[None, :]
    cos, sin = np.cos(ang), np.sin(ang)
    n = ang.shape[0]
    rest = HEAD_DIM - ROPE_DIM
    one, zero, zh = np.ones((n, rest)), np.zeros((n, rest)), np.zeros((n, ROPE_HALF))
    per_head = [np.concatenate(parts, axis=1) for parts in
                ([cos, cos, one], [-sin, zh, zero], [zh, sin, zero])]
    return [jnp.asarray(np.tile(t, (1, HEADS_PER_COL)), F32) for t in per_head]


def kernel(x_prompt, x_sample, state_pool, cache_k_win, cache_v_win, g_mix, w_in, w_pool,
           pool_scale, attn_sinks, w_out, g_ffn, w_gate, w_up, w_down, g_final):
    batch, seq, _ = x_prompt.shape
    nseq, dec_seq, _ = x_sample.shape
    depth = w_in.shape[0]
    win_s = cache_k_win.shape[2]
    ntok_s = nseq * dec_seq
    assert seq % TOKEN_TILE == 0 and ntok_s == TOKEN_TILE and nseq % SEQ_BLOCK == 0
    assert win_s == WINDOW and seq >= WINDOW
    assert PAST_LEN >= max(POOL_HIST, WINDOW)

    xp = x_prompt.reshape(batch * seq, D_MODEL)
    xs = x_sample.reshape(ntok_s, D_MODEL)
    tabs_p = _rope_tables(np.arange(seq))
    tabs_s = _rope_tables(PAST_LEN + np.arange(ntok_s) % dec_seq)
    gf = g_final.reshape(1, D_MODEL)

    outs = [[] for _ in range(6)]
    for l in range(depth):
        w_in_l, wpool_l = w_in[l].astype(BF16), w_pool[l].astype(BF16)
        gm = g_mix[l].reshape(1, D_MODEL)
        gn = g_ffn[l].reshape(1, D_MODEL)
        pscale = pool_scale[l].reshape(1, POOL_WIDTH)
        sink_rows = jnp.tile(jnp.repeat(attn_sinks[l], dec_seq), SEQ_BLOCK)[:, None]

        u, q, kd, vt, u_tail, k_tail, v_tail, w_out_l, wg_l, wd_l = _proj(
            xp, gm, w_in_l, tabs_p, seq // TOKEN_TILE, WINDOW, cast=(w_out[l], w_gate[l], w_down[l]))
        x1, h2, wu_l = _mix_prompt(attn_sinks[l], u, q, kd, vt, wpool_l, pscale, xp, w_out_l, gn,
                                   batch, seq, cast=(w_up[l],))
        xp = _ffn(x1, h2, wg_l, wu_l, wd_l, gf, l == depth - 1, FFN_TOKEN_TILE)
        outs[0].append(u_tail.reshape(batch, HALO, POOL_WIDTH)[:, HALO - POOL_HIST:])
        outs[1].append(k_tail.reshape(batch, WINDOW, N_KV_HEADS, HEAD_DIM))
        outs[2].append(v_tail.reshape(batch, WINDOW, N_KV_HEADS, HEAD_DIM))

        u, q, _, _, _, k_new, v_new = _proj(xs, gm, w_in_l, tabs_s, 1, TOKEN_TILE)
        state_t = jnp.transpose(state_pool[l], (1, 0, 2))
        ckt = jnp.transpose(cache_k_win[l], (0, 2, 3, 1))
        cvt = jnp.transpose(cache_v_win[l], (0, 2, 3, 1))
        u_t = jnp.transpose(u.reshape(nseq, dec_seq, POOL_WIDTH), (1, 0, 2))
        mix, npool, nkt, nvt = _mix_sample_t(sink_rows, u_t, q, k_new, v_new, state_t, ckt, cvt,
                                             wpool_l, pscale, dec_seq)
        x1, h2 = _outproj(xs, mix, w_out_l, gn)
        xs = _ffn(x1, h2, wg_l, wu_l, wd_l, gf, l == depth - 1, min(FFN_TOKEN_TILE, ntok_s))
        outs[3].append(jnp.transpose(npool, (1, 0, 2)))
        outs[4].append(jnp.transpose(nkt, (0, 3, 1, 2)))
        outs[5].append(jnp.transpose(nvt, (0, 3, 1, 2)))

    y_prompt = xp.reshape(batch, seq, D_MODEL)
    y_sample = xs.reshape(nseq, dec_seq, D_MODEL)
    return (y_prompt, y_sample) + tuple(jnp.stack(o) for o in outs)
```

```python
import functools

import jax
import jax.numpy as jnp
import numpy as np
from jax import lax
from jax.experimental import pallas as pl
from jax.experimental.pallas import tpu as pltpu

F32 = jnp.float32
BF16 = jnp.bfloat16

D_MODEL = 2048
POOL_WIDTH = D_MODEL // 2
POOL_WINDOWS = (2, 4, 8, 16)
N_POOL_GROUPS = len(POOL_WINDOWS)
POOL_GROUP = POOL_WIDTH // N_POOL_GROUPS
POOL_HIST = max(POOL_WINDOWS) - 1
HEAD_DIM = 64
N_HEADS = (D_MODEL - POOL_WIDTH) // HEAD_DIM
N_KV_HEADS = 4
GQA_GROUP = N_HEADS // N_KV_HEADS
ATTN_WIDTH = N_HEADS * HEAD_DIM
KV_WIDTH = N_KV_HEADS * HEAD_DIM
IN_WIDTH = POOL_WIDTH + ATTN_WIDTH + 2 * KV_WIDTH
MIX_WIDTH = POOL_WIDTH + ATTN_WIDTH
WINDOW = 128
BLOCK = 128
ROPE_DIM = HEAD_DIM // 4
ROPE_HALF = ROPE_DIM // 2
ROPE_THETA = 500000.0
D_FF = ((8 * D_MODEL // 3 + 255) // 256) * 256
EPS = 1e-5
PAST_LEN = 16384
LOG2E = 1.4426950408889634
Q_SCALE = HEAD_DIM ** -0.5 * LOG2E

LANES = 128
SUBLANES = 8
HEADS_PER_COL = LANES // HEAD_DIM
HALO = 16
POOL_TOP = 2 * HALO
TOKEN_TILE = 512
FFN_TOKEN_TILE = 1024
FFN_FINAL_BLOCKS = 4
OUT_COL_PIECES = 4
FF_TILE = 512
SEQ_BLOCK = 8
VMEM_LIMIT = 60 * 1024 * 1024


def _params(*semantics):
    return pltpu.CompilerParams(dimension_semantics=semantics, vmem_limit_bytes=VMEM_LIMIT)


def _rms(x, g):
    ms = jnp.mean(x * x, axis=-1, keepdims=True)
    return x * lax.rsqrt(ms + EPS) * g


def _swap_halves(z):
    return pltpu.roll(z, HEAD_DIM, 1)


def _cast_rows(in_refs, out_refs):
    for src, dst in zip(in_refs, out_refs):
        dst[...] = src[...].astype(BF16)


def _cast_specs(weights, steps, step_of=lambda i: i):
    for w in weights:
        assert w.shape[0] % (steps * 2 * SUBLANES) == 0
    specs = [pl.BlockSpec((w.shape[0] // steps, w.shape[1]), lambda *idx: (step_of(*idx), 0))
             for w in weights]
    return specs, [jax.ShapeDtypeStruct(w.shape, BF16) for w in weights]


def _proj_kernel(x_ref, g_ref, w_ref, c_ref, s1_ref, s2_ref, *refs):
    ncast = (len(refs) - 7) // 2
    u_ref, q_ref, kd_ref, vt_ref, utail_ref, ktail_ref, vtail_ref = refs[ncast:ncast + 7]
    _cast_rows(refs[:ncast], refs[ncast + 7:])
    tm = x_ref.shape[0]
    tail = ktail_ref.shape[0]
    x = x_ref[...]
    inv_rms = lax.rsqrt(jnp.mean(x * x, axis=-1, keepdims=True) + EPS)
    xg = (x * g_ref[...]).astype(BF16)
    k0 = POOL_WIDTH + ATTN_WIDTH
    project = lambda c0, c1: jnp.dot(xg, w_ref[:, c0:c1], preferred_element_type=F32)
    raw_kv, raw_q, raw_u = project(k0, IN_WIDTH), project(POOL_WIDTH, k0), project(0, POOL_WIDTH)
    c, s1, s2 = c_ref[...], s1_ref[...], s2_ref[...]

    def rope(z):
        return (z * c + pltpu.roll(z, LANES - ROPE_HALF, 1) * s1
                + pltpu.roll(z, ROPE_HALF, 1) * s2)

    low_half = lax.broadcasted_iota(jnp.int32, (1, LANES), 1) < HEAD_DIM
    for a in range(KV_WIDTH // LANES):
        z = rope(raw_kv[:, a * LANES:(a + 1) * LANES] * inv_rms)
        ktail_ref[:, a * LANES:(a + 1) * LANES] = z[tm - tail:, :]
        zr = _swap_halves(z)
        kd_ref[:, (2 * a) * LANES:(2 * a + 1) * LANES] = jnp.where(low_half, z, zr).astype(BF16)
        kd_ref[:, (2 * a + 1) * LANES:(2 * a + 2) * LANES] = jnp.where(low_half, zr, z).astype(BF16)
    v = raw_kv[:, KV_WIDTH:] * inv_rms
    vtail_ref[...] = v[tm - tail:, :]
    vt_ref[...] = v.T.astype(BF16)
    for col in range(ATTN_WIDTH // LANES):
        z = raw_q[:, col * LANES:(col + 1) * LANES] * inv_rms
        q_ref[:, col * LANES:(col + 1) * LANES] = (rope(z) * Q_SCALE).astype(BF16)
    u = raw_u * inv_rms
    u_ref[...] = u
    utail_ref[...] = u[tm - HALO:, :]


def _proj(x, g, w_in, tabs, tiles_per_seq, tail, cast=()):
    t = x.shape[0]
    tm = TOKEN_TILE
    steps = t // tm
    nseq = steps // tiles_per_seq
    row = lambda w: pl.BlockSpec((tm, w), lambda i: (i, 0))
    tab = pl.BlockSpec((tm, LANES), lambda i: (i % tiles_per_seq, 0))
    const = lambda a: pl.BlockSpec(a.shape, lambda i: (0,) * a.ndim, pipeline_mode=pl.Buffered(1))
    per_seq = lambda r, w: pl.BlockSpec((r, w), lambda i: (i // tiles_per_seq, 0))
    cast_specs, cast_shapes = _cast_specs(cast, steps)
    return pl.pallas_call(
        _proj_kernel,
        grid=(steps,),
        in_specs=[row(D_MODEL), const(g), const(w_in), tab, tab, tab] + cast_specs,
        out_specs=[row(POOL_WIDTH), row(ATTN_WIDTH), row(2 * KV_WIDTH),
                   pl.BlockSpec((KV_WIDTH, tm), lambda i: (0, i)),
                   per_seq(HALO, POOL_WIDTH), per_seq(tail, KV_WIDTH), per_seq(tail, KV_WIDTH)]
        + cast_specs,
        out_shape=[jax.ShapeDtypeStruct((t, POOL_WIDTH), F32),
                   jax.ShapeDtypeStruct((t, ATTN_WIDTH), BF16),
                   jax.ShapeDtypeStruct((t, 2 * KV_WIDTH), BF16),
                   jax.ShapeDtypeStruct((KV_WIDTH, t), BF16),
                   jax.ShapeDtypeStruct((nseq * HALO, POOL_WIDTH), F32),
                   jax.ShapeDtypeStruct((nseq * tail, KV_WIDTH), F32),
                   jax.ShapeDtypeStruct((nseq * tail, KV_WIDTH), F32)] + cast_shapes,
        compiler_params=_params("arbitrary"),
        name="proj",
    )(x, g, w_in, *tabs, *cast)


def _mix_prompt_kernel(sink_ref, u_ref, halo_ref, q_ref, kdc_ref, kdp_ref, vtc_ref, vtp_ref,
                       wpool_ref, pscale_ref, x_ref, wout_ref, gffn_ref, *refs, tiles_per_seq):
    ncast = (len(refs) - 8) // 2
    x1_ref, h2_ref = refs[ncast:ncast + 2]
    ext_ref, lvl_ref, kwin_ref, vtwin_ref, bias_ref, mixbuf_ref = refs[2 * ncast + 2:]
    _cast_rows(refs[:ncast], refs[ncast + 2:2 * ncast + 2])
    tq = u_ref.shape[0]
    step = pl.program_id(0)
    tile = jnp.minimum(step, pl.num_programs(0) - 2)
    i = lax.rem(tile, tiles_per_seq)
    first = i == 0
    mix_ref = mixbuf_ref.at[lax.rem(step, 2)]
    mix_prev_ref = mixbuf_ref.at[lax.rem(step + 1, 2)]

    @pl.when(step == 0)
    def _():
        mixbuf_ref[1] = jnp.zeros(mixbuf_ref.shape[1:], BF16)

    def out_projection(piece):
        r, c = divmod(piece, OUT_COL_PIECES)
        rows = slice(r * (tq // 2), (r + 1) * (tq // 2))
        cols = slice(c * (D_MODEL // OUT_COL_PIECES), (c + 1) * (D_MODEL // OUT_COL_PIECES))
        x1_ref[rows, cols] = x_ref[rows, cols] + jnp.dot(mix_prev_ref[rows, :], wout_ref[:, cols],
                                                         preferred_element_type=F32)
        if c == OUT_COL_PIECES - 1:
            h2_ref[rows, :] = _rms(x1_ref[rows, :], gffn_ref[...]).astype(BF16)

    top = POOL_TOP
    ext_ref[0:top - HALO, :] = jnp.zeros((top - HALO, POOL_WIDTH), F32)
    ext_ref[top - HALO:top, :] = halo_ref[...] * jnp.where(first, 0.0, 1.0)
    ext_ref[top:, :] = u_ref[...]
    lvl_ref[:, 0:SUBLANES, :] = jnp.zeros((2, SUBLANES, POOL_GROUP), F32)
    nlive = top + tq - SUBLANES
    pos1 = i * tq + lax.broadcasted_iota(jnp.int32, (tq, 1), 0) + 1

    def pool_group(gi):
        w = POOL_WINDOWS[gi]
        cols = slice(gi * POOL_GROUP, (gi + 1) * POOL_GROUP)
        src = ext_ref.at[:, cols]
        shift, slot = 1, 0
        while shift < w:
            dst = lvl_ref.at[slot]
            dst[SUBLANES:, :] = (src[SUBLANES:SUBLANES + nlive, :]
                                 + src[SUBLANES - shift:SUBLANES - shift + nlive, :])
            src, shift, slot = dst, 2 * shift, 1 - slot
        tok = ext_ref[top:, cols]
        inv_cnt = 1.0 / jnp.minimum(pos1, w).astype(F32)
        d = (src[top:, :] * inv_cnt - tok).astype(BF16)
        po = jnp.dot(d, wpool_ref[gi], preferred_element_type=F32) * pscale_ref[:, cols]
        mix_ref[:, cols] = po.astype(BF16)

    kwin_ref[0:BLOCK, :] = kdp_ref[...]
    kwin_ref[BLOCK:, :] = kdc_ref[...]
    vtwin_ref[:, 0:BLOCK] = vtp_ref[...]
    vtwin_ref[:, BLOCK:] = vtc_ref[...]
    kj = lax.broadcasted_iota(jnp.int32, (2 * BLOCK, BLOCK), 0)
    qi = lax.broadcasted_iota(jnp.int32, (2 * BLOCK, BLOCK), 1)
    band = (kj >= qi) & (kj <= qi + WINDOW)
    first_lo = jnp.where(first, BLOCK, 0)
    bias_ref[0] = jnp.where(band, 0.0, -jnp.inf)
    bias_ref[1] = jnp.where(band & (kj >= first_lo), 0.0, -jnp.inf)
    lane = lax.broadcasted_iota(jnp.int32, (1, LANES), 1)
    keep_half = [(lane // HEAD_DIM == hh).astype(BF16) for hh in range(HEADS_PER_COL)]

    def scores(n, g):
        rows = slice(n * BLOCK, (n + 1) * BLOCK)
        qs = jnp.concatenate(
            [q_ref[rows, (h // HEADS_PER_COL) * LANES:(h // HEADS_PER_COL + 1) * LANES]
             * keep_half[h % HEADS_PER_COL]
             for h in range(GQA_GROUP * g, GQA_GROUP * (g + 1))], axis=0)
        kd = kwin_ref[n * BLOCK:(n + 2) * BLOCK, g * LANES:(g + 1) * LANES]
        return lax.dot_general(kd, qs, (((1,), (1,)), ((), ())), preferred_element_type=F32)

    def attend(n, g, s_t):
        p_t = []
        for j in range(GQA_GROUP):
            s = s_t[:, j * BLOCK:(j + 1) * BLOCK] + bias_ref[1 if n == 0 else 0]
            m = jnp.max(s, axis=0, keepdims=True)
            p = jnp.exp2(s - m)
            den = jnp.sum(p, axis=0, keepdims=True) + jnp.exp2(sink_ref[GQA_GROUP * g + j] * LOG2E - m)
            p_t.append((p * (1.0 / den)).astype(BF16))
        vt = vtwin_ref[g * HEAD_DIM:(g + 1) * HEAD_DIM, n * BLOCK:(n + 2) * BLOCK]
        o_t = jnp.dot(vt, jnp.concatenate(p_t, axis=1), preferred_element_type=F32)
        for c in range(GQA_GROUP // HEADS_PER_COL):
            col_t = jnp.concatenate(
                [o_t[:, (HEADS_PER_COL * c + hh) * BLOCK:(HEADS_PER_COL * c + hh + 1) * BLOCK]
                 for hh in range(HEADS_PER_COL)], axis=0)
            col = POOL_WIDTH + (GQA_GROUP // HEADS_PER_COL * g + c) * LANES
            mix_ref[n * BLOCK:(n + 1) * BLOCK, col:col + LANES] = col_t.T.astype(BF16)

    work = [(n, g) for n in range(tq // BLOCK) for g in range(N_KV_HEADS)]
    pieces = list(range(2 * OUT_COL_PIECES))
    groups = list(range(N_POOL_GROUPS))
    s_next = scores(*work[0])
    for idx, (n, g) in enumerate(work):
        s_t = s_next
        if idx + 1 < len(work):
            s_next = scores(*work[idx + 1])
        if idx % 2 == 0:
            out_projection(pieces.pop(0))
        attend(n, g, s_t)
        if idx % 4 == 1:
            pool_group(groups.pop(0))
    assert not pieces and not groups


def _mix_prompt(sinks, u, q, kd, vt, wpool, pscale, x, w_out, g_ffn, batch, seq, cast=()):
    tq = TOKEN_TILE
    nt = seq // tq
    ntiles = batch * nt
    bpt = tq // BLOCK
    hpt = tq // HALO
    tile = lambda s: jnp.minimum(s, ntiles - 1)
    prev_block = lambda s: jnp.maximum(tile(s) * bpt - 1, 0)
    row = lambda w: pl.BlockSpec((tq, w), lambda s: (tile(s), 0))
    late_row = lambda w: pl.BlockSpec((tq, w), lambda s: (jnp.maximum(s - 1, 0), 0))
    halo = pl.BlockSpec((HALO, POOL_WIDTH), lambda s: (jnp.maximum(tile(s) * hpt - 1, 0), 0))
    const = lambda a: pl.BlockSpec(a.shape, lambda s: (0,) * a.ndim, pipeline_mode=pl.Buffered(1))
    cast_specs, cast_shapes = _cast_specs(cast, ntiles, tile)
    return pl.pallas_call(
        functools.partial(_mix_prompt_kernel, tiles_per_seq=nt),
        grid=(ntiles + 1,),
        in_specs=[pl.BlockSpec(memory_space=pltpu.SMEM), row(POOL_WIDTH), halo, row(ATTN_WIDTH),
                  row(2 * KV_WIDTH),
                  pl.BlockSpec((BLOCK, 2 * KV_WIDTH), lambda s: (prev_block(s), 0)),
                  pl.BlockSpec((KV_WIDTH, tq), lambda s: (0, tile(s))),
                  pl.BlockSpec((KV_WIDTH, BLOCK), lambda s: (0, prev_block(s))),
                  const(wpool), const(pscale), late_row(D_MODEL), const(w_out), const(g_ffn)]
        + cast_specs,
        out_specs=[late_row(D_MODEL), late_row(D_MODEL)] + cast_specs,
        out_shape=[jax.ShapeDtypeStruct((batch * seq, D_MODEL), F32),
                   jax.ShapeDtypeStruct((batch * seq, D_MODEL), BF16)] + cast_shapes,
        scratch_shapes=[pltpu.VMEM((POOL_TOP + tq, POOL_WIDTH), F32),
                        pltpu.VMEM((2, POOL_TOP + tq, POOL_GROUP), F32),
                        pltpu.VMEM((BLOCK + tq, 2 * KV_WIDTH), BF16),
                        pltpu.VMEM((KV_WIDTH, BLOCK + tq), BF16),
                        pltpu.VMEM((2, 2 * BLOCK, BLOCK), F32),
                        pltpu.VMEM((2, tq, MIX_WIDTH), BF16)],
        compiler_params=_params("arbitrary"),
        name="mix_prompt",
    )(sinks, u, u, q, kd, kd, vt, vt, wpool, pscale, x, w_out, g_ffn, *cast)


def _mix_sample_t_kernel(sink_ref, u_ref, q_ref, kn_ref, vn_ref, st_ref, ckt_ref, cvt_ref,
                         wpool_ref, pscale_ref, mix_ref, npool_ref, nkt_ref, nvt_ref, *, dec_seq):
    sb = SEQ_BLOCK
    rows_per_seq = N_HEADS * dec_seq
    nrow = sb * rows_per_seq
    ntok = sb * dec_seq
    win = ckt_ref.shape[3]
    pair_rows = HEADS_PER_COL * GQA_GROUP * dec_seq
    assert win == LANES and ntok <= LANES

    ext = [st_ref[h] for h in range(POOL_HIST)]
    ext += [u_ref[t] for t in range(dec_seq)]
    for h in range(POOL_HIST):
        npool_ref[h] = ext[h + dec_seq]
    r_out = lax.broadcasted_iota(jnp.int32, (ntok, ntok), 0)
    r_in = lax.broadcasted_iota(jnp.int32, (ntok, ntok), 1)
    to_seq_major = ((r_in % sb) * dec_seq + r_in // sb == r_out).astype(BF16)
    for gi, w in enumerate(POOL_WINDOWS):
        cols = slice(gi * POOL_GROUP, (gi + 1) * POOL_GROUP)
        ds = []
        for t in range(dec_seq):
            tok = ext[POOL_HIST + t][:, cols]
            acc = tok
            for j in range(1, w):
                acc = acc + ext[POOL_HIST + t - j][:, cols]
            ds.append(acc * (1.0 / w) - tok)
        d = jnp.concatenate(ds, axis=0).astype(BF16)
        po = jnp.dot(d, wpool_ref[gi], preferred_element_type=F32) * pscale_ref[:, cols]
        po = jnp.dot(to_seq_major, po.astype(BF16), preferred_element_type=F32)
        mix_ref[:, cols] = po.astype(BF16)

    pad_rows = jnp.zeros((LANES - ntok, KV_WIDTH), F32)
    k_new = jnp.concatenate([kn_ref[...], pad_rows], axis=0)
    v_new = jnp.concatenate([vn_ref[...], pad_rows], axis=0)
    k_new_t, v_new_t = k_new.T, v_new.T
    newest = lax.broadcasted_iota(jnp.int32, (HEAD_DIM, LANES), 1) >= win - dec_seq
    for b in range(sb):
        for g in range(N_KV_HEADS):
            dims = slice(g * HEAD_DIM, (g + 1) * HEAD_DIM)
            for old_ref, new_t, out_ref in ((ckt_ref, k_new_t, nkt_ref), (cvt_ref, v_new_t, nvt_ref)):
                kept = pltpu.roll(old_ref[b, g], win - dec_seq, 1)
                fresh = pltpu.roll(new_t[dims, :], win - dec_seq - b * dec_seq, 1)
                out_ref[b, g] = jnp.where(newest, fresh, kept)

    nhalf = 2
    hseq, hrow, htok = sb // nhalf, nrow // nhalf, ntok // nhalf
    r = lax.broadcasted_iota(jnp.int32, (hrow, htok), 0)
    c = lax.broadcasted_iota(jnp.int32, (hrow, htok), 1)
    pick = (c == (r // rows_per_seq) * dec_seq + r % dec_seq).astype(BF16)
    rt = lax.broadcasted_iota(jnp.int32, (htok, hrow), 0)
    ct = lax.broadcasted_iota(jnp.int32, (htok, hrow), 1)
    unpick = (rt == (ct // rows_per_seq) * dec_seq + ct % dec_seq).astype(BF16)
    row = lax.broadcasted_iota(jnp.int32, (hrow, LANES), 0)
    lane = lax.broadcasted_iota(jnp.int32, (hrow, LANES), 1)
    row_seq = row // rows_per_seq
    row_g = (row % rows_per_seq) // (GQA_GROUP * dec_seq)
    row_j = (row // dec_seq) % GQA_GROUP
    row_t = row % dec_seq
    lane_half = lane // HEAD_DIM
    own_half = lane_half == row_g % HEADS_PER_COL
    first_pair = row_g < HEADS_PER_COL
    cache_ok = lane >= row_t + (win - WINDOW)
    k_new_b, v_new_b = k_new.astype(BF16), v_new.astype(BF16)

    def new_ok(hf):
        return (lane // dec_seq == row_seq + hf * hseq) & (lane % dec_seq <= row_t)

    def build_lhs(hf):
        qrep = jnp.dot(pick, q_ref[hf * htok:(hf + 1) * htok, :], preferred_element_type=F32)
        lhs = jnp.zeros((hrow, LANES), F32)
        for qc in range(ATTN_WIDTH // LANES):
            g = qc * HEADS_PER_COL // GQA_GROUP
            src = qrep[:, qc * LANES:(qc + 1) * LANES]
            src_swapped = _swap_halves(src)
            for hh in range(HEADS_PER_COL):
                j = (qc * HEADS_PER_COL + hh) % GQA_GROUP
                here = (row_g == g) & (row_j == j) & own_half
                lhs = lhs + jnp.where(here, src if hh == g % HEADS_PER_COL else src_swapped, 0.0)
        return lhs.astype(BF16)

    def pair_tile(ref, hf, b, a):
        return ref[hf * hseq + b, HEADS_PER_COL * a:HEADS_PER_COL * (a + 1)].reshape(LANES, win).astype(BF16)

    def score(hf, lhs):
        s_cache = jnp.concatenate(
            [jnp.dot(lhs[b * rows_per_seq + a * pair_rows:b * rows_per_seq + (a + 1) * pair_rows],
                     pair_tile(ckt_ref, hf, b, a), preferred_element_type=F32)
             for b in range(hseq) for a in range(N_KV_HEADS // HEADS_PER_COL)], axis=0)
        s_new = [lax.dot_general(lhs, k_new_b[:, a * LANES:(a + 1) * LANES], (((1,), (1,)), ((), ())),
                                 preferred_element_type=F32) for a in range(N_KV_HEADS // HEADS_PER_COL)]
        return s_cache, jnp.where(first_pair, s_new[0], s_new[1])

    def weigh(hf, scores):
        s_cache = jnp.where(cache_ok, scores[0], -jnp.inf)
        s_new = jnp.where(new_ok(hf), scores[1], -jnp.inf)
        sink = sink_ref[hf * hrow:(hf + 1) * hrow, :] * LOG2E
        m = jnp.maximum(jnp.maximum(jnp.max(s_cache, axis=1, keepdims=True),
                                    jnp.max(s_new, axis=1, keepdims=True)), sink)
        p_cache, p_new = jnp.exp2(s_cache - m), jnp.exp2(s_new - m)
        den = (jnp.sum(p_cache, axis=1, keepdims=True) + jnp.sum(p_new, axis=1, keepdims=True)
               + jnp.exp2(sink - m))
        return (p_cache / den).astype(BF16), (p_new / den).astype(BF16)

    def gather_values(hf, probs):
        p_cache, p_new = probs
        o_cache = jnp.concatenate(
            [lax.dot_general(p_cache[b * rows_per_seq + a * pair_rows:b * rows_per_seq + (a + 1) * pair_rows],
                             pair_tile(cvt_ref, hf, b, a), (((1,), (1,)), ((), ())),
                             preferred_element_type=F32)
             for b in range(hseq) for a in range(N_KV_HEADS // HEADS_PER_COL)], axis=0)
        o_new = [jnp.dot(p_new, v_new_b[:, a * LANES:(a + 1) * LANES], preferred_element_type=F32)
                 for a in range(N_KV_HEADS // HEADS_PER_COL)]
        return o_cache + jnp.where(first_pair, o_new[0], o_new[1])

    def emit(hf, o):
        o_swapped = _swap_halves(o)
        z_cols = []
        for qc in range(ATTN_WIDTH // LANES):
            g = qc * HEADS_PER_COL // GQA_GROUP
            zc = jnp.zeros((hrow, LANES), F32)
            for hh in range(HEADS_PER_COL):
                j = (qc * HEADS_PER_COL + hh) % GQA_GROUP
                here = (row_g == g) & (row_j == j) & (lane_half == hh)
                zc = zc + jnp.where(here, o if hh == g % HEADS_PER_COL else o_swapped, 0.0)
            z_cols.append(zc)
        z = jnp.concatenate(z_cols, axis=1).astype(BF16)
        attn = jnp.dot(unpick, z, preferred_element_type=F32)
        mix_ref[hf * htok:(hf + 1) * htok, POOL_WIDTH:] = attn.astype(BF16)

    state = [build_lhs(hf) for hf in range(nhalf)]
    for stage in (score, weigh, gather_values, emit):
        state = [stage(hf, x) for hf, x in enumerate(state)]


def _mix_sample_t(sink_rows, u, q, k_new, v_new, state_t, cache_kt, cache_vt, wpool, pscale, dec_seq):
    nseq = cache_kt.shape[0]
    sb = SEQ_BLOCK
    ntok = sb * dec_seq
    row = lambda w: pl.BlockSpec((ntok, w), lambda i: (i, 0))
    slab = pl.BlockSpec((POOL_HIST, sb, POOL_WIDTH), lambda i: (0, i, 0))
    u_slab = pl.BlockSpec((dec_seq, sb, POOL_WIDTH), lambda i: (0, i, 0))
    cache = pl.BlockSpec((sb,) + cache_kt.shape[1:], lambda i: (i, 0, 0, 0))
    full = lambda a: pl.BlockSpec(a.shape, lambda i: (0,) * a.ndim)
    return pl.pallas_call(
        functools.partial(_mix_sample_t_kernel, dec_seq=dec_seq),
        grid=(nseq // sb,),
        in_specs=[full(sink_rows), u_slab, row(ATTN_WIDTH), row(KV_WIDTH), row(KV_WIDTH),
                  slab, cache, cache, full(wpool), full(pscale)],
        out_specs=[row(MIX_WIDTH), slab, cache, cache],
        out_shape=[jax.ShapeDtypeStruct((nseq * dec_seq, MIX_WIDTH), BF16),
                   jax.ShapeDtypeStruct((POOL_HIST, nseq, POOL_WIDTH), F32),
                   jax.ShapeDtypeStruct(cache_kt.shape, F32),
                   jax.ShapeDtypeStruct(cache_vt.shape, F32)],
        compiler_params=_params("parallel"),
        name="mix_sample",
    )(sink_rows, u, q, k_new, v_new, state_t, cache_kt, cache_vt, wpool, pscale)


def _outproj_kernel(x_ref, mix_ref, w_ref, g_ref, x1_ref, h2_ref):
    x1 = x_ref[...] + jnp.dot(mix_ref[...], w_ref[...], preferred_element_type=F32)
    x1_ref[...] = x1
    h2_ref[...] = _rms(x1, g_ref[...]).astype(BF16)


def _outproj(x, mix, w_out, g):
    t = x.shape[0]
    tm = TOKEN_TILE
    row = lambda w: pl.BlockSpec((tm, w), lambda i: (i, 0))
    const = lambda a: pl.BlockSpec(a.shape, lambda i: (0,) * a.ndim, pipeline_mode=pl.Buffered(1))
    return pl.pallas_call(
        _outproj_kernel,
        grid=(t // tm,),
        in_specs=[row(D_MODEL), row(MIX_WIDTH), const(w_out), const(g)],
        out_specs=[row(D_MODEL), row(D_MODEL)],
        out_shape=[jax.ShapeDtypeStruct((t, D_MODEL), F32),
                   jax.ShapeDtypeStruct((t, D_MODEL), BF16)],
        compiler_params=_params("parallel"),
        name="outproj",
    )(x, mix, w_out, g)


def _ffn_kernel(x1_hbm, h2_ref, wg_ref, wu_ref, wd_ref, gf_ref, y_ref, x1_buf, sem, *, final_norm):
    i, j = pl.program_id(0), pl.program_id(1)
    tm = y_ref.shape[0]
    residual = pltpu.make_async_copy(x1_hbm.at[pl.ds(pl.multiple_of(i * tm, tm), tm)], x1_buf, sem)

    last = pl.num_programs(1) - 1

    def activation():
        h = h2_ref[...]
        gate = jnp.dot(h, wg_ref[...], preferred_element_type=F32)
        up = jnp.dot(h, wu_ref[...], preferred_element_type=F32)
        return (jax.nn.silu(gate) * up).astype(BF16)

    @pl.when(j == 0)
    def _():
        residual.start()
        y_ref[...] = jnp.dot(activation(), wd_ref[...], preferred_element_type=F32)

    @pl.when((j > 0) & (j < last))
    def _():
        y_ref[...] += jnp.dot(activation(), wd_ref[...], preferred_element_type=F32)

    @pl.when(j == last)
    def _():
        residual.wait()
        act = activation()
        for r in range(FFN_FINAL_BLOCKS):
            rows = slice(r * tm // FFN_FINAL_BLOCKS, (r + 1) * tm // FFN_FINAL_BLOCKS)
            y = (x1_buf[rows, :] + y_ref[rows, :]
                 + jnp.dot(act[rows, :], wd_ref[...], preferred_element_type=F32))
            y_ref[rows, :] = _rms(y, gf_ref[...]) if final_norm else y


def _ffn(x1, h2, wg, wu, wd, gf, final_norm, tm):
    t = x1.shape[0]
    tf = FF_TILE
    row = pl.BlockSpec((tm, D_MODEL), lambda i, j: (i, 0))
    return pl.pallas_call(
        functools.partial(_ffn_kernel, final_norm=final_norm),
        grid=(t // tm, D_FF // tf),
        in_specs=[pl.BlockSpec(memory_space=pl.ANY), row,
                  pl.BlockSpec((D_MODEL, tf), lambda i, j: (0, j)),
                  pl.BlockSpec((D_MODEL, tf), lambda i, j: (0, j)),
                  pl.BlockSpec((tf, D_MODEL), lambda i, j: (j, 0)),
                  pl.BlockSpec(gf.shape, lambda i, j: (0, 0))],
        out_specs=row,
        out_shape=jax.ShapeDtypeStruct((t, D_MODEL), F32),
        scratch_shapes=[pltpu.VMEM((tm, D_MODEL), F32), pltpu.SemaphoreType.DMA(())],
        compiler_params=_params("arbitrary", "arbitrary"),
        name="ffn",
    )(x1, h2, wg, wu, wd, gf)


def _rope_tables(pos):
    inv = ROPE_THETA ** (-np.arange(0, ROPE_DIM, 2, dtype=np.float64) / ROPE_DIM)
    ang = np.asarray(pos, np.float64)[:, None] * inv[None, :]
    cos, sin = np.cos(ang), np.sin(ang)
    n = ang.shape[0]
    rest = HEAD_DIM - ROPE_DIM
    one, zero, zh = np.ones((n, rest)), np.zeros((n, rest)), np.zeros((n, ROPE_HALF))
    per_head = [np.concatenate(parts, axis=1) for parts in
                ([cos, cos, one], [-sin, zh, zero], [zh, sin, zero])]
    return [jnp.asarray(np.tile(t, (1, HEADS_PER_COL)), F32) for t in per_head]


def kernel(x_prompt, x_sample, state_pool, cache_k_win, cache_v_win, g_mix, w_in, w_pool,
           pool_scale, attn_sinks, w_out, g_ffn, w_gate, w_up, w_down, g_final):
    batch, seq, _ = x_prompt.shape
    nseq, dec_seq, _ = x_sample.shape
    depth = w_in.shape[0]
    win_s = cache_k_win.shape[2]
    ntok_s = nseq * dec_seq
    assert seq % TOKEN_TILE == 0 and ntok_s == TOKEN_TILE and nseq % SEQ_BLOCK == 0
    assert win_s == WINDOW and seq >= WINDOW
    assert PAST_LEN >= max(POOL_HIST, WINDOW)

    xp = x_prompt.reshape(batch * seq, D_MODEL)
    xs = x_sample.reshape(ntok_s, D_MODEL)
    tabs_p = _rope_tables(np.arange(seq))
    tabs_s = _rope_tables(PAST_LEN + np.arange(ntok_s) % dec_seq)
    gf = g_final.reshape(1, D_MODEL)

    outs = [[] for _ in range(6)]
    for l in range(depth):
        w_in_l, wpool_l = w_in[l].astype(BF16), w_pool[l].astype(BF16)
        gm = g_mix[l].reshape(1, D_MODEL)
        gn = g_ffn[l].reshape(1, D_MODEL)
        pscale = pool_scale[l].reshape(1, POOL_WIDTH)
        sink_rows = jnp.tile(jnp.repeat(attn_sinks[l], dec_seq), SEQ_BLOCK)[:, None]

        u, q, kd, vt, u_tail, k_tail, v_tail, w_out_l, wg_l, wd_l = _proj(
            xp, gm, w_in_l, tabs_p, seq // TOKEN_TILE, WINDOW, cast=(w_out[l], w_gate[l], w_down[l]))
        x1, h2, wu_l = _mix_prompt(attn_sinks[l], u, q, kd, vt, wpool_l, pscale, xp, w_out_l, gn,
                                   batch, seq, cast=(w_up[l],))
        xp = _ffn(x1, h2, wg_l, wu_l, wd_l, gf, l == depth - 1, FFN_TOKEN_TILE)
        outs[0].append(u_tail.reshape(batch, HALO, POOL_WIDTH)[:, HALO - POOL_HIST:])
        outs[1].append(k_tail.reshape(batch, WINDOW, N_KV_HEADS, HEAD_DIM))
        outs[2].append(v_tail.reshape(batch, WINDOW, N_KV_HEADS, HEAD_DIM))

        u, q, _, _, _, k_new, v_new = _proj(xs, gm, w_in_l, tabs_s, 1, TOKEN_TILE)
        state_t = jnp.transpose(state_pool[l], (1, 0, 2))
        ckt = jnp.transpose(cache_k_win[l], (0, 2, 3, 1))
        cvt = jnp.transpose(cache_v_win[l], (0, 2, 3, 1))
        u_t = jnp.transpose(u.reshape(nseq, dec_seq, POOL_WIDTH), (1, 0, 2))
        mix, npool, nkt, nvt = _mix_sample_t(sink_rows, u_t, q, k_new, v_new, state_t, ckt, cvt,
                                             wpool_l, pscale, dec_seq)
        x1, h2 = _outproj(xs, mix, w_out_l, gn)
        xs = _ffn(x1, h2, wg_l, wu_l, wd_l, gf, l == depth - 1, min(FFN_TOKEN_TILE, ntok_s))
        outs[3].append(jnp.transpose(npool, (1, 0, 2)))
        outs[4].append(jnp.transpose(nkt, (0, 3, 1, 2)))
        outs[5].append(jnp.transpose(nvt, (0, 3, 1, 2)))

    y_prompt = xp.reshape(batch, seq, D_MODEL)
    y_sample = xs.reshape(nseq, dec_seq, D_MODEL)
    return (y_prompt, y_sample) + tuple(jnp.stack(o) for o in outs)
```

```python
import functools

import jax
import jax.numpy as jnp
import numpy as np
from jax import lax
from jax.experimental import pallas as pl
from jax.experimental.pallas import tpu as pltpu

F32 = jnp.float32
BF16 = jnp.bfloat16

D_MODEL = 2048
POOL_WIDTH = D_MODEL // 2
POOL_WINDOWS = (2, 4, 8, 16)
N_POOL_GROUPS = len(POOL_WINDOWS)
POOL_GROUP = POOL_WIDTH // N_POOL_GROUPS
POOL_HIST = max(POOL_WINDOWS) - 1
HEAD_DIM = 64
N_HEADS = (D_MODEL - POOL_WIDTH) // HEAD_DIM
N_KV_HEADS = 4
GQA_GROUP = N_HEADS // N_KV_HEADS
ATTN_WIDTH = N_HEADS * HEAD_DIM
KV_WIDTH = N_KV_HEADS * HEAD_DIM
IN_WIDTH = POOL_WIDTH + ATTN_WIDTH + 2 * KV_WIDTH
MIX_WIDTH = POOL_WIDTH + ATTN_WIDTH
WINDOW = 128
BLOCK = 128
ROPE_DIM = HEAD_DIM // 4
ROPE_HALF = ROPE_DIM // 2
ROPE_THETA = 500000.0
D_FF = ((8 * D_MODEL // 3 + 255) // 256) * 256
EPS = 1e-5
PAST_LEN = 16384
LOG2E = 1.4426950408889634
Q_SCALE = HEAD_DIM ** -0.5 * LOG2E

LANES = 128
SUBLANES = 8
HEADS_PER_COL = LANES // HEAD_DIM
HALO = 16
POOL_TOP = 2 * HALO
TOKEN_TILE = 512
FFN_TOKEN_TILE = 1024
FFN_FINAL_BLOCKS = 4
OUT_COL_PIECES = 4
FF_TILE = 512
SEQ_BLOCK = 16
VMEM_LIMIT = 60 * 1024 * 1024


def _params(*semantics):
    return pltpu.CompilerParams(dimension_semantics=semantics, vmem_limit_bytes=VMEM_LIMIT)


def _rms(x, g):
    ms = jnp.mean(x * x, axis=-1, keepdims=True)
    return x * lax.rsqrt(ms + EPS) * g


def _swap_halves(z):
    return pltpu.roll(z, HEAD_DIM, 1)


def _cast_rows(in_refs, out_refs):
    for src, dst in zip(in_refs, out_refs):
        dst[...] = src[...].astype(BF16)


def _cast_specs(weights, steps, step_of=lambda i: i):
    for w in weights:
        assert w.shape[0] % (steps * 2 * SUBLANES) == 0
    specs = [pl.BlockSpec((w.shape[0] // steps, w.shape[1]), lambda *idx: (step_of(*idx), 0))
             for w in weights]
    return specs, [jax.ShapeDtypeStruct(w.shape, BF16) for w in weights]


def _proj_kernel(x_ref, g_ref, w_ref, c_ref, s1_ref, s2_ref, *refs):
    ncast = (len(refs) - 7) // 2
    u_ref, q_ref, kd_ref, vt_ref, utail_ref, ktail_ref, vtail_ref = refs[ncast:ncast + 7]
    _cast_rows(refs[:ncast], refs[ncast + 7:])
    tm = x_ref.shape[0]
    tail = ktail_ref.shape[0]
    x = x_ref[...]
    inv_rms = lax.rsqrt(jnp.mean(x * x, axis=-1, keepdims=True) + EPS)
    xg = (x * g_ref[...]).astype(BF16)
    k0 = POOL_WIDTH + ATTN_WIDTH
    project = lambda c0, c1: jnp.dot(xg, w_ref[:, c0:c1], preferred_element_type=F32)
    raw_kv, raw_q, raw_u = project(k0, IN_WIDTH), project(POOL_WIDTH, k0), project(0, POOL_WIDTH)
    c, s1, s2 = c_ref[...], s1_ref[...], s2_ref[...]

    def rope(z):
        return (z * c + pltpu.roll(z, LANES - ROPE_HALF, 1) * s1
                + pltpu.roll(z, ROPE_HALF, 1) * s2)

    low_half = lax.broadcasted_iota(jnp.int32, (1, LANES), 1) < HEAD_DIM
    for a in range(KV_WIDTH // LANES):
        z = rope(raw_kv[:, a * LANES:(a + 1) * LANES] * inv_rms)
        ktail_ref[:, a * LANES:(a + 1) * LANES] = z[tm - tail:, :]
        zr = _swap_halves(z)
        kd_ref[:, (2 * a) * LANES:(2 * a + 1) * LANES] = jnp.where(low_half, z, zr).astype(BF16)
        kd_ref[:, (2 * a + 1) * LANES:(2 * a + 2) * LANES] = jnp.where(low_half, zr, z).astype(BF16)
    v = raw_kv[:, KV_WIDTH:] * inv_rms
    vtail_ref[...] = v[tm - tail:, :]
    vt_ref[...] = v.T.astype(BF16)
    for col in range(ATTN_WIDTH // LANES):
        z = raw_q[:, col * LANES:(col + 1) * LANES] * inv_rms
        q_ref[:, col * LANES:(col + 1) * LANES] = (rope(z) * Q_SCALE).astype(BF16)
    u = raw_u * inv_rms
    u_ref[...] = u
    utail_ref[...] = u[tm - HALO:, :]


def _proj(x, g, w_in, tabs, tiles_per_seq, tail, cast=()):
    t = x.shape[0]
    tm = TOKEN_TILE
    steps = t // tm
    nseq = steps // tiles_per_seq
    row = lambda w: pl.BlockSpec((tm, w), lambda i: (i, 0))
    tab = pl.BlockSpec((tm, LANES), lambda i: (i % tiles_per_seq, 0))
    const = lambda a: pl.BlockSpec(a.shape, lambda i: (0,) * a.ndim, pipeline_mode=pl.Buffered(1))
    per_seq = lambda r, w: pl.BlockSpec((r, w), lambda i: (i // tiles_per_seq, 0))
    cast_specs, cast_shapes = _cast_specs(cast, steps)
    return pl.pallas_call(
        _proj_kernel,
        grid=(steps,),
        in_specs=[row(D_MODEL), const(g), const(w_in), tab, tab, tab] + cast_specs,
        out_specs=[row(POOL_WIDTH), row(ATTN_WIDTH), row(2 * KV_WIDTH),
                   pl.BlockSpec((KV_WIDTH, tm), lambda i: (0, i)),
                   per_seq(HALO, POOL_WIDTH), per_seq(tail, KV_WIDTH), per_seq(tail, KV_WIDTH)]
        + cast_specs,
        out_shape=[jax.ShapeDtypeStruct((t, POOL_WIDTH), F32),
                   jax.ShapeDtypeStruct((t, ATTN_WIDTH), BF16),
                   jax.ShapeDtypeStruct((t, 2 * KV_WIDTH), BF16),
                   jax.ShapeDtypeStruct((KV_WIDTH, t), BF16),
                   jax.ShapeDtypeStruct((nseq * HALO, POOL_WIDTH), F32),
                   jax.ShapeDtypeStruct((nseq * tail, KV_WIDTH), F32),
                   jax.ShapeDtypeStruct((nseq * tail, KV_WIDTH), F32)] + cast_shapes,
        compiler_params=_params("arbitrary"),
        name="proj",
    )(x, g, w_in, *tabs, *cast)


def _mix_prompt_kernel(sink_ref, u_ref, halo_ref, q_ref, kdc_ref, kdp_ref, vtc_ref, vtp_ref,
                       wpool_ref, pscale_ref, x_ref, wout_ref, gffn_ref, *refs, tiles_per_seq):
    ncast = (len(refs) - 8) // 2
    x1_ref, h2_ref = refs[ncast:ncast + 2]
    ext_ref, lvl_ref, kwin_ref, vtwin_ref, bias_ref, mixbuf_ref = refs[2 * ncast + 2:]
    _cast_rows(refs[:ncast], refs[ncast + 2:2 * ncast + 2])
    tq = u_ref.shape[0]
    step = pl.program_id(0)
    tile = jnp.minimum(step, pl.num_programs(0) - 2)
    i = lax.rem(tile, tiles_per_seq)
    first = i == 0
    mix_ref = mixbuf_ref.at[lax.rem(step, 2)]
    mix_prev_ref = mixbuf_ref.at[lax.rem(step + 1, 2)]

    @pl.when(step == 0)
    def _():
        mixbuf_ref[1] = jnp.zeros(mixbuf_ref.shape[1:], BF16)

    def out_projection(piece):
        r, c = divmod(piece, OUT_COL_PIECES)
        rows = slice(r * (tq // 2), (r + 1) * (tq // 2))
        cols = slice(c * (D_MODEL // OUT_COL_PIECES), (c + 1) * (D_MODEL // OUT_COL_PIECES))
        x1_ref[rows, cols] = x_ref[rows, cols] + jnp.dot(mix_prev_ref[rows, :], wout_ref[:, cols],
                                                         preferred_element_type=F32)
        if c == OUT_COL_PIECES - 1:
            h2_ref[rows, :] = _rms(x1_ref[rows, :], gffn_ref[...]).astype(BF16)

    top = POOL_TOP
    ext_ref[0:top - HALO, :] = jnp.zeros((top - HALO, POOL_WIDTH), F32)
    ext_ref[top - HALO:top, :] = halo_ref[...] * jnp.where(first, 0.0, 1.0)
    ext_ref[top:, :] = u_ref[...]
    lvl_ref[:, 0:SUBLANES, :] = jnp.zeros((2, SUBLANES, POOL_GROUP), F32)
    nlive = top + tq - SUBLANES
    pos1 = i * tq + lax.broadcasted_iota(jnp.int32, (tq, 1), 0) + 1

    def pool_group(gi):
        w = POOL_WINDOWS[gi]
        cols = slice(gi * POOL_GROUP, (gi + 1) * POOL_GROUP)
        src = ext_ref.at[:, cols]
        shift, slot = 1, 0
        while shift < w:
            dst = lvl_ref.at[slot]
            dst[SUBLANES:, :] = (src[SUBLANES:SUBLANES + nlive, :]
                                 + src[SUBLANES - shift:SUBLANES - shift + nlive, :])
            src, shift, slot = dst, 2 * shift, 1 - slot
        tok = ext_ref[top:, cols]
        inv_cnt = 1.0 / jnp.minimum(pos1, w).astype(F32)
        d = (src[top:, :] * inv_cnt - tok).astype(BF16)
        po = jnp.dot(d, wpool_ref[gi], preferred_element_type=F32) * pscale_ref[:, cols]
        mix_ref[:, cols] = po.astype(BF16)

    kwin_ref[0:BLOCK, :] = kdp_ref[...]
    kwin_ref[BLOCK:, :] = kdc_ref[...]
    vtwin_ref[:, 0:BLOCK] = vtp_ref[...]
    vtwin_ref[:, BLOCK:] = vtc_ref[...]
    kj = lax.broadcasted_iota(jnp.int32, (2 * BLOCK, BLOCK), 0)
    qi = lax.broadcasted_iota(jnp.int32, (2 * BLOCK, BLOCK), 1)
    band = (kj >= qi) & (kj <= qi + WINDOW)
    first_lo = jnp.where(first, BLOCK, 0)
    bias_ref[0] = jnp.where(band, 0.0, -jnp.inf)
    bias_ref[1] = jnp.where(band & (kj >= first_lo), 0.0, -jnp.inf)
    lane = lax.broadcasted_iota(jnp.int32, (1, LANES), 1)
    keep_half = [(lane // HEAD_DIM == hh).astype(BF16) for hh in range(HEADS_PER_COL)]

    def scores(n, g):
        rows = slice(n * BLOCK, (n + 1) * BLOCK)
        qs = jnp.concatenate(
            [q_ref[rows, (h // HEADS_PER_COL) * LANES:(h // HEADS_PER_COL + 1) * LANES]
             * keep_half[h % HEADS_PER_COL]
             for h in range(GQA_GROUP * g, GQA_GROUP * (g + 1))], axis=0)
        kd = kwin_ref[n * BLOCK:(n + 2) * BLOCK, g * LANES:(g + 1) * LANES]
        return lax.dot_general(kd, qs, (((1,), (1,)), ((), ())), preferred_element_type=F32)

    def attend(n, g, s_t):
        p_t = []
        for j in range(GQA_GROUP):
            s = s_t[:, j * BLOCK:(j + 1) * BLOCK] + bias_ref[1 if n == 0 else 0]
            m = jnp.max(s, axis=0, keepdims=True)
            p = jnp.exp2(s - m)
            den = jnp.sum(p, axis=0, keepdims=True) + jnp.exp2(sink_ref[GQA_GROUP * g + j] * LOG2E - m)
            p_t.append((p * (1.0 / den)).astype(BF16))
        vt = vtwin_ref[g * HEAD_DIM:(g + 1) * HEAD_DIM, n * BLOCK:(n + 2) * BLOCK]
        o_t = jnp.dot(vt, jnp.concatenate(p_t, axis=1), preferred_element_type=F32)
        for c in range(GQA_GROUP // HEADS_PER_COL):
            col_t = jnp.concatenate(
                [o_t[:, (HEADS_PER_COL * c + hh) * BLOCK:(HEADS_PER_COL * c + hh + 1) * BLOCK]
                 for hh in range(HEADS_PER_COL)], axis=0)
            col = POOL_WIDTH + (GQA_GROUP // HEADS_PER_COL * g + c) * LANES
            mix_ref[n * BLOCK:(n + 1) * BLOCK, col:col + LANES] = col_t.T.astype(BF16)

    work = [(n, g) for n in range(tq // BLOCK) for g in range(N_KV_HEADS)]
    pieces = list(range(2 * OUT_COL_PIECES))
    groups = list(range(N_POOL_GROUPS))
    s_next = scores(*work[0])
    for idx, (n, g) in enumerate(work):
        s_t = s_next
        if idx + 1 < len(work):
            s_next = scores(*work[idx + 1])
        if idx % 2 == 0:
            out_projection(pieces.pop(0))
        attend(n, g, s_t)
        if idx % 4 == 1:
            pool_group(groups.pop(0))
    assert not pieces and not groups


def _mix_prompt(sinks, u, q, kd, vt, wpool, pscale, x, w_out, g_ffn, batch, seq, cast=()):
    tq = TOKEN_TILE
    nt = seq // tq
    ntiles = batch * nt
    bpt = tq // BLOCK
    hpt = tq // HALO
    tile = lambda s: jnp.minimum(s, ntiles - 1)
    prev_block = lambda s: jnp.maximum(tile(s) * bpt - 1, 0)
    row = lambda w: pl.BlockSpec((tq, w), lambda s: (tile(s), 0))
    late_row = lambda w: pl.BlockSpec((tq, w), lambda s: (jnp.maximum(s - 1, 0), 0))
    halo = pl.BlockSpec((HALO, POOL_WIDTH), lambda s: (jnp.maximum(tile(s) * hpt - 1, 0), 0))
    const = lambda a: pl.BlockSpec(a.shape, lambda s: (0,) * a.ndim, pipeline_mode=pl.Buffered(1))
    cast_specs, cast_shapes = _cast_specs(cast, ntiles, tile)
    return pl.pallas_call(
        functools.partial(_mix_prompt_kernel, tiles_per_seq=nt),
        grid=(ntiles + 1,),
        in_specs=[pl.BlockSpec(memory_space=pltpu.SMEM), row(POOL_WIDTH), halo, row(ATTN_WIDTH),
                  row(2 * KV_WIDTH),
                  pl.BlockSpec((BLOCK, 2 * KV_WIDTH), lambda s: (prev_block(s), 0)),
                  pl.BlockSpec((KV_WIDTH, tq), lambda s: (0, tile(s))),
                  pl.BlockSpec((KV_WIDTH, BLOCK), lambda s: (0, prev_block(s))),
                  const(wpool), const(pscale), late_row(D_MODEL), const(w_out), const(g_ffn)]
        + cast_specs,
        out_specs=[late_row(D_MODEL), late_row(D_MODEL)] + cast_specs,
        out_shape=[jax.ShapeDtypeStruct((batch * seq, D_MODEL), F32),
                   jax.ShapeDtypeStruct((batch * seq, D_MODEL), BF16)] + cast_shapes,
        scratch_shapes=[pltpu.VMEM((POOL_TOP + tq, POOL_WIDTH), F32),
                        pltpu.VMEM((2, POOL_TOP + tq, POOL_GROUP), F32),
                        pltpu.VMEM((BLOCK + tq, 2 * KV_WIDTH), BF16),
                        pltpu.VMEM((KV_WIDTH, BLOCK + tq), BF16),
                        pltpu.VMEM((2, 2 * BLOCK, BLOCK), F32),
                        pltpu.VMEM((2, tq, MIX_WIDTH), BF16)],
        compiler_params=_params("arbitrary"),
        name="mix_prompt",
    )(sinks, u, u, q, kd, kd, vt, vt, wpool, pscale, x, w_out, g_ffn, *cast)


def _mix_sample_t_kernel(sink_ref, u_ref, q_ref, kn_ref, vn_ref, st_ref, ckt_ref, cvt_ref,
                         wpool_ref, pscale_ref, mix_ref, npool_ref, nkt_ref, nvt_ref, *, dec_seq):
    sb = SEQ_BLOCK
    rows_per_seq = N_HEADS * dec_seq
    nrow = sb * rows_per_seq
    ntok = sb * dec_seq
    win = ckt_ref.shape[3]
    pair_rows = HEADS_PER_COL * GQA_GROUP * dec_seq
    assert win == LANES and ntok <= LANES

    ext = [st_ref[h] for h in range(POOL_HIST)]
    ext += [u_ref[t] for t in range(dec_seq)]
    for h in range(POOL_HIST):
        npool_ref[h] = ext[h + dec_seq]
    r_out = lax.broadcasted_iota(jnp.int32, (ntok, ntok), 0)
    r_in = lax.broadcasted_iota(jnp.int32, (ntok, ntok), 1)
    to_seq_major = ((r_in % sb) * dec_seq + r_in // sb == r_out).astype(BF16)
    for gi, w in enumerate(POOL_WINDOWS):
        cols = slice(gi * POOL_GROUP, (gi + 1) * POOL_GROUP)
        ds = []
        for t in range(dec_seq):
            tok = ext[POOL_HIST + t][:, cols]
            acc = tok
            for j in range(1, w):
                acc = acc + ext[POOL_HIST + t - j][:, cols]
            ds.append(acc * (1.0 / w) - tok)
        d = jnp.concatenate(ds, axis=0).astype(BF16)
        po = jnp.dot(d, wpool_ref[gi], preferred_element_type=F32) * pscale_ref[:, cols]
        po = jnp.dot(to_seq_major, po.astype(BF16), preferred_element_type=F32)
        mix_ref[:, cols] = po.astype(BF16)

    pad_rows = jnp.zeros((LANES - ntok, KV_WIDTH), F32)
    k_new = jnp.concatenate([kn_ref[...], pad_rows], axis=0)
    v_new = jnp.concatenate([vn_ref[...], pad_rows], axis=0)
    k_new_t, v_new_t = k_new.T, v_new.T
    newest = lax.broadcasted_iota(jnp.int32, (HEAD_DIM, LANES), 1) >= win - dec_seq
    for b in range(sb):
        for g in range(N_KV_HEADS):
            dims = slice(g * HEAD_DIM, (g + 1) * HEAD_DIM)
            for old_ref, new_t, out_ref in ((ckt_ref, k_new_t, nkt_ref), (cvt_ref, v_new_t, nvt_ref)):
                kept = pltpu.roll(old_ref[b, g], win - dec_seq, 1)
                fresh = pltpu.roll(new_t[dims, :], win - dec_seq - b * dec_seq, 1)
                out_ref[b, g] = jnp.where(newest, fresh, kept)

    nhalf = 2
    hseq, hrow, htok = sb // nhalf, nrow // nhalf, ntok // nhalf
    r = lax.broadcasted_iota(jnp.int32, (hrow, htok), 0)
    c = lax.broadcasted_iota(jnp.int32, (hrow, htok), 1)
    pick = (c == (r // rows_per_seq) * dec_seq + r % dec_seq).astype(BF16)
    rt = lax.broadcasted_iota(jnp.int32, (htok, hrow), 0)
    ct = lax.broadcasted_iota(jnp.int32, (htok, hrow), 1)
    unpick = (rt == (ct // rows_per_seq) * dec_seq + ct % dec_seq).astype(BF16)
    row = lax.broadcasted_iota(jnp.int32, (hrow, LANES), 0)
    lane = lax.broadcasted_iota(jnp.int32, (hrow, LANES), 1)
    row_seq = row // rows_per_seq
    row_g = (row % rows_per_seq) // (GQA_GROUP * dec_seq)
    row_j = (row // dec_seq) % GQA_GROUP
    row_t = row % dec_seq
    lane_half = lane // HEAD_DIM
    own_half = lane_half == row_g % HEADS_PER_COL
    first_pair = row_g < HEADS_PER_COL
    cache_ok = lane >= row_t + (win - WINDOW)
    k_new_b, v_new_b = k_new.astype(BF16), v_new.astype(BF16)

    def new_ok(hf):
        return (lane // dec_seq == row_seq + hf * hseq) & (lane % dec_seq <= row_t)

    def build_lhs(hf):
        qrep = jnp.dot(pick, q_ref[hf * htok:(hf + 1) * htok, :], preferred_element_type=F32)
        lhs = jnp.zeros((hrow, LANES), F32)
        for qc in range(ATTN_WIDTH // LANES):
            g = qc * HEADS_PER_COL // GQA_GROUP
            src = qrep[:, qc * LANES:(qc + 1) * LANES]
            src_swapped = _swap_halves(src)
            for hh in range(HEADS_PER_COL):
                j = (qc * HEADS_PER_COL + hh) % GQA_GROUP
                here = (row_g == g) & (row_j == j) & own_half
                lhs = lhs + jnp.where(here, src if hh == g % HEADS_PER_COL else src_swapped, 0.0)
        return lhs.astype(BF16)

    def pair_tile(ref, hf, b, a):
        return ref[hf * hseq + b, HEADS_PER_COL * a:HEADS_PER_COL * (a + 1)].reshape(LANES, win).astype(BF16)

    def score(hf, lhs):
        s_cache = jnp.concatenate(
            [jnp.dot(lhs[b * rows_per_seq + a * pair_rows:b * rows_per_seq + (a + 1) * pair_rows],
                     pair_tile(ckt_ref, hf, b, a), preferred_element_type=F32)
             for b in range(hseq) for a in range(N_KV_HEADS // HEADS_PER_COL)], axis=0)
        s_new = [lax.dot_general(lhs, k_new_b[:, a * LANES:(a + 1) * LANES], (((1,), (1,)), ((), ())),
                                 preferred_element_type=F32) for a in range(N_KV_HEADS // HEADS_PER_COL)]
        return s_cache, jnp.where(first_pair, s_new[0], s_new[1])

    def weigh(hf, scores):
        s_cache = jnp.where(cache_ok, scores[0], -jnp.inf)
        s_new = jnp.where(new_ok(hf), scores[1], -jnp.inf)
        sink = sink_ref[hf * hrow:(hf + 1) * hrow, :] * LOG2E
        m = jnp.maximum(jnp.maximum(jnp.max(s_cache, axis=1, keepdims=True),
                                    jnp.max(s_new, axis=1, keepdims=True)), sink)
        p_cache, p_new = jnp.exp2(s_cache - m), jnp.exp2(s_new - m)
        den = (jnp.sum(p_cache, axis=1, keepdims=True) + jnp.sum(p_new, axis=1, keepdims=True)
               + jnp.exp2(sink - m))
        return (p_cache / den).astype(BF16), (p_new / den).astype(BF16)

    def gather_values(hf, probs):
        p_cache, p_new = probs
        o_cache = jnp.concatenate(
            [lax.dot_general(p_cache[b * rows_per_seq + a * pair_rows:b * rows_per_seq + (a + 1) * pair_rows],
                             pair_tile(cvt_ref, hf, b, a), (((1,), (1,)), ((), ())),
                             preferred_element_type=F32)
             for b in range(hseq) for a in range(N_KV_HEADS // HEADS_PER_COL)], axis=0)
        o_new = [jnp.dot(p_new, v_new_b[:, a * LANES:(a + 1) * LANES], preferred_element_type=F32)
                 for a in range(N_KV_HEADS // HEADS_PER_COL)]
        return o_cache + jnp.where(first_pair, o_new[0], o_new[1])

    def emit(hf, o):
        o_swapped = _swap_halves(o)
        z_cols = []
        for qc in range(ATTN_WIDTH // LANES):
            g = qc * HEADS_PER_COL // GQA_GROUP
            zc = jnp.zeros((hrow, LANES), F32)
            for hh in range(HEADS_PER_COL):
                j = (qc * HEADS_PER_COL + hh) % GQA_GROUP
                here = (row_g == g) & (row_j == j) & (lane_half == hh)
                zc = zc + jnp.where(here, o if hh == g % HEADS_PER_COL else o_swapped, 0.0)
            z_cols.append(zc)
        z = jnp.concatenate(z_cols, axis=1).astype(BF16)
        attn = jnp.dot(unpick, z, preferred_element_type=F32)
        mix_ref[hf * htok:(hf + 1) * htok, POOL_WIDTH:] = attn.astype(BF16)

    state = [build_lhs(hf) for hf in range(nhalf)]
    for stage in (score, weigh, gather_values, emit):
        state = [stage(hf, x) for hf, x in enumerate(state)]


def _mix_sample_t(sink_rows, u, q, k_new, v_new, state_t, cache_kt, cache_vt, wpool, pscale, dec_seq):
    nseq = cache_kt.shape[0]
    sb = SEQ_BLOCK
    ntok = sb * dec_seq
    row = lambda w: pl.BlockSpec((ntok, w), lambda i: (i, 0))
    slab = pl.BlockSpec((POOL_HIST, sb, POOL_WIDTH), lambda i: (0, i, 0))
    u_slab = pl.BlockSpec((dec_seq, sb, POOL_WIDTH), lambda i: (0, i, 0))
    cache = pl.BlockSpec((sb,) + cache_kt.shape[1:], lambda i: (i, 0, 0, 0))
    full = lambda a: pl.BlockSpec(a.shape, lambda i: (0,) * a.ndim)
    return pl.pallas_call(
        functools.partial(_mix_sample_t_kernel, dec_seq=dec_seq),
        grid=(nseq // sb,),
        in_specs=[full(sink_rows), u_slab, row(ATTN_WIDTH), row(KV_WIDTH), row(KV_WIDTH),
                  slab, cache, cache, full(wpool), full(pscale)],
        out_specs=[row(MIX_WIDTH), slab, cache, cache],
        out_shape=[jax.ShapeDtypeStruct((nseq * dec_seq, MIX_WIDTH), BF16),
                   jax.ShapeDtypeStruct((POOL_HIST, nseq, POOL_WIDTH), F32),
                   jax.ShapeDtypeStruct(cache_kt.shape, F32),
                   jax.ShapeDtypeStruct(cache_vt.shape, F32)],
        compiler_params=_params("parallel"),
        name="mix_sample",
    )(sink_rows, u, q, k_new, v_new, state_t, cache_kt, cache_vt, wpool, pscale)


def _ffn_kernel(*refs, final_norm, fused_outproj):
    if fused_outproj:
        (x_ref, mix_ref, wout_ref, gffn_ref, wg_ref, wu_ref, wd_ref, gf_ref, y_ref,
         x1_buf, h2_ref) = refs
    else:
        x1_hbm, h2_ref, wg_ref, wu_ref, wd_ref, gf_ref, y_ref, x1_buf, sem = refs
    i, j = pl.program_id(0), pl.program_id(1)
    tm = y_ref.shape[0]
    last = pl.num_programs(1) - 1

    def activation():
        h = h2_ref[...]
        gate = jnp.dot(h, wg_ref[...], preferred_element_type=F32)
        up = jnp.dot(h, wu_ref[...], preferred_element_type=F32)
        return (jax.nn.silu(gate) * up).astype(BF16)

    @pl.when(j == 0)
    def _():
        if fused_outproj:
            x1 = x_ref[...] + jnp.dot(mix_ref[...], wout_ref[...], preferred_element_type=F32)
            x1_buf[...] = x1
            h2_ref[...] = _rms(x1, gffn_ref[...]).astype(BF16)
        else:
            _ffn_residual_copy(x1_hbm, x1_buf, sem, i, tm).start()
        y_ref[...] = jnp.dot(activation(), wd_ref[...], preferred_element_type=F32)

    @pl.when((j > 0) & (j < last))
    def _():
        y_ref[...] += jnp.dot(activation(), wd_ref[...], preferred_element_type=F32)

    @pl.when(j == last)
    def _():
        if not fused_outproj:
            _ffn_residual_copy(x1_hbm, x1_buf, sem, i, tm).wait()
        act = activation()
        for r in range(FFN_FINAL_BLOCKS):
            rows = slice(r * tm // FFN_FINAL_BLOCKS, (r + 1) * tm // FFN_FINAL_BLOCKS)
            y = (x1_buf[rows, :] + y_ref[rows, :]
                 + jnp.dot(act[rows, :], wd_ref[...], preferred_element_type=F32))
            y_ref[rows, :] = _rms(y, gf_ref[...]) if final_norm else y


def _ffn(x1, h2, wg, wu, wd, gf, final_norm, tm):
    return _ffn_call([x1, h2], [pl.BlockSpec(memory_space=pl.ANY),
                                pl.BlockSpec((tm, D_MODEL), lambda i, j: (i, 0))],
                     [pltpu.VMEM((tm, D_MODEL), F32), pltpu.SemaphoreType.DMA(())],
                     wg, wu, wd, gf, final_norm, tm, fused_outproj=False)


def _outproj_ffn(x, mix, w_out, g_ffn, wg, wu, wd, gf, final_norm, tm):
    row = lambda w: pl.BlockSpec((tm, w), lambda i, j: (i, 0))
    const = lambda a: pl.BlockSpec(a.shape, lambda i, j: (0,) * a.ndim, pipeline_mode=pl.Buffered(1))
    return _ffn_call([x, mix, w_out, g_ffn], [row(D_MODEL), row(MIX_WIDTH), const(w_out), const(g_ffn)],
                     [pltpu.VMEM((tm, D_MODEL), F32), pltpu.VMEM((tm, D_MODEL), BF16)],
                     wg, wu, wd, gf, final_norm, tm, fused_outproj=True)


def _ffn_residual_copy(x1_hbm, x1_buf, sem, i, tm):
    return pltpu.make_async_copy(x1_hbm.at[pl.ds(pl.multiple_of(i * tm, tm), tm)], x1_buf, sem)


def _ffn_call(lead_args, lead_specs, scratch, wg, wu, wd, gf, final_norm, tm, fused_outproj):
    t = lead_args[0].shape[0]
    tf = FF_TILE
    return pl.pallas_call(
        functools.partial(_ffn_kernel, final_norm=final_norm, fused_outproj=fused_outproj),
        grid=(t // tm, D_FF // tf),
        in_specs=lead_specs + [pl.BlockSpec((D_MODEL, tf), lambda i, j: (0, j)),
                               pl.BlockSpec((D_MODEL, tf), lambda i, j: (0, j)),
                               pl.BlockSpec((tf, D_MODEL), lambda i, j: (j, 0)),
                               pl.BlockSpec(gf.shape, lambda i, j: (0, 0))],
        out_specs=pl.BlockSpec((tm, D_MODEL), lambda i, j: (i, 0)),
        out_shape=jax.ShapeDtypeStruct((t, D_MODEL), F32),
        scratch_shapes=scratch,
        compiler_params=_params("arbitrary", "arbitrary"),
        name="ffn",
    )(*lead_args, wg, wu, wd, gf)


def _rope_tables(pos):
    inv = ROPE_THETA ** (-np.arange(0, ROPE_DIM, 2, dtype=np.float64) / ROPE_DIM)
    ang = np.asarray(pos, np.float64)[:, None] * inv[None, :]
    cos, sin = np.cos(ang), np.sin(ang)
    n = ang.shape[0]
    rest = HEAD_DIM - ROPE_DIM
    one, zero, zh = np.ones((n, rest)), np.zeros((n, rest)), np.zeros((n, ROPE_HALF))
    per_head = [np.concatenate(parts, axis=1) for parts in
                ([cos, cos, one], [-sin, zh, zero], [zh, sin, zero])]
    return [jnp.asarray(np.tile(t, (1, HEADS_PER_COL)), F32) for t in per_head]


def kernel(x_prompt, x_sample, state_pool, cache_k_win, cache_v_win, g_mix, w_in, w_pool,
           pool_scale, attn_sinks, w_out, g_ffn, w_gate, w_up, w_down, g_final):
    batch, seq, _ = x_prompt.shape
    nseq, dec_seq, _ = x_sample.shape
    depth = w_in.shape[0]
    win_s = cache_k_win.shape[2]
    ntok_s = nseq * dec_seq
    assert seq % TOKEN_TILE == 0 and ntok_s == TOKEN_TILE and nseq % SEQ_BLOCK == 0
    assert win_s == WINDOW and seq >= WINDOW
    assert PAST_LEN >= max(POOL_HIST, WINDOW)

    xp = x_prompt.reshape(batch * seq, D_MODEL)
    xs = x_sample.reshape(ntok_s, D_MODEL)
    tabs_p = _rope_tables(np.arange(seq))
    tabs_s = _rope_tables(PAST_LEN + np.arange(ntok_s) % dec_seq)
    gf = g_final.reshape(1, D_MODEL)

    outs = [[] for _ in range(6)]
    for l in range(depth):
        w_in_l, wpool_l = w_in[l].astype(BF16), w_pool[l].astype(BF16)
        gm = g_mix[l].reshape(1, D_MODEL)
        gn = g_ffn[l].reshape(1, D_MODEL)
        pscale = pool_scale[l].reshape(1, POOL_WIDTH)
        sink_rows = jnp.tile(jnp.repeat(attn_sinks[l], dec_seq), SEQ_BLOCK)[:, None]

        u, q, kd, vt, u_tail, k_tail, v_tail, w_out_l, wg_l, wd_l = _proj(
            xp, gm, w_in_l, tabs_p, seq // TOKEN_TILE, WINDOW, cast=(w_out[l], w_gate[l], w_down[l]))
        x1, h2, wu_l = _mix_prompt(attn_sinks[l], u, q, kd, vt, wpool_l, pscale, xp, w_out_l, gn,
                                   batch, seq, cast=(w_up[l],))
        xp = _ffn(x1, h2, wg_l, wu_l, wd_l, gf, l == depth - 1, FFN_TOKEN_TILE)
        outs[0].append(u_tail.reshape(batch, HALO, POOL_WIDTH)[:, HALO - POOL_HIST:])
        outs[1].append(k_tail.reshape(batch, WINDOW, N_KV_HEADS, HEAD_DIM))
        outs[2].append(v_tail.reshape(batch, WINDOW, N_KV_HEADS, HEAD_DIM))

        u, q, _, _, _, k_new, v_new = _proj(xs, gm, w_in_l, tabs_s, 1, TOKEN_TILE)
        state_t = jnp.transpose(state_pool[l], (1, 0, 2))
        ckt = jnp.transpose(cache_k_win[l], (0, 2, 3, 1))
        cvt = jnp.transpose(cache_v_win[l], (0, 2, 3, 1))
        u_t = jnp.transpose(u.reshape(nseq, dec_seq, POOL_WIDTH), (1, 0, 2))
        mix, npool, nkt, nvt = _mix_sample_t(sink_rows, u_t, q, k_new, v_new, state_t, ckt, cvt,
                                             wpool_l, pscale, dec_seq)
        xs = _outproj_ffn(xs, mix, w_out_l, gn, wg_l, wu_l, wd_l, gf, l == depth - 1,
                          min(FFN_TOKEN_TILE, ntok_s))
        outs[3].append(jnp.transpose(npool, (1, 0, 2)))
        outs[4].append(jnp.transpose(nkt, (0, 3, 1, 2)))
        outs[5].append(jnp.transpose(nvt, (0, 3, 1, 2)))

    y_prompt = xp.reshape(batch, seq, D_MODEL)
    y_sample = xs.reshape(nseq, dec_seq, D_MODEL)
    return (y_prompt, y_sample) + tuple(jnp.stack(o) for o in outs)
```

```python
import functools

import jax
import jax.numpy as jnp
import numpy as np
from jax import lax
from jax.experimental import pallas as pl
from jax.experimental.pallas import tpu as pltpu

F32 = jnp.float32
BF16 = jnp.bfloat16

D_MODEL = 2048
POOL_WIDTH = D_MODEL // 2
POOL_WINDOWS = (2, 4, 8, 16)
N_POOL_GROUPS = len(POOL_WINDOWS)
POOL_GROUP = POOL_WIDTH // N_POOL_GROUPS
POOL_HIST = max(POOL_WINDOWS) - 1
HEAD_DIM = 64
N_HEADS = (D_MODEL - POOL_WIDTH) // HEAD_DIM
N_KV_HEADS = 4
GQA_GROUP = N_HEADS // N_KV_HEADS
ATTN_WIDTH = N_HEADS * HEAD_DIM
KV_WIDTH = N_KV_HEADS * HEAD_DIM
IN_WIDTH = POOL_WIDTH + ATTN_WIDTH + 2 * KV_WIDTH
MIX_WIDTH = POOL_WIDTH + ATTN_WIDTH
WINDOW = 128
BLOCK = 128
ROPE_DIM = HEAD_DIM // 4
ROPE_HALF = ROPE_DIM // 2
ROPE_THETA = 500000.0
D_FF = ((8 * D_MODEL // 3 + 255) // 256) * 256
EPS = 1e-5
PAST_LEN = 16384
LOG2E = 1.4426950408889634
Q_SCALE = HEAD_DIM ** -0.5 * LOG2E

LANES = 128
SUBLANES = 8
HEADS_PER_COL = LANES // HEAD_DIM
HALO = 16
POOL_TOP = 2 * HALO
TOKEN_TILE = 512
FFN_TOKEN_TILE = 1024
FFN_FINAL_BLOCKS = 4
OUT_COL_PIECES = 4
FF_TILE = 512
SEQ_BLOCK = 16
VMEM_LIMIT = 60 * 1024 * 1024


def _params(*semantics):
    return pltpu.CompilerParams(dimension_semantics=semantics, vmem_limit_bytes=VMEM_LIMIT)


def _rms(x, g):
    ms = jnp.mean(x * x, axis=-1, keepdims=True)
    return x * lax.rsqrt(ms + EPS) * g


def _swap_halves(z):
    return pltpu.roll(z, HEAD_DIM, 1)


def _cast_rows(in_refs, out_refs):
    for src, dst in zip(in_refs, out_refs):
        dst[...] = src[...].astype(BF16)


def _cast_specs(weights, steps, step_of=lambda i: i):
    for w in weights:
        assert w.shape[0] % (steps * 2 * SUBLANES) == 0
    specs = [pl.BlockSpec((w.shape[0] // steps, w.shape[1]), lambda *idx: (step_of(*idx), 0))
             for w in weights]
    return specs, [jax.ShapeDtypeStruct(w.shape, BF16) for w in weights]


def _proj_kernel(x_ref, g_ref, w_ref, c_ref, s1_ref, s2_ref, *refs):
    ncast = (len(refs) - 7) // 2
    u_ref, q_ref, kd_ref, vt_ref, utail_ref, ktail_ref, vtail_ref = refs[ncast:ncast + 7]
    _cast_rows(refs[:ncast], refs[ncast + 7:])
    tm = x_ref.shape[0]
    tail = ktail_ref.shape[0]
    x = x_ref[...]
    inv_rms = lax.rsqrt(jnp.mean(x * x, axis=-1, keepdims=True) + EPS)
    xg = (x * g_ref[...]).astype(BF16)
    k0 = POOL_WIDTH + ATTN_WIDTH
    project = lambda c0, c1: jnp.dot(xg, w_ref[:, c0:c1], preferred_element_type=F32)
    raw_kv, raw_q, raw_u = project(k0, IN_WIDTH), project(POOL_WIDTH, k0), project(0, POOL_WIDTH)
    c, s1, s2 = c_ref[...], s1_ref[...], s2_ref[...]

    def rope(z):
        return (z * c + pltpu.roll(z, LANES - ROPE_HALF, 1) * s1
                + pltpu.roll(z, ROPE_HALF, 1) * s2)

    low_half = lax.broadcasted_iota(jnp.int32, (1, LANES), 1) < HEAD_DIM
    for a in range(KV_WIDTH // LANES):
        z = rope(raw_kv[:, a * LANES:(a + 1) * LANES] * inv_rms)
        ktail_ref[:, a * LANES:(a + 1) * LANES] = z[tm - tail:, :]
        zr = _swap_halves(z)
        kd_ref[:, (2 * a) * LANES:(2 * a + 1) * LANES] = jnp.where(low_half, z, zr).astype(BF16)
        kd_ref[:, (2 * a + 1) * LANES:(2 * a + 2) * LANES] = jnp.where(low_half, zr, z).astype(BF16)
    v = raw_kv[:, KV_WIDTH:] * inv_rms
    vtail_ref[...] = v[tm - tail:, :]
    vt_ref[...] = v.T.astype(BF16)
    for col in range(ATTN_WIDTH // LANES):
        z = raw_q[:, col * LANES:(col + 1) * LANES] * inv_rms
        q_ref[:, col * LANES:(col + 1) * LANES] = (rope(z) * Q_SCALE).astype(BF16)
    u = raw_u * inv_rms
    u_ref[...] = u
    utail_ref[...] = u[tm - HALO:, :]


def _proj(x, g, w_in, tabs, tiles_per_seq, tail, cast=()):
    t = x.shape[0]
    tm = TOKEN_TILE
    steps = t // tm
    nseq = steps // tiles_per_seq
    row = lambda w: pl.BlockSpec((tm, w), lambda i: (i, 0))
    tab = pl.BlockSpec((tm, LANES), lambda i: (i % tiles_per_seq, 0))
    const = lambda a: pl.BlockSpec(a.shape, lambda i: (0,) * a.ndim, pipeline_mode=pl.Buffered(1))
    per_seq = lambda r, w: pl.BlockSpec((r, w), lambda i: (i // tiles_per_seq, 0))
    cast_specs, cast_shapes = _cast_specs(cast, steps)
    return pl.pallas_call(
        _proj_kernel,
        grid=(steps,),
        in_specs=[row(D_MODEL), const(g), const(w_in), tab, tab, tab] + cast_specs,
        out_specs=[row(POOL_WIDTH), row(ATTN_WIDTH), row(2 * KV_WIDTH),
                   pl.BlockSpec((KV_WIDTH, tm), lambda i: (0, i)),
                   per_seq(HALO, POOL_WIDTH), per_seq(tail, KV_WIDTH), per_seq(tail, KV_WIDTH)]
        + cast_specs,
        out_shape=[jax.ShapeDtypeStruct((t, POOL_WIDTH), F32),
                   jax.ShapeDtypeStruct((t, ATTN_WIDTH), BF16),
                   jax.ShapeDtypeStruct((t, 2 * KV_WIDTH), BF16),
                   jax.ShapeDtypeStruct((KV_WIDTH, t), BF16),
                   jax.ShapeDtypeStruct((nseq * HALO, POOL_WIDTH), F32),
                   jax.ShapeDtypeStruct((nseq * tail, KV_WIDTH), F32),
                   jax.ShapeDtypeStruct((nseq * tail, KV_WIDTH), F32)] + cast_shapes,
        compiler_params=_params("arbitrary"),
        name="proj",
    )(x, g, w_in, *tabs, *cast)


def _mix_prompt_kernel(sink_ref, u_ref, halo_ref, q_ref, kdc_ref, kdp_ref, vtc_ref, vtp_ref,
                       wpool_ref, pscale_ref, x_ref, wout_ref, gffn_ref, *refs, tiles_per_seq):
    ncast = (len(refs) - 8) // 2
    x1_ref, h2_ref = refs[ncast:ncast + 2]
    ext_ref, lvl_ref, kwin_ref, vtwin_ref, bias_ref, mixbuf_ref = refs[2 * ncast + 2:]
    _cast_rows(refs[:ncast], refs[ncast + 2:2 * ncast + 2])
    tq = u_ref.shape[0]
    step = pl.program_id(0)
    last_step = pl.num_programs(0) - 1
    i = lax.rem(step, tiles_per_seq)
    first = i == 0
    mix_ref = mixbuf_ref.at[lax.rem(step, 2)]
    mix_prev_ref = mixbuf_ref.at[lax.rem(step + 1, 2)]

    def out_projection(piece):
        r, c = divmod(piece, OUT_COL_PIECES)
        rows = slice(r * (tq // 2), (r + 1) * (tq // 2))
        cols = slice(c * (D_MODEL // OUT_COL_PIECES), (c + 1) * (D_MODEL // OUT_COL_PIECES))
        x1_ref[rows, cols] = x_ref[rows, cols] + jnp.dot(mix_prev_ref[rows, :], wout_ref[:, cols],
                                                         preferred_element_type=F32)
        if c == OUT_COL_PIECES - 1:
            h2_ref[rows, :] = _rms(x1_ref[rows, :], gffn_ref[...]).astype(BF16)

    top = POOL_TOP
    nlive = top + tq - SUBLANES
    pos1 = i * tq + lax.broadcasted_iota(jnp.int32, (tq, 1), 0) + 1
    kj = lax.broadcasted_iota(jnp.int32, (2 * BLOCK, BLOCK), 0)
    qi = lax.broadcasted_iota(jnp.int32, (2 * BLOCK, BLOCK), 1)
    band = (kj >= qi) & (kj <= qi + WINDOW)
    first_lo = jnp.where(first, BLOCK, 0)

    def fill_mixing_scratch():
        ext_ref[0:top - HALO, :] = jnp.zeros((top - HALO, POOL_WIDTH), F32)
        ext_ref[top - HALO:top, :] = halo_ref[...] * jnp.where(first, 0.0, 1.0)
        ext_ref[top:, :] = u_ref[...]
        lvl_ref[:, 0:SUBLANES, :] = jnp.zeros((2, SUBLANES, POOL_GROUP), F32)
        kwin_ref[0:BLOCK, :] = kdp_ref[...]
        kwin_ref[BLOCK:, :] = kdc_ref[...]
        vtwin_ref[:, 0:BLOCK] = vtp_ref[...]
        vtwin_ref[:, BLOCK:] = vtc_ref[...]
        bias_ref[0] = jnp.where(band, 0.0, -jnp.inf)
        bias_ref[1] = jnp.where(band & (kj >= first_lo), 0.0, -jnp.inf)

    def pool_group(gi):
        w = POOL_WINDOWS[gi]
        cols = slice(gi * POOL_GROUP, (gi + 1) * POOL_GROUP)
        src = ext_ref.at[:, cols]
        shift, slot = 1, 0
        while shift < w:
            dst = lvl_ref.at[slot]
            dst[SUBLANES:, :] = (src[SUBLANES:SUBLANES + nlive, :]
                                 + src[SUBLANES - shift:SUBLANES - shift + nlive, :])
            src, shift, slot = dst, 2 * shift, 1 - slot
        tok = ext_ref[top:, cols]
        inv_cnt = 1.0 / jnp.minimum(pos1, w).astype(F32)
        d = (src[top:, :] * inv_cnt - tok).astype(BF16)
        po = jnp.dot(d, wpool_ref[gi], preferred_element_type=F32) * pscale_ref[:, cols]
        mix_ref[:, cols] = po.astype(BF16)

    lane = lax.broadcasted_iota(jnp.int32, (1, LANES), 1)
    keep_half = [(lane // HEAD_DIM == hh).astype(BF16) for hh in range(HEADS_PER_COL)]

    def scores(n, g):
        rows = slice(n * BLOCK, (n + 1) * BLOCK)
        qs = jnp.concatenate(
            [q_ref[rows, (h // HEADS_PER_COL) * LANES:(h // HEADS_PER_COL + 1) * LANES]
             * keep_half[h % HEADS_PER_COL]
             for h in range(GQA_GROUP * g, GQA_GROUP * (g + 1))], axis=0)
        kd = kwin_ref[n * BLOCK:(n + 2) * BLOCK, g * LANES:(g + 1) * LANES]
        return lax.dot_general(kd, qs, (((1,), (1,)), ((), ())), preferred_element_type=F32)

    def attend(n, g, s_t):
        p_t = []
        for j in range(GQA_GROUP):
            s = s_t[:, j * BLOCK:(j + 1) * BLOCK] + bias_ref[1 if n == 0 else 0]
            m = jnp.max(s, axis=0, keepdims=True)
            p = jnp.exp2(s - m)
            den = jnp.sum(p, axis=0, keepdims=True) + jnp.exp2(sink_ref[GQA_GROUP * g + j] * LOG2E - m)
            p_t.append((p * (1.0 / den)).astype(BF16))
        vt = vtwin_ref[g * HEAD_DIM:(g + 1) * HEAD_DIM, n * BLOCK:(n + 2) * BLOCK]
        o_t = jnp.dot(vt, jnp.concatenate(p_t, axis=1), preferred_element_type=F32)
        for c in range(GQA_GROUP // HEADS_PER_COL):
            col_t = jnp.concatenate(
                [o_t[:, (HEADS_PER_COL * c + hh) * BLOCK:(HEADS_PER_COL * c + hh + 1) * BLOCK]
                 for hh in range(HEADS_PER_COL)], axis=0)
            col = POOL_WIDTH + (GQA_GROUP // HEADS_PER_COL * g + c) * LANES
            mix_ref[n * BLOCK:(n + 1) * BLOCK, col:col + LANES] = col_t.T.astype(BF16)

    work = [(n, g) for n in range(tq // BLOCK) for g in range(N_KV_HEADS)]

    def run(mix, project):
        pieces = list(range(2 * OUT_COL_PIECES)) if project else []
        if not mix:
            for piece in pieces:
                out_projection(piece)
            return
        groups = list(range(N_POOL_GROUPS))
        fill_mixing_scratch()
        s_next = scores(*work[0])
        for idx, (n, g) in enumerate(work):
            s_t = s_next
            if idx + 1 < len(work):
                s_next = scores(*work[idx + 1])
            if idx % 2 == 0 and pieces:
                out_projection(pieces.pop(0))
            attend(n, g, s_t)
            if idx % 4 == 1:
                pool_group(groups.pop(0))
        assert not pieces and not groups

    pl.when(step == 0)(lambda: run(True, False))
    pl.when((step > 0) & (step < last_step))(lambda: run(True, True))
    pl.when(step == last_step)(lambda: run(False, True))


def _mix_prompt(sinks, u, q, kd, vt, wpool, pscale, x, w_out, g_ffn, batch, seq, cast=()):
    tq = TOKEN_TILE
    nt = seq // tq
    ntiles = batch * nt
    bpt = tq // BLOCK
    hpt = tq // HALO
    tile = lambda s: jnp.minimum(s, ntiles - 1)
    prev_block = lambda s: jnp.maximum(tile(s) * bpt - 1, 0)
    row = lambda w: pl.BlockSpec((tq, w), lambda s: (tile(s), 0))
    late_row = lambda w: pl.BlockSpec((tq, w), lambda s: (jnp.maximum(s - 1, 0), 0))
    halo = pl.BlockSpec((HALO, POOL_WIDTH), lambda s: (jnp.maximum(tile(s) * hpt - 1, 0), 0))
    const = lambda a: pl.BlockSpec(a.shape, lambda s: (0,) * a.ndim, pipeline_mode=pl.Buffered(1))
    cast_specs, cast_shapes = _cast_specs(cast, ntiles, tile)
    return pl.pallas_call(
        functools.partial(_mix_prompt_kernel, tiles_per_seq=nt),
        grid=(ntiles + 1,),
        in_specs=[pl.BlockSpec(memory_space=pltpu.SMEM), row(POOL_WIDTH), halo, row(ATTN_WIDTH),
                  row(2 * KV_WIDTH),
                  pl.BlockSpec((BLOCK, 2 * KV_WIDTH), lambda s: (prev_block(s), 0)),
                  pl.BlockSpec((KV_WIDTH, tq), lambda s: (0, tile(s))),
                  pl.BlockSpec((KV_WIDTH, BLOCK), lambda s: (0, prev_block(s))),
                  const(wpool), const(pscale), late_row(D_MODEL), const(w_out), const(g_ffn)]
        + cast_specs,
        out_specs=[late_row(D_MODEL), late_row(D_MODEL)] + cast_specs,
        out_shape=[jax.ShapeDtypeStruct((batch * seq, D_MODEL), F32),
                   jax.ShapeDtypeStruct((batch * seq, D_MODEL), BF16)] + cast_shapes,
        scratch_shapes=[pltpu.VMEM((POOL_TOP + tq, POOL_WIDTH), F32),
                        pltpu.VMEM((2, POOL_TOP + tq, POOL_GROUP), F32),
                        pltpu.VMEM((BLOCK + tq, 2 * KV_WIDTH), BF16),
                        pltpu.VMEM((KV_WIDTH, BLOCK + tq), BF16),
                        pltpu.VMEM((2, 2 * BLOCK, BLOCK), F32),
                        pltpu.VMEM((2, tq, MIX_WIDTH), BF16)],
        compiler_params=_params("arbitrary"),
        name="mix_prompt",
    )(sinks, u, u, q, kd, kd, vt, vt, wpool, pscale, x, w_out, g_ffn, *cast)


def _mix_sample_t_kernel(sink_ref, u_ref, q_ref, kn_ref, vn_ref, st_ref, ckt_ref, cvt_ref,
                         wpool_ref, pscale_ref, mix_ref, npool_ref, nkt_ref, nvt_ref, *, dec_seq):
    sb = SEQ_BLOCK
    rows_per_seq = N_HEADS * dec_seq
    nrow = sb * rows_per_seq
    ntok = sb * dec_seq
    win = ckt_ref.shape[3]
    pair_rows = HEADS_PER_COL * GQA_GROUP * dec_seq
    assert win == LANES and ntok <= LANES

    ext = [st_ref[h] for h in range(POOL_HIST)]
    ext += [u_ref[t] for t in range(dec_seq)]
    for h in range(POOL_HIST):
        npool_ref[h] = ext[h + dec_seq]
    r_out = lax.broadcasted_iota(jnp.int32, (ntok, ntok), 0)
    r_in = lax.broadcasted_iota(jnp.int32, (ntok, ntok), 1)
    to_seq_major = ((r_in % sb) * dec_seq + r_in // sb == r_out).astype(BF16)
    for gi, w in enumerate(POOL_WINDOWS):
        cols = slice(gi * POOL_GROUP, (gi + 1) * POOL_GROUP)
        ds = []
        for t in range(dec_seq):
            tok = ext[POOL_HIST + t][:, cols]
            acc = tok
            for j in range(1, w):
                acc = acc + ext[POOL_HIST + t - j][:, cols]
            ds.append(acc * (1.0 / w) - tok)
        d = jnp.concatenate(ds, axis=0).astype(BF16)
        po = jnp.dot(d, wpool_ref[gi], preferred_element_type=F32) * pscale_ref[:, cols]
        po = jnp.dot(to_seq_major, po.astype(BF16), preferred_element_type=F32)
        mix_ref[:, cols] = po.astype(BF16)

    pad_rows = jnp.zeros((LANES - ntok, KV_WIDTH), F32)
    k_new = jnp.concatenate([kn_ref[...], pad_rows], axis=0)
    v_new = jnp.concatenate([vn_ref[...], pad_rows], axis=0)
    k_new_t, v_new_t = k_new.T, v_new.T
    newest = lax.broadcasted_iota(jnp.int32, (HEAD_DIM, LANES), 1) >= win - dec_seq
    for b in range(sb):
        for g in range(N_KV_HEADS):
            dims = slice(g * HEAD_DIM, (g + 1) * HEAD_DIM)
            for old_ref, new_t, out_ref in ((ckt_ref, k_new_t, nkt_ref), (cvt_ref, v_new_t, nvt_ref)):
                kept = pltpu.roll(old_ref[b, g], win - dec_seq, 1)
                fresh = pltpu.roll(new_t[dims, :], win - dec_seq - b * dec_seq, 1)
                out_ref[b, g] = jnp.where(newest, fresh, kept)

    nhalf = 2
    hseq, hrow, htok = sb // nhalf, nrow // nhalf, ntok // nhalf
    r = lax.broadcasted_iota(jnp.int32, (hrow, htok), 0)
    c = lax.broadcasted_iota(jnp.int32, (hrow, htok), 1)
    pick = (c == (r // rows_per_seq) * dec_seq + r % dec_seq).astype(BF16)
    rt = lax.broadcasted_iota(jnp.int32, (htok, hrow), 0)
    ct = lax.broadcasted_iota(jnp.int32, (htok, hrow), 1)
    unpick = (rt == (ct // rows_per_seq) * dec_seq + ct % dec_seq).astype(BF16)
    row = lax.broadcasted_iota(jnp.int32, (hrow, LANES), 0)
    lane = lax.broadcasted_iota(jnp.int32, (hrow, LANES), 1)
    row_seq = row // rows_per_seq
    row_g = (row % rows_per_seq) // (GQA_GROUP * dec_seq)
    row_j = (row // dec_seq) % GQA_GROUP
    row_t = row % dec_seq
    lane_half = lane // HEAD_DIM
    own_half = lane_half == row_g % HEADS_PER_COL
    first_pair = row_g < HEADS_PER_COL
    cache_ok = lane >= row_t + (win - WINDOW)
    k_new_b, v_new_b = k_new.astype(BF16), v_new.astype(BF16)

    def new_ok(hf):
        return (lane // dec_seq == row_seq + hf * hseq) & (lane % dec_seq <= row_t)

    def build_lhs(hf):
        qrep = jnp.dot(pick, q_ref[hf * htok:(hf + 1) * htok, :], preferred_element_type=F32)
        lhs = jnp.zeros((hrow, LANES), F32)
        for qc in range(ATTN_WIDTH // LANES):
            g = qc * HEADS_PER_COL // GQA_GROUP
            src = qrep[:, qc * LANES:(qc + 1) * LANES]
            src_swapped = _swap_halves(src)
            for hh in range(HEADS_PER_COL):
                j = (qc * HEADS_PER_COL + hh) % GQA_GROUP
                here = (row_g == g) & (row_j == j) & own_half
                lhs = lhs + jnp.where(here, src if hh == g % HEADS_PER_COL else src_swapped, 0.0)
        return lhs.astype(BF16)

    def pair_tile(ref, hf, b, a):
        return ref[hf * hseq + b, HEADS_PER_COL * a:HEADS_PER_COL * (a + 1)].reshape(LANES, win).astype(BF16)

    def score(hf, lhs):
        s_cache = jnp.concatenate(
            [jnp.dot(lhs[b * rows_per_seq + a * pair_rows:b * rows_per_seq + (a + 1) * pair_rows],
                     pair_tile(ckt_ref, hf, b, a), preferred_element_type=F32)
             for b in range(hseq) for a in range(N_KV_HEADS // HEADS_PER_COL)], axis=0)
        s_new = [lax.dot_general(lhs, k_new_b[:, a * LANES:(a + 1) * LANES], (((1,), (1,)), ((), ())),
                                 preferred_element_type=F32) for a in range(N_KV_HEADS // HEADS_PER_COL)]
        return s_cache, jnp.where(first_pair, s_new[0], s_new[1])

    def weigh(hf, scores):
        s_cache = jnp.where(cache_ok, scores[0], -jnp.inf)
        s_new = jnp.where(new_ok(hf), scores[1], -jnp.inf)
        sink = sink_ref[hf * hrow:(hf + 1) * hrow, :] * LOG2E
        m = jnp.maximum(jnp.maximum(jnp.max(s_cache, axis=1, keepdims=True),
                                    jnp.max(s_new, axis=1, keepdims=True)), sink)
        p_cache, p_new = jnp.exp2(s_cache - m), jnp.exp2(s_new - m)
        den = (jnp.sum(p_cache, axis=1, keepdims=True) + jnp.sum(p_new, axis=1, keepdims=True)
               + jnp.exp2(sink - m))
        return (p_cache / den).astype(BF16), (p_new / den).astype(BF16)

    def gather_values(hf, probs):
        p_cache, p_new = probs
        o_cache = jnp.concatenate(
            [lax.dot_general(p_cache[b * rows_per_seq + a * pair_rows:b * rows_per_seq + (a + 1) * pair_rows],
                             pair_tile(cvt_ref, hf, b, a), (((1,), (1,)), ((), ())),
                             preferred_element_type=F32)
             for b in range(hseq) for a in range(N_KV_HEADS // HEADS_PER_COL)], axis=0)
        o_new = [jnp.dot(p_new, v_new_b[:, a * LANES:(a + 1) * LANES], preferred_element_type=F32)
                 for a in range(N_KV_HEADS // HEADS_PER_COL)]
        return o_cache + jnp.where(first_pair, o_new[0], o_new[1])

    def emit(hf, o):
        o_swapped = _swap_halves(o)
        z_cols = []
        for qc in range(ATTN_WIDTH // LANES):
            g = qc * HEADS_PER_COL // GQA_GROUP
            zc = jnp.zeros((hrow, LANES), F32)
            for hh in range(HEADS_PER_COL):
                j = (qc * HEADS_PER_COL + hh) % GQA_GROUP
                here = (row_g == g) & (row_j == j) & (lane_half == hh)
                zc = zc + jnp.where(here, o if hh == g % HEADS_PER_COL else o_swapped, 0.0)
            z_cols.append(zc)
        z = jnp.concatenate(z_cols, axis=1).astype(BF16)
        attn = jnp.dot(unpick, z, preferred_element_type=F32)
        mix_ref[hf * htok:(hf + 1) * htok, POOL_WIDTH:] = attn.astype(BF16)

    state = [build_lhs(hf) for hf in range(nhalf)]
    for stage in (score, weigh, gather_values, emit):
        state = [stage(hf, x) for hf, x in enumerate(state)]


def _mix_sample_t(sink_rows, u, q, k_new, v_new, state_t, cache_kt, cache_vt, wpool, pscale, dec_seq):
    nseq = cache_kt.shape[0]
    sb = SEQ_BLOCK
    ntok = sb * dec_seq
    row = lambda w: pl.BlockSpec((ntok, w), lambda i: (i, 0))
    slab = pl.BlockSpec((POOL_HIST, sb, POOL_WIDTH), lambda i: (0, i, 0))
    u_slab = pl.BlockSpec((dec_seq, sb, POOL_WIDTH), lambda i: (0, i, 0))
    cache = pl.BlockSpec((sb,) + cache_kt.shape[1:], lambda i: (i, 0, 0, 0))
    full = lambda a: pl.BlockSpec(a.shape, lambda i: (0,) * a.ndim)
    return pl.pallas_call(
        functools.partial(_mix_sample_t_kernel, dec_seq=dec_seq),
        grid=(nseq // sb,),
        in_specs=[full(sink_rows), u_slab, row(ATTN_WIDTH), row(KV_WIDTH), row(KV_WIDTH),
                  slab, cache, cache, full(wpool), full(pscale)],
        out_specs=[row(MIX_WIDTH), slab, cache, cache],
        out_shape=[jax.ShapeDtypeStruct((nseq * dec_seq, MIX_WIDTH), BF16),
                   jax.ShapeDtypeStruct((POOL_HIST, nseq, POOL_WIDTH), F32),
                   jax.ShapeDtypeStruct(cache_kt.shape, F32),
                   jax.ShapeDtypeStruct(cache_vt.shape, F32)],
        compiler_params=_params("parallel"),
        name="mix_sample",
    )(sink_rows, u, q, k_new, v_new, state_t, cache_kt, cache_vt, wpool, pscale)


def _ffn_kernel(*refs, final_norm, fused_outproj):
    if fused_outproj:
        (x_ref, mix_ref, wout_ref, gffn_ref, wg_ref, wu_ref, wd_ref, gf_ref, y_ref,
         x1_buf, h2_ref) = refs
    else:
        x1_hbm, h2_ref, wg_ref, wu_ref, wd_ref, gf_ref, y_ref, x1_buf, sem = refs
    i, j = pl.program_id(0), pl.program_id(1)
    tm = y_ref.shape[0]
    last = pl.num_programs(1) - 1

    def activation():
        h = h2_ref[...]
        gate = jnp.dot(h, wg_ref[...], preferred_element_type=F32)
        up = jnp.dot(h, wu_ref[...], preferred_element_type=F32)
        return (jax.nn.silu(gate) * up).astype(BF16)

    @pl.when(j == 0)
    def _():
        if fused_outproj:
            x1 = x_ref[...] + jnp.dot(mix_ref[...], wout_ref[...], preferred_element_type=F32)
            x1_buf[...] = x1
            h2_ref[...] = _rms(x1, gffn_ref[...]).astype(BF16)
        else:
            _ffn_residual_copy(x1_hbm, x1_buf, sem, i, tm).start()
        y_ref[...] = jnp.dot(activation(), wd_ref[...], preferred_element_type=F32)

    @pl.when((j > 0) & (j < last))
    def _():
        y_ref[...] += jnp.dot(activation(), wd_ref[...], preferred_element_type=F32)

    @pl.when(j == last)
    def _():
        if not fused_outproj:
            _ffn_residual_copy(x1_hbm, x1_buf, sem, i, tm).wait()
        act = activation()
        for r in range(FFN_FINAL_BLOCKS):
            rows = slice(r * tm // FFN_FINAL_BLOCKS, (r + 1) * tm // FFN_FINAL_BLOCKS)
            y = (x1_buf[rows, :] + y_ref[rows, :]
                 + jnp.dot(act[rows, :], wd_ref[...], preferred_element_type=F32))
            y_ref[rows, :] = _rms(y, gf_ref[...]) if final_norm else y


def _ffn(x1, h2, wg, wu, wd, gf, final_norm, tm):
    return _ffn_call([x1, h2], [pl.BlockSpec(memory_space=pl.ANY),
                                pl.BlockSpec((tm, D_MODEL), lambda i, j: (i, 0))],
                     [pltpu.VMEM((tm, D_MODEL), F32), pltpu.SemaphoreType.DMA(())],
                     wg, wu, wd, gf, final_norm, tm, fused_outproj=False)


def _outproj_ffn(x, mix, w_out, g_ffn, wg, wu, wd, gf, final_norm, tm):
    row = lambda w: pl.BlockSpec((tm, w), lambda i, j: (i, 0))
    const = lambda a: pl.BlockSpec(a.shape, lambda i, j: (0,) * a.ndim, pipeline_mode=pl.Buffered(1))
    return _ffn_call([x, mix, w_out, g_ffn], [row(D_MODEL), row(MIX_WIDTH), const(w_out), const(g_ffn)],
                     [pltpu.VMEM((tm, D_MODEL), F32), pltpu.VMEM((tm, D_MODEL), BF16)],
                     wg, wu, wd, gf, final_norm, tm, fused_outproj=True)


def _ffn_residual_copy(x1_hbm, x1_buf, sem, i, tm):
    return pltpu.make_async_copy(x1_hbm.at[pl.ds(pl.multiple_of(i * tm, tm), tm)], x1_buf, sem)


def _ffn_call(lead_args, lead_specs, scratch, wg, wu, wd, gf, final_norm, tm, fused_outproj):
    t = lead_args[0].shape[0]
    tf = FF_TILE
    return pl.pallas_call(
        functools.partial(_ffn_kernel, final_norm=final_norm, fused_outproj=fused_outproj),
        grid=(t // tm, D_FF // tf),
        in_specs=lead_specs + [pl.BlockSpec((D_MODEL, tf), lambda i, j: (0, j)),
                               pl.BlockSpec((D_MODEL, tf), lambda i, j: (0, j)),
                               pl.BlockSpec((tf, D_MODEL), lambda i, j: (j, 0)),
                               pl.BlockSpec(gf.shape, lambda i, j: (0, 0))],
        out_specs=pl.BlockSpec((tm, D_MODEL), lambda i, j: (i, 0)),
        out_shape=jax.ShapeDtypeStruct((t, D_MODEL), F32),
        scratch_shapes=scratch,
        compiler_params=_params("arbitrary", "arbitrary"),
        name="ffn",
    )(*lead_args, wg, wu, wd, gf)


def _rope_tables(pos):
    inv = ROPE_THETA ** (-np.arange(0, ROPE_DIM, 2, dtype=np.float64) / ROPE_DIM)
    ang = np.asarray(pos, np.float64)[:, None] * inv[None, :]
    cos, sin = np.cos(ang), np.sin(ang)
    n = ang.shape[0]
    rest = HEAD_DIM - ROPE_DIM
    one, zero, zh = np.ones((n, rest)), np.zeros((n, rest)), np.zeros((n, ROPE_HALF))
    per_head = [np.concatenate(parts, axis=1) for parts in
                ([cos, cos, one], [-sin, zh, zero], [zh, sin, zero])]
    return [jnp.asarray(np.tile(t, (1, HEADS_PER_COL)), F32) for t in per_head]


def kernel(x_prompt, x_sample, state_pool, cache_k_win, cache_v_win, g_mix, w_in, w_pool,
           pool_scale, attn_sinks, w_out, g_ffn, w_gate, w_up, w_down, g_final):
    batch, seq, _ = x_prompt.shape
    nseq, dec_seq, _ = x_sample.shape
    depth = w_in.shape[0]
    win_s = cache_k_win.shape[2]
    ntok_s = nseq * dec_seq
    assert seq % TOKEN_TILE == 0 and ntok_s == TOKEN_TILE and nseq % SEQ_BLOCK == 0
    assert win_s == WINDOW and seq >= WINDOW
    assert PAST_LEN >= max(POOL_HIST, WINDOW)

    xp = x_prompt.reshape(batch * seq, D_MODEL)
    xs = x_sample.reshape(ntok_s, D_MODEL)
    tabs_p = _rope_tables(np.arange(seq))
    tabs_s = _rope_tables(PAST_LEN + np.arange(ntok_s) % dec_seq)
    gf = g_final.reshape(1, D_MODEL)

    outs = [[] for _ in range(6)]
    for l in range(depth):
        w_in_l, wpool_l = w_in[l].astype(BF16), w_pool[l].astype(BF16)
        gm = g_mix[l].reshape(1, D_MODEL)
        gn = g_ffn[l].reshape(1, D_MODEL)
        pscale = pool_scale[l].reshape(1, POOL_WIDTH)
        sink_rows = jnp.tile(jnp.repeat(attn_sinks[l], dec_seq), SEQ_BLOCK)[:, None]

        u, q, kd, vt, u_tail, k_tail, v_tail, w_out_l, wg_l, wd_l = _proj(
            xp, gm, w_in_l, tabs_p, seq // TOKEN_TILE, WINDOW, cast=(w_out[l], w_gate[l], w_down[l]))
        x1, h2, wu_l = _mix_prompt(attn_sinks[l], u, q, kd, vt, wpool_l, pscale, xp, w_out_l, gn,
                                   batch, seq, cast=(w_up[l],))
        xp = _ffn(x1, h2, wg_l, wu_l, wd_l, gf, l == depth - 1, FFN_TOKEN_TILE)
        outs[0].append(u_tail.reshape(batch, HALO, POOL_WIDTH)[:, HALO - POOL_HIST:])
        outs[1].append(k_tail.reshape(batch, WINDOW, N_KV_HEADS, HEAD_DIM))
        outs[2].append(v_tail.reshape(batch, WINDOW, N_KV_HEADS, HEAD_DIM))

        u, q, _, _, _, k_new, v_new = _proj(xs, gm, w_in_l, tabs_s, 1, TOKEN_TILE)
        state_t = jnp.transpose(state_pool[l], (1, 0, 2))
        ckt = jnp.transpose(cache_k_win[l], (0, 2, 3, 1))
        cvt = jnp.transpose(cache_v_win[l], (0, 2, 3, 1))
        u_t = jnp.transpose(u.reshape(nseq, dec_seq, POOL_WIDTH), (1, 0, 2))
        mix, npool, nkt, nvt = _mix_sample_t(sink_rows, u_t, q, k_new, v_new, state_t, ckt, cvt,
                                             wpool_l, pscale, dec_seq)
        xs = _outproj_ffn(xs, mix, w_out_l, gn, wg_l, wu_l, wd_l, gf, l == depth - 1,
                          min(FFN_TOKEN_TILE, ntok_s))
        outs[3].append(jnp.transpose(npool, (1, 0, 2)))
        outs[4].append(jnp.transpose(nkt, (0, 3, 1, 2)))
        outs[5].append(jnp.transpose(nvt, (0, 3, 1, 2)))

    y_prompt = xp.reshape(batch, seq, D_MODEL)
    y_sample = xs.reshape(nseq, dec_seq, D_MODEL)
    return (y_prompt, y_sample) + tuple(jnp.stack(o) for o in outs)
```

```python
import functools

import jax
import jax.numpy as jnp
import numpy as np
from jax import lax
from jax.experimental import pallas as pl
from jax.experimental.pallas import tpu as pltpu

F32 = jnp.float32
BF16 = jnp.bfloat16

D_MODEL = 2048
POOL_WIDTH = D_MODEL // 2
POOL_WINDOWS = (2, 4, 8, 16)
N_POOL_GROUPS = len(POOL_WINDOWS)
POOL_GROUP = POOL_WIDTH // N_POOL_GROUPS
POOL_HIST = max(POOL_WINDOWS) - 1
HEAD_DIM = 64
N_HEADS = (D_MODEL - POOL_WIDTH) // HEAD_DIM
N_KV_HEADS = 4
GQA_GROUP = N_HEADS // N_KV_HEADS
ATTN_WIDTH = N_HEADS * HEAD_DIM
KV_WIDTH = N_KV_HEADS * HEAD_DIM
IN_WIDTH = POOL_WIDTH + ATTN_WIDTH + 2 * KV_WIDTH
MIX_WIDTH = POOL_WIDTH + ATTN_WIDTH
WINDOW = 128
BLOCK = 128
ROPE_DIM = HEAD_DIM // 4
ROPE_HALF = ROPE_DIM // 2
ROPE_THETA = 500000.0
D_FF = ((8 * D_MODEL // 3 + 255) // 256) * 256
EPS = 1e-5
PAST_LEN = 16384
LOG2E = 1.4426950408889634
Q_SCALE = HEAD_DIM ** -0.5 * LOG2E

LANES = 128
SUBLANES = 8
HEADS_PER_COL = LANES // HEAD_DIM
HALO = 16
POOL_TOP = 2 * HALO
TOKEN_TILE = 512
FFN_TOKEN_TILE = 1024
FFN_FINAL_BLOCKS = 4
OUT_COL_PIECES = 4
FF_TILE = 512
SEQ_BLOCK = 16
VMEM_LIMIT = 60 * 1024 * 1024


def _params(*semantics):
    return pltpu.CompilerParams(dimension_semantics=semantics, vmem_limit_bytes=VMEM_LIMIT)


def _rms(x, g):
    ms = jnp.mean(x * x, axis=-1, keepdims=True)
    return x * lax.rsqrt(ms + EPS) * g


def _swap_halves(z):
    return pltpu.roll(z, HEAD_DIM, 1)


def _cast_rows(in_refs, out_refs):
    for src, dst in zip(in_refs, out_refs):
        dst[...] = src[...].astype(BF16)


def _cast_specs(weights, steps, step_of=lambda i: i):
    for w in weights:
        assert w.shape[0] % (steps * 2 * SUBLANES) == 0
    specs = [pl.BlockSpec((w.shape[0] // steps, w.shape[1]), lambda *idx: (step_of(*idx), 0))
             for w in weights]
    return specs, [jax.ShapeDtypeStruct(w.shape, BF16) for w in weights]


def _proj_kernel(x_ref, g_ref, w_ref, c_ref, s1_ref, s2_ref, *refs):
    ncast = (len(refs) - 7) // 2
    u_ref, q_ref, kd_ref, vt_ref, utail_ref, ktail_ref, vtail_ref = refs[ncast:ncast + 7]
    _cast_rows(refs[:ncast], refs[ncast + 7:])
    tm = u_ref.shape[0]
    tail = ktail_ref.shape[0]
    x = x_ref[...].reshape(tm, D_MODEL)
    inv_rms = lax.rsqrt(jnp.mean(x * x, axis=-1, keepdims=True) + EPS)
    xg = (x * g_ref[...]).astype(BF16)
    k0 = POOL_WIDTH + ATTN_WIDTH
    project = lambda c0, c1: jnp.dot(xg, w_ref[:, c0:c1], preferred_element_type=F32)
    raw_kv, raw_q, raw_u = project(k0, IN_WIDTH), project(POOL_WIDTH, k0), project(0, POOL_WIDTH)
    c, s1, s2 = c_ref[...], s1_ref[...], s2_ref[...]

    def rope(z):
        return (z * c + pltpu.roll(z, LANES - ROPE_HALF, 1) * s1
                + pltpu.roll(z, ROPE_HALF, 1) * s2)

    low_half = lax.broadcasted_iota(jnp.int32, (1, LANES), 1) < HEAD_DIM
    for a in range(KV_WIDTH // LANES):
        z = rope(raw_kv[:, a * LANES:(a + 1) * LANES] * inv_rms)
        ktail_ref[:, a * LANES:(a + 1) * LANES] = z[tm - tail:, :]
        zr = _swap_halves(z)
        kd_ref[:, (2 * a) * LANES:(2 * a + 1) * LANES] = jnp.where(low_half, z, zr).astype(BF16)
        kd_ref[:, (2 * a + 1) * LANES:(2 * a + 2) * LANES] = jnp.where(low_half, zr, z).astype(BF16)
    v = raw_kv[:, KV_WIDTH:] * inv_rms
    vtail_ref[...] = v[tm - tail:, :]
    vt_ref[...] = v.T.astype(BF16)
    for col in range(ATTN_WIDTH // LANES):
        z = raw_q[:, col * LANES:(col + 1) * LANES] * inv_rms
        q_ref[:, col * LANES:(col + 1) * LANES] = (rope(z) * Q_SCALE).astype(BF16)
    u = raw_u * inv_rms
    u_ref[...] = u
    utail_ref[...] = u[tm - HALO:, :]


def _token_block_spec(a, tm):
    per_seq = a.shape[1]
    assert tm % per_seq == 0
    return pl.BlockSpec((tm // per_seq, per_seq, a.shape[2]), lambda i, *rest: (i, 0, 0))


def _proj(x, g, w_in, tabs, tiles_per_seq, tail, cast=()):
    t = x.size // D_MODEL
    tm = TOKEN_TILE
    steps = t // tm
    nseq = steps // tiles_per_seq
    row = lambda w: pl.BlockSpec((tm, w), lambda i: (i, 0))
    x_spec = row(D_MODEL) if x.ndim == 2 else _token_block_spec(x, tm)
    tab = pl.BlockSpec((tm, LANES), lambda i: (i % tiles_per_seq, 0))
    const = lambda a: pl.BlockSpec(a.shape, lambda i: (0,) * a.ndim, pipeline_mode=pl.Buffered(1))
    per_seq = lambda r, w: pl.BlockSpec((r, w), lambda i: (i // tiles_per_seq, 0))
    cast_specs, cast_shapes = _cast_specs(cast, steps)
    return pl.pallas_call(
        _proj_kernel,
        grid=(steps,),
        in_specs=[x_spec, const(g), const(w_in), tab, tab, tab] + cast_specs,
        out_specs=[row(POOL_WIDTH), row(ATTN_WIDTH), row(2 * KV_WIDTH),
                   pl.BlockSpec((KV_WIDTH, tm), lambda i: (0, i)),
                   per_seq(HALO, POOL_WIDTH), per_seq(tail, KV_WIDTH), per_seq(tail, KV_WIDTH)]
        + cast_specs,
        out_shape=[jax.ShapeDtypeStruct((t, POOL_WIDTH), F32),
                   jax.ShapeDtypeStruct((t, ATTN_WIDTH), BF16),
                   jax.ShapeDtypeStruct((t, 2 * KV_WIDTH), BF16),
                   jax.ShapeDtypeStruct((KV_WIDTH, t), BF16),
                   jax.ShapeDtypeStruct((nseq * HALO, POOL_WIDTH), F32),
                   jax.ShapeDtypeStruct((nseq * tail, KV_WIDTH), F32),
                   jax.ShapeDtypeStruct((nseq * tail, KV_WIDTH), F32)] + cast_shapes,
        compiler_params=_params("arbitrary"),
        name="proj",
    )(x, g, w_in, *tabs, *cast)


def _mix_prompt_kernel(sink_ref, u_ref, halo_ref, q_ref, kdc_ref, kdp_ref, vtc_ref, vtp_ref,
                       wpool_ref, pscale_ref, x_ref, wout_ref, gffn_ref, *refs, tiles_per_seq):
    ncast = (len(refs) - 8) // 2
    x1_ref, h2_ref = refs[ncast:ncast + 2]
    ext_ref, lvl_ref, kwin_ref, vtwin_ref, bias_ref, mixbuf_ref = refs[2 * ncast + 2:]
    _cast_rows(refs[:ncast], refs[ncast + 2:2 * ncast + 2])
    tq = u_ref.shape[0]
    step = pl.program_id(0)
    last_step = pl.num_programs(0) - 1
    i = lax.rem(step, tiles_per_seq)
    first = i == 0
    mix_ref = mixbuf_ref.at[lax.rem(step, 2)]
    mix_prev_ref = mixbuf_ref.at[lax.rem(step + 1, 2)]

    def out_projection(piece):
        r, c = divmod(piece, OUT_COL_PIECES)
        rows = slice(r * (tq // 2), (r + 1) * (tq // 2))
        cols = slice(c * (D_MODEL // OUT_COL_PIECES), (c + 1) * (D_MODEL // OUT_COL_PIECES))
        x1_ref[rows, cols] = x_ref[rows, cols] + jnp.dot(mix_prev_ref[rows, :], wout_ref[:, cols],
                                                         preferred_element_type=F32)
        if c == OUT_COL_PIECES - 1:
            h2_ref[rows, :] = _rms(x1_ref[rows, :], gffn_ref[...]).astype(BF16)

    top = POOL_TOP
    nlive = top + tq - SUBLANES
    pos1 = i * tq + lax.broadcasted_iota(jnp.int32, (tq, 1), 0) + 1
    kj = lax.broadcasted_iota(jnp.int32, (2 * BLOCK, BLOCK), 0)
    qi = lax.broadcasted_iota(jnp.int32, (2 * BLOCK, BLOCK), 1)
    band = (kj >= qi) & (kj <= qi + WINDOW)
    first_lo = jnp.where(first, BLOCK, 0)

    def fill_mixing_scratch():
        ext_ref[0:top - HALO, :] = jnp.zeros((top - HALO, POOL_WIDTH), F32)
        ext_ref[top - HALO:top, :] = halo_ref[...] * jnp.where(first, 0.0, 1.0)
        ext_ref[top:, :] = u_ref[...]
        lvl_ref[:, 0:SUBLANES, :] = jnp.zeros((2, SUBLANES, POOL_GROUP), F32)
        kwin_ref[0:BLOCK, :] = kdp_ref[...]
        kwin_ref[BLOCK:, :] = kdc_ref[...]
        vtwin_ref[:, 0:BLOCK] = vtp_ref[...]
        vtwin_ref[:, BLOCK:] = vtc_ref[...]
        bias_ref[0] = jnp.where(band, 0.0, -jnp.inf)
        bias_ref[1] = jnp.where(band & (kj >= first_lo), 0.0, -jnp.inf)

    def pool_group(gi):
        w = POOL_WINDOWS[gi]
        cols = slice(gi * POOL_GROUP, (gi + 1) * POOL_GROUP)
        src = ext_ref.at[:, cols]
        shift, slot = 1, 0
        while shift < w:
            dst = lvl_ref.at[slot]
            dst[SUBLANES:, :] = (src[SUBLANES:SUBLANES + nlive, :]
                                 + src[SUBLANES - shift:SUBLANES - shift + nlive, :])
            src, shift, slot = dst, 2 * shift, 1 - slot
        tok = ext_ref[top:, cols]
        inv_cnt = 1.0 / jnp.minimum(pos1, w).astype(F32)
        d = (src[top:, :] * inv_cnt - tok).astype(BF16)
        po = jnp.dot(d, wpool_ref[gi], preferred_element_type=F32) * pscale_ref[:, cols]
        mix_ref[:, cols] = po.astype(BF16)

    lane = lax.broadcasted_iota(jnp.int32, (1, LANES), 1)
    keep_half = [(lane // HEAD_DIM == hh).astype(BF16) for hh in range(HEADS_PER_COL)]

    def scores(n, g):
        rows = slice(n * BLOCK, (n + 1) * BLOCK)
        qs = jnp.concatenate(
            [q_ref[rows, (h // HEADS_PER_COL) * LANES:(h // HEADS_PER_COL + 1) * LANES]
             * keep_half[h % HEADS_PER_COL]
             for h in range(GQA_GROUP * g, GQA_GROUP * (g + 1))], axis=0)
        kd = kwin_ref[n * BLOCK:(n + 2) * BLOCK, g * LANES:(g + 1) * LANES]
        return lax.dot_general(kd, qs, (((1,), (1,)), ((), ())), preferred_element_type=F32)

    def attend(n, g, s_t):
        p_t = []
        for j in range(GQA_GROUP):
            s = s_t[:, j * BLOCK:(j + 1) * BLOCK] + bias_ref[1 if n == 0 else 0]
            m = jnp.max(s, axis=0, keepdims=True)
            p = jnp.exp2(s - m)
            den = jnp.sum(p, axis=0, keepdims=True) + jnp.exp2(sink_ref[GQA_GROUP * g + j] * LOG2E - m)
            p_t.append((p * (1.0 / den)).astype(BF16))
        vt = vtwin_ref[g * HEAD_DIM:(g + 1) * HEAD_DIM, n * BLOCK:(n + 2) * BLOCK]
        o_t = jnp.dot(vt, jnp.concatenate(p_t, axis=1), preferred_element_type=F32)
        for c in range(GQA_GROUP // HEADS_PER_COL):
            col_t = jnp.concatenate(
                [o_t[:, (HEADS_PER_COL * c + hh) * BLOCK:(HEADS_PER_COL * c + hh + 1) * BLOCK]
                 for hh in range(HEADS_PER_COL)], axis=0)
            col = POOL_WIDTH + (GQA_GROUP // HEADS_PER_COL * g + c) * LANES
            mix_ref[n * BLOCK:(n + 1) * BLOCK, col:col + LANES] = col_t.T.astype(BF16)

    work = [(n, g) for n in range(tq // BLOCK) for g in range(N_KV_HEADS)]

    def run(mix, project):
        pieces = list(range(2 * OUT_COL_PIECES)) if project else []
        if not mix:
            for piece in pieces:
                out_projection(piece)
            return
        groups = list(range(N_POOL_GROUPS))
        fill_mixing_scratch()
        s_next = scores(*work[0])
        for idx, (n, g) in enumerate(work):
            s_t = s_next
            if idx + 1 < len(work):
                s_next = scores(*work[idx + 1])
            if idx % 2 == 0 and pieces:
                out_projection(pieces.pop(0))
            attend(n, g, s_t)
            if idx % 4 == 1:
                pool_group(groups.pop(0))
        assert not pieces and not groups

    pl.when(step == 0)(lambda: run(True, False))
    pl.when((step > 0) & (step < last_step))(lambda: run(True, True))
    pl.when(step == last_step)(lambda: run(False, True))


def _mix_prompt(sinks, u, q, kd, vt, wpool, pscale, x, w_out, g_ffn, batch, seq, cast=()):
    tq = TOKEN_TILE
    nt = seq // tq
    ntiles = batch * nt
    bpt = tq // BLOCK
    hpt = tq // HALO
    tile = lambda s: jnp.minimum(s, ntiles - 1)
    prev_block = lambda s: jnp.maximum(tile(s) * bpt - 1, 0)
    row = lambda w: pl.BlockSpec((tq, w), lambda s: (tile(s), 0))
    late_row = lambda w: pl.BlockSpec((tq, w), lambda s: (jnp.maximum(s - 1, 0), 0))
    halo = pl.BlockSpec((HALO, POOL_WIDTH), lambda s: (jnp.maximum(tile(s) * hpt - 1, 0), 0))
    const = lambda a: pl.BlockSpec(a.shape, lambda s: (0,) * a.ndim, pipeline_mode=pl.Buffered(1))
    cast_specs, cast_shapes = _cast_specs(cast, ntiles, tile)
    return pl.pallas_call(
        functools.partial(_mix_prompt_kernel, tiles_per_seq=nt),
        grid=(ntiles + 1,),
        in_specs=[pl.BlockSpec(memory_space=pltpu.SMEM), row(POOL_WIDTH), halo, row(ATTN_WIDTH),
                  row(2 * KV_WIDTH),
                  pl.BlockSpec((BLOCK, 2 * KV_WIDTH), lambda s: (prev_block(s), 0)),
                  pl.BlockSpec((KV_WIDTH, tq), lambda s: (0, tile(s))),
                  pl.BlockSpec((KV_WIDTH, BLOCK), lambda s: (0, prev_block(s))),
                  const(wpool), const(pscale), late_row(D_MODEL), const(w_out), const(g_ffn)]
        + cast_specs,
        out_specs=[late_row(D_MODEL), late_row(D_MODEL)] + cast_specs,
        out_shape=[jax.ShapeDtypeStruct((batch * seq, D_MODEL), F32),
                   jax.ShapeDtypeStruct((batch * seq, D_MODEL), BF16)] + cast_shapes,
        scratch_shapes=[pltpu.VMEM((POOL_TOP + tq, POOL_WIDTH), F32),
                        pltpu.VMEM((2, POOL_TOP + tq, POOL_GROUP), F32),
                        pltpu.VMEM((BLOCK + tq, 2 * KV_WIDTH), BF16),
                        pltpu.VMEM((KV_WIDTH, BLOCK + tq), BF16),
                        pltpu.VMEM((2, 2 * BLOCK, BLOCK), F32),
                        pltpu.VMEM((2, tq, MIX_WIDTH), BF16)],
        compiler_params=_params("arbitrary"),
        name="mix_prompt",
    )(sinks, u, u, q, kd, kd, vt, vt, wpool, pscale, x, w_out, g_ffn, *cast)


def _mix_sample_t_kernel(sink_ref, u_ref, q_ref, kn_ref, vn_ref, st_ref, ckt_ref, cvt_ref,
                         wpool_ref, pscale_ref, mix_ref, npool_ref, nkt_ref, nvt_ref, *, dec_seq):
    sb = SEQ_BLOCK
    rows_per_seq = N_HEADS * dec_seq
    nrow = sb * rows_per_seq
    ntok = sb * dec_seq
    win = ckt_ref.shape[3]
    pair_rows = HEADS_PER_COL * GQA_GROUP * dec_seq
    assert win == LANES and ntok <= LANES

    ext = [st_ref[h] for h in range(POOL_HIST)]
    ext += [u_ref[t] for t in range(dec_seq)]
    for h in range(POOL_HIST):
        npool_ref[h] = ext[h + dec_seq]
    r_out = lax.broadcasted_iota(jnp.int32, (ntok, ntok), 0)
    r_in = lax.broadcasted_iota(jnp.int32, (ntok, ntok), 1)
    to_seq_major = ((r_in % sb) * dec_seq + r_in // sb == r_out).astype(BF16)
    for gi, w in enumerate(POOL_WINDOWS):
        cols = slice(gi * POOL_GROUP, (gi + 1) * POOL_GROUP)
        ds = []
        for t in range(dec_seq):
            tok = ext[POOL_HIST + t][:, cols]
            acc = tok
            for j in range(1, w):
                acc = acc + ext[POOL_HIST + t - j][:, cols]
            ds.append(acc * (1.0 / w) - tok)
        d = jnp.concatenate(ds, axis=0).astype(BF16)
        po = jnp.dot(d, wpool_ref[gi], preferred_element_type=F32) * pscale_ref[:, cols]
        po = jnp.dot(to_seq_major, po.astype(BF16), preferred_element_type=F32)
        mix_ref[:, cols] = po.astype(BF16)

    pad_rows = jnp.zeros((LANES - ntok, KV_WIDTH), F32)
    k_new = jnp.concatenate([kn_ref[...], pad_rows], axis=0)
    v_new = jnp.concatenate([vn_ref[...], pad_rows], axis=0)
    k_new_t, v_new_t = k_new.T, v_new.T
    newest = lax.broadcasted_iota(jnp.int32, (HEAD_DIM, LANES), 1) >= win - dec_seq
    for b in range(sb):
        for g in range(N_KV_HEADS):
            dims = slice(g * HEAD_DIM, (g + 1) * HEAD_DIM)
            for old_ref, new_t, out_ref in ((ckt_ref, k_new_t, nkt_ref), (cvt_ref, v_new_t, nvt_ref)):
                kept = pltpu.roll(old_ref[b, g], win - dec_seq, 1)
                fresh = pltpu.roll(new_t[dims, :], win - dec_seq - b * dec_seq, 1)
                out_ref[b, g] = jnp.where(newest, fresh, kept)

    nhalf = 2
    hseq, hrow, htok = sb // nhalf, nrow // nhalf, ntok // nhalf
    r = lax.broadcasted_iota(jnp.int32, (hrow, htok), 0)
    c = lax.broadcasted_iota(jnp.int32, (hrow, htok), 1)
    pick = (c == (r // rows_per_seq) * dec_seq + r % dec_seq).astype(BF16)
    rt = lax.broadcasted_iota(jnp.int32, (htok, hrow), 0)
    ct = lax.broadcasted_iota(jnp.int32, (htok, hrow), 1)
    unpick = (rt == (ct // rows_per_seq) * dec_seq + ct % dec_seq).astype(BF16)
    row = lax.broadcasted_iota(jnp.int32, (hrow, LANES), 0)
    lane = lax.broadcasted_iota(jnp.int32, (hrow, LANES), 1)
    row_seq = row // rows_per_seq
    row_g = (row % rows_per_seq) // (GQA_GROUP * dec_seq)
    row_j = (row // dec_seq) % GQA_GROUP
    row_t = row % dec_seq
    lane_half = lane // HEAD_DIM
    own_half = lane_half == row_g % HEADS_PER_COL
    first_pair = row_g < HEADS_PER_COL
    cache_ok = lane >= row_t + (win - WINDOW)
    k_new_b, v_new_b = k_new.astype(BF16), v_new.astype(BF16)

    def new_ok(hf):
        return (lane // dec_seq == row_seq + hf * hseq) & (lane % dec_seq <= row_t)

    def build_lhs(hf):
        qrep = jnp.dot(pick, q_ref[hf * htok:(hf + 1) * htok, :], preferred_element_type=F32)
        lhs = jnp.zeros((hrow, LANES), F32)
        for qc in range(ATTN_WIDTH // LANES):
            g = qc * HEADS_PER_COL // GQA_GROUP
            src = qrep[:, qc * LANES:(qc + 1) * LANES]
            src_swapped = _swap_halves(src)
            for hh in range(HEADS_PER_COL):
                j = (qc * HEADS_PER_COL + hh) % GQA_GROUP
                here = (row_g == g) & (row_j == j) & own_half
                lhs = lhs + jnp.where(here, src if hh == g % HEADS_PER_COL else src_swapped, 0.0)
        return lhs.astype(BF16)

    def pair_tile(ref, hf, b, a):
        return ref[hf * hseq + b, HEADS_PER_COL * a:HEADS_PER_COL * (a + 1)].reshape(LANES, win).astype(BF16)

    def score(hf, lhs):
        s_cache = jnp.concatenate(
            [jnp.dot(lhs[b * rows_per_seq + a * pair_rows:b * rows_per_seq + (a + 1) * pair_rows],
                     pair_tile(ckt_ref, hf, b, a), preferred_element_type=F32)
             for b in range(hseq) for a in range(N_KV_HEADS // HEADS_PER_COL)], axis=0)
        s_new = [lax.dot_general(lhs, k_new_b[:, a * LANES:(a + 1) * LANES], (((1,), (1,)), ((), ())),
                                 preferred_element_type=F32) for a in range(N_KV_HEADS // HEADS_PER_COL)]
        return s_cache, jnp.where(first_pair, s_new[0], s_new[1])

    def weigh(hf, scores):
        s_cache = jnp.where(cache_ok, scores[0], -jnp.inf)
        s_new = jnp.where(new_ok(hf), scores[1], -jnp.inf)
        sink = sink_ref[hf * hrow:(hf + 1) * hrow, :] * LOG2E
        m = jnp.maximum(jnp.maximum(jnp.max(s_cache, axis=1, keepdims=True),
                                    jnp.max(s_new, axis=1, keepdims=True)), sink)
        p_cache, p_new = jnp.exp2(s_cache - m), jnp.exp2(s_new - m)
        den = (jnp.sum(p_cache, axis=1, keepdims=True) + jnp.sum(p_new, axis=1, keepdims=True)
               + jnp.exp2(sink - m))
        return (p_cache / den).astype(BF16), (p_new / den).astype(BF16)

    def gather_values(hf, probs):
        p_cache, p_new = probs
        o_cache = jnp.concatenate(
            [lax.dot_general(p_cache[b * rows_per_seq + a * pair_rows:b * rows_per_seq + (a + 1) * pair_rows],
                             pair_tile(cvt_ref, hf, b, a), (((1,), (1,)), ((), ())),
                             preferred_element_type=F32)
             for b in range(hseq) for a in range(N_KV_HEADS // HEADS_PER_COL)], axis=0)
        o_new = [jnp.dot(p_new, v_new_b[:, a * LANES:(a + 1) * LANES], preferred_element_type=F32)
                 for a in range(N_KV_HEADS // HEADS_PER_COL)]
        return o_cache + jnp.where(first_pair, o_new[0], o_new[1])

    def emit(hf, o):
        o_swapped = _swap_halves(o)
        z_cols = []
        for qc in range(ATTN_WIDTH // LANES):
            g = qc * HEADS_PER_COL // GQA_GROUP
            zc = jnp.zeros((hrow, LANES), F32)
            for hh in range(HEADS_PER_COL):
                j = (qc * HEADS_PER_COL + hh) % GQA_GROUP
                here = (row_g == g) & (row_j == j) & (lane_half == hh)
                zc = zc + jnp.where(here, o if hh == g % HEADS_PER_COL else o_swapped, 0.0)
            z_cols.append(zc)
        z = jnp.concatenate(z_cols, axis=1).astype(BF16)
        attn = jnp.dot(unpick, z, preferred_element_type=F32)
        mix_ref[hf * htok:(hf + 1) * htok, POOL_WIDTH:] = attn.astype(BF16)

    state = [build_lhs(hf) for hf in range(nhalf)]
    for stage in (score, weigh, gather_values, emit):
        state = [stage(hf, x) for hf, x in enumerate(state)]


def _mix_sample_t(sink_rows, u, q, k_new, v_new, state_t, cache_kt, cache_vt, wpool, pscale, dec_seq):
    nseq = cache_kt.shape[0]
    sb = SEQ_BLOCK
    ntok = sb * dec_seq
    row = lambda w: pl.BlockSpec((ntok, w), lambda i: (i, 0))
    slab = pl.BlockSpec((POOL_HIST, sb, POOL_WIDTH), lambda i: (0, i, 0))
    u_slab = pl.BlockSpec((dec_seq, sb, POOL_WIDTH), lambda i: (0, i, 0))
    cache = pl.BlockSpec((sb,) + cache_kt.shape[1:], lambda i: (i, 0, 0, 0))
    full = lambda a: pl.BlockSpec(a.shape, lambda i: (0,) * a.ndim)
    return pl.pallas_call(
        functools.partial(_mix_sample_t_kernel, dec_seq=dec_seq),
        grid=(nseq // sb,),
        in_specs=[full(sink_rows), u_slab, row(ATTN_WIDTH), row(KV_WIDTH), row(KV_WIDTH),
                  slab, cache, cache, full(wpool), full(pscale)],
        out_specs=[row(MIX_WIDTH), slab, cache, cache],
        out_shape=[jax.ShapeDtypeStruct((nseq * dec_seq, MIX_WIDTH), BF16),
                   jax.ShapeDtypeStruct((POOL_HIST, nseq, POOL_WIDTH), F32),
                   jax.ShapeDtypeStruct(cache_kt.shape, F32),
                   jax.ShapeDtypeStruct(cache_vt.shape, F32)],
        compiler_params=_params("parallel"),
        name="mix_sample",
    )(sink_rows, u, q, k_new, v_new, state_t, cache_kt, cache_vt, wpool, pscale)


def _ffn_kernel(*refs, final_norm, fused_outproj):
    if fused_outproj:
        (x_ref, mix_ref, wout_ref, gffn_ref, wg_ref, wu_ref, wd_ref, gf_ref, y_ref,
         x1_buf, h2_ref, acc_ref) = refs
    else:
        x1_hbm, h2_ref, wg_ref, wu_ref, wd_ref, gf_ref, y_ref, x1_buf, sem = refs
        acc_ref = y_ref
    i, j = pl.program_id(0), pl.program_id(1)
    tm = x1_buf.shape[0]
    last = pl.num_programs(1) - 1

    def activation():
        h = h2_ref[...]
        gate = jnp.dot(h, wg_ref[...], preferred_element_type=F32)
        up = jnp.dot(h, wu_ref[...], preferred_element_type=F32)
        return (jax.nn.silu(gate) * up).astype(BF16)

    @pl.when(j == 0)
    def _():
        if fused_outproj:
            x1 = x_ref[...].reshape(tm, D_MODEL) + jnp.dot(mix_ref[...], wout_ref[...],
                                                           preferred_element_type=F32)
            x1_buf[...] = x1
            h2_ref[...] = _rms(x1, gffn_ref[...]).astype(BF16)
        else:
            _ffn_residual_copy(x1_hbm, x1_buf, sem, i, tm).start()
        acc_ref[...] = jnp.dot(activation(), wd_ref[...], preferred_element_type=F32)

    @pl.when((j > 0) & (j < last))
    def _():
        acc_ref[...] += jnp.dot(activation(), wd_ref[...], preferred_element_type=F32)

    @pl.when(j == last)
    def _():
        if not fused_outproj:
            _ffn_residual_copy(x1_hbm, x1_buf, sem, i, tm).wait()
        act = activation()
        for r in range(FFN_FINAL_BLOCKS):
            rows = slice(r * tm // FFN_FINAL_BLOCKS, (r + 1) * tm // FFN_FINAL_BLOCKS)
            y = (x1_buf[rows, :] + acc_ref[rows, :]
                 + jnp.dot(act[rows, :], wd_ref[...], preferred_element_type=F32))
            y = _rms(y, gf_ref[...]) if final_norm else y
            if fused_outproj:
                per_seq = y_ref.shape[1]
                y_ref[rows.start // per_seq:rows.stop // per_seq] = y.reshape(-1, per_seq, D_MODEL)
            else:
                y_ref[rows, :] = y


def _ffn(x1, h2, wg, wu, wd, gf, final_norm, tm):
    return _ffn_call([x1, h2], [pl.BlockSpec(memory_space=pl.ANY),
                                pl.BlockSpec((tm, D_MODEL), lambda i, j: (i, 0))],
                     [pltpu.VMEM((tm, D_MODEL), F32), pltpu.SemaphoreType.DMA(())],
                     wg, wu, wd, gf, final_norm, tm, fused_outproj=False)


def _outproj_ffn(x, mix, w_out, g_ffn, wg, wu, wd, gf, final_norm, tm):
    const = lambda a: pl.BlockSpec(a.shape, lambda i, j: (0,) * a.ndim, pipeline_mode=pl.Buffered(1))
    return _ffn_call([x, mix, w_out, g_ffn],
                     [_token_block_spec(x, tm), pl.BlockSpec((tm, MIX_WIDTH), lambda i, j: (i, 0)),
                      const(w_out), const(g_ffn)],
                     [pltpu.VMEM((tm, D_MODEL), F32), pltpu.VMEM((tm, D_MODEL), BF16),
                      pltpu.VMEM((tm, D_MODEL), F32)],
                     wg, wu, wd, gf, final_norm, tm, fused_outproj=True)


def _ffn_residual_copy(x1_hbm, x1_buf, sem, i, tm):
    return pltpu.make_async_copy(x1_hbm.at[pl.ds(pl.multiple_of(i * tm, tm), tm)], x1_buf, sem)


def _ffn_call(lead_args, lead_specs, scratch, wg, wu, wd, gf, final_norm, tm, fused_outproj):
    x = lead_args[0]
    tf = FF_TILE
    out_spec = (pl.BlockSpec((tm, D_MODEL), lambda i, j: (i, 0)) if x.ndim == 2
                else _token_block_spec(x, tm))
    return pl.pallas_call(
        functools.partial(_ffn_kernel, final_norm=final_norm, fused_outproj=fused_outproj),
        grid=(x.size // D_MODEL // tm, D_FF // tf),
        in_specs=lead_specs + [pl.BlockSpec((D_MODEL, tf), lambda i, j: (0, j)),
                               pl.BlockSpec((D_MODEL, tf), lambda i, j: (0, j)),
                               pl.BlockSpec((tf, D_MODEL), lambda i, j: (j, 0)),
                               pl.BlockSpec(gf.shape, lambda i, j: (0, 0))],
        out_specs=out_spec,
        out_shape=jax.ShapeDtypeStruct(x.shape, F32),
        scratch_shapes=scratch,
        compiler_params=_params("arbitrary", "arbitrary"),
        name="ffn",
    )(*lead_args, wg, wu, wd, gf)


def _rope_tables(pos):
    inv = ROPE_THETA ** (-np.arange(0, ROPE_DIM, 2, dtype=np.float64) / ROPE_DIM)
    ang = np.asarray(pos, np.float64)[:, None] * inv[None, :]
    cos, sin = np.cos(ang), np.sin(ang)
    n = ang.shape[0]
    rest = HEAD_DIM - ROPE_DIM
    one, zero, zh = np.ones((n, rest)), np.zeros((n, rest)), np.zeros((n, ROPE_HALF))
    per_head = [np.concatenate(parts, axis=1) for parts in
                ([cos, cos, one], [-sin, zh, zero], [zh, sin, zero])]
    return [jnp.asarray(np.tile(t, (1, HEADS_PER_COL)), F32) for t in per_head]


def kernel(x_prompt, x_sample, state_pool, cache_k_win, cache_v_win, g_mix, w_in, w_pool,
           pool_scale, attn_sinks, w_out, g_ffn, w_gate, w_up, w_down, g_final):
    batch, seq, _ = x_prompt.shape
    nseq, dec_seq, _ = x_sample.shape
    depth = w_in.shape[0]
    win_s = cache_k_win.shape[2]
    ntok_s = nseq * dec_seq
    assert seq % TOKEN_TILE == 0 and ntok_s == TOKEN_TILE and nseq % SEQ_BLOCK == 0
    assert win_s == WINDOW and seq >= WINDOW
    assert PAST_LEN >= max(POOL_HIST, WINDOW)

    xp = x_prompt.reshape(batch * seq, D_MODEL)
    xs = x_sample
    tabs_p = _rope_tables(np.arange(seq))
    tabs_s = _rope_tables(PAST_LEN + np.arange(ntok_s) % dec_seq)
    gf = g_final.reshape(1, D_MODEL)

    outs = [[] for _ in range(6)]
    for l in range(depth):
        w_in_l, wpool_l = w_in[l].astype(BF16), w_pool[l].astype(BF16)
        gm = g_mix[l].reshape(1, D_MODEL)
        gn = g_ffn[l].reshape(1, D_MODEL)
        pscale = pool_scale[l].reshape(1, POOL_WIDTH)
        sink_rows = jnp.tile(jnp.repeat(attn_sinks[l], dec_seq), SEQ_BLOCK)[:, None]

        u, q, kd, vt, u_tail, k_tail, v_tail, w_out_l, wg_l, wd_l = _proj(
            xp, gm, w_in_l, tabs_p, seq // TOKEN_TILE, WINDOW, cast=(w_out[l], w_gate[l], w_down[l]))
        x1, h2, wu_l = _mix_prompt(attn_sinks[l], u, q, kd, vt, wpool_l, pscale, xp, w_out_l, gn,
                                   batch, seq, cast=(w_up[l],))
        xp = _ffn(x1, h2, wg_l, wu_l, wd_l, gf, l == depth - 1, FFN_TOKEN_TILE)
        outs[0].append(u_tail.reshape(batch, HALO, POOL_WIDTH)[:, HALO - POOL_HIST:])
        outs[1].append(k_tail.reshape(batch, WINDOW, N_KV_HEADS, HEAD_DIM))
        outs[2].append(v_tail.reshape(batch, WINDOW, N_KV_HEADS, HEAD_DIM))

        u, q, _, _, _, k_new, v_new = _proj(xs, gm, w_in_l, tabs_s, 1, TOKEN_TILE)
        state_t = jnp.transpose(state_pool[l], (1, 0, 2))
        ckt = jnp.transpose(cache_k_win[l], (0, 2, 3, 1))
        cvt = jnp.transpose(cache_v_win[l], (0, 2, 3, 1))
        u_t = jnp.transpose(u.reshape(nseq, dec_seq, POOL_WIDTH), (1, 0, 2))
        mix, npool, nkt, nvt = _mix_sample_t(sink_rows, u_t, q, k_new, v_new, state_t, ckt, cvt,
                                             wpool_l, pscale, dec_seq)
        xs = _outproj_ffn(xs, mix, w_out_l, gn, wg_l, wu_l, wd_l, gf, l == depth - 1,
                          min(FFN_TOKEN_TILE, ntok_s))
        outs[3].append(jnp.transpose(npool, (1, 0, 2)))
        outs[4].append(jnp.transpose(nkt, (0, 3, 1, 2)))
        outs[5].append(jnp.transpose(nvt, (0, 3, 1, 2)))

    y_prompt = xp.reshape(batch, seq, D_MODEL)
    y_sample = xs
    return (y_prompt, y_sample) + tuple(jnp.stack(o) for o in outs)
```

```python
import functools

import jax
import jax.numpy as jnp
import numpy as np
from jax import lax
from jax.experimental import pallas as pl
from jax.experimental.pallas import tpu as pltpu

F32 = jnp.float32
BF16 = jnp.bfloat16

D_MODEL = 2048
POOL_WIDTH = D_MODEL // 2
POOL_WINDOWS = (2, 4, 8, 16)
N_POOL_GROUPS = len(POOL_WINDOWS)
POOL_GROUP = POOL_WIDTH // N_POOL_GROUPS
POOL_HIST = max(POOL_WINDOWS) - 1
HEAD_DIM = 64
N_HEADS = (D_MODEL - POOL_WIDTH) // HEAD_DIM
N_KV_HEADS = 4
GQA_GROUP = N_HEADS // N_KV_HEADS
ATTN_WIDTH = N_HEADS * HEAD_DIM
KV_WIDTH = N_KV_HEADS * HEAD_DIM
IN_WIDTH = POOL_WIDTH + ATTN_WIDTH + 2 * KV_WIDTH
MIX_WIDTH = POOL_WIDTH + ATTN_WIDTH
WINDOW = 128
BLOCK = 128
ROPE_DIM = HEAD_DIM // 4
ROPE_HALF = ROPE_DIM // 2
ROPE_THETA = 500000.0
D_FF = ((8 * D_MODEL // 3 + 255) // 256) * 256
EPS = 1e-5
PAST_LEN = 16384
LOG2E = 1.4426950408889634
Q_SCALE = HEAD_DIM ** -0.5 * LOG2E

LANES = 128
SUBLANES = 8
HEADS_PER_COL = LANES // HEAD_DIM
HALO = 16
POOL_TOP = 2 * HALO
TOKEN_TILE = 512
FFN_TOKEN_TILE = 1024
FFN_FINAL_BLOCKS = 4
OUT_COL_PIECES = 4
FF_TILE = 512
SEQ_BLOCK = 16
VMEM_LIMIT = 60 * 1024 * 1024


def _params(*semantics):
    return pltpu.CompilerParams(dimension_semantics=semantics, vmem_limit_bytes=VMEM_LIMIT)


def _rms(x, g):
    ms = jnp.mean(x * x, axis=-1, keepdims=True)
    return x * lax.rsqrt(ms + EPS) * g


def _swap_halves(z):
    return pltpu.roll(z, HEAD_DIM, 1)


def _cast_rows(in_refs, out_refs):
    for src, dst in zip(in_refs, out_refs):
        dst[...] = src[...].astype(BF16)


def _cast_specs(weights, steps, step_of=lambda i: i):
    for w in weights:
        assert w.shape[0] % (steps * 2 * SUBLANES) == 0
    specs = [pl.BlockSpec((w.shape[0] // steps, w.shape[1]), lambda *idx: (step_of(*idx), 0))
             for w in weights]
    return specs, [jax.ShapeDtypeStruct(w.shape, BF16) for w in weights]


def _proj_kernel(x_ref, g_ref, w_ref, c_ref, s1_ref, s2_ref, *refs):
    if w_ref.dtype == F32:
        *refs, wb_ref = refs
        wb_ref[...] = w_ref[...].astype(BF16)
        w_ref = wb_ref
    ncast = (len(refs) - 7) // 2
    u_ref, q_ref, kd_ref, vt_ref, utail_ref, ktail_ref, vtail_ref = refs[ncast:ncast + 7]
    _cast_rows(refs[:ncast], refs[ncast + 7:])
    tm = u_ref.shape[0]
    tail = ktail_ref.shape[0]
    x = x_ref[...].reshape(tm, D_MODEL)
    inv_rms = lax.rsqrt(jnp.mean(x * x, axis=-1, keepdims=True) + EPS)
    xg = (x * g_ref[...]).astype(BF16)
    k0 = POOL_WIDTH + ATTN_WIDTH
    project = lambda c0, c1: jnp.dot(xg, w_ref[:, c0:c1], preferred_element_type=F32)
    raw_kv, raw_q, raw_u = project(k0, IN_WIDTH), project(POOL_WIDTH, k0), project(0, POOL_WIDTH)
    c, s1, s2 = c_ref[...], s1_ref[...], s2_ref[...]

    def rope(z):
        return (z * c + pltpu.roll(z, LANES - ROPE_HALF, 1) * s1
                + pltpu.roll(z, ROPE_HALF, 1) * s2)

    low_half = lax.broadcasted_iota(jnp.int32, (1, LANES), 1) < HEAD_DIM
    for a in range(KV_WIDTH // LANES):
        z = rope(raw_kv[:, a * LANES:(a + 1) * LANES] * inv_rms)
        ktail_ref[:, a * LANES:(a + 1) * LANES] = z[tm - tail:, :]
        zr = _swap_halves(z)
        kd_ref[:, (2 * a) * LANES:(2 * a + 1) * LANES] = jnp.where(low_half, z, zr).astype(BF16)
        kd_ref[:, (2 * a + 1) * LANES:(2 * a + 2) * LANES] = jnp.where(low_half, zr, z).astype(BF16)
    v = raw_kv[:, KV_WIDTH:] * inv_rms
    vtail_ref[...] = v[tm - tail:, :]
    vt_ref[...] = v.T.astype(BF16)
    for col in range(ATTN_WIDTH // LANES):
        z = raw_q[:, col * LANES:(col + 1) * LANES] * inv_rms
        q_ref[:, col * LANES:(col + 1) * LANES] = (rope(z) * Q_SCALE).astype(BF16)
    u = raw_u * inv_rms
    u_ref[...] = u
    utail_ref[...] = u[tm - HALO:, :]


def _token_block_spec(a, tm):
    per_seq = a.shape[1]
    assert tm % per_seq == 0
    return pl.BlockSpec((tm // per_seq, per_seq, a.shape[2]), lambda i, *rest: (i, 0, 0))


def _proj(x, g, w_in, tabs, tiles_per_seq, tail, cast=()):
    t = x.size // D_MODEL
    tm = TOKEN_TILE
    steps = t // tm
    nseq = steps // tiles_per_seq
    row = lambda w: pl.BlockSpec((tm, w), lambda i: (i, 0))
    x_spec = row(D_MODEL) if x.ndim == 2 else _token_block_spec(x, tm)
    tab = pl.BlockSpec((tm, LANES), lambda i: (i % tiles_per_seq, 0))
    const = lambda a: pl.BlockSpec(a.shape, lambda i: (0,) * a.ndim, pipeline_mode=pl.Buffered(1))
    per_seq = lambda r, w: pl.BlockSpec((r, w), lambda i: (i // tiles_per_seq, 0))
    cast_specs, cast_shapes = _cast_specs(cast, steps)
    if w_in.dtype == F32:
        assert steps == 1
        cast_specs = cast_specs + [pl.BlockSpec(w_in.shape, lambda i: (0, 0))]
        cast_shapes = cast_shapes + [jax.ShapeDtypeStruct(w_in.shape, BF16)]
    return pl.pallas_call(
        _proj_kernel,
        grid=(steps,),
        in_specs=[x_spec, const(g), const(w_in), tab, tab, tab] + cast_specs[:len(cast)],
        out_specs=[row(POOL_WIDTH), row(ATTN_WIDTH), row(2 * KV_WIDTH),
                   pl.BlockSpec((KV_WIDTH, tm), lambda i: (0, i)),
                   per_seq(HALO, POOL_WIDTH), per_seq(tail, KV_WIDTH), per_seq(tail, KV_WIDTH)]
        + cast_specs,
        out_shape=[jax.ShapeDtypeStruct((t, POOL_WIDTH), F32),
                   jax.ShapeDtypeStruct((t, ATTN_WIDTH), BF16),
                   jax.ShapeDtypeStruct((t, 2 * KV_WIDTH), BF16),
                   jax.ShapeDtypeStruct((KV_WIDTH, t), BF16),
                   jax.ShapeDtypeStruct((nseq * HALO, POOL_WIDTH), F32),
                   jax.ShapeDtypeStruct((nseq * tail, KV_WIDTH), F32),
                   jax.ShapeDtypeStruct((nseq * tail, KV_WIDTH), F32)] + cast_shapes,
        compiler_params=_params("arbitrary"),
        name="proj",
    )(x, g, w_in, *tabs, *cast)


def _mix_prompt_kernel(sink_ref, u_ref, halo_ref, q_ref, kdc_ref, kdp_ref, vtc_ref, vtp_ref,
                       wpool_ref, pscale_ref, x_ref, wout_ref, gffn_ref, *refs, tiles_per_seq):
    ncast = (len(refs) - 8) // 2
    x1_ref, h2_ref = refs[ncast:ncast + 2]
    ext_ref, lvl_ref, kwin_ref, vtwin_ref, bias_ref, mixbuf_ref = refs[2 * ncast + 2:]
    _cast_rows(refs[:ncast], refs[ncast + 2:2 * ncast + 2])
    tq = u_ref.shape[0]
    step = pl.program_id(0)
    last_step = pl.num_programs(0) - 1
    i = lax.rem(step, tiles_per_seq)
    first = i == 0
    mix_ref = mixbuf_ref.at[lax.rem(step, 2)]
    mix_prev_ref = mixbuf_ref.at[lax.rem(step + 1, 2)]

    def out_projection(piece):
        r, c = divmod(piece, OUT_COL_PIECES)
        rows = slice(r * (tq // 2), (r + 1) * (tq // 2))
        cols = slice(c * (D_MODEL // OUT_COL_PIECES), (c + 1) * (D_MODEL // OUT_COL_PIECES))
        x1_ref[rows, cols] = x_ref[rows, cols] + jnp.dot(mix_prev_ref[rows, :], wout_ref[:, cols],
                                                         preferred_element_type=F32)
        if c == OUT_COL_PIECES - 1:
            h2_ref[rows, :] = _rms(x1_ref[rows, :], gffn_ref[...]).astype(BF16)

    top = POOL_TOP
    nlive = top + tq - SUBLANES
    pos1 = i * tq + lax.broadcasted_iota(jnp.int32, (tq, 1), 0) + 1
    kj = lax.broadcasted_iota(jnp.int32, (2 * BLOCK, BLOCK), 0)
    qi = lax.broadcasted_iota(jnp.int32, (2 * BLOCK, BLOCK), 1)
    band = (kj >= qi) & (kj <= qi + WINDOW)
    first_lo = jnp.where(first, BLOCK, 0)

    def fill_mixing_scratch():
        ext_ref[0:top - HALO, :] = jnp.zeros((top - HALO, POOL_WIDTH), F32)
        ext_ref[top - HALO:top, :] = halo_ref[...] * jnp.where(first, 0.0, 1.0)
        ext_ref[top:, :] = u_ref[...]
        lvl_ref[:, 0:SUBLANES, :] = jnp.zeros((2, SUBLANES, POOL_GROUP), F32)
        kwin_ref[0:BLOCK, :] = kdp_ref[...]
        kwin_ref[BLOCK:, :] = kdc_ref[...]
        vtwin_ref[:, 0:BLOCK] = vtp_ref[...]
        vtwin_ref[:, BLOCK:] = vtc_ref[...]
        bias_ref[0] = jnp.where(band, 0.0, -jnp.inf)
        bias_ref[1] = jnp.where(band & (kj >= first_lo), 0.0, -jnp.inf)

    def pool_group(gi):
        w = POOL_WINDOWS[gi]
        cols = slice(gi * POOL_GROUP, (gi + 1) * POOL_GROUP)
        src = ext_ref.at[:, cols]
        shift, slot = 1, 0
        while shift < w:
            dst = lvl_ref.at[slot]
            dst[SUBLANES:, :] = (src[SUBLANES:SUBLANES + nlive, :]
                                 + src[SUBLANES - shift:SUBLANES - shift + nlive, :])
            src, shift, slot = dst, 2 * shift, 1 - slot
        tok = ext_ref[top:, cols]
        inv_cnt = 1.0 / jnp.minimum(pos1, w).astype(F32)
        d = (src[top:, :] * inv_cnt - tok).astype(BF16)
        po = jnp.dot(d, wpool_ref[gi], preferred_element_type=F32) * pscale_ref[:, cols]
        mix_ref[:, cols] = po.astype(BF16)

    lane = lax.broadcasted_iota(jnp.int32, (1, LANES), 1)
    keep_half = [(lane // HEAD_DIM == hh).astype(BF16) for hh in range(HEADS_PER_COL)]

    def scores(n, g):
        rows = slice(n * BLOCK, (n + 1) * BLOCK)
        qs = jnp.concatenate(
            [q_ref[rows, (h // HEADS_PER_COL) * LANES:(h // HEADS_PER_COL + 1) * LANES]
             * keep_half[h % HEADS_PER_COL]
             for h in range(GQA_GROUP * g, GQA_GROUP * (g + 1))], axis=0)
        kd = kwin_ref[n * BLOCK:(n + 2) * BLOCK, g * LANES:(g + 1) * LANES]
        return lax.dot_general(kd, qs, (((1,), (1,)), ((), ())), preferred_element_type=F32)

    def attend(n, g, s_t):
        p_t = []
        for j in range(GQA_GROUP):
            s = s_t[:, j * BLOCK:(j + 1) * BLOCK] + bias_ref[1 if n == 0 else 0]
            m = jnp.max(s, axis=0, keepdims=True)
            p = jnp.exp2(s - m)
            den = jnp.sum(p, axis=0, keepdims=True) + jnp.exp2(sink_ref[GQA_GROUP * g + j] * LOG2E - m)
            p_t.append((p * (1.0 / den)).astype(BF16))
        vt = vtwin_ref[g * HEAD_DIM:(g + 1) * HEAD_DIM, n * BLOCK:(n + 2) * BLOCK]
        o_t = jnp.dot(vt, jnp.concatenate(p_t, axis=1), preferred_element_type=F32)
        for c in range(GQA_GROUP // HEADS_PER_COL):
            col_t = jnp.concatenate(
                [o_t[:, (HEADS_PER_COL * c + hh) * BLOCK:(HEADS_PER_COL * c + hh + 1) * BLOCK]
                 for hh in range(HEADS_PER_COL)], axis=0)
            col = POOL_WIDTH + (GQA_GROUP // HEADS_PER_COL * g + c) * LANES
            mix_ref[n * BLOCK:(n + 1) * BLOCK, col:col + LANES] = col_t.T.astype(BF16)

    work = [(n, g) for n in range(tq // BLOCK) for g in range(N_KV_HEADS)]

    def run(mix, project):
        pieces = list(range(2 * OUT_COL_PIECES)) if project else []
        if not mix:
            for piece in pieces:
                out_projection(piece)
            return
        groups = list(range(N_POOL_GROUPS))
        fill_mixing_scratch()
        s_next = scores(*work[0])
        for idx, (n, g) in enumerate(work):
            s_t = s_next
            if idx + 1 < len(work):
                s_next = scores(*work[idx + 1])
            if idx % 2 == 0 and pieces:
                out_projection(pieces.pop(0))
            attend(n, g, s_t)
            if idx % 4 == 1:
                pool_group(groups.pop(0))
        assert not pieces and not groups

    pl.when(step == 0)(lambda: run(True, False))
    pl.when((step > 0) & (step < last_step))(lambda: run(True, True))
    pl.when(step == last_step)(lambda: run(False, True))


def _mix_prompt(sinks, u, q, kd, vt, wpool, pscale, x, w_out, g_ffn, batch, seq, cast=()):
    tq = TOKEN_TILE
    nt = seq // tq
    ntiles = batch * nt
    bpt = tq // BLOCK
    hpt = tq // HALO
    tile = lambda s: jnp.minimum(s, ntiles - 1)
    prev_block = lambda s: jnp.maximum(tile(s) * bpt - 1, 0)
    row = lambda w: pl.BlockSpec((tq, w), lambda s: (tile(s), 0))
    late_row = lambda w: pl.BlockSpec((tq, w), lambda s: (jnp.maximum(s - 1, 0), 0))
    halo = pl.BlockSpec((HALO, POOL_WIDTH), lambda s: (jnp.maximum(tile(s) * hpt - 1, 0), 0))
    const = lambda a: pl.BlockSpec(a.shape, lambda s: (0,) * a.ndim, pipeline_mode=pl.Buffered(1))
    cast_specs, cast_shapes = _cast_specs(cast, ntiles, tile)
    return pl.pallas_call(
        functools.partial(_mix_prompt_kernel, tiles_per_seq=nt),
        grid=(ntiles + 1,),
        in_specs=[pl.BlockSpec(memory_space=pltpu.SMEM), row(POOL_WIDTH), halo, row(ATTN_WIDTH),
                  row(2 * KV_WIDTH),
                  pl.BlockSpec((BLOCK, 2 * KV_WIDTH), lambda s: (prev_block(s), 0)),
                  pl.BlockSpec((KV_WIDTH, tq), lambda s: (0, tile(s))),
                  pl.BlockSpec((KV_WIDTH, BLOCK), lambda s: (0, prev_block(s))),
                  const(wpool), const(pscale), late_row(D_MODEL), const(w_out), const(g_ffn)]
        + cast_specs,
        out_specs=[late_row(D_MODEL), late_row(D_MODEL)] + cast_specs,
        out_shape=[jax.ShapeDtypeStruct((batch * seq, D_MODEL), F32),
                   jax.ShapeDtypeStruct((batch * seq, D_MODEL), BF16)] + cast_shapes,
        scratch_shapes=[pltpu.VMEM((POOL_TOP + tq, POOL_WIDTH), F32),
                        pltpu.VMEM((2, POOL_TOP + tq, POOL_GROUP), F32),
                        pltpu.VMEM((BLOCK + tq, 2 * KV_WIDTH), BF16),
                        pltpu.VMEM((KV_WIDTH, BLOCK + tq), BF16),
                        pltpu.VMEM((2, 2 * BLOCK, BLOCK), F32),
                        pltpu.VMEM((2, tq, MIX_WIDTH), BF16)],
        compiler_params=_params("arbitrary"),
        name="mix_prompt",
    )(sinks, u, u, q, kd, kd, vt, vt, wpool, pscale, x, w_out, g_ffn, *cast)


def _mix_sample_t_kernel(sink_ref, u_ref, q_ref, kn_ref, vn_ref, st_ref, ckt_ref, cvt_ref,
                         wpool_ref, pscale_ref, mix_ref, npool_ref, nkt_ref, nvt_ref, *, dec_seq):
    sb = SEQ_BLOCK
    rows_per_seq = N_HEADS * dec_seq
    nrow = sb * rows_per_seq
    ntok = sb * dec_seq
    win = ckt_ref.shape[3]
    pair_rows = HEADS_PER_COL * GQA_GROUP * dec_seq
    assert win == LANES and ntok <= LANES

    ext = [st_ref[h] for h in range(POOL_HIST)]
    ext += [u_ref[t] for t in range(dec_seq)]
    for h in range(POOL_HIST):
        npool_ref[h] = ext[h + dec_seq]
    r_out = lax.broadcasted_iota(jnp.int32, (ntok, ntok), 0)
    r_in = lax.broadcasted_iota(jnp.int32, (ntok, ntok), 1)
    to_seq_major = ((r_in % sb) * dec_seq + r_in // sb == r_out).astype(BF16)
    for gi, w in enumerate(POOL_WINDOWS):
        cols = slice(gi * POOL_GROUP, (gi + 1) * POOL_GROUP)
        ds = []
        for t in range(dec_seq):
            tok = ext[POOL_HIST + t][:, cols]
            acc = tok
            for j in range(1, w):
                acc = acc + ext[POOL_HIST + t - j][:, cols]
            ds.append(acc * (1.0 / w) - tok)
        d = jnp.concatenate(ds, axis=0).astype(BF16)
        po = jnp.dot(d, wpool_ref[gi], preferred_element_type=F32) * pscale_ref[:, cols]
        po = jnp.dot(to_seq_major, po.astype(BF16), preferred_element_type=F32)
        mix_ref[:, cols] = po.astype(BF16)

    pad_rows = jnp.zeros((LANES - ntok, KV_WIDTH), F32)
    k_new = jnp.concatenate([kn_ref[...], pad_rows], axis=0)
    v_new = jnp.concatenate([vn_ref[...], pad_rows], axis=0)
    k_new_t, v_new_t = k_new.T, v_new.T
    newest = lax.broadcasted_iota(jnp.int32, (HEAD_DIM, LANES), 1) >= win - dec_seq
    for b in range(sb):
        for g in range(N_KV_HEADS):
            dims = slice(g * HEAD_DIM, (g + 1) * HEAD_DIM)
            for old_ref, new_t, out_ref in ((ckt_ref, k_new_t, nkt_ref), (cvt_ref, v_new_t, nvt_ref)):
                kept = pltpu.roll(old_ref[b, g], win - dec_seq, 1)
                fresh = pltpu.roll(new_t[dims, :], win - dec_seq - b * dec_seq, 1)
                out_ref[b, g] = jnp.where(newest, fresh, kept)

    nhalf = 2
    hseq, hrow, htok = sb // nhalf, nrow // nhalf, ntok // nhalf
    r = lax.broadcasted_iota(jnp.int32, (hrow, htok), 0)
    c = lax.broadcasted_iota(jnp.int32, (hrow, htok), 1)
    pick = (c == (r // rows_per_seq) * dec_seq + r % dec_seq).astype(BF16)
    rt = lax.broadcasted_iota(jnp.int32, (htok, hrow), 0)
    ct = lax.broadcasted_iota(jnp.int32, (htok, hrow), 1)
    unpick = (rt == (ct // rows_per_seq) * dec_seq + ct % dec_seq).astype(BF16)
    row = lax.broadcasted_iota(jnp.int32, (hrow, LANES), 0)
    lane = lax.broadcasted_iota(jnp.int32, (hrow, LANES), 1)
    row_seq = row // rows_per_seq
    row_g = (row % rows_per_seq) // (GQA_GROUP * dec_seq)
    row_j = (row // dec_seq) % GQA_GROUP
    row_t = row % dec_seq
    lane_half = lane // HEAD_DIM
    own_half = lane_half == row_g % HEADS_PER_COL
    first_pair = row_g < HEADS_PER_COL
    cache_ok = lane >= row_t + (win - WINDOW)
    k_new_b, v_new_b = k_new.astype(BF16), v_new.astype(BF16)

    def new_ok(hf):
        return (lane // dec_seq == row_seq + hf * hseq) & (lane % dec_seq <= row_t)

    def build_lhs(hf):
        qrep = jnp.dot(pick, q_ref[hf * htok:(hf + 1) * htok, :], preferred_element_type=F32)
        lhs = jnp.zeros((hrow, LANES), F32)
        for qc in range(ATTN_WIDTH // LANES):
            g = qc * HEADS_PER_COL // GQA_GROUP
            src = qrep[:, qc * LANES:(qc + 1) * LANES]
            src_swapped = _swap_halves(src)
            for hh in range(HEADS_PER_COL):
                j = (qc * HEADS_PER_COL + hh) % GQA_GROUP
                here = (row_g == g) & (row_j == j) & own_half
                lhs = lhs + jnp.where(here, src if hh == g % HEADS_PER_COL else src_swapped, 0.0)
        return lhs.astype(BF16)

    def pair_tile(ref, hf, b, a):
        return ref[hf * hseq + b, HEADS_PER_COL * a:HEADS_PER_COL * (a + 1)].reshape(LANES, win).astype(BF16)

    def score(hf, lhs):
        s_cache = jnp.concatenate(
            [jnp.dot(lhs[b * rows_per_seq + a * pair_rows:b * rows_per_seq + (a + 1) * pair_rows],
                     pair_tile(ckt_ref, hf, b, a), preferred_element_type=F32)
             for b in range(hseq) for a in range(N_KV_HEADS // HEADS_PER_COL)], axis=0)
        s_new = [lax.dot_general(lhs, k_new_b[:, a * LANES:(a + 1) * LANES], (((1,), (1,)), ((), ())),
                                 preferred_element_type=F32) for a in range(N_KV_HEADS // HEADS_PER_COL)]
        return s_cache, jnp.where(first_pair, s_new[0], s_new[1])

    def weigh(hf, scores):
        s_cache = jnp.where(cache_ok, scores[0], -jnp.inf)
        s_new = jnp.where(new_ok(hf), scores[1], -jnp.inf)
        sink = sink_ref[hf * hrow:(hf + 1) * hrow, :] * LOG2E
        m = jnp.maximum(jnp.maximum(jnp.max(s_cache, axis=1, keepdims=True),
                                    jnp.max(s_new, axis=1, keepdims=True)), sink)
        p_cache, p_new = jnp.exp2(s_cache - m), jnp.exp2(s_new - m)
        den = (jnp.sum(p_cache, axis=1, keepdims=True) + jnp.sum(p_new, axis=1, keepdims=True)
               + jnp.exp2(sink - m))
        return (p_cache / den).astype(BF16), (p_new / den).astype(BF16)

    def gather_values(hf, probs):
        p_cache, p_new = probs
        o_cache = jnp.concatenate(
            [lax.dot_general(p_cache[b * rows_per_seq + a * pair_rows:b * rows_per_seq + (a + 1) * pair_rows],
                             pair_tile(cvt_ref, hf, b, a), (((1,), (1,)), ((), ())),
                             preferred_element_type=F32)
             for b in range(hseq) for a in range(N_KV_HEADS // HEADS_PER_COL)], axis=0)
        o_new = [jnp.dot(p_new, v_new_b[:, a * LANES:(a + 1) * LANES], preferred_element_type=F32)
                 for a in range(N_KV_HEADS // HEADS_PER_COL)]
        return o_cache + jnp.where(first_pair, o_new[0], o_new[1])

    def emit(hf, o):
        o_swapped = _swap_halves(o)
        z_cols = []
        for qc in range(ATTN_WIDTH // LANES):
            g = qc * HEADS_PER_COL // GQA_GROUP
            zc = jnp.zeros((hrow, LANES), F32)
            for hh in range(HEADS_PER_COL):
                j = (qc * HEADS_PER_COL + hh) % GQA_GROUP
                here = (row_g == g) & (row_j == j) & (lane_half == hh)
                zc = zc + jnp.where(here, o if hh == g % HEADS_PER_COL else o_swapped, 0.0)
            z_cols.append(zc)
        z = jnp.concatenate(z_cols, axis=1).astype(BF16)
        attn = jnp.dot(unpick, z, preferred_element_type=F32)
        mix_ref[hf * htok:(hf + 1) * htok, POOL_WIDTH:] = attn.astype(BF16)

    state = [build_lhs(hf) for hf in range(nhalf)]
    for stage in (score, weigh, gather_values, emit):
        state = [stage(hf, x) for hf, x in enumerate(state)]


def _mix_sample_t(sink_rows, u, q, k_new, v_new, state_t, cache_kt, cache_vt, wpool, pscale, dec_seq):
    nseq = cache_kt.shape[0]
    sb = SEQ_BLOCK
    ntok = sb * dec_seq
    row = lambda w: pl.BlockSpec((ntok, w), lambda i: (i, 0))
    slab = pl.BlockSpec((POOL_HIST, sb, POOL_WIDTH), lambda i: (0, i, 0))
    u_slab = pl.BlockSpec((dec_seq, sb, POOL_WIDTH), lambda i: (0, i, 0))
    cache = pl.BlockSpec((sb,) + cache_kt.shape[1:], lambda i: (i, 0, 0, 0))
    full = lambda a: pl.BlockSpec(a.shape, lambda i: (0,) * a.ndim)
    return pl.pallas_call(
        functools.partial(_mix_sample_t_kernel, dec_seq=dec_seq),
        grid=(nseq // sb,),
        in_specs=[full(sink_rows), u_slab, row(ATTN_WIDTH), row(KV_WIDTH), row(KV_WIDTH),
                  slab, cache, cache, full(wpool), full(pscale)],
        out_specs=[row(MIX_WIDTH), slab, cache, cache],
        out_shape=[jax.ShapeDtypeStruct((nseq * dec_seq, MIX_WIDTH), BF16),
                   jax.ShapeDtypeStruct((POOL_HIST, nseq, POOL_WIDTH), F32),
                   jax.ShapeDtypeStruct(cache_kt.shape, F32),
                   jax.ShapeDtypeStruct(cache_vt.shape, F32)],
        compiler_params=_params("parallel"),
        name="mix_sample",
    )(sink_rows, u, q, k_new, v_new, state_t, cache_kt, cache_vt, wpool, pscale)


def _ffn_kernel(*refs, final_norm, fused_outproj):
    if fused_outproj:
        (x_ref, mix_ref, wout_ref, gffn_ref, wg_ref, wu_ref, wd_ref, gf_ref, y_ref,
         x1_buf, h2_ref, acc_ref) = refs
    else:
        x1_hbm, h2_ref, wg_ref, wu_ref, wd_ref, gf_ref, y_ref, x1_buf, sem = refs
        acc_ref = y_ref
    i, j = pl.program_id(0), pl.program_id(1)
    tm = x1_buf.shape[0]
    last = pl.num_programs(1) - 1

    def activation():
        h = h2_ref[...]
        gate = jnp.dot(h, wg_ref[...], preferred_element_type=F32)
        up = jnp.dot(h, wu_ref[...], preferred_element_type=F32)
        return (jax.nn.silu(gate) * up).astype(BF16)

    @pl.when(j == 0)
    def _():
        if fused_outproj:
            x1 = x_ref[...].reshape(tm, D_MODEL) + jnp.dot(mix_ref[...], wout_ref[...],
                                                           preferred_element_type=F32)
            x1_buf[...] = x1
            h2_ref[...] = _rms(x1, gffn_ref[...]).astype(BF16)
        else:
            _ffn_residual_copy(x1_hbm, x1_buf, sem, i, tm).start()
        acc_ref[...] = jnp.dot(activation(), wd_ref[...], preferred_element_type=F32)

    @pl.when((j > 0) & (j < last))
    def _():
        acc_ref[...] += jnp.dot(activation(), wd_ref[...], preferred_element_type=F32)

    @pl.when(j == last)
    def _():
        if not fused_outproj:
            _ffn_residual_copy(x1_hbm, x1_buf, sem, i, tm).wait()
        act = activation()
        for r in range(FFN_FINAL_BLOCKS):
            rows = slice(r * tm // FFN_FINAL_BLOCKS, (r + 1) * tm // FFN_FINAL_BLOCKS)
            y = (x1_buf[rows, :] + acc_ref[rows, :]
                 + jnp.dot(act[rows, :], wd_ref[...], preferred_element_type=F32))
            y = _rms(y, gf_ref[...]) if final_norm else y
            if fused_outproj:
                per_seq = y_ref.shape[1]
                y_ref[rows.start // per_seq:rows.stop // per_seq] = y.reshape(-1, per_seq, D_MODEL)
            else:
                y_ref[rows, :] = y


def _ffn(x1, h2, wg, wu, wd, gf, final_norm, tm):
    return _ffn_call([x1, h2], [pl.BlockSpec(memory_space=pl.ANY),
                                pl.BlockSpec((tm, D_MODEL), lambda i, j: (i, 0))],
                     [pltpu.VMEM((tm, D_MODEL), F32), pltpu.SemaphoreType.DMA(())],
                     wg, wu, wd, gf, final_norm, tm, fused_outproj=False)


def _outproj_ffn(x, mix, w_out, g_ffn, wg, wu, wd, gf, final_norm, tm):
    const = lambda a: pl.BlockSpec(a.shape, lambda i, j: (0,) * a.ndim, pipeline_mode=pl.Buffered(1))
    return _ffn_call([x, mix, w_out, g_ffn],
                     [_token_block_spec(x, tm), pl.BlockSpec((tm, MIX_WIDTH), lambda i, j: (i, 0)),
                      const(w_out), const(g_ffn)],
                     [pltpu.VMEM((tm, D_MODEL), F32), pltpu.VMEM((tm, D_MODEL), BF16),
                      pltpu.VMEM((tm, D_MODEL), F32)],
                     wg, wu, wd, gf, final_norm, tm, fused_outproj=True)


def _ffn_residual_copy(x1_hbm, x1_buf, sem, i, tm):
    return pltpu.make_async_copy(x1_hbm.at[pl.ds(pl.multiple_of(i * tm, tm), tm)], x1_buf, sem)


def _ffn_call(lead_args, lead_specs, scratch, wg, wu, wd, gf, final_norm, tm, fused_outproj):
    x = lead_args[0]
    tf = FF_TILE
    out_spec = (pl.BlockSpec((tm, D_MODEL), lambda i, j: (i, 0)) if x.ndim == 2
                else _token_block_spec(x, tm))
    return pl.pallas_call(
        functools.partial(_ffn_kernel, final_norm=final_norm, fused_outproj=fused_outproj),
        grid=(x.size // D_MODEL // tm, D_FF // tf),
        in_specs=lead_specs + [pl.BlockSpec((D_MODEL, tf), lambda i, j: (0, j)),
                               pl.BlockSpec((D_MODEL, tf), lambda i, j: (0, j)),
                               pl.BlockSpec((tf, D_MODEL), lambda i, j: (j, 0)),
                               pl.BlockSpec(gf.shape, lambda i, j: (0, 0))],
        out_specs=out_spec,
        out_shape=jax.ShapeDtypeStruct(x.shape, F32),
        scratch_shapes=scratch,
        compiler_params=_params("arbitrary", "arbitrary"),
        name="ffn",
    )(*lead_args, wg, wu, wd, gf)


def _rope_tables(pos):
    inv = ROPE_THETA ** (-np.arange(0, ROPE_DIM, 2, dtype=np.float64) / ROPE_DIM)
    ang = np.asarray(pos, np.float64)[:, None] * inv[None, :]
    cos, sin = np.cos(ang), np.sin(ang)
    n = ang.shape[0]
    rest = HEAD_DIM - ROPE_DIM
    one, zero, zh = np.ones((n, rest)), np.zeros((n, rest)), np.zeros((n, ROPE_HALF))
    per_head = [np.concatenate(parts, axis=1) for parts in
                ([cos, cos, one], [-sin, zh, zero], [zh, sin, zero])]
    return [jnp.asarray(np.tile(t, (1, HEADS_PER_COL)), F32) for t in per_head]


def kernel(x_prompt, x_sample, state_pool, cache_k_win, cache_v_win, g_mix, w_in, w_pool,
           pool_scale, attn_sinks, w_out, g_ffn, w_gate, w_up, w_down, g_final):
    batch, seq, _ = x_prompt.shape
    nseq, dec_seq, _ = x_sample.shape
    depth = w_in.shape[0]
    win_s = cache_k_win.shape[2]
    ntok_s = nseq * dec_seq
    assert seq % TOKEN_TILE == 0 and ntok_s == TOKEN_TILE and nseq % SEQ_BLOCK == 0
    assert win_s == WINDOW and seq >= WINDOW
    assert PAST_LEN >= max(POOL_HIST, WINDOW)

    xp = x_prompt.reshape(batch * seq, D_MODEL)
    xs = x_sample
    tabs_p = _rope_tables(np.arange(seq))
    tabs_s = _rope_tables(PAST_LEN + np.arange(ntok_s) % dec_seq)
    gf = g_final.reshape(1, D_MODEL)

    outs = [[] for _ in range(6)]
    for l in range(depth):
        wpool_l = w_pool[l].astype(BF16)
        gm = g_mix[l].reshape(1, D_MODEL)
        gn = g_ffn[l].reshape(1, D_MODEL)
        pscale = pool_scale[l].reshape(1, POOL_WIDTH)
        sink_rows = jnp.tile(jnp.repeat(attn_sinks[l], dec_seq), SEQ_BLOCK)[:, None]

        us, qs, _, _, _, k_new, v_new, w_in_l = _proj(xs, gm, w_in[l], tabs_s, 1, TOKEN_TILE)

        u, q, kd, vt, u_tail, k_tail, v_tail, w_out_l, wg_l, wd_l = _proj(
            xp, gm, w_in_l, tabs_p, seq // TOKEN_TILE, WINDOW, cast=(w_out[l], w_gate[l], w_down[l]))
        x1, h2, wu_l = _mix_prompt(attn_sinks[l], u, q, kd, vt, wpool_l, pscale, xp, w_out_l, gn,
                                   batch, seq, cast=(w_up[l],))
        xp = _ffn(x1, h2, wg_l, wu_l, wd_l, gf, l == depth - 1, FFN_TOKEN_TILE)
        outs[0].append(u_tail.reshape(batch, HALO, POOL_WIDTH)[:, HALO - POOL_HIST:])
        outs[1].append(k_tail.reshape(batch, WINDOW, N_KV_HEADS, HEAD_DIM))
        outs[2].append(v_tail.reshape(batch, WINDOW, N_KV_HEADS, HEAD_DIM))

        state_t = jnp.transpose(state_pool[l], (1, 0, 2))
        ckt = jnp.transpose(cache_k_win[l], (0, 2, 3, 1))
        cvt = jnp.transpose(cache_v_win[l], (0, 2, 3, 1))
        u_t = jnp.transpose(us.reshape(nseq, dec_seq, POOL_WIDTH), (1, 0, 2))
        mix, npool, nkt, nvt = _mix_sample_t(sink_rows, u_t, qs, k_new, v_new, state_t, ckt, cvt,
                                             wpool_l, pscale, dec_seq)
        xs = _outproj_ffn(xs, mix, w_out_l, gn, wg_l, wu_l, wd_l, gf, l == depth - 1,
                          min(FFN_TOKEN_TILE, ntok_s))
        outs[3].append(jnp.transpose(npool, (1, 0, 2)))
        outs[4].append(jnp.transpose(nkt, (0, 3, 1, 2)))
        outs[5].append(jnp.transpose(nvt, (0, 3, 1, 2)))

    y_prompt = xp.reshape(batch, seq, D_MODEL)
    y_sample = xs
    return (y_prompt, y_sample) + tuple(jnp.stack(o) for o in outs)
```
